```python
import math
import jax, jax.numpy as jnp
from jax import lax
import numpy as np

D_MODEL = 1024
BATCH = 4
SEQ = 4096
DEPTH = 2

CHUNK = 64
Q_BLOCK = 128
NUM_BUCKETS = 32
MAX_DISTANCE = 128
EPS = 1e-6
NEG = -1e30

A_HEADS = 4
A_HEAD_DIM = 64
A_V_DIM = 2 * A_HEAD_DIM
B_HEADS = 8
B_HEAD_DIM = 64
IDX_HEADS = 8
IDX_DIM = 32
TOPK_MAX = 256
N_ATTN_HEADS = A_HEADS + B_HEADS

ATTN_SPLITS = (
    2 * A_HEADS * A_HEAD_DIM,
    2 * A_HEADS * A_HEAD_DIM,
    A_HEADS * A_V_DIM,
    B_HEADS * B_HEAD_DIM,
    B_HEAD_DIM,
    B_HEAD_DIM,
    IDX_HEADS * IDX_DIM,
    IDX_DIM,
    IDX_HEADS,
)
ATTN_IN = sum(ATTN_SPLITS)
ATTN_MIX = A_HEADS * A_V_DIM + B_HEADS * B_HEAD_DIM

CONV_CH = 512
CONV_WIDTH = 31
SC_CH = 512
SC_WIDTH = 3
CONV_IN = 2 * CONV_CH + 3 * SC_CH
CONV_MIX = CONV_CH + SC_CH

D_FF = 4 * D_MODEL
N_EVEN = (DEPTH + 1) // 2
N_ODD = DEPTH // 2

kernel_name = 'hybrid_chunk_causal_diff_dsa_conv_block'


def rms_norm(x, g):
    xf = x.astype(jnp.float32)
    y = xf * lax.rsqrt(jnp.mean(xf * xf, axis=-1, keepdims=True) + EPS)
    return (y * g.astype(jnp.float32)).astype(x.dtype)


def layer_norm(x, g, b):
    xf = x.astype(jnp.float32)
    mu = jnp.mean(xf, axis=-1, keepdims=True)
    var = jnp.mean(jnp.square(xf - mu), axis=-1, keepdims=True)
    y = (xf - mu) * lax.rsqrt(var + EPS)
    return (y * g.astype(jnp.float32) + b.astype(jnp.float32)).astype(x.dtype)


def t5_bucket(rel):
    nb = NUM_BUCKETS // 2
    ret = jnp.where(rel > 0, nb, 0)
    n = jnp.abs(rel)
    max_exact = nb // 2
    nf = jnp.maximum(n, 1).astype(jnp.float32)
    large = max_exact + (jnp.log(nf / max_exact) / math.log(MAX_DISTANCE / max_exact)
                         * (nb - max_exact)).astype(jnp.int32)
    large = jnp.minimum(large, nb - 1)
    return ret + jnp.where(n < max_exact, n, large)


def admissible(q_pos, k_pos):
    return (k_pos // CHUNK) <= (q_pos // CHUNK)


def to_blocks(t, nblk):
    return t.reshape(t.shape[0], nblk, Q_BLOCK, *t.shape[2:]).swapaxes(0, 1)


def from_blocks(t):
    t = t.swapaxes(0, 1)
    return t.reshape(t.shape[0], t.shape[1] * t.shape[2], *t.shape[3:])


def diff_attention(q1, q2, k1, k2, v, bias_tab, lam, subln_g, lambda_init):
    S = q1.shape[1]
    nblk = S // Q_BLOCK
    k_pos = jnp.arange(S)
    scale = A_HEAD_DIM ** -0.5

    def one_block(args):
        qb1, qb2, blk = args
        q_pos = blk * Q_BLOCK + jnp.arange(Q_BLOCK)
        rel = k_pos[None, :] - q_pos[:, None]
        bias = bias_tab[t5_bucket(rel)].astype(jnp.float32).transpose(2, 0, 1)
        mask = admissible(q_pos[:, None], k_pos[None, :])

        def probs(q, k):
            s = jnp.einsum('bqhd,bkhd->bhqk', q, k).astype(jnp.float32) * scale + bias
            return jax.nn.softmax(jnp.where(mask, s, NEG), axis=-1)

        a = probs(qb1, k1) - lam * probs(qb2, k2)
        return jnp.einsum('bhqk,bkhd->bqhd', a.astype(v.dtype), v)

    out = from_blocks(lax.map(one_block, (to_blocks(q1, nblk), to_blocks(q2, nblk), jnp.arange(nblk))))
    out = rms_norm(out, subln_g) * (1.0 - lambda_init)
    return out.reshape(out.shape[0], S, -1)


def dsa_attention(q, qi, wi, k, v, ki, bias_tab, top_k):
    S = q.shape[1]
    nblk = S // Q_BLOCK
    k_pos = jnp.arange(S)
    scale = B_HEAD_DIM ** -0.5
    idx_scale = IDX_DIM ** -0.5
    w_scale = IDX_HEADS ** -0.5

    def one_block(args):
        qb, qib, wib, blk = args
        q_pos = blk * Q_BLOCK + jnp.arange(Q_BLOCK)
        mask = admissible(q_pos[:, None], k_pos[None, :])
        idx_logits = jnp.einsum('bqhd,bkd->bhqk', qib, ki).astype(jnp.float32) * idx_scale
        score = jnp.einsum('bhqk,bqh->bqk', jax.nn.relu(idx_logits),
                           wib.astype(jnp.float32) * w_scale)
        score = jnp.where(mask[None], score, -jnp.inf)
        top_val, top_idx = lax.top_k(score, top_k)
        valid = jnp.isfinite(top_val)
        k_sel = jax.vmap(lambda kb, ib: kb[ib])(k, top_idx)
        v_sel = jax.vmap(lambda vb, ib: vb[ib])(v, top_idx)
        rel = top_idx - q_pos[None, :, None]
        bias = bias_tab[t5_bucket(rel)].astype(jnp.float32).transpose(0, 3, 1, 2)
        s = jnp.einsum('bqhd,bqkd->bhqk', qb, k_sel).astype(jnp.float32) * scale + bias
        p = jax.nn.softmax(jnp.where(valid[:, None], s, NEG), axis=-1)
        return jnp.einsum('bhqk,bqkd->bqhd', p.astype(v_sel.dtype), v_sel)

    out = from_blocks(lax.map(one_block, (to_blocks(q, nblk), to_blocks(qi, nblk),
                                          to_blocks(wi, nblk), jnp.arange(nblk))))
    return out.reshape(out.shape[0], S, -1)


def attention_mixer(h, w_in, w_out, rel_bias, lam_vecs, subln_g, layer_idx, top_k):
    Bsz, S, _ = h.shape
    proj = h @ w_in
    offs = np.cumsum(ATTN_SPLITS)[:-1].tolist()
    aq, ak, av, bq, bk, bv, iq, ik, iw = jnp.split(proj, offs, axis=-1)
    aq = aq.reshape(Bsz, S, A_HEADS, 2, A_HEAD_DIM)
    ak = ak.reshape(Bsz, S, A_HEADS, 2, A_HEAD_DIM)
    av = av.reshape(Bsz, S, A_HEADS, A_V_DIM)
    lambda_init = 0.8 - 0.6 * math.exp(-0.3 * layer_idx)
    lv = lam_vecs.astype(jnp.float32)
    lam = jnp.exp(jnp.sum(lv[0] * lv[1])) - jnp.exp(jnp.sum(lv[2] * lv[3])) + lambda_init
    ya = diff_attention(aq[..., 0, :], aq[..., 1, :], ak[..., 0, :], ak[..., 1, :], av,
                        rel_bias[:, :A_HEADS], lam, subln_g, lambda_init)
    yb = dsa_attention(bq.reshape(Bsz, S, B_HEADS, B_HEAD_DIM),
                       iq.reshape(Bsz, S, IDX_HEADS, IDX_DIM), iw, bk, bv, ik,
                       rel_bias[:, A_HEADS:], top_k)
    return jnp.concatenate([ya, yb], axis=-1) @ w_out


def depthwise_causal_conv(u, w):
    W, C = w.shape
    return lax.conv_general_dilated(u, w[:, None, :].astype(u.dtype), window_strides=(1,),
                                    padding=[(W - 1, 0)],
                                    dimension_numbers=('NWC', 'WIO', 'NWC'),
                                    feature_group_count=C)


def conv_mixer(h, w_in, w_out, dw_w, dw_b, ln_g, ln_b, sc_w):
    proj = h @ w_in
    ca, cg, db, dc, dh = jnp.split(
        proj, [CONV_CH, 2 * CONV_CH, 2 * CONV_CH + SC_CH, 2 * CONV_CH + 2 * SC_CH], axis=-1)
    u = ca * jax.nn.sigmoid(cg)
    u = depthwise_causal_conv(u, dw_w) + dw_b.astype(u.dtype)
    u = jax.nn.silu(layer_norm(u, ln_g, ln_b))
    z = db * depthwise_causal_conv(dc * dh, sc_w)
    return jnp.concatenate([u, z], axis=-1) @ w_out


def setup_inputs(seed: int = 0) -> dict:
    key = jax.random.key(seed)
    ks = jax.random.split(key, 20)
    f32 = jnp.float32
    nrm = lambda k, shape, s: jax.random.normal(k, shape, f32) * s
    return {
        'x': nrm(ks[0], (BATCH, SEQ, D_MODEL), 1.0),
        'rel_bias': nrm(ks[1], (NUM_BUCKETS, N_ATTN_HEADS), 0.5),
        'norm_g': 1.0 + nrm(ks[2], (DEPTH, 4, D_MODEL), 0.05),
        'w_mlp_up': nrm(ks[3], (DEPTH, D_MODEL, D_FF), D_MODEL ** -0.5),
        'w_mlp_down': nrm(ks[4], (DEPTH, D_FF, D_MODEL), D_FF ** -0.5),
        'attn_w_in': nrm(ks[5], (N_EVEN, D_MODEL, ATTN_IN), D_MODEL ** -0.5),
        'attn_w_out': nrm(ks[6], (N_EVEN, ATTN_MIX, D_MODEL), ATTN_MIX ** -0.5),
        'diff_lambda': nrm(ks[7], (N_EVEN, 4, A_HEAD_DIM), 0.1),
        'diff_subln_g': 1.0 + nrm(ks[8], (N_EVEN, A_V_DIM), 0.05),
        'conv_w_in': nrm(ks[9], (N_ODD, D_MODEL, CONV_IN), D_MODEL ** -0.5),
        'conv_w_out': nrm(ks[10], (N_ODD, CONV_MIX, D_MODEL), CONV_MIX ** -0.5),
        'conv_dw_w': nrm(ks[11], (N_ODD, CONV_WIDTH, CONV_CH), CONV_WIDTH ** -0.5),
        'conv_dw_b': nrm(ks[12], (N_ODD, CONV_CH), 0.02),
        'conv_ln_g': 1.0 + nrm(ks[13], (N_ODD, CONV_CH), 0.05),
        'conv_ln_b': nrm(ks[14], (N_ODD, CONV_CH), 0.02),
        'sconv_w': nrm(ks[15], (N_ODD, SC_WIDTH, SC_CH), SC_WIDTH ** -0.5),
    }


def reference(x, rel_bias, norm_g, w_mlp_up, w_mlp_down, attn_w_in, attn_w_out, diff_lambda,
              diff_subln_g, conv_w_in, conv_w_out, conv_dw_w, conv_dw_b, conv_ln_g, conv_ln_b,
              sconv_w):
    top_k = min(TOPK_MAX, x.shape[1] // 4)
    for i in range(DEPTH):
        j = i // 2
        h = rms_norm(x, norm_g[i, 0])
        if i % 2 == 0:
            m = attention_mixer(h, attn_w_in[j], attn_w_out[j], rel_bias, diff_lambda[j],
                                diff_subln_g[j], i, top_k)
        else:
            m = conv_mixer(h, conv_w_in[j], conv_w_out[j], conv_dw_w[j], conv_dw_b[j],
                           conv_ln_g[j], conv_ln_b[j], sconv_w[j])
        x = x + rms_norm(m, norm_g[i, 1])
        h = rms_norm(x, norm_g[i, 2])
        u = jax.nn.relu(h @ w_mlp_up[i])
        x = x + rms_norm((u * u) @ w_mlp_down[i], norm_g[i, 3])
    return x
```

```python
import functools
import math

import numpy as np
import jax
import jax.numpy as jnp
from jax import lax
from jax.experimental import pallas as pl
from jax.experimental.pallas import tpu as pltpu

F32 = jnp.float32
BF16 = jnp.bfloat16
I32 = jnp.int32

D_MODEL = 1024
CHUNK = 64
NUM_BUCKETS = 32
MAX_DISTANCE = 128
EPS = 1e-6
NEG = -1e30
A_HEADS = 4
A_HEAD_DIM = 64
A_V_DIM = 2 * A_HEAD_DIM
B_HEADS = 8
B_HEAD_DIM = 64
IDX_HEADS = 8
IDX_DIM = 32
TOPK_MAX = 256
CONV_CH = 512
CONV_WIDTH = 31
SC_CH = 512
SC_WIDTH = 3
D_FF = 4 * D_MODEL

LANES = 128
SUBLANES = 8
VMEM_LIMIT = 56 * 1024 * 1024

TQ = 256
TK = 256
COL_AQ, COL_AK, COL_AV, COL_BQ = 0, 512, 1024, 1536
COL_IQ, COL_KB, COL_IK, PROJ_W = 2048, 2304, 2432, 2560
VT_ROWS = 128

INT_MIN = -2 ** 31
KEY_NEG_INF = INT_MIN + 0x7FFFFF


def _rms(x, g):
    return x * lax.rsqrt(jnp.mean(x * x, axis=-1, keepdims=True) + EPS) * g


def _dot_nt(a, b):
    return lax.dot_general(a, b, (((1,), (1,)), ((), ())), preferred_element_type=F32)


def _dot(a, b):
    return jnp.dot(a, b, preferred_element_type=F32)


def _attn_in_kernel(x_ref, g_ref, w_ref, wt_ref, proj_ref, vt_ref, iwt_ref, *, tm):
    hb = _rms(x_ref[0], g_ref[...]).astype(BF16)
    proj_ref[0] = _dot(hb, w_ref[...]).astype(BF16)
    t = _dot_nt(wt_ref[...], hb)
    for c in range(tm // TK):
        vt_ref[0, c] = t[0:B_HEAD_DIM, c * TK:(c + 1) * TK].astype(BF16)
    iwt_ref[0] = t[B_HEAD_DIM:B_HEAD_DIM + IDX_HEADS, :] * (IDX_HEADS ** -0.5 * IDX_DIM ** -0.5)


def _attn_in(x, g, w, wt, *, tm=512):
    bsz, s, d = x.shape
    return pl.pallas_call(
        functools.partial(_attn_in_kernel, tm=tm),
        grid=(bsz, s // tm),
        in_specs=[
            pl.BlockSpec((1, tm, d), lambda b, i: (b, i, 0)),
            pl.BlockSpec((1, d), lambda b, i: (0, 0)),
            pl.BlockSpec((d, PROJ_W), lambda b, i: (0, 0)),
            pl.BlockSpec((VT_ROWS, d), lambda b, i: (0, 0)),
        ],
        out_specs=[
            pl.BlockSpec((1, tm, PROJ_W), lambda b, i: (b, i, 0)),
            pl.BlockSpec((1, tm // TK, B_HEAD_DIM, TK), lambda b, i: (b, i, 0, 0)),
            pl.BlockSpec((1, IDX_HEADS, tm), lambda b, i: (b, 0, i)),
        ],
        out_shape=[
            jax.ShapeDtypeStruct((bsz, s, PROJ_W), BF16),
            jax.ShapeDtypeStruct((bsz, s // TK, B_HEAD_DIM, TK), BF16),
            jax.ShapeDtypeStruct((bsz, IDX_HEADS, s), F32),
        ],
        compiler_params=pltpu.CompilerParams(
            dimension_semantics=("arbitrary", "arbitrary"), vmem_limit_bytes=VMEM_LIMIT),
        name="attn_in",
    )(x, g, w, wt)


def _diff_attn_kernel(lam_ref, q_ref, k_ref, v_ref, bias_ref, g_ref, o_ref, s_ref, vaug_ref,
                      *, lambda_init):
    i = pl.program_id(2)

    @pl.when(i == 0)
    def _():
        vaug_ref[:, 0:A_V_DIM] = v_ref[0]
        vaug_ref[:, A_V_DIM:2 * A_V_DIM] = jnp.ones((vaug_ref.shape[0], A_V_DIM), BF16)

    lv = lam_ref[...]
    lam = (jnp.exp(jnp.sum(lv[0:1] * lv[1:2], axis=-1, keepdims=True))
           - jnp.exp(jnp.sum(lv[2:3] * lv[3:4], axis=-1, keepdims=True)) + lambda_init)

    q = q_ref[0] * (A_HEAD_DIM ** -0.5)
    lane = lax.broadcasted_iota(I32, q.shape, 1)
    zero = jnp.zeros_like(q)
    qm = (jnp.where(lane < A_HEAD_DIM, q, zero), jnp.where(lane >= A_HEAD_DIM, q, zero))

    def kblock(j):
        return k_ref[0, pl.ds(pl.multiple_of(j * TK, TK), TK), :]

    first_near = jnp.maximum(i - 1, 0)
    m_init = (jnp.full((TQ, 1), NEG, F32),) * 2

    def far(j, m):
        kb = kblock(j)
        out = []
        for t in range(2):
            s = _dot_nt(qm[t], kb)
            s_ref[t, j] = s
            out.append(jnp.maximum(m[t], jnp.max(s, axis=-1, keepdims=True)))
        return tuple(out)

    def near(j, m):
        kb = kblock(j)
        bias = bias_ref[0, j - i + 1]
        out = []
        for t in range(2):
            s = _dot_nt(qm[t], kb) + bias
            s_ref[t, j] = s
            out.append(jnp.maximum(m[t], jnp.max(s, axis=-1, keepdims=True)))
        return tuple(out)

    m = lax.fori_loop(0, first_near, far, m_init)
    m = lax.fori_loop(first_near, i + 1, near, m)

    def pv(j, acc):
        vb = vaug_ref[pl.ds(pl.multiple_of(j * TK, TK), TK), :]
        return tuple(acc[t] + _dot(jnp.exp(s_ref[t, j] - m[t]).astype(BF16), vb) for t in range(2))

    acc = lax.fori_loop(0, i + 1, pv, (jnp.zeros((TQ, 2 * A_V_DIM), F32),) * 2)
    o1 = acc[0][:, 0:A_V_DIM] / acc[0][:, A_V_DIM:]
    o2 = acc[1][:, 0:A_V_DIM] / acc[1][:, A_V_DIM:]
    a = o1 - lam * o2
    o_ref[0] = (_rms(a, g_ref[...]) * (1.0 - lambda_init)).astype(BF16)


def _diff_attn(proj, lam_vecs, bias, subln_g, *, lambda_init):
    bsz, s, _ = proj.shape
    nq = s // TQ
    blk = A_V_DIM
    return pl.pallas_call(
        functools.partial(_diff_attn_kernel, lambda_init=lambda_init),
        grid=(bsz, A_HEADS, nq),
        in_specs=[
            pl.BlockSpec((4, A_HEAD_DIM), lambda b, h, i: (0, 0)),
            pl.BlockSpec((1, TQ, blk), lambda b, h, i: (b, i, COL_AQ // blk + h)),
            pl.BlockSpec((1, s, blk), lambda b, h, i: (b, 0, COL_AK // blk + h)),
            pl.BlockSpec((1, s, blk), lambda b, h, i: (b, 0, COL_AV // blk + h)),
            pl.BlockSpec((1, 2, TQ, TK), lambda b, h, i: (h, 0, 0, 0)),
            pl.BlockSpec((1, A_V_DIM), lambda b, h, i: (0, 0)),
        ],
        out_specs=pl.BlockSpec((1, TQ, blk), lambda b, h, i: (b, i, h)),
        out_shape=jax.ShapeDtypeStruct((bsz, s, A_HEADS * A_V_DIM), BF16),
        scratch_shapes=[
            pltpu.VMEM((2, s // TK, TQ, TK), F32),
            pltpu.VMEM((s, 2 * A_V_DIM), BF16),
        ],
        compiler_params=pltpu.CompilerParams(
            dimension_semantics=("arbitrary", "arbitrary", "arbitrary"),
            vmem_limit_bytes=VMEM_LIMIT),
        name="diff_attn",
    )(lam_vecs, proj, proj, proj, bias, subln_g)


def _dsa_kernel(qb_ref, iq_ref, kb_ref, ik_ref, vt_ref, iwt_ref, bias_ref, o_ref,
                key_ref, mask_ref, s_ref, vaug_ref, ot_ref, j_ref, *, top_k, seq):
    i = pl.program_id(1)
    nkb = seq // TK

    @pl.when(i == 0)
    def _():
        for c in range(nkb):
            vaug_ref[c, 0:B_HEAD_DIM, :] = vt_ref[0, c]
            vaug_ref[c, B_HEAD_DIM:2 * B_HEAD_DIM, :] = jnp.ones((B_HEAD_DIM, TK), BF16)

    def rows(ref, j):
        return ref[0, pl.ds(pl.multiple_of(j * TK, TK), TK), :]

    lane = lax.broadcasted_iota(I32, (TQ, LANES), 1)
    krow = lax.broadcasted_iota(I32, (TK, TQ), 0)
    qcol = lax.broadcasted_iota(I32, (TK, TQ), 1)

    iq = iq_ref[0]
    per_blk = LANES // IDX_DIM
    iqh = []
    for h in range(IDX_HEADS):
        blk = iq[:, (h // per_blk) * LANES:(h // per_blk + 1) * LANES]
        lo = (h % per_blk) * IDX_DIM
        iqh.append(jnp.where((lane >= lo) & (lane < lo + IDX_DIM), blk, jnp.zeros_like(blk)))
    w = iwt_ref[0]

    def score_keys(j):
        ikb = rows(ik_ref, j)
        acc = jnp.zeros((TK, TQ), F32)
        for h in range(IDX_HEADS):
            acc = acc + jnp.maximum(_dot_nt(ikb, iqh[h]), 0.0) * w[h:h + 1, :]
        acc = jnp.where(acc == 0.0, 0.0, acc)
        bits = lax.bitcast_convert_type(acc, I32)
        return bits ^ ((bits >> 31) & 0x7FFFFFFF)

    def score_body(j, c):
        key_ref[j] = score_keys(j)
        return c

    lax.fori_loop(0, i, score_body, 0)
    admissible = (krow // CHUNK) <= (qcol // CHUNK)
    key_ref[i] = jnp.where(admissible, score_keys(i), KEY_NEG_INF)

    def count(indicator):
        def body(j, acc):
            return acc + jnp.sum(indicator(key_ref[j], j), axis=0, keepdims=True)
        return lax.fori_loop(0, i + 1, body, jnp.zeros((1, TQ), F32))

    def count_ge(cand):
        return count(lambda kb, j: jnp.where(kb >= cand, 1.0, 0.0))

    thr = jnp.where(count_ge(jnp.zeros((1, TQ), I32)) >= top_k, 0, INT_MIN)

    def bit_body(b, thr):
        cand = thr + lax.shift_left(jnp.int32(1), 30 - b)
        return jnp.where(count_ge(cand) >= top_k, cand, thr)

    thr = lax.fori_loop(0, 31, bit_body, thr)

    n_ge = count_ge(thr)
    need = top_k - count(lambda kb, j: jnp.where(kb > thr, 1.0, 0.0))
    j_ref[...] = jnp.full((1, TQ), seq, I32)
    tied = jnp.max(jnp.where(n_ge > top_k, jnp.where(thr > KEY_NEG_INF, 1.0, 0.0), 0.0))

    @pl.when(tied > 0)
    def _():
        def idx_body(b, lim):
            cand = lim + lax.shift_left(jnp.int32(1), 12 - b)
            c = count(lambda kb, j: jnp.where(
                kb == thr, jnp.where((krow + j * TK) < cand, 1.0, 0.0), 0.0))
            return jnp.where(c <= need, cand, lim)
        j_ref[...] = lax.fori_loop(0, 13, idx_body, jnp.zeros((1, TQ), I32))

    lim = j_ref[...]

    def mask_body(j, c):
        kb = key_ref[j]
        tie_ok = jnp.where((krow + j * TK) < lim, 0.0, NEG)
        sel = jnp.where(kb > thr, 0.0, jnp.where(kb == thr, tie_ok, NEG))
        mask_ref[j] = jnp.where(kb > KEY_NEG_INF, sel, NEG)
        return c

    lax.fori_loop(0, i + 1, mask_body, 0)

    first_near = jnp.maximum(i - 1, 0)
    qb = qb_ref[0] * (B_HEAD_DIM ** -0.5)
    for h in range(B_HEADS):
        pair = qb[:, (h // 2) * LANES:(h // 2 + 1) * LANES]
        lo = (h % 2) * B_HEAD_DIM
        qh = jnp.where((lane >= lo) & (lane < lo + B_HEAD_DIM), pair, jnp.zeros_like(pair))

        def far(j, m, qh=qh):
            s = _dot_nt(rows(kb_ref, j), qh) + mask_ref[j]
            s_ref[j] = s
            return jnp.maximum(m, jnp.max(s, axis=0, keepdims=True))

        def near(j, m, qh=qh, h=h):
            s = _dot_nt(rows(kb_ref, j), qh) + mask_ref[j] + bias_ref[j - i + 1, h]
            s_ref[j] = s
            return jnp.maximum(m, jnp.max(s, axis=0, keepdims=True))

        m = lax.fori_loop(0, first_near, far, jnp.full((1, TQ), NEG, F32))
        m = lax.fori_loop(first_near, i + 1, near, m)

        def pv(j, acc, m=m):
            return acc + _dot(vaug_ref[j], jnp.exp(s_ref[j] - m).astype(BF16))

        acc = lax.fori_loop(0, i + 1, pv, jnp.zeros((2 * B_HEAD_DIM, TQ), F32))
        ot_ref[h * B_HEAD_DIM:(h + 1) * B_HEAD_DIM, :] = (
            acc[0:B_HEAD_DIM] / acc[B_HEAD_DIM:B_HEAD_DIM + 1])

    o_ref[0] = ot_ref[...].T.astype(BF16)


def _dsa_attn(proj, vt, iwt, bias, *, top_k):
    bsz, s, _ = proj.shape
    nq = s // TQ
    nkb = s // TK
    return pl.pallas_call(
        functools.partial(_dsa_kernel, top_k=top_k, seq=s),
        grid=(bsz, nq),
        in_specs=[
            pl.BlockSpec((1, TQ, 512), lambda b, i: (b, i, COL_BQ // 512)),
            pl.BlockSpec((1, TQ, 256), lambda b, i: (b, i, COL_IQ // 256)),
            pl.BlockSpec((1, s, LANES), lambda b, i: (b, 0, COL_KB // LANES)),
            pl.BlockSpec((1, s, LANES), lambda b, i: (b, 0, COL_IK // LANES)),
            pl.BlockSpec((1, nkb, B_HEAD_DIM, TK), lambda b, i: (b, 0, 0, 0)),
            pl.BlockSpec((1, IDX_HEADS, TQ), lambda b, i: (b, 0, i)),
            pl.BlockSpec((2, B_HEADS, TK, TQ), lambda b, i: (0, 0, 0, 0)),
        ],
        out_specs=pl.BlockSpec((1, TQ, B_HEADS * B_HEAD_DIM), lambda b, i: (b, i, 0)),
        out_shape=jax.ShapeDtypeStruct((bsz, s, B_HEADS * B_HEAD_DIM), BF16),
        scratch_shapes=[
            pltpu.VMEM((nkb, TK, TQ), I32),
            pltpu.VMEM((nkb, TK, TQ), F32),
            pltpu.VMEM((nkb, TK, TQ), F32),
            pltpu.VMEM((nkb, 2 * B_HEAD_DIM, TK), BF16),
            pltpu.VMEM((B_HEADS * B_HEAD_DIM, TQ), F32),
            pltpu.VMEM((1, TQ), I32),
        ],
        compiler_params=pltpu.CompilerParams(
            dimension_semantics=("arbitrary", "arbitrary"), vmem_limit_bytes=VMEM_LIMIT),
        name="dsa_attn",
    )(proj, proj, proj, proj, vt, iwt, bias)


def _attn_out_kernel(ya_ref, yb_ref, w_ref, x_ref, g_ref, o_ref):
    half = ya_ref.shape[-1]
    m = _dot(ya_ref[...], w_ref[0:half, :]) + _dot(yb_ref[...], w_ref[half:2 * half, :])
    o_ref[...] = x_ref[...] + _rms(m, g_ref[...])


def _attn_out(ya, yb, w, x, g, *, tm=512):
    n, d = x.shape
    half = ya.shape[-1]
    return pl.pallas_call(
        _attn_out_kernel,
        grid=(n // tm,),
        in_specs=[
            pl.BlockSpec((tm, half), lambda i: (i, 0)),
            pl.BlockSpec((tm, half), lambda i: (i, 0)),
            pl.BlockSpec((2 * half, d), lambda i: (0, 0)),
            pl.BlockSpec((tm, d), lambda i: (i, 0)),
            pl.BlockSpec((1, d), lambda i: (0, 0)),
        ],
        out_specs=pl.BlockSpec((tm, d), lambda i: (i, 0)),
        out_shape=jax.ShapeDtypeStruct((n, d), F32),
        compiler_params=pltpu.CompilerParams(
            dimension_semantics=("arbitrary",), vmem_limit_bytes=VMEM_LIMIT),
        name="attn_out",
    )(ya, yb, w, x, g)


def _mlp_kernel(x_ref, g_in_ref, wu_ref, wd_ref, g_out_ref, o_ref, h_ref, acc_ref):
    f = pl.program_id(1)

    @pl.when(f == 0)
    def _():
        h_ref[...] = _rms(x_ref[...], g_in_ref[...]).astype(BF16)
        acc_ref[...] = jnp.zeros_like(acc_ref)

    u = jnp.maximum(_dot(h_ref[...], wu_ref[...]), 0.0)
    acc_ref[...] += _dot((u * u).astype(BF16), wd_ref[...])

    @pl.when(f == pl.num_programs(1) - 1)
    def _():
        o_ref[...] = x_ref[...] + _rms(acc_ref[...], g_out_ref[...])


def _mlp(x, g_in, wu, wd, g_out, *, tm=1024, tf=1024):
    n, d = x.shape
    ff = wu.shape[1]
    return pl.pallas_call(
        _mlp_kernel,
        grid=(n // tm, ff // tf),
        in_specs=[
            pl.BlockSpec((tm, d), lambda i, f: (i, 0)),
            pl.BlockSpec((1, d), lambda i, f: (0, 0)),
            pl.BlockSpec((d, tf), lambda i, f: (0, f)),
            pl.BlockSpec((tf, d), lambda i, f: (f, 0)),
            pl.BlockSpec((1, d), lambda i, f: (0, 0)),
        ],
        out_specs=pl.BlockSpec((tm, d), lambda i, f: (i, 0)),
        out_shape=jax.ShapeDtypeStruct((n, d), F32),
        scratch_shapes=[pltpu.VMEM((tm, d), BF16), pltpu.VMEM((tm, d), F32)],
        compiler_params=pltpu.CompilerParams(
            dimension_semantics=("arbitrary", "arbitrary"), vmem_limit_bytes=VMEM_LIMIT),
        name="mlp",
    )(x, g_in, wu, wd, g_out)


U_HIST = 32
Z_HIST = 8


def _conv_kernel(x_ref, g_in_ref, w_in_ref, dw_w_ref, dw_b_ref, ln_g_ref, ln_b_ref, sc_w_ref,
                 w_out_ref, g_out_ref, o_ref, ubuf_ref, zbuf_ref, *, ts):
    @pl.when(pl.program_id(1) == 0)
    def _():
        ubuf_ref[0:U_HIST, :] = jnp.zeros((U_HIST, CONV_CH), F32)
        zbuf_ref[0:Z_HIST, :] = jnp.zeros((Z_HIST, SC_CH), F32)

    x = x_ref[0]
    hb = _rms(x, g_in_ref[...]).astype(BF16)
    c = CONV_CH
    ca = _dot(hb, w_in_ref[:, 0:c])
    cg = _dot(hb, w_in_ref[:, c:2 * c])
    ubuf_ref[U_HIST:U_HIST + ts, :] = ca * jax.nn.sigmoid(cg)
    dc = _dot(hb, w_in_ref[:, 3 * c:4 * c])
    dh = _dot(hb, w_in_ref[:, 4 * c:5 * c])
    zbuf_ref[Z_HIST:Z_HIST + ts, :] = dc * dh

    u = jnp.zeros((ts, c), F32)
    for j in range(CONV_WIDTH):
        off = U_HIST - (CONV_WIDTH - 1) + j
        u = u + dw_w_ref[j:j + 1, :] * ubuf_ref[off:off + ts, :]
    u = u + dw_b_ref[...]
    mu = jnp.mean(u, axis=-1, keepdims=True)
    ctr = u - mu
    var = jnp.mean(ctr * ctr, axis=-1, keepdims=True)
    u = ctr * lax.rsqrt(var + EPS) * ln_g_ref[...] + ln_b_ref[...]
    u = u * jax.nn.sigmoid(u)

    z = jnp.zeros((ts, c), F32)
    for j in range(SC_WIDTH):
        off = Z_HIST - (SC_WIDTH - 1) + j
        z = z + sc_w_ref[j:j + 1, :] * zbuf_ref[off:off + ts, :]
    z = _dot(hb, w_in_ref[:, 2 * c:3 * c]) * z

    ubuf_ref[0:U_HIST, :] = ubuf_ref[ts:ts + U_HIST, :]
    zbuf_ref[0:Z_HIST, :] = zbuf_ref[ts:ts + Z_HIST, :]

    m = _dot(u.astype(BF16), w_out_ref[0:c, :]) + _dot(z.astype(BF16), w_out_ref[c:2 * c, :])
    o_ref[0] = x + _rms(m, g_out_ref[...])


def _conv_mixer(x, g_in, w_in, dw_w, dw_b, ln_g, ln_b, sc_w, w_out, g_out, *, ts=512):
    bsz, s, d = x.shape
    full = lambda a: pl.BlockSpec(a.shape, lambda b, i: (0,) * a.ndim)
    return pl.pallas_call(
        functools.partial(_conv_kernel, ts=ts),
        grid=(bsz, s // ts),
        in_specs=[pl.BlockSpec((1, ts, d), lambda b, i: (b, i, 0)),
                  full(g_in), full(w_in), full(dw_w), full(dw_b), full(ln_g), full(ln_b),
                  full(sc_w), full(w_out), full(g_out)],
        out_specs=pl.BlockSpec((1, ts, d), lambda b, i: (b, i, 0)),
        out_shape=jax.ShapeDtypeStruct((bsz, s, d), F32),
        scratch_shapes=[pltpu.VMEM((U_HIST + ts, CONV_CH), F32),
                        pltpu.VMEM((Z_HIST + ts, SC_CH), F32)],
        compiler_params=pltpu.CompilerParams(
            dimension_semantics=("arbitrary", "arbitrary"), vmem_limit_bytes=VMEM_LIMIT),
        name="conv_mixer",
    )(x, g_in, w_in, dw_w, dw_b, ln_g, ln_b, sc_w, w_out, g_out)


def _t5_bucket(rel):
    nb = NUM_BUCKETS // 2
    ret = jnp.where(rel > 0, nb, 0)
    n = jnp.abs(rel)
    max_exact = nb // 2
    nf = jnp.maximum(n, 1).astype(jnp.float32)
    large = max_exact + (jnp.log(nf / max_exact) / math.log(MAX_DISTANCE / max_exact)
                         * (nb - max_exact)).astype(jnp.int32)
    large = jnp.minimum(large, nb - 1)
    return ret + jnp.where(n < max_exact, n, large)


def _bias_tables(rel_bias):
    lo = -(TQ + TK - 1)
    rel = jnp.arange(lo, TK, dtype=jnp.int32)
    far_bucket = NUM_BUCKETS // 2 - 1
    f = rel_bias[_t5_bucket(rel)].astype(F32) - rel_bias[far_bucket].astype(F32)[None, :]
    qi = np.arange(TQ)[:, None]
    ki = np.arange(TK)[None, :]
    idx_diag = (ki - qi) - lo
    idx_prev = (ki - TK - qi) - lo
    diag = jnp.take(f, jnp.asarray(idx_diag), axis=0)
    prev = jnp.take(f, jnp.asarray(idx_prev), axis=0)
    adm = jnp.asarray((ki // CHUNK) <= (qi // CHUNK))
    a_diag = jnp.where(adm[:, :, None], diag[:, :, :A_HEADS], NEG)
    a_bias = jnp.stack([prev[:, :, :A_HEADS], a_diag], axis=0).transpose(3, 0, 1, 2)
    b_bias = jnp.stack([prev[:, :, A_HEADS:], diag[:, :, A_HEADS:]], axis=0).transpose(0, 3, 2, 1)
    return a_bias, b_bias


def _attn_weights(w_in):
    o = np.cumsum([0, 512, 512, 512, 512, 64, 64, 256, 32, 8])
    aq, ak, av, bq, bk, bv, iq, ik, iw = [w_in[:, o[n]:o[n + 1]] for n in range(9)]
    w = jnp.concatenate([aq, ak, av, bq, iq, bk, bk, ik, ik, ik, ik], axis=1).astype(BF16)
    pad = jnp.zeros((w_in.shape[0], VT_ROWS - B_HEAD_DIM - IDX_HEADS), w_in.dtype)
    wt = jnp.concatenate([bv, iw, pad], axis=1).T.astype(BF16)
    return w, wt


def kernel(x, rel_bias, norm_g, w_mlp_up, w_mlp_down, attn_w_in, attn_w_out, diff_lambda,
           diff_subln_g, conv_w_in, conv_w_out, conv_dw_w, conv_dw_b, conv_ln_g, conv_ln_b,
           sconv_w):
    bsz, s, d = x.shape
    depth = norm_g.shape[0]
    top_k = min(TOPK_MAX, s // 4)
    row = lambda v: v.reshape(1, -1)
    a_bias, b_bias = _bias_tables(rel_bias)
    for i in range(depth):
        j = i // 2
        if i % 2 == 0:
            lambda_init = 0.8 - 0.6 * math.exp(-0.3 * i)
            w, wt = _attn_weights(attn_w_in[j])
            proj, vt, iwt = _attn_in(x.reshape(bsz, s, d), row(norm_g[i, 0]), w, wt)
            ya = _diff_attn(proj, diff_lambda[j], a_bias, row(diff_subln_g[j]),
                            lambda_init=lambda_init)
            yb = _dsa_attn(proj, vt, iwt, b_bias, top_k=top_k)
            x = _attn_out(ya.reshape(bsz * s, -1), yb.reshape(bsz * s, -1),
                          attn_w_out[j].astype(BF16), x.reshape(bsz * s, d),
                          row(norm_g[i, 1]))
        else:
            x = _conv_mixer(x.reshape(bsz, s, d), row(norm_g[i, 0]), conv_w_in[j].astype(BF16),
                            conv_dw_w[j], row(conv_dw_b[j]), row(conv_ln_g[j]),
                            row(conv_ln_b[j]), sconv_w[j], conv_w_out[j].astype(BF16),
                            row(norm_g[i, 1])).reshape(bsz * s, d)
        x = _mlp(x, row(norm_g[i, 2]), w_mlp_up[i].astype(BF16), w_mlp_down[i].astype(BF16),
                 row(norm_g[i, 3]))
    return x.reshape(bsz, s, d)
```

```python
import functools
import math

import numpy as np
import jax
import jax.numpy as jnp
from jax import lax
from jax.experimental import pallas as pl
from jax.experimental.pallas import tpu as pltpu

F32 = jnp.float32
BF16 = jnp.bfloat16
I32 = jnp.int32

D_MODEL = 1024
CHUNK = 64
NUM_BUCKETS = 32
MAX_DISTANCE = 128
EPS = 1e-6
NEG = -1e30
A_HEADS = 4
A_HEAD_DIM = 64
A_V_DIM = 2 * A_HEAD_DIM
B_HEADS = 8
B_HEAD_DIM = 64
IDX_HEADS = 8
IDX_DIM = 32
TOPK_MAX = 256
CONV_CH = 512
CONV_WIDTH = 31
SC_CH = 512
SC_WIDTH = 3
D_FF = 4 * D_MODEL

LANES = 128
SUBLANES = 8
VMEM_LIMIT = 56 * 1024 * 1024

TQ = 256
TK = 256
COL_AQ, COL_AK, COL_BQ, COL_IQ, COL_KB, COL_IK, PROJ_W = 0, 512, 1024, 1536, 1792, 1920, 2048
ROW_AV, ROW_BV, ROW_IW, VT_ROWS = 0, 512, 576, 640
ONES_ROWS = 16

INT_MIN = -2 ** 31
KEY_NEG_INF = INT_MIN + 0x7FFFFF


def _rms(x, g):
    return x * lax.rsqrt(jnp.mean(x * x, axis=-1, keepdims=True) + EPS) * g


def _dot_nt(a, b):
    return lax.dot_general(a, b, (((1,), (1,)), ((), ())), preferred_element_type=F32)


def _dot(a, b):
    return jnp.dot(a, b, preferred_element_type=F32)


def _lane_band(x, lo, width):
    lane = lax.broadcasted_iota(I32, x.shape, 1)
    return jnp.where((lane >= lo) & (lane < lo + width), x, jnp.zeros_like(x))


def _bias_kind(j, i):
    return jnp.clip(j - i + 2, 0, 2)


def _store_logits(s, s_ref, slot, c):
    s_ref[slot, c] = s
    return jnp.max(s, axis=0, keepdims=True)


def _online_softmax_step(s_ref, slot, c, m_old, block_max, v_aug, acc_ref):
    m_new = jnp.maximum(m_old, block_max)
    alpha = jnp.exp(m_old - m_new)
    p = jnp.exp(s_ref[slot, c] - m_new).astype(BF16)
    acc_ref[c] = acc_ref[c] * alpha + _dot(v_aug, p)
    return m_new


def _stream_key_blocks(i, n_chains, logits, accumulate):
    ms = (jnp.full((1, TQ), NEG, F32),) * n_chains

    def body(j, carry):
        ms, bms = carry
        slot = j & 1
        nxt = logits(j + 1, 1 - slot)
        return accumulate(j, slot, ms, bms), nxt

    ms, bms = lax.fori_loop(0, i, body, (ms, logits(0, 0)))
    return accumulate(i, i & 1, ms, bms)


def _attn_in_kernel(x_ref, g_ref, w_ref, wt_ref, proj_ref, avt_ref, bvt_ref, iwt_ref, *, tm):
    hb = _rms(x_ref[0], g_ref[...]).astype(BF16)
    proj_ref[0] = _dot(hb, w_ref[...]).astype(BF16)
    t = _dot_nt(wt_ref[...], hb)
    for c in range(tm // TK):
        avt_ref[0, c] = t[ROW_AV:ROW_BV, c * TK:(c + 1) * TK].astype(BF16)
        bvt_ref[0, c] = t[ROW_BV:ROW_IW, c * TK:(c + 1) * TK].astype(BF16)
    iwt_ref[0] = t[ROW_IW:ROW_IW + IDX_HEADS, :] * (IDX_HEADS ** -0.5 * IDX_DIM ** -0.5)


def _attn_in(x, g, w, wt, *, tm=512):
    bsz, s, d = x.shape
    av_rows = ROW_BV - ROW_AV
    return pl.pallas_call(
        functools.partial(_attn_in_kernel, tm=tm),
        grid=(bsz, s // tm),
        in_specs=[
            pl.BlockSpec((1, tm, d), lambda b, i: (b, i, 0)),
            pl.BlockSpec((1, d), lambda b, i: (0, 0)),
            pl.BlockSpec((d, PROJ_W), lambda b, i: (0, 0)),
            pl.BlockSpec((VT_ROWS, d), lambda b, i: (0, 0)),
        ],
        out_specs=[
            pl.BlockSpec((1, tm, PROJ_W), lambda b, i: (b, i, 0)),
            pl.BlockSpec((1, tm // TK, av_rows, TK), lambda b, i: (b, i, 0, 0)),
            pl.BlockSpec((1, tm // TK, B_HEAD_DIM, TK), lambda b, i: (b, i, 0, 0)),
            pl.BlockSpec((1, IDX_HEADS, tm), lambda b, i: (b, 0, i)),
        ],
        out_shape=[
            jax.ShapeDtypeStruct((bsz, s, PROJ_W), BF16),
            jax.ShapeDtypeStruct((bsz, s // TK, av_rows, TK), BF16),
            jax.ShapeDtypeStruct((bsz, s // TK, B_HEAD_DIM, TK), BF16),
            jax.ShapeDtypeStruct((bsz, IDX_HEADS, s), F32),
        ],
        compiler_params=pltpu.CompilerParams(
            dimension_semantics=("arbitrary", "arbitrary"), vmem_limit_bytes=VMEM_LIMIT),
        name="attn_in",
    )(x, g, w, wt)


def _diff_attn_kernel(lam_ref, q_ref, k_ref, vt_ref, bias_ref, g_ref, o_ref,
                      vaug_ref, s_ref, acc_ref, ot_ref, *, lambda_init, seq):
    i = pl.program_id(1)
    nkb = seq // TK
    n_maps = 2 * A_HEADS

    @pl.when(i == 0)
    def _():
        for c in range(nkb):
            for h in range(A_HEADS):
                vaug_ref[h, c, 0:A_V_DIM, :] = vt_ref[0, c, h * A_V_DIM:(h + 1) * A_V_DIM, :]
                vaug_ref[h, c, A_V_DIM:A_V_DIM + ONES_ROWS, :] = jnp.ones((ONES_ROWS, TK), BF16)

    lv = lam_ref[...]
    lam = (jnp.exp(jnp.sum(lv[0:1] * lv[1:2], axis=-1, keepdims=True))
           - jnp.exp(jnp.sum(lv[2:3] * lv[3:4], axis=-1, keepdims=True)) + lambda_init)

    q = q_ref[0] * (A_HEAD_DIM ** -0.5)
    qm = []
    for h in range(A_HEADS):
        qh = q[:, h * LANES:(h + 1) * LANES]
        qm += [_lane_band(qh, 0, A_HEAD_DIM), _lane_band(qh, A_HEAD_DIM, A_HEAD_DIM)]

    acc_ref[...] = jnp.zeros_like(acc_ref)

    def logits(j, slot):
        krows = k_ref[0, pl.ds(pl.multiple_of(j * TK, TK), TK), :]
        t = _bias_kind(j, i)
        out = []
        for h in range(A_HEADS):
            kb = krows[:, h * LANES:(h + 1) * LANES]
            bias = bias_ref[t, h]
            for c in (2 * h, 2 * h + 1):
                out.append(_store_logits(_dot_nt(kb, qm[c]) + bias, s_ref, slot, c))
        return tuple(out)

    def accumulate(j, slot, ms, bms):
        return tuple(_online_softmax_step(s_ref, slot, c, ms[c], bms[c], vaug_ref[c // 2, j],
                                          acc_ref) for c in range(n_maps))

    _stream_key_blocks(i, n_maps, logits, accumulate)

    for h in range(A_HEADS):
        a1 = acc_ref[2 * h]
        a2 = acc_ref[2 * h + 1]
        a = (a1[0:A_V_DIM] / a1[A_V_DIM:A_V_DIM + 1]
             - lam * (a2[0:A_V_DIM] / a2[A_V_DIM:A_V_DIM + 1]))
        y = a * lax.rsqrt(jnp.mean(a * a, axis=0, keepdims=True) + EPS) * g_ref[...]
        ot_ref[h * A_V_DIM:(h + 1) * A_V_DIM, :] = y * (1.0 - lambda_init)
    o_ref[0] = ot_ref[...].T.astype(BF16)


def _diff_attn(proj, avt, lam_vecs, bias, subln_g_col, *, lambda_init):
    bsz, s, _ = proj.shape
    nq = s // TQ
    nkb = s // TK
    width = A_HEADS * A_V_DIM
    return pl.pallas_call(
        functools.partial(_diff_attn_kernel, lambda_init=lambda_init, seq=s),
        grid=(bsz, nq),
        in_specs=[
            pl.BlockSpec((4, A_HEAD_DIM), lambda b, i: (0, 0)),
            pl.BlockSpec((1, TQ, width), lambda b, i: (b, i, COL_AQ // width)),
            pl.BlockSpec((1, s, width), lambda b, i: (b, 0, COL_AK // width)),
            pl.BlockSpec((1, nkb, width, TK), lambda b, i: (b, 0, 0, 0)),
            pl.BlockSpec((3, A_HEADS, TK, TQ), lambda b, i: (0, 0, 0, 0)),
            pl.BlockSpec((A_V_DIM, 1), lambda b, i: (0, 0)),
        ],
        out_specs=pl.BlockSpec((1, TQ, width), lambda b, i: (b, i, 0)),
        out_shape=jax.ShapeDtypeStruct((bsz, s, width), BF16),
        scratch_shapes=[
            pltpu.VMEM((A_HEADS, nkb, A_V_DIM + ONES_ROWS, TK), BF16),
            pltpu.VMEM((2, 2 * A_HEADS, TK, TQ), F32),
            pltpu.VMEM((2 * A_HEADS, A_V_DIM + ONES_ROWS, TQ), F32),
            pltpu.VMEM((width, TQ), F32),
        ],
        compiler_params=pltpu.CompilerParams(
            dimension_semantics=("arbitrary", "arbitrary"), vmem_limit_bytes=VMEM_LIMIT),
        name="diff_attn",
    )(lam_vecs, proj, proj, avt, bias, subln_g_col)


def _dsa_kernel(qb_ref, iq_ref, kb_ref, ik_ref, vt_ref, iwt_ref, bias_ref, o_ref,
                key_ref, mask_ref, vaug_ref, s_ref, acc_ref, ot_ref, j_ref, *, top_k, seq):
    i = pl.program_id(1)
    nkb = seq // TK

    @pl.when(i == 0)
    def _():
        for c in range(nkb):
            vaug_ref[c, 0:B_HEAD_DIM, :] = vt_ref[0, c]
            vaug_ref[c, B_HEAD_DIM:B_HEAD_DIM + ONES_ROWS, :] = jnp.ones((ONES_ROWS, TK), BF16)

    def rows(ref, j):
        return ref[0, pl.ds(pl.multiple_of(j * TK, TK), TK), :]

    krow = lax.broadcasted_iota(I32, (TK, TQ), 0)
    qcol = lax.broadcasted_iota(I32, (TK, TQ), 1)

    iq = iq_ref[0]
    per_blk = LANES // IDX_DIM
    iqh = [_lane_band(iq[:, (h // per_blk) * LANES:(h // per_blk + 1) * LANES],
                      (h % per_blk) * IDX_DIM, IDX_DIM) for h in range(IDX_HEADS)]
    w = iwt_ref[0]

    def score_keys(j):
        ikb = rows(ik_ref, j)
        acc = jnp.zeros((TK, TQ), F32)
        for h in range(IDX_HEADS):
            acc = acc + jnp.maximum(_dot_nt(ikb, iqh[h]), 0.0) * w[h:h + 1, :]
        acc = jnp.where(acc == 0.0, 0.0, acc)
        bits = lax.bitcast_convert_type(acc, I32)
        return bits ^ ((bits >> 31) & 0x7FFFFFFF)

    def score_body(j, c):
        key_ref[j] = score_keys(j)
        return c

    lax.fori_loop(0, i, score_body, 0)
    admissible = (krow // CHUNK) <= (qcol // CHUNK)
    key_ref[i] = jnp.where(admissible, score_keys(i), KEY_NEG_INF)

    def count(indicator):
        def body(j, acc):
            return acc + jnp.sum(indicator(key_ref[j], j), axis=0, keepdims=True)
        return lax.fori_loop(0, i + 1, body, jnp.zeros((1, TQ), F32))

    def count_ge(cand):
        return count(lambda kb, j: jnp.where(kb >= cand, 1.0, 0.0))

    thr = jnp.where(count_ge(jnp.zeros((1, TQ), I32)) >= top_k, 0, INT_MIN)

    def bit_body(b, thr):
        cand = thr + lax.shift_left(jnp.int32(1), 30 - b)
        return jnp.where(count_ge(cand) >= top_k, cand, thr)

    thr = lax.fori_loop(0, 31, bit_body, thr)

    n_ge = count_ge(thr)
    need = top_k - count(lambda kb, j: jnp.where(kb > thr, 1.0, 0.0))
    j_ref[...] = jnp.full((1, TQ), seq, I32)
    tied = jnp.max(jnp.where(n_ge > top_k, jnp.where(thr > KEY_NEG_INF, 1.0, 0.0), 0.0))

    @pl.when(tied > 0)
    def _():
        def idx_body(b, lim):
            cand = lim + lax.shift_left(jnp.int32(1), 12 - b)
            c = count(lambda kb, j: jnp.where(
                kb == thr, jnp.where((krow + j * TK) < cand, 1.0, 0.0), 0.0))
            return jnp.where(c <= need, cand, lim)
        j_ref[...] = lax.fori_loop(0, 13, idx_body, jnp.zeros((1, TQ), I32))

    lim = j_ref[...]

    def mask_body(j, c):
        kb = key_ref[j]
        tie_ok = jnp.where((krow + j * TK) < lim, 0.0, NEG)
        sel = jnp.where(kb > thr, 0.0, jnp.where(kb == thr, tie_ok, NEG))
        mask_ref[j] = jnp.where(kb > KEY_NEG_INF, sel, NEG)
        return c

    lax.fori_loop(0, i + 1, mask_body, 0)

    qb = qb_ref[0] * (B_HEAD_DIM ** -0.5)
    qh = [_lane_band(qb[:, (h // 2) * LANES:(h // 2 + 1) * LANES],
                     (h % 2) * B_HEAD_DIM, B_HEAD_DIM) for h in range(B_HEADS)]
    acc_ref[...] = jnp.zeros_like(acc_ref)

    def logits(j, slot):
        kblk = rows(kb_ref, j)
        msk = mask_ref[j]
        t = _bias_kind(j, i)
        return tuple(_store_logits(_dot_nt(kblk, qh[h]) + msk + bias_ref[t, h], s_ref, slot, h)
                     for h in range(B_HEADS))

    def accumulate(j, slot, ms, bms):
        v_aug = vaug_ref[j]
        return tuple(_online_softmax_step(s_ref, slot, h, ms[h], bms[h], v_aug, acc_ref)
                     for h in range(B_HEADS))

    ms = _stream_key_blocks(i, B_HEADS, logits, accumulate)

    for h in range(B_HEADS):
        a = acc_ref[h]
        ot_ref[h * B_HEAD_DIM:(h + 1) * B_HEAD_DIM, :] = (
            a[0:B_HEAD_DIM] / a[B_HEAD_DIM:B_HEAD_DIM + 1])
    o_ref[0] = ot_ref[...].T.astype(BF16)


def _dsa_attn(proj, bvt, iwt, bias, *, top_k):
    bsz, s, _ = proj.shape
    nq = s // TQ
    nkb = s // TK
    width = B_HEADS * B_HEAD_DIM
    iq_w = IDX_HEADS * IDX_DIM
    return pl.pallas_call(
        functools.partial(_dsa_kernel, top_k=top_k, seq=s),
        grid=(bsz, nq),
        in_specs=[
            pl.BlockSpec((1, TQ, width), lambda b, i: (b, i, COL_BQ // width)),
            pl.BlockSpec((1, TQ, iq_w), lambda b, i: (b, i, COL_IQ // iq_w)),
            pl.BlockSpec((1, s, LANES), lambda b, i: (b, 0, COL_KB // LANES)),
            pl.BlockSpec((1, s, LANES), lambda b, i: (b, 0, COL_IK // LANES)),
            pl.BlockSpec((1, nkb, B_HEAD_DIM, TK), lambda b, i: (b, 0, 0, 0)),
            pl.BlockSpec((1, IDX_HEADS, TQ), lambda b, i: (b, 0, i)),
            pl.BlockSpec((3, B_HEADS, TK, TQ), lambda b, i: (0, 0, 0, 0)),
        ],
        out_specs=pl.BlockSpec((1, TQ, width), lambda b, i: (b, i, 0)),
        out_shape=jax.ShapeDtypeStruct((bsz, s, width), BF16),
        scratch_shapes=[
            pltpu.VMEM((nkb, TK, TQ), I32),
            pltpu.VMEM((nkb, TK, TQ), F32),
            pltpu.VMEM((nkb, B_HEAD_DIM + ONES_ROWS, TK), BF16),
            pltpu.VMEM((2, B_HEADS, TK, TQ), F32),
            pltpu.VMEM((B_HEADS, B_HEAD_DIM + ONES_ROWS, TQ), F32),
            pltpu.VMEM((width, TQ), F32),
            pltpu.VMEM((1, TQ), I32),
        ],
        compiler_params=pltpu.CompilerParams(
            dimension_semantics=("arbitrary", "arbitrary"), vmem_limit_bytes=VMEM_LIMIT),
        name="dsa_attn",
    )(proj, proj, proj, proj, bvt, iwt, bias)


def _attn_out_kernel(ya_ref, yb_ref, w_ref, x_ref, g_ref, o_ref):
    half = ya_ref.shape[-1]
    m = _dot(ya_ref[...], w_ref[0:half, :]) + _dot(yb_ref[...], w_ref[half:2 * half, :])
    o_ref[...] = x_ref[...] + _rms(m, g_ref[...])


def _attn_out(ya, yb, w, x, g, *, tm=512):
    n, d = x.shape
    half = ya.shape[-1]
    return pl.pallas_call(
        _attn_out_kernel,
        grid=(n // tm,),
        in_specs=[
            pl.BlockSpec((tm, half), lambda i: (i, 0)),
            pl.BlockSpec((tm, half), lambda i: (i, 0)),
            pl.BlockSpec((2 * half, d), lambda i: (0, 0)),
            pl.BlockSpec((tm, d), lambda i: (i, 0)),
            pl.BlockSpec((1, d), lambda i: (0, 0)),
        ],
        out_specs=pl.BlockSpec((tm, d), lambda i: (i, 0)),
        out_shape=jax.ShapeDtypeStruct((n, d), F32),
        compiler_params=pltpu.CompilerParams(
            dimension_semantics=("arbitrary",), vmem_limit_bytes=VMEM_LIMIT),
        name="attn_out",
    )(ya, yb, w, x, g)


def _mlp_kernel(x_ref, g_in_ref, wu_ref, wd_ref, g_out_ref, o_ref, h_ref, acc_ref):
    f = pl.program_id(1)

    @pl.when(f == 0)
    def _():
        h_ref[...] = _rms(x_ref[...], g_in_ref[...]).astype(BF16)
        acc_ref[...] = jnp.zeros_like(acc_ref)

    u = jnp.maximum(_dot(h_ref[...], wu_ref[...]), 0.0)
    acc_ref[...] += _dot((u * u).astype(BF16), wd_ref[...])

    @pl.when(f == pl.num_programs(1) - 1)
    def _():
        o_ref[...] = x_ref[...] + _rms(acc_ref[...], g_out_ref[...])


def _mlp(x, g_in, wu, wd, g_out, *, tm=1024, tf=1024):
    n, d = x.shape
    ff = wu.shape[1]
    return pl.pallas_call(
        _mlp_kernel,
        grid=(n // tm, ff // tf),
        in_specs=[
            pl.BlockSpec((tm, d), lambda i, f: (i, 0)),
            pl.BlockSpec((1, d), lambda i, f: (0, 0)),
            pl.BlockSpec((d, tf), lambda i, f: (0, f)),
            pl.BlockSpec((tf, d), lambda i, f: (f, 0)),
            pl.BlockSpec((1, d), lambda i, f: (0, 0)),
        ],
        out_specs=pl.BlockSpec((tm, d), lambda i, f: (i, 0)),
        out_shape=jax.ShapeDtypeStruct((n, d), F32),
        scratch_shapes=[pltpu.VMEM((tm, d), BF16), pltpu.VMEM((tm, d), F32)],
        compiler_params=pltpu.CompilerParams(
            dimension_semantics=("arbitrary", "arbitrary"), vmem_limit_bytes=VMEM_LIMIT),
        name="mlp",
    )(x, g_in, wu, wd, g_out)


U_HIST = 32
Z_HIST = 8


def _conv_kernel(x_ref, g_in_ref, w_in_ref, dw_w_ref, dw_b_ref, ln_g_ref, ln_b_ref, sc_w_ref,
                 w_out_ref, g_out_ref, o_ref, ubuf_ref, zbuf_ref, *, ts):
    @pl.when(pl.program_id(1) == 0)
    def _():
        ubuf_ref[0:U_HIST, :] = jnp.zeros((U_HIST, CONV_CH), F32)
        zbuf_ref[0:Z_HIST, :] = jnp.zeros((Z_HIST, SC_CH), F32)

    x = x_ref[0]
    hb = _rms(x, g_in_ref[...]).astype(BF16)
    c = CONV_CH
    ca = _dot(hb, w_in_ref[:, 0:c])
    cg = _dot(hb, w_in_ref[:, c:2 * c])
    ubuf_ref[U_HIST:U_HIST + ts, :] = ca * jax.nn.sigmoid(cg)
    dc = _dot(hb, w_in_ref[:, 3 * c:4 * c])
    dh = _dot(hb, w_in_ref[:, 4 * c:5 * c])
    zbuf_ref[Z_HIST:Z_HIST + ts, :] = dc * dh

    u = jnp.zeros((ts, c), F32)
    for j in range(CONV_WIDTH):
        off = U_HIST - (CONV_WIDTH - 1) + j
        u = u + dw_w_ref[j:j + 1, :] * ubuf_ref[off:off + ts, :]
    u = u + dw_b_ref[...]
    mu = jnp.mean(u, axis=-1, keepdims=True)
    ctr = u - mu
    var = jnp.mean(ctr * ctr, axis=-1, keepdims=True)
    u = ctr * lax.rsqrt(var + EPS) * ln_g_ref[...] + ln_b_ref[...]
    u = u * jax.nn.sigmoid(u)

    z = jnp.zeros((ts, c), F32)
    for j in range(SC_WIDTH):
        off = Z_HIST - (SC_WIDTH - 1) + j
        z = z + sc_w_ref[j:j + 1, :] * zbuf_ref[off:off + ts, :]
    z = _dot(hb, w_in_ref[:, 2 * c:3 * c]) * z

    ubuf_ref[0:U_HIST, :] = ubuf_ref[ts:ts + U_HIST, :]
    zbuf_ref[0:Z_HIST, :] = zbuf_ref[ts:ts + Z_HIST, :]

    m = _dot(u.astype(BF16), w_out_ref[0:c, :]) + _dot(z.astype(BF16), w_out_ref[c:2 * c, :])
    o_ref[0] = x + _rms(m, g_out_ref[...])


def _conv_mixer(x, g_in, w_in, dw_w, dw_b, ln_g, ln_b, sc_w, w_out, g_out, *, ts=512):
    bsz, s, d = x.shape
    full = lambda a: pl.BlockSpec(a.shape, lambda b, i: (0,) * a.ndim)
    return pl.pallas_call(
        functools.partial(_conv_kernel, ts=ts),
        grid=(bsz, s // ts),
        in_specs=[pl.BlockSpec((1, ts, d), lambda b, i: (b, i, 0)),
                  full(g_in), full(w_in), full(dw_w), full(dw_b), full(ln_g), full(ln_b),
                  full(sc_w), full(w_out), full(g_out)],
        out_specs=pl.BlockSpec((1, ts, d), lambda b, i: (b, i, 0)),
        out_shape=jax.ShapeDtypeStruct((bsz, s, d), F32),
        scratch_shapes=[pltpu.VMEM((U_HIST + ts, CONV_CH), F32),
                        pltpu.VMEM((Z_HIST + ts, SC_CH), F32)],
        compiler_params=pltpu.CompilerParams(
            dimension_semantics=("arbitrary", "arbitrary"), vmem_limit_bytes=VMEM_LIMIT),
        name="conv_mixer",
    )(x, g_in, w_in, dw_w, dw_b, ln_g, ln_b, sc_w, w_out, g_out)


def _t5_bucket(rel):
    nb = NUM_BUCKETS // 2
    ret = jnp.where(rel > 0, nb, 0)
    n = jnp.abs(rel)
    max_exact = nb // 2
    nf = jnp.maximum(n, 1).astype(jnp.float32)
    large = max_exact + (jnp.log(nf / max_exact) / math.log(MAX_DISTANCE / max_exact)
                         * (nb - max_exact)).astype(jnp.int32)
    large = jnp.minimum(large, nb - 1)
    return ret + jnp.where(n < max_exact, n, large)


def _toeplitz(f, off):
    heads, length = f.shape
    g = jnp.tile(jnp.pad(f, ((0, 0), (0, 1))), (1, TQ))[:, :TQ * length]
    return g.reshape(heads, TQ, length)[:, :, off:off + TK]


def _bias_tables(rel_bias):
    lo = -(TQ + TK - 1)
    rel = jnp.arange(lo, TK, dtype=jnp.int32)
    far_bucket = NUM_BUCKETS // 2 - 1
    f = (rel_bias[_t5_bucket(rel)].astype(F32) - rel_bias[far_bucket].astype(F32)[None, :]).T
    diag = jnp.swapaxes(_toeplitz(f, -lo), 1, 2)
    prev = jnp.swapaxes(_toeplitz(f, -lo - TK), 1, 2)
    ki = np.arange(TK)[:, None]
    qi = np.arange(TQ)[None, :]
    adm = jnp.asarray((ki // CHUNK) <= (qi // CHUNK))
    none = jnp.zeros_like(diag)
    a_diag = jnp.where(adm[None], diag[:A_HEADS], NEG)
    a_bias = jnp.stack([none[:A_HEADS], prev[:A_HEADS], a_diag], axis=0)
    b_bias = jnp.stack([none[A_HEADS:], prev[A_HEADS:], diag[A_HEADS:]], axis=0)
    return a_bias, b_bias


def _attn_weights(w_in):
    o = np.cumsum([0, 512, 512, 512, 512, 64, 64, 256, 32, 8])
    aq, ak, av, bq, bk, bv, iq, ik, iw = [w_in[:, o[n]:o[n + 1]] for n in range(9)]
    w = jnp.concatenate([aq, ak, bq, iq, bk, bk, ik, ik, ik, ik], axis=1).astype(BF16)
    pad = jnp.zeros((w_in.shape[0], VT_ROWS - ROW_IW - IDX_HEADS), w_in.dtype)
    wt = jnp.concatenate([av, bv, iw, pad], axis=1).T.astype(BF16)
    return w, wt


def kernel(x, rel_bias, norm_g, w_mlp_up, w_mlp_down, attn_w_in, attn_w_out, diff_lambda,
           diff_subln_g, conv_w_in, conv_w_out, conv_dw_w, conv_dw_b, conv_ln_g, conv_ln_b,
           sconv_w):
    bsz, s, d = x.shape
    depth = norm_g.shape[0]
    top_k = min(TOPK_MAX, s // 4)
    row = lambda v: v.reshape(1, -1)
    a_bias, b_bias = _bias_tables(rel_bias)
    for i in range(depth):
        j = i // 2
        if i % 2 == 0:
            lambda_init = 0.8 - 0.6 * math.exp(-0.3 * i)
            w, wt = _attn_weights(attn_w_in[j])
            proj, avt, bvt, iwt = _attn_in(x.reshape(bsz, s, d), row(norm_g[i, 0]), w, wt)
            ya = _diff_attn(proj, avt, diff_lambda[j], a_bias, diff_subln_g[j].reshape(-1, 1),
                            lambda_init=lambda_init)
            yb = _dsa_attn(proj, bvt, iwt, b_bias, top_k=top_k)
            x = _attn_out(ya.reshape(bsz * s, -1), yb.reshape(bsz * s, -1),
                          attn_w_out[j].astype(BF16), x.reshape(bsz * s, d),
                          row(norm_g[i, 1]))
        else:
            x = _conv_mixer(x.reshape(bsz, s, d), row(norm_g[i, 0]), conv_w_in[j].astype(BF16),
                            conv_dw_w[j], row(conv_dw_b[j]), row(conv_ln_g[j]),
                            row(conv_ln_b[j]), sconv_w[j], conv_w_out[j].astype(BF16),
                            row(norm_g[i, 1])).reshape(bsz * s, d)
        x = _mlp(x, row(norm_g[i, 2]), w_mlp_up[i].astype(BF16), w_mlp_down[i].astype(BF16),
                 row(norm_g[i, 3]))
    return x.reshape(bsz, s, d)
```

```python
import functools
import math

import numpy as np
import jax
import jax.numpy as jnp
from jax import lax
from jax.experimental import pallas as pl
from jax.experimental.pallas import tpu as pltpu

F32 = jnp.float32
BF16 = jnp.bfloat16
I32 = jnp.int32

D_MODEL = 1024
CHUNK = 64
NUM_BUCKETS = 32
MAX_DISTANCE = 128
EPS = 1e-6
NEG = -1e30
A_HEADS = 4
A_HEAD_DIM = 64
A_V_DIM = 2 * A_HEAD_DIM
B_HEADS = 8
B_HEAD_DIM = 64
IDX_HEADS = 8
IDX_DIM = 32
TOPK_MAX = 256
CONV_CH = 512
CONV_WIDTH = 31
SC_CH = 512
SC_WIDTH = 3
D_FF = 4 * D_MODEL

LANES = 128
SUBLANES = 8
VMEM_LIMIT = 56 * 1024 * 1024

TQ = 256
TK = 256
COL_AQ, COL_AK, COL_BQ, COL_IQ, COL_KB, COL_IK, PROJ_W = 0, 512, 1024, 1536, 1792, 1920, 2048
ROW_AV, ROW_BV, ROW_IW, VT_ROWS = 0, 512, 576, 640
ONES_ROWS = 16

INT_MIN = -2 ** 31
KEY_NEG_INF = INT_MIN + 0x7FFFFF


def _rms(x, g):
    return x * lax.rsqrt(jnp.mean(x * x, axis=-1, keepdims=True) + EPS) * g


def _dot_nt(a, b):
    return lax.dot_general(a, b, (((1,), (1,)), ((), ())), preferred_element_type=F32)


def _dot(a, b):
    return jnp.dot(a, b, preferred_element_type=F32)


def _lane_band(x, lo, width):
    lane = lax.broadcasted_iota(I32, x.shape, 1)
    return jnp.where((lane >= lo) & (lane < lo + width), x, jnp.zeros_like(x))


KEY_BITS = 32


def _bit_planes(keys):
    assert keys.shape[0] == KEY_BITS * SUBLANES
    u = keys ^ INT_MIN
    a = [u[SUBLANES * r:SUBLANES * (r + 1), :] for r in range(KEY_BITS)]
    j, m = KEY_BITS // 2, (1 << (KEY_BITS // 2)) - 1
    while j:
        mask = np.int32(np.uint32(m))
        k = 0
        while k < KEY_BITS:
            t = (a[k] ^ lax.shift_right_logical(a[k + j], jnp.int32(j))) & mask
            a[k] = a[k] ^ t
            a[k + j] = a[k + j] ^ lax.shift_left(t, jnp.int32(j))
            k = (k + j + 1) & ~j
        j >>= 1
        m = (m ^ (m << j)) & 0xFFFFFFFF
    return a


def _bias_kind(j, i):
    return jnp.clip(j - i + 2, 0, 2)


def _store_logits(s, s_ref, slot, c):
    s_ref[slot, c] = s
    return jnp.max(s, axis=0, keepdims=True)


def _online_softmax_step(s_ref, slot, c, m_old, block_max, v_aug, acc_ref):
    m_new = jnp.maximum(m_old, block_max)
    alpha = jnp.exp(m_old - m_new)
    p = jnp.exp(s_ref[slot, c] - m_new).astype(BF16)
    acc_ref[c] = acc_ref[c] * alpha + _dot(v_aug, p)
    return m_new


def _stream_key_blocks(i, n_chains, logits, accumulate):
    ms = (jnp.full((1, TQ), NEG, F32),) * n_chains

    def body(j, carry):
        ms, bms = carry
        slot = j & 1
        nxt = logits(j + 1, 1 - slot)
        return accumulate(j, slot, ms, bms), nxt

    ms, bms = lax.fori_loop(0, i, body, (ms, logits(0, 0)))
    return accumulate(i, i & 1, ms, bms)


def _attn_in_kernel(x_ref, g_ref, w_ref, wt_ref, proj_ref, avt_ref, bvt_ref, iwt_ref, *, tm):
    hb = _rms(x_ref[0], g_ref[...]).astype(BF16)
    proj_ref[0] = _dot(hb, w_ref[...]).astype(BF16)
    t = _dot_nt(wt_ref[...], hb)
    for c in range(tm // TK):
        avt_ref[0, c] = t[ROW_AV:ROW_BV, c * TK:(c + 1) * TK].astype(BF16)
        bvt_ref[0, c] = t[ROW_BV:ROW_IW, c * TK:(c + 1) * TK].astype(BF16)
    iwt_ref[0] = t[ROW_IW:ROW_IW + IDX_HEADS, :] * (IDX_HEADS ** -0.5 * IDX_DIM ** -0.5)


def _attn_in(x, g, w, wt, *, tm=512):
    bsz, s, d = x.shape
    av_rows = ROW_BV - ROW_AV
    return pl.pallas_call(
        functools.partial(_attn_in_kernel, tm=tm),
        grid=(bsz, s // tm),
        in_specs=[
            pl.BlockSpec((1, tm, d), lambda b, i: (b, i, 0)),
            pl.BlockSpec((1, d), lambda b, i: (0, 0)),
            pl.BlockSpec((d, PROJ_W), lambda b, i: (0, 0)),
            pl.BlockSpec((VT_ROWS, d), lambda b, i: (0, 0)),
        ],
        out_specs=[
            pl.BlockSpec((1, tm, PROJ_W), lambda b, i: (b, i, 0)),
            pl.BlockSpec((1, tm // TK, av_rows, TK), lambda b, i: (b, i, 0, 0)),
            pl.BlockSpec((1, tm // TK, B_HEAD_DIM, TK), lambda b, i: (b, i, 0, 0)),
            pl.BlockSpec((1, IDX_HEADS, tm), lambda b, i: (b, 0, i)),
        ],
        out_shape=[
            jax.ShapeDtypeStruct((bsz, s, PROJ_W), BF16),
            jax.ShapeDtypeStruct((bsz, s // TK, av_rows, TK), BF16),
            jax.ShapeDtypeStruct((bsz, s // TK, B_HEAD_DIM, TK), BF16),
            jax.ShapeDtypeStruct((bsz, IDX_HEADS, s), F32),
        ],
        compiler_params=pltpu.CompilerParams(
            dimension_semantics=("arbitrary", "arbitrary"), vmem_limit_bytes=VMEM_LIMIT),
        name="attn_in",
    )(x, g, w, wt)


def _diff_attn_kernel(lam_ref, q_ref, k_ref, vt_ref, bias_ref, g_ref, o_ref,
                      vaug_ref, s_ref, acc_ref, ot_ref, *, lambda_init, seq):
    i = pl.program_id(1)
    nkb = seq // TK
    n_maps = 2 * A_HEADS

    @pl.when(i == 0)
    def _():
        for c in range(nkb):
            for h in range(A_HEADS):
                vaug_ref[h, c, 0:A_V_DIM, :] = vt_ref[0, c, h * A_V_DIM:(h + 1) * A_V_DIM, :]
                vaug_ref[h, c, A_V_DIM:A_V_DIM + ONES_ROWS, :] = jnp.ones((ONES_ROWS, TK), BF16)

    lv = lam_ref[...]
    lam = (jnp.exp(jnp.sum(lv[0:1] * lv[1:2], axis=-1, keepdims=True))
           - jnp.exp(jnp.sum(lv[2:3] * lv[3:4], axis=-1, keepdims=True)) + lambda_init)

    q = q_ref[0] * (A_HEAD_DIM ** -0.5)
    qm = []
    for h in range(A_HEADS):
        qh = q[:, h * LANES:(h + 1) * LANES]
        qm += [_lane_band(qh, 0, A_HEAD_DIM), _lane_band(qh, A_HEAD_DIM, A_HEAD_DIM)]

    acc_ref[...] = jnp.zeros_like(acc_ref)

    def logits(j, slot):
        krows = k_ref[0, pl.ds(pl.multiple_of(j * TK, TK), TK), :]
        t = _bias_kind(j, i)
        out = []
        for h in range(A_HEADS):
            kb = krows[:, h * LANES:(h + 1) * LANES]
            bias = bias_ref[t, h]
            for c in (2 * h, 2 * h + 1):
                out.append(_store_logits(_dot_nt(kb, qm[c]) + bias, s_ref, slot, c))
        return tuple(out)

    def accumulate(j, slot, ms, bms):
        return tuple(_online_softmax_step(s_ref, slot, c, ms[c], bms[c], vaug_ref[c // 2, j],
                                          acc_ref) for c in range(n_maps))

    _stream_key_blocks(i, n_maps, logits, accumulate)

    for h in range(A_HEADS):
        a1 = acc_ref[2 * h]
        a2 = acc_ref[2 * h + 1]
        a = (a1[0:A_V_DIM] / a1[A_V_DIM:A_V_DIM + 1]
             - lam * (a2[0:A_V_DIM] / a2[A_V_DIM:A_V_DIM + 1]))
        y = a * lax.rsqrt(jnp.mean(a * a, axis=0, keepdims=True) + EPS) * g_ref[...]
        ot_ref[h * A_V_DIM:(h + 1) * A_V_DIM, :] = y * (1.0 - lambda_init)
    o_ref[0] = ot_ref[...].T.astype(BF16)


def _diff_attn(proj, avt, lam_vecs, bias, subln_g_col, *, lambda_init):
    bsz, s, _ = proj.shape
    nq = s // TQ
    nkb = s // TK
    width = A_HEADS * A_V_DIM
    return pl.pallas_call(
        functools.partial(_diff_attn_kernel, lambda_init=lambda_init, seq=s),
        grid=(bsz, nq),
        in_specs=[
            pl.BlockSpec((4, A_HEAD_DIM), lambda b, i: (0, 0)),
            pl.BlockSpec((1, TQ, width), lambda b, i: (b, i, COL_AQ // width)),
            pl.BlockSpec((1, s, width), lambda b, i: (b, 0, COL_AK // width)),
            pl.BlockSpec((1, nkb, width, TK), lambda b, i: (b, 0, 0, 0)),
            pl.BlockSpec((3, A_HEADS, TK, TQ), lambda b, i: (0, 0, 0, 0)),
            pl.BlockSpec((A_V_DIM, 1), lambda b, i: (0, 0)),
        ],
        out_specs=pl.BlockSpec((1, TQ, width), lambda b, i: (b, i, 0)),
        out_shape=jax.ShapeDtypeStruct((bsz, s, width), BF16),
        scratch_shapes=[
            pltpu.VMEM((A_HEADS, nkb, A_V_DIM + ONES_ROWS, TK), BF16),
            pltpu.VMEM((2, 2 * A_HEADS, TK, TQ), F32),
            pltpu.VMEM((2 * A_HEADS, A_V_DIM + ONES_ROWS, TQ), F32),
            pltpu.VMEM((width, TQ), F32),
        ],
        compiler_params=pltpu.CompilerParams(
            dimension_semantics=("arbitrary", "arbitrary"), vmem_limit_bytes=VMEM_LIMIT),
        name="diff_attn",
    )(lam_vecs, proj, proj, avt, bias, subln_g_col)


def _dsa_kernel(qb_ref, iq_ref, kb_ref, ik_ref, vt_ref, iwt_ref, bias_ref, o_ref,
                key_ref, plane_ref, mask_ref, vaug_ref, s_ref, acc_ref, ot_ref, j_ref,
                *, top_k, seq):
    i = pl.program_id(1)
    nkb = seq // TK

    @pl.when(i == 0)
    def _():
        for c in range(nkb):
            vaug_ref[c, 0:B_HEAD_DIM, :] = vt_ref[0, c]
            vaug_ref[c, B_HEAD_DIM:B_HEAD_DIM + ONES_ROWS, :] = jnp.ones((ONES_ROWS, TK), BF16)

    @pl.when((i == 0) & (pl.program_id(0) == 0))
    def _():
        plane_ref[...] = jnp.zeros_like(plane_ref)

    def rows(ref, j):
        return ref[0, pl.ds(pl.multiple_of(j * TK, TK), TK), :]

    krow = lax.broadcasted_iota(I32, (TK, TQ), 0)
    qcol = lax.broadcasted_iota(I32, (TK, TQ), 1)

    iq = iq_ref[0]
    per_blk = LANES // IDX_DIM
    iqh = [_lane_band(iq[:, (h // per_blk) * LANES:(h // per_blk + 1) * LANES],
                      (h % per_blk) * IDX_DIM, IDX_DIM) for h in range(IDX_HEADS)]
    w = iwt_ref[0]

    def score_keys(j):
        ikb = rows(ik_ref, j)
        acc = jnp.zeros((TK, TQ), F32)
        for h in range(IDX_HEADS):
            acc = acc + jnp.maximum(_dot_nt(ikb, iqh[h]), 0.0) * w[h:h + 1, :]
        acc = jnp.where(acc == 0.0, 0.0, acc)
        bits = lax.bitcast_convert_type(acc, I32)
        return bits ^ ((bits >> 31) & 0x7FFFFFFF)

    def store_keys(j, keys):
        key_ref[j] = keys
        for p, plane in enumerate(_bit_planes(keys)):
            plane_ref[p, j] = plane

    def score_body(j, c):
        store_keys(j, score_keys(j))
        return c

    lax.fori_loop(0, i, score_body, 0)
    admissible = (krow // CHUNK) <= (qcol // CHUNK)
    store_keys(i, jnp.where(admissible, score_keys(i), KEY_NEG_INF))

    blk = lax.broadcasted_iota(I32, (nkb, SUBLANES, TQ), 0)
    alive0 = jnp.where(blk <= i, -1, 0)

    def ones_per_query(words):
        pc = lax.population_count(words).astype(F32)
        return jnp.sum(jnp.sum(pc, axis=0), axis=0, keepdims=True)

    def bit_body(p, carry):
        alive, n_gt, thr_u = carry
        hit = alive & plane_ref[p]
        n_hit = ones_per_query(hit)
        take = (n_gt + n_hit) >= top_k
        take_m = jnp.where(take, -1, 0)
        thr_u = thr_u | (take_m & lax.shift_left(jnp.int32(1), KEY_BITS - 1 - p))
        n_gt = n_gt + jnp.where(take, 0.0, n_hit)
        alive = (alive ^ hit) ^ (alive & take_m)
        return alive, n_gt, thr_u

    alive, n_gt, thr_u = lax.fori_loop(
        0, KEY_BITS, bit_body, (alive0, jnp.zeros((1, TQ), F32), jnp.zeros((1, TQ), I32)))
    thr = thr_u ^ INT_MIN

    def count(indicator):
        def body(j, acc):
            return acc + jnp.sum(indicator(key_ref[j], j), axis=0, keepdims=True)
        return lax.fori_loop(0, i + 1, body, jnp.zeros((1, TQ), F32))

    n_ge = n_gt + ones_per_query(alive)
    need = top_k - n_gt
    j_ref[...] = jnp.full((1, TQ), seq, I32)
    tied = jnp.max(jnp.where(n_ge > top_k, jnp.where(thr > KEY_NEG_INF, 1.0, 0.0), 0.0))

    @pl.when(tied > 0)
    def _():
        def idx_body(b, lim):
            cand = lim + lax.shift_left(jnp.int32(1), 12 - b)
            c = count(lambda kb, j: jnp.where(
                kb == thr, jnp.where((krow + j * TK) < cand, 1.0, 0.0), 0.0))
            return jnp.where(c <= need, cand, lim)
        j_ref[...] = lax.fori_loop(0, 13, idx_body, jnp.zeros((1, TQ), I32))

    lim = j_ref[...]

    def mask_body(j, c):
        kb = key_ref[j]
        tie_ok = jnp.where((krow + j * TK) < lim, 0.0, NEG)
        sel = jnp.where(kb > thr, 0.0, jnp.where(kb == thr, tie_ok, NEG))
        mask_ref[j] = jnp.where(kb > KEY_NEG_INF, sel, NEG)
        return c

    lax.fori_loop(0, i + 1, mask_body, 0)

    qb = qb_ref[0] * (B_HEAD_DIM ** -0.5)
    qh = [_lane_band(qb[:, (h // 2) * LANES:(h // 2 + 1) * LANES],
                     (h % 2) * B_HEAD_DIM, B_HEAD_DIM) for h in range(B_HEADS)]
    acc_ref[...] = jnp.zeros_like(acc_ref)

    def logits(j, slot):
        kblk = rows(kb_ref, j)
        msk = mask_ref[j]
        t = _bias_kind(j, i)
        return tuple(_store_logits(_dot_nt(kblk, qh[h]) + msk + bias_ref[t, h], s_ref, slot, h)
                     for h in range(B_HEADS))

    def accumulate(j, slot, ms, bms):
        v_aug = vaug_ref[j]
        return tuple(_online_softmax_step(s_ref, slot, h, ms[h], bms[h], v_aug, acc_ref)
                     for h in range(B_HEADS))

    ms = _stream_key_blocks(i, B_HEADS, logits, accumulate)

    for h in range(B_HEADS):
        a = acc_ref[h]
        ot_ref[h * B_HEAD_DIM:(h + 1) * B_HEAD_DIM, :] = (
            a[0:B_HEAD_DIM] / a[B_HEAD_DIM:B_HEAD_DIM + 1])
    o_ref[0] = ot_ref[...].T.astype(BF16)


def _dsa_attn(proj, bvt, iwt, bias, *, top_k):
    bsz, s, _ = proj.shape
    nq = s // TQ
    nkb = s // TK
    width = B_HEADS * B_HEAD_DIM
    iq_w = IDX_HEADS * IDX_DIM
    return pl.pallas_call(
        functools.partial(_dsa_kernel, top_k=top_k, seq=s),
        grid=(bsz, nq),
        in_specs=[
            pl.BlockSpec((1, TQ, width), lambda b, i: (b, i, COL_BQ // width)),
            pl.BlockSpec((1, TQ, iq_w), lambda b, i: (b, i, COL_IQ // iq_w)),
            pl.BlockSpec((1, s, LANES), lambda b, i: (b, 0, COL_KB // LANES)),
            pl.BlockSpec((1, s, LANES), lambda b, i: (b, 0, COL_IK // LANES)),
            pl.BlockSpec((1, nkb, B_HEAD_DIM, TK), lambda b, i: (b, 0, 0, 0)),
            pl.BlockSpec((1, IDX_HEADS, TQ), lambda b, i: (b, 0, i)),
            pl.BlockSpec((3, B_HEADS, TK, TQ), lambda b, i: (0, 0, 0, 0)),
        ],
        out_specs=pl.BlockSpec((1, TQ, width), lambda b, i: (b, i, 0)),
        out_shape=jax.ShapeDtypeStruct((bsz, s, width), BF16),
        scratch_shapes=[
            pltpu.VMEM((nkb, TK, TQ), I32),
            pltpu.VMEM((KEY_BITS, nkb, SUBLANES, TQ), I32),
            pltpu.VMEM((nkb, TK, TQ), F32),
            pltpu.VMEM((nkb, B_HEAD_DIM + ONES_ROWS, TK), BF16),
            pltpu.VMEM((2, B_HEADS, TK, TQ), F32),
            pltpu.VMEM((B_HEADS, B_HEAD_DIM + ONES_ROWS, TQ), F32),
            pltpu.VMEM((width, TQ), F32),
            pltpu.VMEM((1, TQ), I32),
        ],
        compiler_params=pltpu.CompilerParams(
            dimension_semantics=("arbitrary", "arbitrary"), vmem_limit_bytes=VMEM_LIMIT),
        name="dsa_attn",
    )(proj, proj, proj, proj, bvt, iwt, bias)


def _attn_out_kernel(ya_ref, yb_ref, w_ref, x_ref, g_ref, o_ref):
    half = ya_ref.shape[-1]
    m = _dot(ya_ref[...], w_ref[0:half, :]) + _dot(yb_ref[...], w_ref[half:2 * half, :])
    o_ref[...] = x_ref[...] + _rms(m, g_ref[...])


def _attn_out(ya, yb, w, x, g, *, tm=512):
    n, d = x.shape
    half = ya.shape[-1]
    return pl.pallas_call(
        _attn_out_kernel,
        grid=(n // tm,),
        in_specs=[
            pl.BlockSpec((tm, half), lambda i: (i, 0)),
            pl.BlockSpec((tm, half), lambda i: (i, 0)),
            pl.BlockSpec((2 * half, d), lambda i: (0, 0)),
            pl.BlockSpec((tm, d), lambda i: (i, 0)),
            pl.BlockSpec((1, d), lambda i: (0, 0)),
        ],
        out_specs=pl.BlockSpec((tm, d), lambda i: (i, 0)),
        out_shape=jax.ShapeDtypeStruct((n, d), F32),
        compiler_params=pltpu.CompilerParams(
            dimension_semantics=("arbitrary",), vmem_limit_bytes=VMEM_LIMIT),
        name="attn_out",
    )(ya, yb, w, x, g)


def _mlp_kernel(x_ref, g_in_ref, wu_ref, wd_ref, g_out_ref, o_ref, h_ref, acc_ref):
    f = pl.program_id(1)

    @pl.when(f == 0)
    def _():
        h_ref[...] = _rms(x_ref[...], g_in_ref[...]).astype(BF16)
        acc_ref[...] = jnp.zeros_like(acc_ref)

    u = jnp.maximum(_dot(h_ref[...], wu_ref[...]), 0.0)
    acc_ref[...] += _dot((u * u).astype(BF16), wd_ref[...])

    @pl.when(f == pl.num_programs(1) - 1)
    def _():
        o_ref[...] = x_ref[...] + _rms(acc_ref[...], g_out_ref[...])


def _mlp(x, g_in, wu, wd, g_out, *, tm=1024, tf=1024):
    n, d = x.shape
    ff = wu.shape[1]
    return pl.pallas_call(
        _mlp_kernel,
        grid=(n // tm, ff // tf),
        in_specs=[
            pl.BlockSpec((tm, d), lambda i, f: (i, 0)),
            pl.BlockSpec((1, d), lambda i, f: (0, 0)),
            pl.BlockSpec((d, tf), lambda i, f: (0, f)),
            pl.BlockSpec((tf, d), lambda i, f: (f, 0)),
            pl.BlockSpec((1, d), lambda i, f: (0, 0)),
        ],
        out_specs=pl.BlockSpec((tm, d), lambda i, f: (i, 0)),
        out_shape=jax.ShapeDtypeStruct((n, d), F32),
        scratch_shapes=[pltpu.VMEM((tm, d), BF16), pltpu.VMEM((tm, d), F32)],
        compiler_params=pltpu.CompilerParams(
            dimension_semantics=("arbitrary", "arbitrary"), vmem_limit_bytes=VMEM_LIMIT),
        name="mlp",
    )(x, g_in, wu, wd, g_out)


U_HIST = 32
Z_HIST = 8


def _conv_kernel(x_ref, g_in_ref, w_in_ref, dw_w_ref, dw_b_ref, ln_g_ref, ln_b_ref, sc_w_ref,
                 w_out_ref, g_out_ref, o_ref, ubuf_ref, zbuf_ref, *, ts):
    @pl.when(pl.program_id(1) == 0)
    def _():
        ubuf_ref[0:U_HIST, :] = jnp.zeros((U_HIST, CONV_CH), F32)
        zbuf_ref[0:Z_HIST, :] = jnp.zeros((Z_HIST, SC_CH), F32)

    x = x_ref[0]
    hb = _rms(x, g_in_ref[...]).astype(BF16)
    c = CONV_CH
    ca = _dot(hb, w_in_ref[:, 0:c])
    cg = _dot(hb, w_in_ref[:, c:2 * c])
    ubuf_ref[U_HIST:U_HIST + ts, :] = ca * jax.nn.sigmoid(cg)
    dc = _dot(hb, w_in_ref[:, 3 * c:4 * c])
    dh = _dot(hb, w_in_ref[:, 4 * c:5 * c])
    zbuf_ref[Z_HIST:Z_HIST + ts, :] = dc * dh

    u = jnp.zeros((ts, c), F32)
    for j in range(CONV_WIDTH):
        off = U_HIST - (CONV_WIDTH - 1) + j
        u = u + dw_w_ref[j:j + 1, :] * ubuf_ref[off:off + ts, :]
    u = u + dw_b_ref[...]
    mu = jnp.mean(u, axis=-1, keepdims=True)
    ctr = u - mu
    var = jnp.mean(ctr * ctr, axis=-1, keepdims=True)
    u = ctr * lax.rsqrt(var + EPS) * ln_g_ref[...] + ln_b_ref[...]
    u = u * jax.nn.sigmoid(u)

    z = jnp.zeros((ts, c), F32)
    for j in range(SC_WIDTH):
        off = Z_HIST - (SC_WIDTH - 1) + j
        z = z + sc_w_ref[j:j + 1, :] * zbuf_ref[off:off + ts, :]
    z = _dot(hb, w_in_ref[:, 2 * c:3 * c]) * z

    ubuf_ref[0:U_HIST, :] = ubuf_ref[ts:ts + U_HIST, :]
    zbuf_ref[0:Z_HIST, :] = zbuf_ref[ts:ts + Z_HIST, :]

    m = _dot(u.astype(BF16), w_out_ref[0:c, :]) + _dot(z.astype(BF16), w_out_ref[c:2 * c, :])
    o_ref[0] = x + _rms(m, g_out_ref[...])


def _conv_mixer(x, g_in, w_in, dw_w, dw_b, ln_g, ln_b, sc_w, w_out, g_out, *, ts=512):
    bsz, s, d = x.shape
    full = lambda a: pl.BlockSpec(a.shape, lambda b, i: (0,) * a.ndim)
    return pl.pallas_call(
        functools.partial(_conv_kernel, ts=ts),
        grid=(bsz, s // ts),
        in_specs=[pl.BlockSpec((1, ts, d), lambda b, i: (b, i, 0)),
                  full(g_in), full(w_in), full(dw_w), full(dw_b), full(ln_g), full(ln_b),
                  full(sc_w), full(w_out), full(g_out)],
        out_specs=pl.BlockSpec((1, ts, d), lambda b, i: (b, i, 0)),
        out_shape=jax.ShapeDtypeStruct((bsz, s, d), F32),
        scratch_shapes=[pltpu.VMEM((U_HIST + ts, CONV_CH), F32),
                        pltpu.VMEM((Z_HIST + ts, SC_CH), F32)],
        compiler_params=pltpu.CompilerParams(
            dimension_semantics=("arbitrary", "arbitrary"), vmem_limit_bytes=VMEM_LIMIT),
        name="conv_mixer",
    )(x, g_in, w_in, dw_w, dw_b, ln_g, ln_b, sc_w, w_out, g_out)


def _t5_bucket(rel):
    nb = NUM_BUCKETS // 2
    ret = jnp.where(rel > 0, nb, 0)
    n = jnp.abs(rel)
    max_exact = nb // 2
    nf = jnp.maximum(n, 1).astype(jnp.float32)
    large = max_exact + (jnp.log(nf / max_exact) / math.log(MAX_DISTANCE / max_exact)
                         * (nb - max_exact)).astype(jnp.int32)
    large = jnp.minimum(large, nb - 1)
    return ret + jnp.where(n < max_exact, n, large)


def _toeplitz(f, off):
    heads, length = f.shape
    g = jnp.tile(jnp.pad(f, ((0, 0), (0, 1))), (1, TQ))[:, :TQ * length]
    return g.reshape(heads, TQ, length)[:, :, off:off + TK]


def _bias_tables(rel_bias):
    lo = -(TQ + TK - 1)
    rel = jnp.arange(lo, TK, dtype=jnp.int32)
    far_bucket = NUM_BUCKETS // 2 - 1
    f = (rel_bias[_t5_bucket(rel)].astype(F32) - rel_bias[far_bucket].astype(F32)[None, :]).T
    diag = jnp.swapaxes(_toeplitz(f, -lo), 1, 2)
    prev = jnp.swapaxes(_toeplitz(f, -lo - TK), 1, 2)
    ki = np.arange(TK)[:, None]
    qi = np.arange(TQ)[None, :]
    adm = jnp.asarray((ki // CHUNK) <= (qi // CHUNK))
    none = jnp.zeros_like(diag)
    a_diag = jnp.where(adm[None], diag[:A_HEADS], NEG)
    a_bias = jnp.stack([none[:A_HEADS], prev[:A_HEADS], a_diag], axis=0)
    b_bias = jnp.stack([none[A_HEADS:], prev[A_HEADS:], diag[A_HEADS:]], axis=0)
    return a_bias, b_bias


def _attn_weights(w_in):
    o = np.cumsum([0, 512, 512, 512, 512, 64, 64, 256, 32, 8])
    aq, ak, av, bq, bk, bv, iq, ik, iw = [w_in[:, o[n]:o[n + 1]] for n in range(9)]
    w = jnp.concatenate([aq, ak, bq, iq, bk, bk, ik, ik, ik, ik], axis=1).astype(BF16)
    pad = jnp.zeros((w_in.shape[0], VT_ROWS - ROW_IW - IDX_HEADS), w_in.dtype)
    wt = jnp.concatenate([av, bv, iw, pad], axis=1).T.astype(BF16)
    return w, wt


def kernel(x, rel_bias, norm_g, w_mlp_up, w_mlp_down, attn_w_in, attn_w_out, diff_lambda,
           diff_subln_g, conv_w_in, conv_w_out, conv_dw_w, conv_dw_b, conv_ln_g, conv_ln_b,
           sconv_w):
    bsz, s, d = x.shape
    depth = norm_g.shape[0]
    top_k = min(TOPK_MAX, s // 4)
    row = lambda v: v.reshape(1, -1)
    a_bias, b_bias = _bias_tables(rel_bias)
    for i in range(depth):
        j = i // 2
        if i % 2 == 0:
            lambda_init = 0.8 - 0.6 * math.exp(-0.3 * i)
            w, wt = _attn_weights(attn_w_in[j])
            proj, avt, bvt, iwt = _attn_in(x.reshape(bsz, s, d), row(norm_g[i, 0]), w, wt)
            ya = _diff_attn(proj, avt, diff_lambda[j], a_bias, diff_subln_g[j].reshape(-1, 1),
                            lambda_init=lambda_init)
            yb = _dsa_attn(proj, bvt, iwt, b_bias, top_k=top_k)
            x = _attn_out(ya.reshape(bsz * s, -1), yb.reshape(bsz * s, -1),
                          attn_w_out[j].astype(BF16), x.reshape(bsz * s, d),
                          row(norm_g[i, 1]))
        else:
            x = _conv_mixer(x.reshape(bsz, s, d), row(norm_g[i, 0]), conv_w_in[j].astype(BF16),
                            conv_dw_w[j], row(conv_dw_b[j]), row(conv_ln_g[j]),
                            row(conv_ln_b[j]), sconv_w[j], conv_w_out[j].astype(BF16),
                            row(norm_g[i, 1])).reshape(bsz * s, d)
        x = _mlp(x, row(norm_g[i, 2]), w_mlp_up[i].astype(BF16), w_mlp_down[i].astype(BF16),
                 row(norm_g[i, 3]))
    return x.reshape(bsz, s, d)
```

```python
import functools
import math

import numpy as np
import jax
import jax.numpy as jnp
from jax import lax
from jax.experimental import pallas as pl
from jax.experimental.pallas import tpu as pltpu

F32 = jnp.float32
BF16 = jnp.bfloat16
I32 = jnp.int32

D_MODEL = 1024
CHUNK = 64
NUM_BUCKETS = 32
MAX_DISTANCE = 128
EPS = 1e-6
NEG = -1e30
A_HEADS = 4
A_HEAD_DIM = 64
A_V_DIM = 2 * A_HEAD_DIM
B_HEADS = 8
B_HEAD_DIM = 64
IDX_HEADS = 8
IDX_DIM = 32
TOPK_MAX = 256
CONV_CH = 512
CONV_WIDTH = 31
SC_CH = 512
SC_WIDTH = 3
D_FF = 4 * D_MODEL

LANES = 128
SUBLANES = 8
VMEM_LIMIT = 56 * 1024 * 1024

TQ = 256
TK = 256
COL_AQ, COL_AK, COL_BQ, COL_IQ, COL_KB, COL_IK, PROJ_W = 0, 512, 1024, 1536, 1792, 1920, 2048
ROW_AV, ROW_BV, ROW_IW, VT_ROWS = 0, 512, 576, 640
ONES_ROWS = 16

INT_MIN = -2 ** 31
KEY_NEG_INF = INT_MIN + 0x7FFFFF


def _rms(x, g):
    return x * lax.rsqrt(jnp.mean(x * x, axis=-1, keepdims=True) + EPS) * g


def _dot_nt(a, b):
    return lax.dot_general(a, b, (((1,), (1,)), ((), ())), preferred_element_type=F32)


def _dot(a, b):
    return jnp.dot(a, b, preferred_element_type=F32)


def _lane_band(x, lo, width):
    lane = lax.broadcasted_iota(I32, x.shape, 1)
    return jnp.where((lane >= lo) & (lane < lo + width), x, jnp.zeros_like(x))


KEY_BITS = 32


def _bit_planes(keys):
    assert keys.shape[0] == KEY_BITS * SUBLANES
    u = keys ^ INT_MIN
    a = [u[SUBLANES * r:SUBLANES * (r + 1), :] for r in range(KEY_BITS)]
    j, m = KEY_BITS // 2, (1 << (KEY_BITS // 2)) - 1
    while j:
        mask = np.int32(np.uint32(m))
        k = 0
        while k < KEY_BITS:
            t = (a[k] ^ lax.shift_right_logical(a[k + j], jnp.int32(j))) & mask
            a[k] = a[k] ^ t
            a[k + j] = a[k + j] ^ lax.shift_left(t, jnp.int32(j))
            k = (k + j + 1) & ~j
        j >>= 1
        m = (m ^ (m << j)) & 0xFFFFFFFF
    return a


def _bias_kind(j, i):
    return jnp.clip(j - i + 2, 0, 2)


def _store_logits(s, s_ref, slot, c):
    s_ref[slot, c] = s
    return jnp.max(s, axis=0, keepdims=True)


def _online_softmax_step(s_ref, slot, c, m_old, block_max, v_aug, acc_ref):
    m_new = jnp.maximum(m_old, block_max)
    alpha = jnp.exp(m_old - m_new)
    p = jnp.exp(s_ref[slot, c] - m_new).astype(BF16)
    acc_ref[c] = acc_ref[c] * alpha + _dot(v_aug, p)
    return m_new


def _stream_key_blocks(i, n_chains, logits, accumulate):
    ms = (jnp.full((1, TQ), NEG, F32),) * n_chains

    def pair(t, carry):
        ms, bm0 = carry
        j = 2 * t
        bm1 = logits(j + 1, 1)
        ms = accumulate(j, 0, ms, bm0)
        bm0 = logits(jnp.minimum(j + 2, i), 0)
        return accumulate(j + 1, 1, ms, bm1), bm0

    ms, bm0 = lax.fori_loop(0, (i + 1) // 2, pair, (ms, logits(0, 0)))

    @pl.when(i % 2 == 0)
    def _():
        accumulate(i, 0, ms, bm0)


def _attn_in_kernel(x_ref, g_ref, w_ref, wt_ref, proj_ref, avt_ref, bvt_ref, iwt_ref, *, tm):
    hb = _rms(x_ref[0], g_ref[...]).astype(BF16)
    proj_ref[0] = _dot(hb, w_ref[...]).astype(BF16)
    t = _dot_nt(wt_ref[...], hb)
    for c in range(tm // TK):
        avt_ref[0, c] = t[ROW_AV:ROW_BV, c * TK:(c + 1) * TK].astype(BF16)
        bvt_ref[0, c] = t[ROW_BV:ROW_IW, c * TK:(c + 1) * TK].astype(BF16)
    iwt_ref[0] = t[ROW_IW:ROW_IW + IDX_HEADS, :] * (IDX_HEADS ** -0.5 * IDX_DIM ** -0.5)


def _attn_in(x, g, w, wt, *, tm=512):
    bsz, s, d = x.shape
    av_rows = ROW_BV - ROW_AV
    return pl.pallas_call(
        functools.partial(_attn_in_kernel, tm=tm),
        grid=(bsz, s // tm),
        in_specs=[
            pl.BlockSpec((1, tm, d), lambda b, i: (b, i, 0)),
            pl.BlockSpec((1, d), lambda b, i: (0, 0)),
            pl.BlockSpec((d, PROJ_W), lambda b, i: (0, 0)),
            pl.BlockSpec((VT_ROWS, d), lambda b, i: (0, 0)),
        ],
        out_specs=[
            pl.BlockSpec((1, tm, PROJ_W), lambda b, i: (b, i, 0)),
            pl.BlockSpec((1, tm // TK, av_rows, TK), lambda b, i: (b, i, 0, 0)),
            pl.BlockSpec((1, tm // TK, B_HEAD_DIM, TK), lambda b, i: (b, i, 0, 0)),
            pl.BlockSpec((1, IDX_HEADS, tm), lambda b, i: (b, 0, i)),
        ],
        out_shape=[
            jax.ShapeDtypeStruct((bsz, s, PROJ_W), BF16),
            jax.ShapeDtypeStruct((bsz, s // TK, av_rows, TK), BF16),
            jax.ShapeDtypeStruct((bsz, s // TK, B_HEAD_DIM, TK), BF16),
            jax.ShapeDtypeStruct((bsz, IDX_HEADS, s), F32),
        ],
        compiler_params=pltpu.CompilerParams(
            dimension_semantics=("arbitrary", "arbitrary"), vmem_limit_bytes=VMEM_LIMIT),
        name="attn_in",
    )(x, g, w, wt)


def _diff_attn_kernel(lam_ref, q_ref, k_ref, vt_ref, bias_ref, g_ref, o_ref,
                      vaug_ref, s_ref, acc_ref, ot_ref, *, lambda_init, seq):
    i = pl.program_id(1)
    nkb = seq // TK
    n_maps = 2 * A_HEADS

    @pl.when(i == 0)
    def _():
        for c in range(nkb):
            for h in range(A_HEADS):
                vaug_ref[h, c, 0:A_V_DIM, :] = vt_ref[0, c, h * A_V_DIM:(h + 1) * A_V_DIM, :]
                vaug_ref[h, c, A_V_DIM:A_V_DIM + ONES_ROWS, :] = jnp.ones((ONES_ROWS, TK), BF16)

    lv = lam_ref[...]
    lam = (jnp.exp(jnp.sum(lv[0:1] * lv[1:2], axis=-1, keepdims=True))
           - jnp.exp(jnp.sum(lv[2:3] * lv[3:4], axis=-1, keepdims=True)) + lambda_init)

    q = q_ref[0] * (A_HEAD_DIM ** -0.5)
    qm = []
    for h in range(A_HEADS):
        qh = q[:, h * LANES:(h + 1) * LANES]
        qm += [_lane_band(qh, 0, A_HEAD_DIM), _lane_band(qh, A_HEAD_DIM, A_HEAD_DIM)]

    acc_ref[...] = jnp.zeros_like(acc_ref)

    def logits(j, slot):
        krows = k_ref[0, pl.ds(pl.multiple_of(j * TK, TK), TK), :]
        t = _bias_kind(j, i)
        out = []
        for h in range(A_HEADS):
            kb = krows[:, h * LANES:(h + 1) * LANES]
            bias = bias_ref[t, h]
            for c in (2 * h, 2 * h + 1):
                out.append(_store_logits(_dot_nt(kb, qm[c]) + bias, s_ref, slot, c))
        return tuple(out)

    def accumulate(j, slot, ms, bms):
        return tuple(_online_softmax_step(s_ref, slot, c, ms[c], bms[c], vaug_ref[c // 2, j],
                                          acc_ref) for c in range(n_maps))

    _stream_key_blocks(i, n_maps, logits, accumulate)

    for h in range(A_HEADS):
        a1 = acc_ref[2 * h]
        a2 = acc_ref[2 * h + 1]
        a = (a1[0:A_V_DIM] / a1[A_V_DIM:A_V_DIM + 1]
             - lam * (a2[0:A_V_DIM] / a2[A_V_DIM:A_V_DIM + 1]))
        y = a * lax.rsqrt(jnp.mean(a * a, axis=0, keepdims=True) + EPS) * g_ref[...]
        ot_ref[h * A_V_DIM:(h + 1) * A_V_DIM, :] = y * (1.0 - lambda_init)
    o_ref[0] = ot_ref[...].T.astype(BF16)


def _diff_attn(proj, avt, lam_vecs, bias, subln_g_col, *, lambda_init):
    bsz, s, _ = proj.shape
    nq = s // TQ
    nkb = s // TK
    width = A_HEADS * A_V_DIM
    return pl.pallas_call(
        functools.partial(_diff_attn_kernel, lambda_init=lambda_init, seq=s),
        grid=(bsz, nq),
        in_specs=[
            pl.BlockSpec((4, A_HEAD_DIM), lambda b, i: (0, 0)),
            pl.BlockSpec((1, TQ, width), lambda b, i: (b, i, COL_AQ // width)),
            pl.BlockSpec((1, s, width), lambda b, i: (b, 0, COL_AK // width)),
            pl.BlockSpec((1, nkb, width, TK), lambda b, i: (b, 0, 0, 0)),
            pl.BlockSpec((3, A_HEADS, TK, TQ), lambda b, i: (0, 0, 0, 0)),
            pl.BlockSpec((A_V_DIM, 1), lambda b, i: (0, 0)),
        ],
        out_specs=pl.BlockSpec((1, TQ, width), lambda b, i: (b, i, 0)),
        out_shape=jax.ShapeDtypeStruct((bsz, s, width), BF16),
        scratch_shapes=[
            pltpu.VMEM((A_HEADS, nkb, A_V_DIM + ONES_ROWS, TK), BF16),
            pltpu.VMEM((2, 2 * A_HEADS, TK, TQ), F32),
            pltpu.VMEM((2 * A_HEADS, A_V_DIM + ONES_ROWS, TQ), F32),
            pltpu.VMEM((width, TQ), F32),
        ],
        compiler_params=pltpu.CompilerParams(
            dimension_semantics=("arbitrary", "arbitrary"), vmem_limit_bytes=VMEM_LIMIT),
        name="diff_attn",
    )(lam_vecs, proj, proj, avt, bias, subln_g_col)


def _dsa_kernel(qb_ref, iq_ref, kb_ref, ik_ref, vt_ref, iwt_ref, bias_ref, o_ref,
                key_ref, plane_ref, mask_ref, vaug_ref, s_ref, acc_ref, ot_ref, j_ref,
                *, top_k, seq):
    i = pl.program_id(1)
    nkb = seq // TK

    @pl.when(i == 0)
    def _():
        for c in range(nkb):
            vaug_ref[c, 0:B_HEAD_DIM, :] = vt_ref[0, c]
            vaug_ref[c, B_HEAD_DIM:B_HEAD_DIM + ONES_ROWS, :] = jnp.ones((ONES_ROWS, TK), BF16)

    @pl.when((i == 0) & (pl.program_id(0) == 0))
    def _():
        plane_ref[...] = jnp.zeros_like(plane_ref)

    def rows(ref, j):
        return ref[0, pl.ds(pl.multiple_of(j * TK, TK), TK), :]

    krow = lax.broadcasted_iota(I32, (TK, TQ), 0)
    qcol = lax.broadcasted_iota(I32, (TK, TQ), 1)

    iq = iq_ref[0]
    per_blk = LANES // IDX_DIM
    iqh = [_lane_band(iq[:, (h // per_blk) * LANES:(h // per_blk + 1) * LANES],
                      (h % per_blk) * IDX_DIM, IDX_DIM) for h in range(IDX_HEADS)]
    w = iwt_ref[0]

    def score_keys(j):
        ikb = rows(ik_ref, j)
        acc = jnp.zeros((TK, TQ), F32)
        for h in range(IDX_HEADS):
            acc = acc + jnp.maximum(_dot_nt(ikb, iqh[h]), 0.0) * w[h:h + 1, :]
        acc = jnp.where(acc == 0.0, 0.0, acc)
        bits = lax.bitcast_convert_type(acc, I32)
        return bits ^ ((bits >> 31) & 0x7FFFFFFF)

    def store_keys(j, keys):
        key_ref[j] = keys
        for p, plane in enumerate(_bit_planes(keys)):
            plane_ref[p, j] = plane

    def score_body(j, c):
        store_keys(j, score_keys(j))
        return c

    lax.fori_loop(0, i, score_body, 0)
    admissible = (krow // CHUNK) <= (qcol // CHUNK)
    store_keys(i, jnp.where(admissible, score_keys(i), KEY_NEG_INF))

    blk = lax.broadcasted_iota(I32, (nkb, SUBLANES, TQ), 0)
    alive0 = jnp.where(blk <= i, -1, 0)

    def ones_per_query(words):
        pc = lax.population_count(words).astype(F32)
        return jnp.sum(jnp.sum(pc, axis=0), axis=0, keepdims=True)

    def bit_body(p, carry):
        alive, n_gt, thr_u = carry
        hit = alive & plane_ref[p]
        n_hit = ones_per_query(hit)
        take = (n_gt + n_hit) >= top_k
        take_m = jnp.where(take, -1, 0)
        thr_u = thr_u | (take_m & lax.shift_left(jnp.int32(1), KEY_BITS - 1 - p))
        n_gt = n_gt + jnp.where(take, 0.0, n_hit)
        alive = (alive ^ hit) ^ (alive & take_m)
        return alive, n_gt, thr_u

    alive, n_gt, thr_u = lax.fori_loop(
        0, KEY_BITS, bit_body, (alive0, jnp.zeros((1, TQ), F32), jnp.zeros((1, TQ), I32)))
    thr = thr_u ^ INT_MIN

    def count(indicator):
        def body(j, acc):
            return acc + jnp.sum(indicator(key_ref[j], j), axis=0, keepdims=True)
        return lax.fori_loop(0, i + 1, body, jnp.zeros((1, TQ), F32))

    n_ge = n_gt + ones_per_query(alive)
    need = top_k - n_gt
    j_ref[...] = jnp.full((1, TQ), seq, I32)
    tied = jnp.max(jnp.where(n_ge > top_k, jnp.where(thr > KEY_NEG_INF, 1.0, 0.0), 0.0))

    @pl.when(tied > 0)
    def _():
        def idx_body(b, lim):
            cand = lim + lax.shift_left(jnp.int32(1), 12 - b)
            c = count(lambda kb, j: jnp.where(
                kb == thr, jnp.where((krow + j * TK) < cand, 1.0, 0.0), 0.0))
            return jnp.where(c <= need, cand, lim)
        j_ref[...] = lax.fori_loop(0, 13, idx_body, jnp.zeros((1, TQ), I32))

    lim = j_ref[...]

    def mask_body(j, c):
        kb = key_ref[j]
        tie_ok = jnp.where((krow + j * TK) < lim, 0.0, NEG)
        sel = jnp.where(kb > thr, 0.0, jnp.where(kb == thr, tie_ok, NEG))
        mask_ref[j] = jnp.where(kb > KEY_NEG_INF, sel, NEG)
        return c

    lax.fori_loop(0, i + 1, mask_body, 0)

    qb = qb_ref[0] * (B_HEAD_DIM ** -0.5)
    qh = [_lane_band(qb[:, (h // 2) * LANES:(h // 2 + 1) * LANES],
                     (h % 2) * B_HEAD_DIM, B_HEAD_DIM) for h in range(B_HEADS)]
    acc_ref[...] = jnp.zeros_like(acc_ref)

    def logits(j, slot):
        kblk = rows(kb_ref, j)
        msk = mask_ref[j]
        t = _bias_kind(j, i)
        return tuple(_store_logits(_dot_nt(kblk, qh[h]) + msk + bias_ref[t, h], s_ref, slot, h)
                     for h in range(B_HEADS))

    def accumulate(j, slot, ms, bms):
        v_aug = vaug_ref[j]
        return tuple(_online_softmax_step(s_ref, slot, h, ms[h], bms[h], v_aug, acc_ref)
                     for h in range(B_HEADS))

    _stream_key_blocks(i, B_HEADS, logits, accumulate)

    for h in range(B_HEADS):
        a = acc_ref[h]
        ot_ref[h * B_HEAD_DIM:(h + 1) * B_HEAD_DIM, :] = (
            a[0:B_HEAD_DIM] / a[B_HEAD_DIM:B_HEAD_DIM + 1])
    o_ref[0] = ot_ref[...].T.astype(BF16)


def _dsa_attn(proj, bvt, iwt, bias, *, top_k):
    bsz, s, _ = proj.shape
    nq = s // TQ
    nkb = s // TK
    width = B_HEADS * B_HEAD_DIM
    iq_w = IDX_HEADS * IDX_DIM
    return pl.pallas_call(
        functools.partial(_dsa_kernel, top_k=top_k, seq=s),
        grid=(bsz, nq),
        in_specs=[
            pl.BlockSpec((1, TQ, width), lambda b, i: (b, i, COL_BQ // width)),
            pl.BlockSpec((1, TQ, iq_w), lambda b, i: (b, i, COL_IQ // iq_w)),
            pl.BlockSpec((1, s, LANES), lambda b, i: (b, 0, COL_KB // LANES)),
            pl.BlockSpec((1, s, LANES), lambda b, i: (b, 0, COL_IK // LANES)),
            pl.BlockSpec((1, nkb, B_HEAD_DIM, TK), lambda b, i: (b, 0, 0, 0)),
            pl.BlockSpec((1, IDX_HEADS, TQ), lambda b, i: (b, 0, i)),
            pl.BlockSpec((3, B_HEADS, TK, TQ), lambda b, i: (0, 0, 0, 0)),
        ],
        out_specs=pl.BlockSpec((1, TQ, width), lambda b, i: (b, i, 0)),
        out_shape=jax.ShapeDtypeStruct((bsz, s, width), BF16),
        scratch_shapes=[
            pltpu.VMEM((nkb, TK, TQ), I32),
            pltpu.VMEM((KEY_BITS, nkb, SUBLANES, TQ), I32),
            pltpu.VMEM((nkb, TK, TQ), F32),
            pltpu.VMEM((nkb, B_HEAD_DIM + ONES_ROWS, TK), BF16),
            pltpu.VMEM((2, B_HEADS, TK, TQ), F32),
            pltpu.VMEM((B_HEADS, B_HEAD_DIM + ONES_ROWS, TQ), F32),
            pltpu.VMEM((width, TQ), F32),
            pltpu.VMEM((1, TQ), I32),
        ],
        compiler_params=pltpu.CompilerParams(
            dimension_semantics=("arbitrary", "arbitrary"), vmem_limit_bytes=VMEM_LIMIT),
        name="dsa_attn",
    )(proj, proj, proj, proj, bvt, iwt, bias)


def _attn_out_kernel(ya_ref, yb_ref, w_ref, x_ref, g_ref, o_ref):
    half = ya_ref.shape[-1]
    m = _dot(ya_ref[...], w_ref[0:half, :]) + _dot(yb_ref[...], w_ref[half:2 * half, :])
    o_ref[...] = x_ref[...] + _rms(m, g_ref[...])


def _attn_out(ya, yb, w, x, g, *, tm=512):
    n, d = x.shape
    half = ya.shape[-1]
    return pl.pallas_call(
        _attn_out_kernel,
        grid=(n // tm,),
        in_specs=[
            pl.BlockSpec((tm, half), lambda i: (i, 0)),
            pl.BlockSpec((tm, half), lambda i: (i, 0)),
            pl.BlockSpec((2 * half, d), lambda i: (0, 0)),
            pl.BlockSpec((tm, d), lambda i: (i, 0)),
            pl.BlockSpec((1, d), lambda i: (0, 0)),
        ],
        out_specs=pl.BlockSpec((tm, d), lambda i: (i, 0)),
        out_shape=jax.ShapeDtypeStruct((n, d), F32),
        compiler_params=pltpu.CompilerParams(
            dimension_semantics=("arbitrary",), vmem_limit_bytes=VMEM_LIMIT),
        name="attn_out",
    )(ya, yb, w, x, g)


def _mlp_kernel(x_ref, g_in_ref, wu_ref, wd_ref, g_out_ref, o_ref, h_ref, acc_ref):
    f = pl.program_id(1)

    @pl.when(f == 0)
    def _():
        h_ref[...] = _rms(x_ref[...], g_in_ref[...]).astype(BF16)
        acc_ref[...] = jnp.zeros_like(acc_ref)

    u = jnp.maximum(_dot(h_ref[...], wu_ref[...]), 0.0)
    acc_ref[...] += _dot((u * u).astype(BF16), wd_ref[...])

    @pl.when(f == pl.num_programs(1) - 1)
    def _():
        o_ref[...] = x_ref[...] + _rms(acc_ref[...], g_out_ref[...])


def _mlp(x, g_in, wu, wd, g_out, *, tm=1024, tf=1024):
    n, d = x.shape
    ff = wu.shape[1]
    return pl.pallas_call(
        _mlp_kernel,
        grid=(n // tm, ff // tf),
        in_specs=[
            pl.BlockSpec((tm, d), lambda i, f: (i, 0)),
            pl.BlockSpec((1, d), lambda i, f: (0, 0)),
            pl.BlockSpec((d, tf), lambda i, f: (0, f)),
            pl.BlockSpec((tf, d), lambda i, f: (f, 0)),
            pl.BlockSpec((1, d), lambda i, f: (0, 0)),
        ],
        out_specs=pl.BlockSpec((tm, d), lambda i, f: (i, 0)),
        out_shape=jax.ShapeDtypeStruct((n, d), F32),
        scratch_shapes=[pltpu.VMEM((tm, d), BF16), pltpu.VMEM((tm, d), F32)],
        compiler_params=pltpu.CompilerParams(
            dimension_semantics=("arbitrary", "arbitrary"), vmem_limit_bytes=VMEM_LIMIT),
        name="mlp",
    )(x, g_in, wu, wd, g_out)


U_HIST = 32
Z_HIST = 8


def _conv_kernel(x_ref, g_in_ref, w_in_ref, dw_w_ref, dw_b_ref, ln_g_ref, ln_b_ref, sc_w_ref,
                 w_out_ref, g_out_ref, o_ref, ubuf_ref, zbuf_ref, *, ts):
    @pl.when(pl.program_id(1) == 0)
    def _():
        ubuf_ref[0:U_HIST, :] = jnp.zeros((U_HIST, CONV_CH), F32)
        zbuf_ref[0:Z_HIST, :] = jnp.zeros((Z_HIST, SC_CH), F32)

    x = x_ref[0]
    hb = _rms(x, g_in_ref[...]).astype(BF16)
    c = CONV_CH
    ca = _dot(hb, w_in_ref[:, 0:c])
    cg = _dot(hb, w_in_ref[:, c:2 * c])
    ubuf_ref[U_HIST:U_HIST + ts, :] = ca * jax.nn.sigmoid(cg)
    dc = _dot(hb, w_in_ref[:, 3 * c:4 * c])
    dh = _dot(hb, w_in_ref[:, 4 * c:5 * c])
    zbuf_ref[Z_HIST:Z_HIST + ts, :] = dc * dh

    u = jnp.zeros((ts, c), F32)
    for j in range(CONV_WIDTH):
        off = U_HIST - (CONV_WIDTH - 1) + j
        u = u + dw_w_ref[j:j + 1, :] * ubuf_ref[off:off + ts, :]
    u = u + dw_b_ref[...]
    mu = jnp.mean(u, axis=-1, keepdims=True)
    ctr = u - mu
    var = jnp.mean(ctr * ctr, axis=-1, keepdims=True)
    u = ctr * lax.rsqrt(var + EPS) * ln_g_ref[...] + ln_b_ref[...]
    u = u * jax.nn.sigmoid(u)

    z = jnp.zeros((ts, c), F32)
    for j in range(SC_WIDTH):
        off = Z_HIST - (SC_WIDTH - 1) + j
        z = z + sc_w_ref[j:j + 1, :] * zbuf_ref[off:off + ts, :]
    z = _dot(hb, w_in_ref[:, 2 * c:3 * c]) * z

    ubuf_ref[0:U_HIST, :] = ubuf_ref[ts:ts + U_HIST, :]
    zbuf_ref[0:Z_HIST, :] = zbuf_ref[ts:ts + Z_HIST, :]

    m = _dot(u.astype(BF16), w_out_ref[0:c, :]) + _dot(z.astype(BF16), w_out_ref[c:2 * c, :])
    o_ref[0] = x + _rms(m, g_out_ref[...])


def _conv_mixer(x, g_in, w_in, dw_w, dw_b, ln_g, ln_b, sc_w, w_out, g_out, *, ts=512):
    bsz, s, d = x.shape
    full = lambda a: pl.BlockSpec(a.shape, lambda b, i: (0,) * a.ndim)
    return pl.pallas_call(
        functools.partial(_conv_kernel, ts=ts),
        grid=(bsz, s // ts),
        in_specs=[pl.BlockSpec((1, ts, d), lambda b, i: (b, i, 0)),
                  full(g_in), full(w_in), full(dw_w), full(dw_b), full(ln_g), full(ln_b),
                  full(sc_w), full(w_out), full(g_out)],
        out_specs=pl.BlockSpec((1, ts, d), lambda b, i: (b, i, 0)),
        out_shape=jax.ShapeDtypeStruct((bsz, s, d), F32),
        scratch_shapes=[pltpu.VMEM((U_HIST + ts, CONV_CH), F32),
                        pltpu.VMEM((Z_HIST + ts, SC_CH), F32)],
        compiler_params=pltpu.CompilerParams(
            dimension_semantics=("arbitrary", "arbitrary"), vmem_limit_bytes=VMEM_LIMIT),
        name="conv_mixer",
    )(x, g_in, w_in, dw_w, dw_b, ln_g, ln_b, sc_w, w_out, g_out)


def _t5_bucket(rel):
    nb = NUM_BUCKETS // 2
    ret = jnp.where(rel > 0, nb, 0)
    n = jnp.abs(rel)
    max_exact = nb // 2
    nf = jnp.maximum(n, 1).astype(jnp.float32)
    large = max_exact + (jnp.log(nf / max_exact) / math.log(MAX_DISTANCE / max_exact)
                         * (nb - max_exact)).astype(jnp.int32)
    large = jnp.minimum(large, nb - 1)
    return ret + jnp.where(n < max_exact, n, large)


def _toeplitz(f, off):
    heads, length = f.shape
    g = jnp.tile(jnp.pad(f, ((0, 0), (0, 1))), (1, TQ))[:, :TQ * length]
    return g.reshape(heads, TQ, length)[:, :, off:off + TK]


def _bias_tables(rel_bias):
    lo = -(TQ + TK - 1)
    rel = jnp.arange(lo, TK, dtype=jnp.int32)
    far_bucket = NUM_BUCKETS // 2 - 1
    f = (rel_bias[_t5_bucket(rel)].astype(F32) - rel_bias[far_bucket].astype(F32)[None, :]).T
    diag = jnp.swapaxes(_toeplitz(f, -lo), 1, 2)
    prev = jnp.swapaxes(_toeplitz(f, -lo - TK), 1, 2)
    ki = np.arange(TK)[:, None]
    qi = np.arange(TQ)[None, :]
    adm = jnp.asarray((ki // CHUNK) <= (qi // CHUNK))
    none = jnp.zeros_like(diag)
    a_diag = jnp.where(adm[None], diag[:A_HEADS], NEG)
    a_bias = jnp.stack([none[:A_HEADS], prev[:A_HEADS], a_diag], axis=0)
    b_bias = jnp.stack([none[A_HEADS:], prev[A_HEADS:], diag[A_HEADS:]], axis=0)
    return a_bias, b_bias


def _attn_weights(w_in):
    o = np.cumsum([0, 512, 512, 512, 512, 64, 64, 256, 32, 8])
    aq, ak, av, bq, bk, bv, iq, ik, iw = [w_in[:, o[n]:o[n + 1]] for n in range(9)]
    w = jnp.concatenate([aq, ak, bq, iq, bk, bk, ik, ik, ik, ik], axis=1).astype(BF16)
    pad = jnp.zeros((w_in.shape[0], VT_ROWS - ROW_IW - IDX_HEADS), w_in.dtype)
    wt = jnp.concatenate([av, bv, iw, pad], axis=1).T.astype(BF16)
    return w, wt


def kernel(x, rel_bias, norm_g, w_mlp_up, w_mlp_down, attn_w_in, attn_w_out, diff_lambda,
           diff_subln_g, conv_w_in, conv_w_out, conv_dw_w, conv_dw_b, conv_ln_g, conv_ln_b,
           sconv_w):
    bsz, s, d = x.shape
    depth = norm_g.shape[0]
    top_k = min(TOPK_MAX, s // 4)
    row = lambda v: v.reshape(1, -1)
    a_bias, b_bias = _bias_tables(rel_bias)
    for i in range(depth):
        j = i // 2
        if i % 2 == 0:
            lambda_init = 0.8 - 0.6 * math.exp(-0.3 * i)
            w, wt = _attn_weights(attn_w_in[j])
            proj, avt, bvt, iwt = _attn_in(x.reshape(bsz, s, d), row(norm_g[i, 0]), w, wt)
            ya = _diff_attn(proj, avt, diff_lambda[j], a_bias, diff_subln_g[j].reshape(-1, 1),
                            lambda_init=lambda_init)
            yb = _dsa_attn(proj, bvt, iwt, b_bias, top_k=top_k)
            x = _attn_out(ya.reshape(bsz * s, -1), yb.reshape(bsz * s, -1),
                          attn_w_out[j].astype(BF16), x.reshape(bsz * s, d),
                          row(norm_g[i, 1]))
        else:
            x = _conv_mixer(x.reshape(bsz, s, d), row(norm_g[i, 0]), conv_w_in[j].astype(BF16),
                            conv_dw_w[j], row(conv_dw_b[j]), row(conv_ln_g[j]),
                            row(conv_ln_b[j]), sconv_w[j], conv_w_out[j].astype(BF16),
                            row(norm_g[i, 1])).reshape(bsz * s, d)
        x = _mlp(x, row(norm_g[i, 2]), w_mlp_up[i].astype(BF16), w_mlp_down[i].astype(BF16),
                 row(norm_g[i, 3]))
    return x.reshape(bsz, s, d)
```

```python
import functools
import math

import numpy as np
import jax
import jax.numpy as jnp
from jax import lax
from jax.experimental import pallas as pl
from jax.experimental.pallas import tpu as pltpu

F32 = jnp.float32
BF16 = jnp.bfloat16
I32 = jnp.int32

D_MODEL = 1024
CHUNK = 64
NUM_BUCKETS = 32
MAX_DISTANCE = 128
EPS = 1e-6
NEG = -1e30
LOG2E = math.log2(math.e)
A_HEADS = 4
A_HEAD_DIM = 64
A_V_DIM = 2 * A_HEAD_DIM
B_HEADS = 8
B_HEAD_DIM = 64
IDX_HEADS = 8
IDX_DIM = 32
TOPK_MAX = 256
CONV_CH = 512
CONV_WIDTH = 31
SC_CH = 512
SC_WIDTH = 3
D_FF = 4 * D_MODEL

LANES = 128
SUBLANES = 8
VMEM_LIMIT = 56 * 1024 * 1024

TQ = 256
TK = 256
COL_AQ, COL_AK, COL_BQ, COL_IQ, COL_KB, COL_IK, PROJ_W = 0, 512, 1024, 1536, 1792, 1920, 2048
ROW_AV, ROW_BV, ROW_IW, VT_ROWS = 0, 512, 576, 640
ONES_ROWS = 16

INT_MIN = -2 ** 31
KEY_NEG_INF = INT_MIN + 0x7FFFFF


def _rms(x, g):
    return x * lax.rsqrt(jnp.mean(x * x, axis=-1, keepdims=True) + EPS) * g


def _dot_nt(a, b):
    return lax.dot_general(a, b, (((1,), (1,)), ((), ())), preferred_element_type=F32)


def _dot(a, b):
    return jnp.dot(a, b, preferred_element_type=F32)


def _lane_band(x, lo, width):
    lane = lax.broadcasted_iota(I32, x.shape, 1)
    return jnp.where((lane >= lo) & (lane < lo + width), x, jnp.zeros_like(x))


KEY_BITS = 32


def _bit_planes(keys):
    assert keys.shape[0] == KEY_BITS * SUBLANES
    u = keys ^ INT_MIN
    a = [u[SUBLANES * r:SUBLANES * (r + 1), :] for r in range(KEY_BITS)]
    j, m = KEY_BITS // 2, (1 << (KEY_BITS // 2)) - 1
    while j:
        mask = np.int32(np.uint32(m))
        k = 0
        while k < KEY_BITS:
            t = (a[k] ^ lax.shift_right_logical(a[k + j], jnp.int32(j))) & mask
            a[k] = a[k] ^ t
            a[k + j] = a[k + j] ^ lax.shift_left(t, jnp.int32(j))
            k = (k + j + 1) & ~j
        j >>= 1
        m = (m ^ (m << j)) & 0xFFFFFFFF
    return a


def _bias_kind(j, i):
    return jnp.clip(j - i + 2, 0, 2)


def _stream_key_blocks(i, n_chains, s_ref, acc_ref, logits, v_aug):
    chains = range(n_chains)

    def stage(j, c, kind):
        s = logits(j, c, kind)
        s_ref[c] = s
        return jnp.max(s, axis=0, keepdims=True)

    def absorb(j, c, m_old, block_max):
        m_new = jnp.maximum(m_old, block_max)
        alpha = jnp.exp2(m_old - m_new)
        p = jnp.exp2(s_ref[c] - m_new).astype(BF16)
        acc_ref[c] = acc_ref[c] * alpha + _dot(v_aug(j, c), p)
        return m_new

    def step(j, carry, kind):
        ms, bms = carry
        out = [(absorb(j, c, ms[c], bms[c]), stage(j + 1, c, kind)) for c in chains]
        return tuple(o[0] for o in out), tuple(o[1] for o in out)

    acc_ref[...] = jnp.zeros_like(acc_ref)
    carry = ((jnp.full((1, TQ), NEG, F32),) * n_chains, tuple(stage(0, c, 'any') for c in chains))
    n_far = jnp.maximum(i - 2, 0)
    carry = lax.fori_loop(0, n_far, lambda j, cr: step(j, cr, 'far'), carry)
    ms, bms = lax.fori_loop(n_far, i, lambda j, cr: step(j, cr, 'near'), carry)
    for c in chains:
        absorb(i, c, ms[c], bms[c])


def _attn_in_kernel(x_ref, g_ref, w_ref, wt_ref, proj_ref, avt_ref, bvt_ref, iwt_ref, *, tm):
    hb = _rms(x_ref[0], g_ref[...]).astype(BF16)
    proj_ref[0] = _dot(hb, w_ref[...]).astype(BF16)
    t = _dot_nt(wt_ref[...], hb)
    for c in range(tm // TK):
        avt_ref[0, c] = t[ROW_AV:ROW_BV, c * TK:(c + 1) * TK].astype(BF16)
        bvt_ref[0, c] = t[ROW_BV:ROW_IW, c * TK:(c + 1) * TK].astype(BF16)
    iwt_ref[0] = t[ROW_IW:ROW_IW + IDX_HEADS, :] * (IDX_HEADS ** -0.5 * IDX_DIM ** -0.5)


def _attn_in(x, g, w, wt, *, tm=512):
    bsz, s, d = x.shape
    av_rows = ROW_BV - ROW_AV
    return pl.pallas_call(
        functools.partial(_attn_in_kernel, tm=tm),
        grid=(bsz, s // tm),
        in_specs=[
            pl.BlockSpec((1, tm, d), lambda b, i: (b, i, 0)),
            pl.BlockSpec((1, d), lambda b, i: (0, 0)),
            pl.BlockSpec((d, PROJ_W), lambda b, i: (0, 0)),
            pl.BlockSpec((VT_ROWS, d), lambda b, i: (0, 0)),
        ],
        out_specs=[
            pl.BlockSpec((1, tm, PROJ_W), lambda b, i: (b, i, 0)),
            pl.BlockSpec((1, tm // TK, av_rows, TK), lambda b, i: (b, i, 0, 0)),
            pl.BlockSpec((1, tm // TK, B_HEAD_DIM, TK), lambda b, i: (b, i, 0, 0)),
            pl.BlockSpec((1, IDX_HEADS, tm), lambda b, i: (b, 0, i)),
        ],
        out_shape=[
            jax.ShapeDtypeStruct((bsz, s, PROJ_W), BF16),
            jax.ShapeDtypeStruct((bsz, s // TK, av_rows, TK), BF16),
            jax.ShapeDtypeStruct((bsz, s // TK, B_HEAD_DIM, TK), BF16),
            jax.ShapeDtypeStruct((bsz, IDX_HEADS, s), F32),
        ],
        compiler_params=pltpu.CompilerParams(
            dimension_semantics=("arbitrary", "arbitrary"), vmem_limit_bytes=VMEM_LIMIT),
        name="attn_in",
    )(x, g, w, wt)


def _diff_attn_kernel(lam_ref, q_ref, k_ref, vt_ref, bias_ref, g_ref, o_ref,
                      vaug_ref, s_ref, acc_ref, ot_ref, *, lambda_init, seq):
    i = pl.program_id(1)
    nkb = seq // TK
    n_maps = 2 * A_HEADS

    @pl.when(i == 0)
    def _():
        for c in range(nkb):
            for h in range(A_HEADS):
                vaug_ref[h, c, 0:A_V_DIM, :] = vt_ref[0, c, h * A_V_DIM:(h + 1) * A_V_DIM, :]
                vaug_ref[h, c, A_V_DIM:A_V_DIM + ONES_ROWS, :] = jnp.ones((ONES_ROWS, TK), BF16)

    lv = lam_ref[...]
    lam = (jnp.exp(jnp.sum(lv[0:1] * lv[1:2], axis=-1, keepdims=True))
           - jnp.exp(jnp.sum(lv[2:3] * lv[3:4], axis=-1, keepdims=True)) + lambda_init)

    q = q_ref[0]
    qm = []
    for h in range(A_HEADS):
        qh = q[:, h * LANES:(h + 1) * LANES]
        qm += [_lane_band(qh, 0, A_HEAD_DIM), _lane_band(qh, A_HEAD_DIM, A_HEAD_DIM)]

    def logits(j, c, kind):
        h = c // 2
        kb = k_ref[0, pl.ds(pl.multiple_of(j * TK, TK), TK), h * LANES:(h + 1) * LANES]
        s = _dot_nt(kb, qm[c])
        return s if kind == 'far' else s + bias_ref[_bias_kind(j, i), h]

    _stream_key_blocks(i, n_maps, s_ref, acc_ref, logits, lambda j, c: vaug_ref[c // 2, j])

    for h in range(A_HEADS):
        a1 = acc_ref[2 * h]
        a2 = acc_ref[2 * h + 1]
        a = (a1[0:A_V_DIM] / a1[A_V_DIM:A_V_DIM + 1]
             - lam * (a2[0:A_V_DIM] / a2[A_V_DIM:A_V_DIM + 1]))
        y = a * lax.rsqrt(jnp.mean(a * a, axis=0, keepdims=True) + EPS) * g_ref[...]
        ot_ref[h * A_V_DIM:(h + 1) * A_V_DIM, :] = y * (1.0 - lambda_init)
    o_ref[0] = ot_ref[...].T.astype(BF16)


def _diff_attn(proj, avt, lam_vecs, bias, subln_g_col, *, lambda_init):
    bsz, s, _ = proj.shape
    nq = s // TQ
    nkb = s // TK
    width = A_HEADS * A_V_DIM
    return pl.pallas_call(
        functools.partial(_diff_attn_kernel, lambda_init=lambda_init, seq=s),
        grid=(bsz, nq),
        in_specs=[
            pl.BlockSpec((4, A_HEAD_DIM), lambda b, i: (0, 0)),
            pl.BlockSpec((1, TQ, width), lambda b, i: (b, i, COL_AQ // width)),
            pl.BlockSpec((1, s, width), lambda b, i: (b, 0, COL_AK // width)),
            pl.BlockSpec((1, nkb, width, TK), lambda b, i: (b, 0, 0, 0)),
            pl.BlockSpec((3, A_HEADS, TK, TQ), lambda b, i: (0, 0, 0, 0)),
            pl.BlockSpec((A_V_DIM, 1), lambda b, i: (0, 0)),
        ],
        out_specs=pl.BlockSpec((1, TQ, width), lambda b, i: (b, i, 0)),
        out_shape=jax.ShapeDtypeStruct((bsz, s, width), BF16),
        scratch_shapes=[
            pltpu.VMEM((A_HEADS, nkb, A_V_DIM + ONES_ROWS, TK), BF16),
            pltpu.VMEM((2 * A_HEADS, TK, TQ), F32),
            pltpu.VMEM((2 * A_HEADS, A_V_DIM + ONES_ROWS, TQ), F32),
            pltpu.VMEM((width, TQ), F32),
        ],
        compiler_params=pltpu.CompilerParams(
            dimension_semantics=("arbitrary", "arbitrary"), vmem_limit_bytes=VMEM_LIMIT),
        name="diff_attn",
    )(lam_vecs, proj, proj, avt, bias, subln_g_col)


def _dsa_kernel(qb_ref, iq_ref, kb_ref, ik_ref, vt_ref, iwt_ref, bias_ref, o_ref,
                key_ref, plane_ref, mask_ref, vaug_ref, s_ref, acc_ref, ot_ref, j_ref,
                *, top_k, seq):
    i = pl.program_id(1)
    nkb = seq // TK

    @pl.when(i == 0)
    def _():
        for c in range(nkb):
            vaug_ref[c, 0:B_HEAD_DIM, :] = vt_ref[0, c]
            vaug_ref[c, B_HEAD_DIM:B_HEAD_DIM + ONES_ROWS, :] = jnp.ones((ONES_ROWS, TK), BF16)

    @pl.when((i == 0) & (pl.program_id(0) == 0))
    def _():
        plane_ref[...] = jnp.zeros_like(plane_ref)

    def rows(ref, j):
        return ref[0, pl.ds(pl.multiple_of(j * TK, TK), TK), :]

    krow = lax.broadcasted_iota(I32, (TK, TQ), 0)
    qcol = lax.broadcasted_iota(I32, (TK, TQ), 1)

    iq = iq_ref[0]
    per_blk = LANES // IDX_DIM
    iqh = [_lane_band(iq[:, (h // per_blk) * LANES:(h // per_blk + 1) * LANES],
                      (h % per_blk) * IDX_DIM, IDX_DIM) for h in range(IDX_HEADS)]
    w = iwt_ref[0]

    def score_keys(j):
        ikb = rows(ik_ref, j)
        acc = jnp.zeros((TK, TQ), F32)
        for h in range(IDX_HEADS):
            acc = acc + jnp.maximum(_dot_nt(ikb, iqh[h]), 0.0) * w[h:h + 1, :]
        acc = jnp.where(acc == 0.0, 0.0, acc)
        bits = lax.bitcast_convert_type(acc, I32)
        return bits ^ ((bits >> 31) & 0x7FFFFFFF)

    def store_keys(j, keys):
        key_ref[j] = keys
        for p, plane in enumerate(_bit_planes(keys)):
            plane_ref[p, j] = plane

    def score_body(j, c):
        store_keys(j, score_keys(j))
        return c

    lax.fori_loop(0, i, score_body, 0)
    admissible = (krow // CHUNK) <= (qcol // CHUNK)
    store_keys(i, jnp.where(admissible, score_keys(i), KEY_NEG_INF))

    blk = lax.broadcasted_iota(I32, (nkb, SUBLANES, TQ), 0)
    alive0 = jnp.where(blk <= i, -1, 0)

    def ones_per_query(words):
        pc = lax.population_count(words).astype(F32)
        return jnp.sum(jnp.sum(pc, axis=0), axis=0, keepdims=True)

    def bit_body(p, carry):
        alive, n_gt, thr_u = carry
        hit = alive & plane_ref[p]
        n_hit = ones_per_query(hit)
        take = (n_gt + n_hit) >= top_k
        take_m = jnp.where(take, -1, 0)
        thr_u = thr_u | (take_m & lax.shift_left(jnp.int32(1), KEY_BITS - 1 - p))
        n_gt = n_gt + jnp.where(take, 0.0, n_hit)
        alive = (alive ^ hit) ^ (alive & take_m)
        return alive, n_gt, thr_u

    alive, n_gt, thr_u = lax.fori_loop(
        0, KEY_BITS, bit_body, (alive0, jnp.zeros((1, TQ), F32), jnp.zeros((1, TQ), I32)))
    thr = thr_u ^ INT_MIN

    def count(indicator):
        def body(j, acc):
            return acc + jnp.sum(indicator(key_ref[j], j), axis=0, keepdims=True)
        return lax.fori_loop(0, i + 1, body, jnp.zeros((1, TQ), F32))

    n_ge = n_gt + ones_per_query(alive)
    need = top_k - n_gt
    j_ref[...] = jnp.full((1, TQ), seq, I32)
    tied = jnp.max(jnp.where(n_ge > top_k, jnp.where(thr > KEY_NEG_INF, 1.0, 0.0), 0.0))

    @pl.when(tied > 0)
    def _():
        def idx_body(b, lim):
            cand = lim + lax.shift_left(jnp.int32(1), 12 - b)
            c = count(lambda kb, j: jnp.where(
                kb == thr, jnp.where((krow + j * TK) < cand, 1.0, 0.0), 0.0))
            return jnp.where(c <= need, cand, lim)
        j_ref[...] = lax.fori_loop(0, 13, idx_body, jnp.zeros((1, TQ), I32))

    lim = j_ref[...]

    def mask_body(j, c):
        kb = key_ref[j]
        tie_ok = jnp.where((krow + j * TK) < lim, 0.0, NEG)
        sel = jnp.where(kb > thr, 0.0, jnp.where(kb == thr, tie_ok, NEG))
        mask_ref[j] = jnp.where(kb > KEY_NEG_INF, sel, NEG)
        return c

    lax.fori_loop(0, i + 1, mask_body, 0)

    qb = qb_ref[0]
    qh = [_lane_band(qb[:, (h // 2) * LANES:(h // 2 + 1) * LANES],
                     (h % 2) * B_HEAD_DIM, B_HEAD_DIM) for h in range(B_HEADS)]

    def logits(j, h, kind):
        s = _dot_nt(rows(kb_ref, j), qh[h]) + mask_ref[j]
        return s if kind == 'far' else s + bias_ref[_bias_kind(j, i), h]

    _stream_key_blocks(i, B_HEADS, s_ref, acc_ref, logits, lambda j, h: vaug_ref[j])

    for h in range(B_HEADS):
        a = acc_ref[h]
        ot_ref[h * B_HEAD_DIM:(h + 1) * B_HEAD_DIM, :] = (
            a[0:B_HEAD_DIM] / a[B_HEAD_DIM:B_HEAD_DIM + 1])
    o_ref[0] = ot_ref[...].T.astype(BF16)


def _dsa_attn(proj, bvt, iwt, bias, *, top_k):
    bsz, s, _ = proj.shape
    nq = s // TQ
    nkb = s // TK
    width = B_HEADS * B_HEAD_DIM
    iq_w = IDX_HEADS * IDX_DIM
    return pl.pallas_call(
        functools.partial(_dsa_kernel, top_k=top_k, seq=s),
        grid=(bsz, nq),
        in_specs=[
            pl.BlockSpec((1, TQ, width), lambda b, i: (b, i, COL_BQ // width)),
            pl.BlockSpec((1, TQ, iq_w), lambda b, i: (b, i, COL_IQ // iq_w)),
            pl.BlockSpec((1, s, LANES), lambda b, i: (b, 0, COL_KB // LANES)),
            pl.BlockSpec((1, s, LANES), lambda b, i: (b, 0, COL_IK // LANES)),
            pl.BlockSpec((1, nkb, B_HEAD_DIM, TK), lambda b, i: (b, 0, 0, 0)),
            pl.BlockSpec((1, IDX_HEADS, TQ), lambda b, i: (b, 0, i)),
            pl.BlockSpec((3, B_HEADS, TK, TQ), lambda b, i: (0, 0, 0, 0)),
        ],
        out_specs=pl.BlockSpec((1, TQ, width), lambda b, i: (b, i, 0)),
        out_shape=jax.ShapeDtypeStruct((bsz, s, width), BF16),
        scratch_shapes=[
            pltpu.VMEM((nkb, TK, TQ), I32),
            pltpu.VMEM((KEY_BITS, nkb, SUBLANES, TQ), I32),
            pltpu.VMEM((nkb, TK, TQ), F32),
            pltpu.VMEM((nkb, B_HEAD_DIM + ONES_ROWS, TK), BF16),
            pltpu.VMEM((B_HEADS, TK, TQ), F32),
            pltpu.VMEM((B_HEADS, B_HEAD_DIM + ONES_ROWS, TQ), F32),
            pltpu.VMEM((width, TQ), F32),
            pltpu.VMEM((1, TQ), I32),
        ],
        compiler_params=pltpu.CompilerParams(
            dimension_semantics=("arbitrary", "arbitrary"), vmem_limit_bytes=VMEM_LIMIT),
        name="dsa_attn",
    )(proj, proj, proj, proj, bvt, iwt, bias)


def _attn_out_kernel(ya_ref, yb_ref, w_ref, x_ref, g_ref, o_ref):
    half = ya_ref.shape[-1]
    m = _dot(ya_ref[...], w_ref[0:half, :]) + _dot(yb_ref[...], w_ref[half:2 * half, :])
    o_ref[...] = x_ref[...] + _rms(m, g_ref[...])


def _attn_out(ya, yb, w, x, g, *, tm=512):
    n, d = x.shape
    half = ya.shape[-1]
    return pl.pallas_call(
        _attn_out_kernel,
        grid=(n // tm,),
        in_specs=[
            pl.BlockSpec((tm, half), lambda i: (i, 0)),
            pl.BlockSpec((tm, half), lambda i: (i, 0)),
            pl.BlockSpec((2 * half, d), lambda i: (0, 0)),
            pl.BlockSpec((tm, d), lambda i: (i, 0)),
            pl.BlockSpec((1, d), lambda i: (0, 0)),
        ],
        out_specs=pl.BlockSpec((tm, d), lambda i: (i, 0)),
        out_shape=jax.ShapeDtypeStruct((n, d), F32),
        compiler_params=pltpu.CompilerParams(
            dimension_semantics=("arbitrary",), vmem_limit_bytes=VMEM_LIMIT),
        name="attn_out",
    )(ya, yb, w, x, g)


def _mlp_kernel(x_ref, g_in_ref, wu_ref, wd_ref, g_out_ref, o_ref, h_ref, acc_ref):
    f = pl.program_id(1)

    @pl.when(f == 0)
    def _():
        h_ref[...] = _rms(x_ref[...], g_in_ref[...]).astype(BF16)
        acc_ref[...] = jnp.zeros_like(acc_ref)

    u = jnp.maximum(_dot(h_ref[...], wu_ref[...]), 0.0)
    acc_ref[...] += _dot((u * u).astype(BF16), wd_ref[...])

    @pl.when(f == pl.num_programs(1) - 1)
    def _():
        o_ref[...] = x_ref[...] + _rms(acc_ref[...], g_out_ref[...])


def _mlp(x, g_in, wu, wd, g_out, *, tm=1024, tf=1024):
    n, d = x.shape
    ff = wu.shape[1]
    return pl.pallas_call(
        _mlp_kernel,
        grid=(n // tm, ff // tf),
        in_specs=[
            pl.BlockSpec((tm, d), lambda i, f: (i, 0)),
            pl.BlockSpec((1, d), lambda i, f: (0, 0)),
            pl.BlockSpec((d, tf), lambda i, f: (0, f)),
            pl.BlockSpec((tf, d), lambda i, f: (f, 0)),
            pl.BlockSpec((1, d), lambda i, f: (0, 0)),
        ],
        out_specs=pl.BlockSpec((tm, d), lambda i, f: (i, 0)),
        out_shape=jax.ShapeDtypeStruct((n, d), F32),
        scratch_shapes=[pltpu.VMEM((tm, d), BF16), pltpu.VMEM((tm, d), F32)],
        compiler_params=pltpu.CompilerParams(
            dimension_semantics=("arbitrary", "arbitrary"), vmem_limit_bytes=VMEM_LIMIT),
        name="mlp",
    )(x, g_in, wu, wd, g_out)


U_HIST = 32
Z_HIST = 8


def _conv_kernel(x_ref, g_in_ref, w_in_ref, dw_w_ref, dw_b_ref, ln_g_ref, ln_b_ref, sc_w_ref,
                 w_out_ref, g_out_ref, o_ref, ubuf_ref, zbuf_ref, *, ts):
    @pl.when(pl.program_id(1) == 0)
    def _():
        ubuf_ref[0:U_HIST, :] = jnp.zeros((U_HIST, CONV_CH), F32)
        zbuf_ref[0:Z_HIST, :] = jnp.zeros((Z_HIST, SC_CH), F32)

    x = x_ref[0]
    hb = _rms(x, g_in_ref[...]).astype(BF16)
    c = CONV_CH
    ca = _dot(hb, w_in_ref[:, 0:c])
    cg = _dot(hb, w_in_ref[:, c:2 * c])
    ubuf_ref[U_HIST:U_HIST + ts, :] = ca * jax.nn.sigmoid(cg)
    dc = _dot(hb, w_in_ref[:, 3 * c:4 * c])
    dh = _dot(hb, w_in_ref[:, 4 * c:5 * c])
    zbuf_ref[Z_HIST:Z_HIST + ts, :] = dc * dh

    u = jnp.zeros((ts, c), F32)
    for j in range(CONV_WIDTH):
        off = U_HIST - (CONV_WIDTH - 1) + j
        u = u + dw_w_ref[j:j + 1, :] * ubuf_ref[off:off + ts, :]
    u = u + dw_b_ref[...]
    mu = jnp.mean(u, axis=-1, keepdims=True)
    ctr = u - mu
    var = jnp.mean(ctr * ctr, axis=-1, keepdims=True)
    u = ctr * lax.rsqrt(var + EPS) * ln_g_ref[...] + ln_b_ref[...]
    u = u * jax.nn.sigmoid(u)

    z = jnp.zeros((ts, c), F32)
    for j in range(SC_WIDTH):
        off = Z_HIST - (SC_WIDTH - 1) + j
        z = z + sc_w_ref[j:j + 1, :] * zbuf_ref[off:off + ts, :]
    z = _dot(hb, w_in_ref[:, 2 * c:3 * c]) * z

    ubuf_ref[0:U_HIST, :] = ubuf_ref[ts:ts + U_HIST, :]
    zbuf_ref[0:Z_HIST, :] = zbuf_ref[ts:ts + Z_HIST, :]

    m = _dot(u.astype(BF16), w_out_ref[0:c, :]) + _dot(z.astype(BF16), w_out_ref[c:2 * c, :])
    o_ref[0] = x + _rms(m, g_out_ref[...])


def _conv_mixer(x, g_in, w_in, dw_w, dw_b, ln_g, ln_b, sc_w, w_out, g_out, *, ts=512):
    bsz, s, d = x.shape
    full = lambda a: pl.BlockSpec(a.shape, lambda b, i: (0,) * a.ndim)
    return pl.pallas_call(
        functools.partial(_conv_kernel, ts=ts),
        grid=(bsz, s // ts),
        in_specs=[pl.BlockSpec((1, ts, d), lambda b, i: (b, i, 0)),
                  full(g_in), full(w_in), full(dw_w), full(dw_b), full(ln_g), full(ln_b),
                  full(sc_w), full(w_out), full(g_out)],
        out_specs=pl.BlockSpec((1, ts, d), lambda b, i: (b, i, 0)),
        out_shape=jax.ShapeDtypeStruct((bsz, s, d), F32),
        scratch_shapes=[pltpu.VMEM((U_HIST + ts, CONV_CH), F32),
                        pltpu.VMEM((Z_HIST + ts, SC_CH), F32)],
        compiler_params=pltpu.CompilerParams(
            dimension_semantics=("arbitrary", "arbitrary"), vmem_limit_bytes=VMEM_LIMIT),
        name="conv_mixer",
    )(x, g_in, w_in, dw_w, dw_b, ln_g, ln_b, sc_w, w_out, g_out)


def _t5_bucket(rel):
    nb = NUM_BUCKETS // 2
    ret = jnp.where(rel > 0, nb, 0)
    n = jnp.abs(rel)
    max_exact = nb // 2
    nf = jnp.maximum(n, 1).astype(jnp.float32)
    large = max_exact + (jnp.log(nf / max_exact) / math.log(MAX_DISTANCE / max_exact)
                         * (nb - max_exact)).astype(jnp.int32)
    large = jnp.minimum(large, nb - 1)
    return ret + jnp.where(n < max_exact, n, large)


def _toeplitz(f, off):
    heads, length = f.shape
    g = jnp.tile(jnp.pad(f, ((0, 0), (0, 1))), (1, TQ))[:, :TQ * length]
    return g.reshape(heads, TQ, length)[:, :, off:off + TK]


def _bias_tables(rel_bias):
    lo = -(TQ + TK - 1)
    rel = jnp.arange(lo, TK, dtype=jnp.int32)
    far_bucket = NUM_BUCKETS // 2 - 1
    f = (rel_bias[_t5_bucket(rel)].astype(F32) - rel_bias[far_bucket].astype(F32)[None, :]).T
    f = f * LOG2E
    diag = jnp.swapaxes(_toeplitz(f, -lo), 1, 2)
    prev = jnp.swapaxes(_toeplitz(f, -lo - TK), 1, 2)
    ki = np.arange(TK)[:, None]
    qi = np.arange(TQ)[None, :]
    adm = jnp.asarray((ki // CHUNK) <= (qi // CHUNK))
    none = jnp.zeros_like(diag)
    a_diag = jnp.where(adm[None], diag[:A_HEADS], NEG)
    a_bias = jnp.stack([none[:A_HEADS], prev[:A_HEADS], a_diag], axis=0)
    b_bias = jnp.stack([none[A_HEADS:], prev[A_HEADS:], diag[A_HEADS:]], axis=0)
    return a_bias, b_bias


def _attn_weights(w_in):
    o = np.cumsum([0, 512, 512, 512, 512, 64, 64, 256, 32, 8])
    aq, ak, av, bq, bk, bv, iq, ik, iw = [w_in[:, o[n]:o[n + 1]] for n in range(9)]
    aq = aq * (A_HEAD_DIM ** -0.5 * LOG2E)
    bq = bq * (B_HEAD_DIM ** -0.5 * LOG2E)
    w = jnp.concatenate([aq, ak, bq, iq, bk, bk, ik, ik, ik, ik], axis=1).astype(BF16)
    pad = jnp.zeros((w_in.shape[0], VT_ROWS - ROW_IW - IDX_HEADS), w_in.dtype)
    wt = jnp.concatenate([av, bv, iw, pad], axis=1).T.astype(BF16)
    return w, wt


def kernel(x, rel_bias, norm_g, w_mlp_up, w_mlp_down, attn_w_in, attn_w_out, diff_lambda,
           diff_subln_g, conv_w_in, conv_w_out, conv_dw_w, conv_dw_b, conv_ln_g, conv_ln_b,
           sconv_w):
    bsz, s, d = x.shape
    depth = norm_g.shape[0]
    top_k = min(TOPK_MAX, s // 4)
    row = lambda v: v.reshape(1, -1)
    a_bias, b_bias = _bias_tables(rel_bias)
    for i in range(depth):
        j = i // 2
        if i % 2 == 0:
            lambda_init = 0.8 - 0.6 * math.exp(-0.3 * i)
            w, wt = _attn_weights(attn_w_in[j])
            proj, avt, bvt, iwt = _attn_in(x.reshape(bsz, s, d), row(norm_g[i, 0]), w, wt)
            ya = _diff_attn(proj, avt, diff_lambda[j], a_bias, diff_subln_g[j].reshape(-1, 1),
                            lambda_init=lambda_init)
            yb = _dsa_attn(proj, bvt, iwt, b_bias, top_k=top_k)
            x = _attn_out(ya.reshape(bsz * s, -1), yb.reshape(bsz * s, -1),
                          attn_w_out[j].astype(BF16), x.reshape(bsz * s, d),
                          row(norm_g[i, 1]))
        else:
            x = _conv_mixer(x.reshape(bsz, s, d), row(norm_g[i, 0]), conv_w_in[j].astype(BF16),
                            conv_dw_w[j], row(conv_dw_b[j]), row(conv_ln_g[j]),
                            row(conv_ln_b[j]), sconv_w[j], conv_w_out[j].astype(BF16),
                            row(norm_g[i, 1])).reshape(bsz * s, d)
        x = _mlp(x, row(norm_g[i, 2]), w_mlp_up[i].astype(BF16), w_mlp_down[i].astype(BF16),
                 row(norm_g[i, 3]))
    return x.reshape(bsz, s, d)
```

```python
import functools
import math

import numpy as np
import jax
import jax.numpy as jnp
from jax import lax
from jax.experimental import pallas as pl
from jax.experimental.pallas import tpu as pltpu

F32 = jnp.float32
BF16 = jnp.bfloat16
I32 = jnp.int32

D_MODEL = 1024
CHUNK = 64
NUM_BUCKETS = 32
MAX_DISTANCE = 128
EPS = 1e-6
NEG = -1e30
LOG2E = math.log2(math.e)
A_HEADS = 4
A_HEAD_DIM = 64
A_V_DIM = 2 * A_HEAD_DIM
B_HEADS = 8
B_HEAD_DIM = 64
IDX_HEADS = 8
IDX_DIM = 32
TOPK_MAX = 256
CONV_CH = 512
CONV_WIDTH = 31
SC_CH = 512
SC_WIDTH = 3
D_FF = 4 * D_MODEL

LANES = 128
SUBLANES = 8
VMEM_LIMIT = 56 * 1024 * 1024

TQ = 256
TK = 256
COL_AQ, COL_AK, COL_BQ, COL_IQ, COL_KB, COL_IK, PROJ_W = 0, 512, 1024, 1536, 1792, 1920, 2048
ROW_AV, ROW_BV, ROW_IW, VT_ROWS = 0, 512, 576, 640
ONES_ROWS = 16

INT_MIN = -2 ** 31
KEY_NEG_INF = INT_MIN + 0x7FFFFF


def _rms(x, g):
    return x * lax.rsqrt(jnp.mean(x * x, axis=-1, keepdims=True) + EPS) * g


def _dot_nt(a, b):
    return lax.dot_general(a, b, (((1,), (1,)), ((), ())), preferred_element_type=F32)


def _dot(a, b):
    return jnp.dot(a, b, preferred_element_type=F32)


def _lane_band(x, lo, width):
    lane = lax.broadcasted_iota(I32, x.shape, 1)
    return jnp.where((lane >= lo) & (lane < lo + width), x, jnp.zeros_like(x))


KEY_BITS = 32


def _bit_planes(keys):
    assert keys.shape[0] == KEY_BITS * SUBLANES
    u = keys ^ INT_MIN
    a = [u[SUBLANES * r:SUBLANES * (r + 1), :] for r in range(KEY_BITS)]
    j, m = KEY_BITS // 2, (1 << (KEY_BITS // 2)) - 1
    while j:
        mask = np.int32(np.uint32(m))
        k = 0
        while k < KEY_BITS:
            t = (a[k] ^ lax.shift_right_logical(a[k + j], jnp.int32(j))) & mask
            a[k] = a[k] ^ t
            a[k + j] = a[k + j] ^ lax.shift_left(t, jnp.int32(j))
            k = (k + j + 1) & ~j
        j >>= 1
        m = (m ^ (m << j)) & 0xFFFFFFFF
    return a


def _index_bits(seq):
    bits = seq.bit_length() - 1
    assert seq == 1 << bits and seq >= KEY_BITS * SUBLANES
    return bits


def _index_planes(nkb):
    shape = (nkb, SUBLANES, TQ)
    blk = lax.broadcasted_iota(I32, shape, 0)
    sub = lax.broadcasted_iota(I32, shape, 1)
    sub_bits = SUBLANES.bit_length() - 1
    word_bits = KEY_BITS.bit_length() - 1
    planes = []
    for b in range(_index_bits(nkb * TK) - 1, -1, -1):
        if b >= sub_bits + word_bits:
            on = ((nkb - 1 - blk) >> (b - sub_bits - word_bits)) & 1
            planes.append(jnp.where(on == 1, -1, 0))
        elif b >= sub_bits:
            t = b - sub_bits
            pattern = sum(1 << k for k in range(KEY_BITS) if (k >> t) & 1)
            planes.append(jnp.full(shape, np.int32(np.uint32(pattern)), I32))
        else:
            on = ((SUBLANES - 1 - sub) >> b) & 1
            planes.append(jnp.where(on == 1, -1, 0))
    return planes


def _bias_kind(j, i):
    return jnp.clip(j - i + 2, 0, 2)


def _stream_key_blocks(i, n_chains, s_ref, acc_ref, logits, v_aug):
    chains = range(n_chains)

    def stage(j, c, kind):
        s = logits(j, c, kind)
        s_ref[c] = s
        return jnp.max(s, axis=0, keepdims=True)

    def absorb(j, c, m_old, block_max):
        m_new = jnp.maximum(m_old, block_max)
        alpha = jnp.exp2(m_old - m_new)
        p = jnp.exp2(s_ref[c] - m_new).astype(BF16)
        acc_ref[c] = acc_ref[c] * alpha + _dot(v_aug(j, c), p)
        return m_new

    def step(j, carry, kind):
        ms, bms = carry
        out = [(absorb(j, c, ms[c], bms[c]), stage(j + 1, c, kind)) for c in chains]
        return tuple(o[0] for o in out), tuple(o[1] for o in out)

    acc_ref[...] = jnp.zeros_like(acc_ref)
    carry = ((jnp.full((1, TQ), NEG, F32),) * n_chains, tuple(stage(0, c, 'any') for c in chains))
    n_far = jnp.maximum(i - 2, 0)
    carry = lax.fori_loop(0, n_far, lambda j, cr: step(j, cr, 'far'), carry)
    ms, bms = lax.fori_loop(n_far, i, lambda j, cr: step(j, cr, 'near'), carry)
    for c in chains:
        absorb(i, c, ms[c], bms[c])


def _attn_in_kernel(x_ref, g_ref, w_ref, wt_ref, proj_ref, avt_ref, bvt_ref, iwt_ref, *, tm):
    hb = _rms(x_ref[0], g_ref[...]).astype(BF16)
    proj_ref[0] = _dot(hb, w_ref[...]).astype(BF16)
    t = _dot_nt(wt_ref[...], hb)
    for c in range(tm // TK):
        avt_ref[0, c] = t[ROW_AV:ROW_BV, c * TK:(c + 1) * TK].astype(BF16)
        bvt_ref[0, c] = t[ROW_BV:ROW_IW, c * TK:(c + 1) * TK].astype(BF16)
    iwt_ref[0] = t[ROW_IW:ROW_IW + IDX_HEADS, :] * (IDX_HEADS ** -0.5 * IDX_DIM ** -0.5)


def _attn_in(x, g, w, wt, *, tm=512):
    bsz, s, d = x.shape
    av_rows = ROW_BV - ROW_AV
    return pl.pallas_call(
        functools.partial(_attn_in_kernel, tm=tm),
        grid=(bsz, s // tm),
        in_specs=[
            pl.BlockSpec((1, tm, d), lambda b, i: (b, i, 0)),
            pl.BlockSpec((1, d), lambda b, i: (0, 0)),
            pl.BlockSpec((d, PROJ_W), lambda b, i: (0, 0)),
            pl.BlockSpec((VT_ROWS, d), lambda b, i: (0, 0)),
        ],
        out_specs=[
            pl.BlockSpec((1, tm, PROJ_W), lambda b, i: (b, i, 0)),
            pl.BlockSpec((1, tm // TK, av_rows, TK), lambda b, i: (b, i, 0, 0)),
            pl.BlockSpec((1, tm // TK, B_HEAD_DIM, TK), lambda b, i: (b, i, 0, 0)),
            pl.BlockSpec((1, IDX_HEADS, tm), lambda b, i: (b, 0, i)),
        ],
        out_shape=[
            jax.ShapeDtypeStruct((bsz, s, PROJ_W), BF16),
            jax.ShapeDtypeStruct((bsz, s // TK, av_rows, TK), BF16),
            jax.ShapeDtypeStruct((bsz, s // TK, B_HEAD_DIM, TK), BF16),
            jax.ShapeDtypeStruct((bsz, IDX_HEADS, s), F32),
        ],
        compiler_params=pltpu.CompilerParams(
            dimension_semantics=("arbitrary", "arbitrary"), vmem_limit_bytes=VMEM_LIMIT),
        name="attn_in",
    )(x, g, w, wt)


def _diff_attn_kernel(lam_ref, q_ref, k_ref, vt_ref, bias_ref, g_ref, o_ref,
                      vaug_ref, s_ref, acc_ref, ot_ref, *, lambda_init, seq):
    i = pl.program_id(1)
    nkb = seq // TK
    n_maps = 2 * A_HEADS

    @pl.when(i == 0)
    def _():
        for c in range(nkb):
            for h in range(A_HEADS):
                vaug_ref[h, c, 0:A_V_DIM, :] = vt_ref[0, c, h * A_V_DIM:(h + 1) * A_V_DIM, :]
                vaug_ref[h, c, A_V_DIM:A_V_DIM + ONES_ROWS, :] = jnp.ones((ONES_ROWS, TK), BF16)

    lv = lam_ref[...]
    lam = (jnp.exp(jnp.sum(lv[0:1] * lv[1:2], axis=-1, keepdims=True))
           - jnp.exp(jnp.sum(lv[2:3] * lv[3:4], axis=-1, keepdims=True)) + lambda_init)

    q = q_ref[0]
    qm = []
    for h in range(A_HEADS):
        qh = q[:, h * LANES:(h + 1) * LANES]
        qm += [_lane_band(qh, 0, A_HEAD_DIM), _lane_band(qh, A_HEAD_DIM, A_HEAD_DIM)]

    def logits(j, c, kind):
        h = c // 2
        kb = k_ref[0, pl.ds(pl.multiple_of(j * TK, TK), TK), h * LANES:(h + 1) * LANES]
        s = _dot_nt(kb, qm[c])
        return s if kind == 'far' else s + bias_ref[_bias_kind(j, i), h]

    _stream_key_blocks(i, n_maps, s_ref, acc_ref, logits, lambda j, c: vaug_ref[c // 2, j])

    for h in range(A_HEADS):
        a1 = acc_ref[2 * h]
        a2 = acc_ref[2 * h + 1]
        a = (a1[0:A_V_DIM] / a1[A_V_DIM:A_V_DIM + 1]
             - lam * (a2[0:A_V_DIM] / a2[A_V_DIM:A_V_DIM + 1]))
        y = a * lax.rsqrt(jnp.mean(a * a, axis=0, keepdims=True) + EPS) * g_ref[...]
        ot_ref[h * A_V_DIM:(h + 1) * A_V_DIM, :] = y * (1.0 - lambda_init)
    o_ref[0] = ot_ref[...].T.astype(BF16)


def _diff_attn(proj, avt, lam_vecs, bias, subln_g_col, *, lambda_init):
    bsz, s, _ = proj.shape
    nq = s // TQ
    nkb = s // TK
    width = A_HEADS * A_V_DIM
    return pl.pallas_call(
        functools.partial(_diff_attn_kernel, lambda_init=lambda_init, seq=s),
        grid=(bsz, nq),
        in_specs=[
            pl.BlockSpec((4, A_HEAD_DIM), lambda b, i: (0, 0)),
            pl.BlockSpec((1, TQ, width), lambda b, i: (b, i, COL_AQ // width)),
            pl.BlockSpec((1, s, width), lambda b, i: (b, 0, COL_AK // width)),
            pl.BlockSpec((1, nkb, width, TK), lambda b, i: (b, 0, 0, 0)),
            pl.BlockSpec((3, A_HEADS, TK, TQ), lambda b, i: (0, 0, 0, 0)),
            pl.BlockSpec((A_V_DIM, 1), lambda b, i: (0, 0)),
        ],
        out_specs=pl.BlockSpec((1, TQ, width), lambda b, i: (b, i, 0)),
        out_shape=jax.ShapeDtypeStruct((bsz, s, width), BF16),
        scratch_shapes=[
            pltpu.VMEM((A_HEADS, nkb, A_V_DIM + ONES_ROWS, TK), BF16),
            pltpu.VMEM((2 * A_HEADS, TK, TQ), F32),
            pltpu.VMEM((2 * A_HEADS, A_V_DIM + ONES_ROWS, TQ), F32),
            pltpu.VMEM((width, TQ), F32),
        ],
        compiler_params=pltpu.CompilerParams(
            dimension_semantics=("arbitrary", "arbitrary"), vmem_limit_bytes=VMEM_LIMIT),
        name="diff_attn",
    )(lam_vecs, proj, proj, avt, bias, subln_g_col)


def _dsa_kernel(qb_ref, iq_ref, kb_ref, ik_ref, vt_ref, iwt_ref, bias_ref, o_ref,
                key_ref, plane_ref, mask_ref, vaug_ref, s_ref, acc_ref, ot_ref, *, top_k, seq):
    i = pl.program_id(1)
    nkb = seq // TK

    @pl.when(i == 0)
    def _():
        for c in range(nkb):
            vaug_ref[c, 0:B_HEAD_DIM, :] = vt_ref[0, c]
            vaug_ref[c, B_HEAD_DIM:B_HEAD_DIM + ONES_ROWS, :] = jnp.ones((ONES_ROWS, TK), BF16)

    @pl.when((i == 0) & (pl.program_id(0) == 0))
    def _():
        plane_ref[0:KEY_BITS] = jnp.zeros((KEY_BITS, nkb, SUBLANES, TQ), I32)
        for t, plane in enumerate(_index_planes(nkb)):
            plane_ref[KEY_BITS + t] = plane

    def rows(ref, j):
        return ref[0, pl.ds(pl.multiple_of(j * TK, TK), TK), :]

    krow = lax.broadcasted_iota(I32, (TK, TQ), 0)
    qcol = lax.broadcasted_iota(I32, (TK, TQ), 1)

    iq = iq_ref[0]
    per_blk = LANES // IDX_DIM
    iqh = [_lane_band(iq[:, (h // per_blk) * LANES:(h // per_blk + 1) * LANES],
                      (h % per_blk) * IDX_DIM, IDX_DIM) for h in range(IDX_HEADS)]
    w = iwt_ref[0]

    def score_keys(j):
        ikb = rows(ik_ref, j)
        acc = jnp.zeros((TK, TQ), F32)
        for h in range(IDX_HEADS):
            acc = acc + jnp.maximum(_dot_nt(ikb, iqh[h]), 0.0) * w[h:h + 1, :]
        acc = jnp.where(acc == 0.0, 0.0, acc)
        bits = lax.bitcast_convert_type(acc, I32)
        return bits ^ ((bits >> 31) & 0x7FFFFFFF)

    def store_keys(j, keys):
        key_ref[j] = keys
        for p, plane in enumerate(_bit_planes(keys)):
            plane_ref[p, j] = plane

    def score_body(j, c):
        store_keys(j, score_keys(j))
        return c

    lax.fori_loop(0, i, score_body, 0)
    admissible = (krow // CHUNK) <= (qcol // CHUNK)
    store_keys(i, jnp.where(admissible, score_keys(i), KEY_NEG_INF))

    blk = lax.broadcasted_iota(I32, (nkb, SUBLANES, TQ), 0)
    alive0 = jnp.where(blk <= i, -1, 0)

    def bit_body(p, carry, n_bits):
        alive, n_gt, thr_u = carry
        hit = alive & plane_ref[p]
        pc = lax.population_count(hit).astype(F32)
        n_hit = jnp.sum(jnp.sum(pc, axis=0), axis=0, keepdims=True)
        take = (n_gt + n_hit) >= top_k
        take_m = jnp.where(take, -1, 0)
        thr_u = thr_u | (take_m & lax.shift_left(jnp.int32(1), n_bits - 1 - p))
        n_gt = n_gt + jnp.where(take, 0.0, n_hit)
        alive = (alive ^ hit) ^ (alive & take_m)
        return alive, n_gt, thr_u

    zero = jnp.zeros((1, TQ), I32)
    alive, n_gt, thr_u = lax.fori_loop(
        0, KEY_BITS, lambda p, c: bit_body(p, c, KEY_BITS), (alive0, jnp.zeros((1, TQ), F32), zero))
    idx_bits = _index_bits(seq)
    _, _, rev_idx = lax.fori_loop(
        KEY_BITS, KEY_BITS + idx_bits,
        lambda p, c: bit_body(p, c, KEY_BITS + idx_bits), (alive, n_gt, zero))
    thr = thr_u ^ INT_MIN
    lim = seq - rev_idx

    def mask_body(j, c):
        kb = key_ref[j]
        tie_ok = jnp.where((krow + j * TK) < lim, 0.0, NEG)
        sel = jnp.where(kb > thr, 0.0, jnp.where(kb == thr, tie_ok, NEG))
        mask_ref[j] = jnp.where(kb > KEY_NEG_INF, sel, NEG)
        return c

    lax.fori_loop(0, i + 1, mask_body, 0)

    qb = qb_ref[0]
    qh = [_lane_band(qb[:, (h // 2) * LANES:(h // 2 + 1) * LANES],
                     (h % 2) * B_HEAD_DIM, B_HEAD_DIM) for h in range(B_HEADS)]

    def logits(j, h, kind):
        s = _dot_nt(rows(kb_ref, j), qh[h]) + mask_ref[j]
        return s if kind == 'far' else s + bias_ref[_bias_kind(j, i), h]

    _stream_key_blocks(i, B_HEADS, s_ref, acc_ref, logits, lambda j, h: vaug_ref[j])

    for h in range(B_HEADS):
        a = acc_ref[h]
        ot_ref[h * B_HEAD_DIM:(h + 1) * B_HEAD_DIM, :] = (
            a[0:B_HEAD_DIM] / a[B_HEAD_DIM:B_HEAD_DIM + 1])
    o_ref[0] = ot_ref[...].T.astype(BF16)


def _dsa_attn(proj, bvt, iwt, bias, *, top_k):
    bsz, s, _ = proj.shape
    nq = s // TQ
    nkb = s // TK
    width = B_HEADS * B_HEAD_DIM
    iq_w = IDX_HEADS * IDX_DIM
    return pl.pallas_call(
        functools.partial(_dsa_kernel, top_k=top_k, seq=s),
        grid=(bsz, nq),
        in_specs=[
            pl.BlockSpec((1, TQ, width), lambda b, i: (b, i, COL_BQ // width)),
            pl.BlockSpec((1, TQ, iq_w), lambda b, i: (b, i, COL_IQ // iq_w)),
            pl.BlockSpec((1, s, LANES), lambda b, i: (b, 0, COL_KB // LANES)),
            pl.BlockSpec((1, s, LANES), lambda b, i: (b, 0, COL_IK // LANES)),
            pl.BlockSpec((1, nkb, B_HEAD_DIM, TK), lambda b, i: (b, 0, 0, 0)),
            pl.BlockSpec((1, IDX_HEADS, TQ), lambda b, i: (b, 0, i)),
            pl.BlockSpec((3, B_HEADS, TK, TQ), lambda b, i: (0, 0, 0, 0)),
        ],
        out_specs=pl.BlockSpec((1, TQ, width), lambda b, i: (b, i, 0)),
        out_shape=jax.ShapeDtypeStruct((bsz, s, width), BF16),
        scratch_shapes=[
            pltpu.VMEM((nkb, TK, TQ), I32),
            pltpu.VMEM((KEY_BITS + _index_bits(s), nkb, SUBLANES, TQ), I32),
            pltpu.VMEM((nkb, TK, TQ), F32),
            pltpu.VMEM((nkb, B_HEAD_DIM + ONES_ROWS, TK), BF16),
            pltpu.VMEM((B_HEADS, TK, TQ), F32),
            pltpu.VMEM((B_HEADS, B_HEAD_DIM + ONES_ROWS, TQ), F32),
            pltpu.VMEM((width, TQ), F32),
        ],
        compiler_params=pltpu.CompilerParams(
            dimension_semantics=("arbitrary", "arbitrary"), vmem_limit_bytes=VMEM_LIMIT),
        name="dsa_attn",
    )(proj, proj, proj, proj, bvt, iwt, bias)


def _attn_out_kernel(ya_ref, yb_ref, w_ref, x_ref, g_ref, o_ref):
    half = ya_ref.shape[-1]
    m = _dot(ya_ref[...], w_ref[0:half, :]) + _dot(yb_ref[...], w_ref[half:2 * half, :])
    o_ref[...] = x_ref[...] + _rms(m, g_ref[...])


def _attn_out(ya, yb, w, x, g, *, tm=512):
    n, d = x.shape
    half = ya.shape[-1]
    return pl.pallas_call(
        _attn_out_kernel,
        grid=(n // tm,),
        in_specs=[
            pl.BlockSpec((tm, half), lambda i: (i, 0)),
            pl.BlockSpec((tm, half), lambda i: (i, 0)),
            pl.BlockSpec((2 * half, d), lambda i: (0, 0)),
            pl.BlockSpec((tm, d), lambda i: (i, 0)),
            pl.BlockSpec((1, d), lambda i: (0, 0)),
        ],
        out_specs=pl.BlockSpec((tm, d), lambda i: (i, 0)),
        out_shape=jax.ShapeDtypeStruct((n, d), F32),
        compiler_params=pltpu.CompilerParams(
            dimension_semantics=("arbitrary",), vmem_limit_bytes=VMEM_LIMIT),
        name="attn_out",
    )(ya, yb, w, x, g)


def _mlp_kernel(x_ref, g_in_ref, wu_ref, wd_ref, g_out_ref, o_ref, h_ref, acc_ref):
    f = pl.program_id(1)

    @pl.when(f == 0)
    def _():
        h_ref[...] = _rms(x_ref[...], g_in_ref[...]).astype(BF16)
        acc_ref[...] = jnp.zeros_like(acc_ref)

    u = jnp.maximum(_dot(h_ref[...], wu_ref[...]), 0.0)
    acc_ref[...] += _dot((u * u).astype(BF16), wd_ref[...])

    @pl.when(f == pl.num_programs(1) - 1)
    def _():
        o_ref[...] = x_ref[...] + _rms(acc_ref[...], g_out_ref[...])


def _mlp(x, g_in, wu, wd, g_out, *, tm=1024, tf=1024):
    n, d = x.shape
    ff = wu.shape[1]
    return pl.pallas_call(
        _mlp_kernel,
        grid=(n // tm, ff // tf),
        in_specs=[
            pl.BlockSpec((tm, d), lambda i, f: (i, 0)),
            pl.BlockSpec((1, d), lambda i, f: (0, 0)),
            pl.BlockSpec((d, tf), lambda i, f: (0, f)),
            pl.BlockSpec((tf, d), lambda i, f: (f, 0)),
            pl.BlockSpec((1, d), lambda i, f: (0, 0)),
        ],
        out_specs=pl.BlockSpec((tm, d), lambda i, f: (i, 0)),
        out_shape=jax.ShapeDtypeStruct((n, d), F32),
        scratch_shapes=[pltpu.VMEM((tm, d), BF16), pltpu.VMEM((tm, d), F32)],
        compiler_params=pltpu.CompilerParams(
            dimension_semantics=("arbitrary", "arbitrary"), vmem_limit_bytes=VMEM_LIMIT),
        name="mlp",
    )(x, g_in, wu, wd, g_out)


U_HIST = 32
Z_HIST = 8


def _conv_kernel(x_ref, g_in_ref, w_in_ref, dw_w_ref, dw_b_ref, ln_g_ref, ln_b_ref, sc_w_ref,
                 w_out_ref, g_out_ref, o_ref, ubuf_ref, zbuf_ref, *, ts):
    @pl.when(pl.program_id(1) == 0)
    def _():
        ubuf_ref[0:U_HIST, :] = jnp.zeros((U_HIST, CONV_CH), F32)
        zbuf_ref[0:Z_HIST, :] = jnp.zeros((Z_HIST, SC_CH), F32)

    x = x_ref[0]
    hb = _rms(x, g_in_ref[...]).astype(BF16)
    c = CONV_CH
    ca = _dot(hb, w_in_ref[:, 0:c])
    cg = _dot(hb, w_in_ref[:, c:2 * c])
    ubuf_ref[U_HIST:U_HIST + ts, :] = ca * jax.nn.sigmoid(cg)
    dc = _dot(hb, w_in_ref[:, 3 * c:4 * c])
    dh = _dot(hb, w_in_ref[:, 4 * c:5 * c])
    zbuf_ref[Z_HIST:Z_HIST + ts, :] = dc * dh

    u = jnp.zeros((ts, c), F32)
    for j in range(CONV_WIDTH):
        off = U_HIST - (CONV_WIDTH - 1) + j
        u = u + dw_w_ref[j:j + 1, :] * ubuf_ref[off:off + ts, :]
    u = u + dw_b_ref[...]
    mu = jnp.mean(u, axis=-1, keepdims=True)
    ctr = u - mu
    var = jnp.mean(ctr * ctr, axis=-1, keepdims=True)
    u = ctr * lax.rsqrt(var + EPS) * ln_g_ref[...] + ln_b_ref[...]
    u = u * jax.nn.sigmoid(u)

    z = jnp.zeros((ts, c), F32)
    for j in range(SC_WIDTH):
        off = Z_HIST - (SC_WIDTH - 1) + j
        z = z + sc_w_ref[j:j + 1, :] * zbuf_ref[off:off + ts, :]
    z = _dot(hb, w_in_ref[:, 2 * c:3 * c]) * z

    ubuf_ref[0:U_HIST, :] = ubuf_ref[ts:ts + U_HIST, :]
    zbuf_ref[0:Z_HIST, :] = zbuf_ref[ts:ts + Z_HIST, :]

    m = _dot(u.astype(BF16), w_out_ref[0:c, :]) + _dot(z.astype(BF16), w_out_ref[c:2 * c, :])
    o_ref[0] = x + _rms(m, g_out_ref[...])


def _conv_mixer(x, g_in, w_in, dw_w, dw_b, ln_g, ln_b, sc_w, w_out, g_out, *, ts=512):
    bsz, s, d = x.shape
    full = lambda a: pl.BlockSpec(a.shape, lambda b, i: (0,) * a.ndim)
    return pl.pallas_call(
        functools.partial(_conv_kernel, ts=ts),
        grid=(bsz, s // ts),
        in_specs=[pl.BlockSpec((1, ts, d), lambda b, i: (b, i, 0)),
                  full(g_in), full(w_in), full(dw_w), full(dw_b), full(ln_g), full(ln_b),
                  full(sc_w), full(w_out), full(g_out)],
        out_specs=pl.BlockSpec((1, ts, d), lambda b, i: (b, i, 0)),
        out_shape=jax.ShapeDtypeStruct((bsz, s, d), F32),
        scratch_shapes=[pltpu.VMEM((U_HIST + ts, CONV_CH), F32),
                        pltpu.VMEM((Z_HIST + ts, SC_CH), F32)],
        compiler_params=pltpu.CompilerParams(
            dimension_semantics=("arbitrary", "arbitrary"), vmem_limit_bytes=VMEM_LIMIT),
        name="conv_mixer",
    )(x, g_in, w_in, dw_w, dw_b, ln_g, ln_b, sc_w, w_out, g_out)


def _t5_bucket(rel):
    nb = NUM_BUCKETS // 2
    ret = jnp.where(rel > 0, nb, 0)
    n = jnp.abs(rel)
    max_exact = nb // 2
    nf = jnp.maximum(n, 1).astype(jnp.float32)
    large = max_exact + (jnp.log(nf / max_exact) / math.log(MAX_DISTANCE / max_exact)
                         * (nb - max_exact)).astype(jnp.int32)
    large = jnp.minimum(large, nb - 1)
    return ret + jnp.where(n < max_exact, n, large)


def _toeplitz(f, off):
    heads, length = f.shape
    g = jnp.tile(jnp.pad(f, ((0, 0), (0, 1))), (1, TQ))[:, :TQ * length]
    return g.reshape(heads, TQ, length)[:, :, off:off + TK]


def _bias_tables(rel_bias):
    lo = -(TQ + TK - 1)
    rel = jnp.arange(lo, TK, dtype=jnp.int32)
    far_bucket = NUM_BUCKETS // 2 - 1
    f = (rel_bias[_t5_bucket(rel)].astype(F32) - rel_bias[far_bucket].astype(F32)[None, :]).T
    f = f * LOG2E
    diag = jnp.swapaxes(_toeplitz(f, -lo), 1, 2)
    prev = jnp.swapaxes(_toeplitz(f, -lo - TK), 1, 2)
    ki = np.arange(TK)[:, None]
    qi = np.arange(TQ)[None, :]
    adm = jnp.asarray((ki // CHUNK) <= (qi // CHUNK))
    none = jnp.zeros_like(diag)
    a_diag = jnp.where(adm[None], diag[:A_HEADS], NEG)
    a_bias = jnp.stack([none[:A_HEADS], prev[:A_HEADS], a_diag], axis=0)
    b_bias = jnp.stack([none[A_HEADS:], prev[A_HEADS:], diag[A_HEADS:]], axis=0)
    return a_bias, b_bias


def _attn_weights(w_in):
    o = np.cumsum([0, 512, 512, 512, 512, 64, 64, 256, 32, 8])
    aq, ak, av, bq, bk, bv, iq, ik, iw = [w_in[:, o[n]:o[n + 1]] for n in range(9)]
    aq = aq * (A_HEAD_DIM ** -0.5 * LOG2E)
    bq = bq * (B_HEAD_DIM ** -0.5 * LOG2E)
    w = jnp.concatenate([aq, ak, bq, iq, bk, bk, ik, ik, ik, ik], axis=1).astype(BF16)
    pad = jnp.zeros((w_in.shape[0], VT_ROWS - ROW_IW - IDX_HEADS), w_in.dtype)
    wt = jnp.concatenate([av, bv, iw, pad], axis=1).T.astype(BF16)
    return w, wt


def kernel(x, rel_bias, norm_g, w_mlp_up, w_mlp_down, attn_w_in, attn_w_out, diff_lambda,
           diff_subln_g, conv_w_in, conv_w_out, conv_dw_w, conv_dw_b, conv_ln_g, conv_ln_b,
           sconv_w):
    bsz, s, d = x.shape
    depth = norm_g.shape[0]
    top_k = min(TOPK_MAX, s // 4)
    row = lambda v: v.reshape(1, -1)
    a_bias, b_bias = _bias_tables(rel_bias)
    for i in range(depth):
        j = i // 2
        if i % 2 == 0:
            lambda_init = 0.8 - 0.6 * math.exp(-0.3 * i)
            w, wt = _attn_weights(attn_w_in[j])
            proj, avt, bvt, iwt = _attn_in(x.reshape(bsz, s, d), row(norm_g[i, 0]), w, wt)
            ya = _diff_attn(proj, avt, diff_lambda[j], a_bias, diff_subln_g[j].reshape(-1, 1),
                            lambda_init=lambda_init)
            yb = _dsa_attn(proj, bvt, iwt, b_bias, top_k=top_k)
            x = _attn_out(ya.reshape(bsz * s, -1), yb.reshape(bsz * s, -1),
                          attn_w_out[j].astype(BF16), x.reshape(bsz * s, d),
                          row(norm_g[i, 1]))
        else:
            x = _conv_mixer(x.reshape(bsz, s, d), row(norm_g[i, 0]), conv_w_in[j].astype(BF16),
                            conv_dw_w[j], row(conv_dw_b[j]), row(conv_ln_g[j]),
                            row(conv_ln_b[j]), sconv_w[j], conv_w_out[j].astype(BF16),
                            row(norm_g[i, 1])).reshape(bsz * s, d)
        x = _mlp(x, row(norm_g[i, 2]), w_mlp_up[i].astype(BF16), w_mlp_down[i].astype(BF16),
                 row(norm_g[i, 3]))
    return x.reshape(bsz, s, d)
```

```python
import functools
import math

import numpy as np
import jax
import jax.numpy as jnp
from jax import lax
from jax.experimental import pallas as pl
from jax.experimental.pallas import tpu as pltpu

F32 = jnp.float32
BF16 = jnp.bfloat16
I32 = jnp.int32

D_MODEL = 1024
CHUNK = 64
NUM_BUCKETS = 32
MAX_DISTANCE = 128
EPS = 1e-6
NEG = -1e30
LOG2E = math.log2(math.e)
A_HEADS = 4
A_HEAD_DIM = 64
A_V_DIM = 2 * A_HEAD_DIM
B_HEADS = 8
B_HEAD_DIM = 64
IDX_HEADS = 8
IDX_DIM = 32
TOPK_MAX = 256
CONV_CH = 512
CONV_WIDTH = 31
SC_CH = 512
SC_WIDTH = 3
D_FF = 4 * D_MODEL

LANES = 128
SUBLANES = 8
VMEM_LIMIT = 56 * 1024 * 1024

TQ = 256
TK = 256
COL_AQ, COL_AK, COL_BQ, COL_IQ, COL_KB, COL_IK, PROJ_W = 0, 512, 1024, 1536, 1792, 1920, 2048
ROW_AV, ROW_BV, ROW_IW, VT_ROWS = 0, 512, 576, 640
ONES_ROWS = 16

INT_MIN = -2 ** 31
KEY_NEG_INF = INT_MIN + 0x7FFFFF


def _rms(x, g):
    return x * lax.rsqrt(jnp.mean(x * x, axis=-1, keepdims=True) + EPS) * g


def _dot_nt(a, b):
    return lax.dot_general(a, b, (((1,), (1,)), ((), ())), preferred_element_type=F32)


def _dot(a, b):
    return jnp.dot(a, b, preferred_element_type=F32)


def _lane_band(x, lo, width):
    lane = lax.broadcasted_iota(I32, x.shape, 1)
    return jnp.where((lane >= lo) & (lane < lo + width), x, jnp.zeros_like(x))


KEY_BITS = 32


def _bit_planes(keys):
    assert keys.shape[0] == KEY_BITS * SUBLANES
    u = keys ^ INT_MIN
    a = [u[SUBLANES * r:SUBLANES * (r + 1), :] for r in range(KEY_BITS)]
    j, m = KEY_BITS // 2, (1 << (KEY_BITS // 2)) - 1
    while j:
        mask = np.int32(np.uint32(m))
        k = 0
        while k < KEY_BITS:
            t = (a[k] ^ lax.shift_right_logical(a[k + j], jnp.int32(j))) & mask
            a[k] = a[k] ^ t
            a[k + j] = a[k + j] ^ lax.shift_left(t, jnp.int32(j))
            k = (k + j + 1) & ~j
        j >>= 1
        m = (m ^ (m << j)) & 0xFFFFFFFF
    return a


def _index_bits(seq):
    bits = seq.bit_length() - 1
    assert seq == 1 << bits and seq >= KEY_BITS * SUBLANES
    return bits


def _index_planes(nkb):
    shape = (nkb, SUBLANES, TQ)
    blk = lax.broadcasted_iota(I32, shape, 0)
    sub = lax.broadcasted_iota(I32, shape, 1)
    sub_bits = SUBLANES.bit_length() - 1
    word_bits = KEY_BITS.bit_length() - 1
    planes = []
    for b in range(_index_bits(nkb * TK) - 1, -1, -1):
        if b >= sub_bits + word_bits:
            on = ((nkb - 1 - blk) >> (b - sub_bits - word_bits)) & 1
            planes.append(jnp.where(on == 1, -1, 0))
        elif b >= sub_bits:
            t = b - sub_bits
            pattern = sum(1 << k for k in range(KEY_BITS) if (k >> t) & 1)
            planes.append(jnp.full(shape, np.int32(np.uint32(pattern)), I32))
        else:
            on = ((SUBLANES - 1 - sub) >> b) & 1
            planes.append(jnp.where(on == 1, -1, 0))
    return planes


def _bias_kind(j, i):
    return jnp.clip(j - i + 2, 0, 2)


def _stream_key_blocks(i, n_chains, s_ref, acc_ref, logits, v_aug):
    chains = range(n_chains)

    def stage(j, c, kind):
        s = logits(j, c, kind)
        s_ref[c] = s
        return jnp.max(s, axis=0, keepdims=True)

    def absorb(j, c, m_old, block_max):
        m_new = jnp.maximum(m_old, block_max)
        alpha = jnp.exp2(m_old - m_new)
        p = jnp.exp2(s_ref[c] - m_new).astype(BF16)
        acc_ref[c] = acc_ref[c] * alpha + _dot(v_aug(j, c), p)
        return m_new

    def step(j, carry, kind):
        ms, bms = carry
        out = [(absorb(j, c, ms[c], bms[c]), stage(j + 1, c, kind)) for c in chains]
        return tuple(o[0] for o in out), tuple(o[1] for o in out)

    acc_ref[...] = jnp.zeros_like(acc_ref)
    carry = ((jnp.full((1, TQ), NEG, F32),) * n_chains, tuple(stage(0, c, 'any') for c in chains))
    n_far = jnp.maximum(i - 2, 0)
    carry = lax.fori_loop(0, n_far, lambda j, cr: step(j, cr, 'far'), carry)
    ms, bms = lax.fori_loop(n_far, i, lambda j, cr: step(j, cr, 'near'), carry)
    for c in chains:
        absorb(i, c, ms[c], bms[c])


def _attn_in_kernel(x_ref, g_ref, w_ref, wt_ref, proj_ref, avt_ref, bvt_ref, iwt_ref, *, tm):
    hb = _rms(x_ref[0], g_ref[...]).astype(BF16)
    proj_ref[0] = _dot(hb, w_ref[...]).astype(BF16)
    t = _dot_nt(wt_ref[...], hb)
    for c in range(tm // TK):
        avt_ref[0, c] = t[ROW_AV:ROW_BV, c * TK:(c + 1) * TK].astype(BF16)
        bvt_ref[0, c] = t[ROW_BV:ROW_IW, c * TK:(c + 1) * TK].astype(BF16)
    iwt_ref[0] = t[ROW_IW:ROW_IW + IDX_HEADS, :] * (IDX_HEADS ** -0.5 * IDX_DIM ** -0.5)


def _attn_in(x, g, w, wt, *, tm=512):
    bsz, s, d = x.shape
    av_rows = ROW_BV - ROW_AV
    return pl.pallas_call(
        functools.partial(_attn_in_kernel, tm=tm),
        grid=(bsz, s // tm),
        in_specs=[
            pl.BlockSpec((1, tm, d), lambda b, i: (b, i, 0)),
            pl.BlockSpec((1, d), lambda b, i: (0, 0)),
            pl.BlockSpec((d, PROJ_W), lambda b, i: (0, 0)),
            pl.BlockSpec((VT_ROWS, d), lambda b, i: (0, 0)),
        ],
        out_specs=[
            pl.BlockSpec((1, tm, PROJ_W), lambda b, i: (b, i, 0)),
            pl.BlockSpec((1, tm // TK, av_rows, TK), lambda b, i: (b, i, 0, 0)),
            pl.BlockSpec((1, tm // TK, B_HEAD_DIM, TK), lambda b, i: (b, i, 0, 0)),
            pl.BlockSpec((1, IDX_HEADS, tm), lambda b, i: (b, 0, i)),
        ],
        out_shape=[
            jax.ShapeDtypeStruct((bsz, s, PROJ_W), BF16),
            jax.ShapeDtypeStruct((bsz, s // TK, av_rows, TK), BF16),
            jax.ShapeDtypeStruct((bsz, s // TK, B_HEAD_DIM, TK), BF16),
            jax.ShapeDtypeStruct((bsz, IDX_HEADS, s), F32),
        ],
        compiler_params=pltpu.CompilerParams(
            dimension_semantics=("arbitrary", "arbitrary"), vmem_limit_bytes=VMEM_LIMIT),
        name="attn_in",
    )(x, g, w, wt)


def _diff_attn_kernel(lam_ref, q_ref, k_ref, vt_ref, bias_ref, g_ref, o_ref,
                      vaug_ref, s_ref, acc_ref, ot_ref, *, lambda_init, seq):
    i = pl.program_id(1)
    nkb = seq // TK
    n_maps = 2 * A_HEADS

    @pl.when(i == 0)
    def _():
        for c in range(nkb):
            for h in range(A_HEADS):
                vaug_ref[h, c, 0:A_V_DIM, :] = vt_ref[0, c, h * A_V_DIM:(h + 1) * A_V_DIM, :]
                vaug_ref[h, c, A_V_DIM:A_V_DIM + ONES_ROWS, :] = jnp.ones((ONES_ROWS, TK), BF16)

    lv = lam_ref[...]
    lam = (jnp.exp(jnp.sum(lv[0:1] * lv[1:2], axis=-1, keepdims=True))
           - jnp.exp(jnp.sum(lv[2:3] * lv[3:4], axis=-1, keepdims=True)) + lambda_init)

    q = q_ref[0]
    qm = []
    for h in range(A_HEADS):
        qh = q[:, h * LANES:(h + 1) * LANES]
        qm += [_lane_band(qh, 0, A_HEAD_DIM), _lane_band(qh, A_HEAD_DIM, A_HEAD_DIM)]

    def logits(j, c, kind):
        h = c // 2
        kb = k_ref[0, pl.ds(pl.multiple_of(j * TK, TK), TK), h * LANES:(h + 1) * LANES]
        s = _dot_nt(kb, qm[c])
        return s if kind == 'far' else s + bias_ref[_bias_kind(j, i), h]

    _stream_key_blocks(i, n_maps, s_ref, acc_ref, logits, lambda j, c: vaug_ref[c // 2, j])

    for h in range(A_HEADS):
        a1 = acc_ref[2 * h]
        a2 = acc_ref[2 * h + 1]
        a = (a1[0:A_V_DIM] / a1[A_V_DIM:A_V_DIM + 1]
             - lam * (a2[0:A_V_DIM] / a2[A_V_DIM:A_V_DIM + 1]))
        y = a * lax.rsqrt(jnp.mean(a * a, axis=0, keepdims=True) + EPS) * g_ref[...]
        ot_ref[h * A_V_DIM:(h + 1) * A_V_DIM, :] = y * (1.0 - lambda_init)
    o_ref[0] = ot_ref[...].T.astype(BF16)


def _diff_attn(proj, avt, lam_vecs, bias, subln_g_col, *, lambda_init):
    bsz, s, _ = proj.shape
    nq = s // TQ
    nkb = s // TK
    width = A_HEADS * A_V_DIM
    return pl.pallas_call(
        functools.partial(_diff_attn_kernel, lambda_init=lambda_init, seq=s),
        grid=(bsz, nq),
        in_specs=[
            pl.BlockSpec((4, A_HEAD_DIM), lambda b, i: (0, 0)),
            pl.BlockSpec((1, TQ, width), lambda b, i: (b, i, COL_AQ // width)),
            pl.BlockSpec((1, s, width), lambda b, i: (b, 0, COL_AK // width)),
            pl.BlockSpec((1, nkb, width, TK), lambda b, i: (b, 0, 0, 0)),
            pl.BlockSpec((3, A_HEADS, TK, TQ), lambda b, i: (0, 0, 0, 0)),
            pl.BlockSpec((A_V_DIM, 1), lambda b, i: (0, 0)),
        ],
        out_specs=pl.BlockSpec((1, TQ, width), lambda b, i: (b, i, 0)),
        out_shape=jax.ShapeDtypeStruct((bsz, s, width), BF16),
        scratch_shapes=[
            pltpu.VMEM((A_HEADS, nkb, A_V_DIM + ONES_ROWS, TK), BF16),
            pltpu.VMEM((2 * A_HEADS, TK, TQ), F32),
            pltpu.VMEM((2 * A_HEADS, A_V_DIM + ONES_ROWS, TQ), F32),
            pltpu.VMEM((width, TQ), F32),
        ],
        compiler_params=pltpu.CompilerParams(
            dimension_semantics=("arbitrary", "arbitrary"), vmem_limit_bytes=VMEM_LIMIT),
        name="diff_attn",
    )(lam_vecs, proj, proj, avt, bias, subln_g_col)


def _dsa_kernel(qb_ref, iq_ref, kb_ref, ik_ref, vt_ref, iwt_ref, bias_ref, o_ref,
                key_ref, plane_ref, mask_ref, vaug_ref, s_ref, acc_ref, ot_ref, *, top_k, seq):
    i = pl.program_id(1)
    nkb = seq // TK

    @pl.when(i == 0)
    def _():
        for c in range(nkb):
            vaug_ref[c, 0:B_HEAD_DIM, :] = vt_ref[0, c]
            vaug_ref[c, B_HEAD_DIM:B_HEAD_DIM + ONES_ROWS, :] = jnp.ones((ONES_ROWS, TK), BF16)

    @pl.when((i == 0) & (pl.program_id(0) == 0))
    def _():
        plane_ref[0:KEY_BITS] = jnp.zeros((KEY_BITS, nkb, SUBLANES, TQ), I32)
        for t, plane in enumerate(_index_planes(nkb)):
            plane_ref[KEY_BITS + t] = plane

    def rows(ref, j):
        return ref[0, pl.ds(pl.multiple_of(j * TK, TK), TK), :]

    krow = lax.broadcasted_iota(I32, (TK, TQ), 0)
    qcol = lax.broadcasted_iota(I32, (TK, TQ), 1)

    iq = iq_ref[0]
    per_blk = LANES // IDX_DIM
    iqh = [_lane_band(iq[:, (h // per_blk) * LANES:(h // per_blk + 1) * LANES],
                      (h % per_blk) * IDX_DIM, IDX_DIM) for h in range(IDX_HEADS)]
    w = iwt_ref[0]

    def score_keys(j):
        ikb = rows(ik_ref, j)
        acc = jnp.zeros((TK, TQ), F32)
        for h in range(IDX_HEADS):
            acc = acc + jnp.maximum(_dot_nt(ikb, iqh[h]), 0.0) * w[h:h + 1, :]
        acc = jnp.where(acc == 0.0, 0.0, acc)
        bits = lax.bitcast_convert_type(acc, I32)
        return bits ^ ((bits >> 31) & 0x7FFFFFFF)

    def store_keys(j, keys):
        key_ref[j] = keys
        for p, plane in enumerate(_bit_planes(keys)):
            plane_ref[p, j] = plane

    def score_body(j, c):
        store_keys(j, score_keys(j))
        return c

    lax.fori_loop(0, i, score_body, 0)
    admissible = (krow // CHUNK) <= (qcol // CHUNK)
    store_keys(i, jnp.where(admissible, score_keys(i), KEY_NEG_INF))

    blk = lax.broadcasted_iota(I32, (nkb, SUBLANES, TQ), 0)
    alive0 = jnp.where(blk <= i, -1, 0)

    def bit_body(p, carry, n_bits):
        alive, n_gt, thr_u = carry
        hit = alive & plane_ref[p]
        pc = jnp.sum(lax.population_count(hit), axis=0)
        n_hit = jnp.sum(pc.astype(F32), axis=0, keepdims=True)
        take = (n_gt + n_hit) >= top_k
        thr_u = thr_u | jnp.where(take, lax.shift_left(jnp.int32(1), n_bits - 1 - p), 0)
        n_gt = n_gt + jnp.where(take, 0.0, n_hit)
        alive = jnp.where(take, hit, alive ^ hit)
        return alive, n_gt, thr_u

    zero = jnp.zeros((1, TQ), I32)
    alive, n_gt, thr_u = lax.fori_loop(
        0, KEY_BITS, lambda p, c: bit_body(p, c, KEY_BITS), (alive0, jnp.zeros((1, TQ), F32), zero))
    idx_bits = _index_bits(seq)
    _, _, rev_idx = lax.fori_loop(
        KEY_BITS, KEY_BITS + idx_bits,
        lambda p, c: bit_body(p, c, KEY_BITS + idx_bits), (alive, n_gt, zero))
    thr = thr_u ^ INT_MIN
    lim = seq - rev_idx

    def mask_body(j, c):
        kb = key_ref[j]
        tie_ok = jnp.where((krow + j * TK) < lim, 0.0, NEG)
        sel = jnp.where(kb > thr, 0.0, jnp.where(kb == thr, tie_ok, NEG))
        mask_ref[j] = jnp.where(kb > KEY_NEG_INF, sel, NEG)
        return c

    lax.fori_loop(0, i + 1, mask_body, 0)

    qb = qb_ref[0]
    qh = [_lane_band(qb[:, (h // 2) * LANES:(h // 2 + 1) * LANES],
                     (h % 2) * B_HEAD_DIM, B_HEAD_DIM) for h in range(B_HEADS)]

    def logits(j, h, kind):
        s = _dot_nt(rows(kb_ref, j), qh[h]) + mask_ref[j]
        return s if kind == 'far' else s + bias_ref[_bias_kind(j, i), h]

    _stream_key_blocks(i, B_HEADS, s_ref, acc_ref, logits, lambda j, h: vaug_ref[j])

    for h in range(B_HEADS):
        a = acc_ref[h]
        ot_ref[h * B_HEAD_DIM:(h + 1) * B_HEAD_DIM, :] = (
            a[0:B_HEAD_DIM] / a[B_HEAD_DIM:B_HEAD_DIM + 1])
    o_ref[0] = ot_ref[...].T.astype(BF16)


def _dsa_attn(proj, bvt, iwt, bias, *, top_k):
    bsz, s, _ = proj.shape
    nq = s // TQ
    nkb = s // TK
    width = B_HEADS * B_HEAD_DIM
    iq_w = IDX_HEADS * IDX_DIM
    return pl.pallas_call(
        functools.partial(_dsa_kernel, top_k=top_k, seq=s),
        grid=(bsz, nq),
        in_specs=[
            pl.BlockSpec((1, TQ, width), lambda b, i: (b, i, COL_BQ // width)),
            pl.BlockSpec((1, TQ, iq_w), lambda b, i: (b, i, COL_IQ // iq_w)),
            pl.BlockSpec((1, s, LANES), lambda b, i: (b, 0, COL_KB // LANES)),
            pl.BlockSpec((1, s, LANES), lambda b, i: (b, 0, COL_IK // LANES)),
            pl.BlockSpec((1, nkb, B_HEAD_DIM, TK), lambda b, i: (b, 0, 0, 0)),
            pl.BlockSpec((1, IDX_HEADS, TQ), lambda b, i: (b, 0, i)),
            pl.BlockSpec((3, B_HEADS, TK, TQ), lambda b, i: (0, 0, 0, 0)),
        ],
        out_specs=pl.BlockSpec((1, TQ, width), lambda b, i: (b, i, 0)),
        out_shape=jax.ShapeDtypeStruct((bsz, s, width), BF16),
        scratch_shapes=[
            pltpu.VMEM((nkb, TK, TQ), I32),
            pltpu.VMEM((KEY_BITS + _index_bits(s), nkb, SUBLANES, TQ), I32),
            pltpu.VMEM((nkb, TK, TQ), F32),
            pltpu.VMEM((nkb, B_HEAD_DIM + ONES_ROWS, TK), BF16),
            pltpu.VMEM((B_HEADS, TK, TQ), F32),
            pltpu.VMEM((B_HEADS, B_HEAD_DIM + ONES_ROWS, TQ), F32),
            pltpu.VMEM((width, TQ), F32),
        ],
        compiler_params=pltpu.CompilerParams(
            dimension_semantics=("arbitrary", "arbitrary"), vmem_limit_bytes=VMEM_LIMIT),
        name="dsa_attn",
    )(proj, proj, proj, proj, bvt, iwt, bias)


def _attn_out_kernel(ya_ref, yb_ref, w_ref, x_ref, g_ref, o_ref):
    half = ya_ref.shape[-1]
    m = _dot(ya_ref[...], w_ref[0:half, :]) + _dot(yb_ref[...], w_ref[half:2 * half, :])
    o_ref[...] = x_ref[...] + _rms(m, g_ref[...])


def _attn_out(ya, yb, w, x, g, *, tm=512):
    n, d = x.shape
    half = ya.shape[-1]
    return pl.pallas_call(
        _attn_out_kernel,
        grid=(n // tm,),
        in_specs=[
            pl.BlockSpec((tm, half), lambda i: (i, 0)),
            pl.BlockSpec((tm, half), lambda i: (i, 0)),
            pl.BlockSpec((2 * half, d), lambda i: (0, 0)),
            pl.BlockSpec((tm, d), lambda i: (i, 0)),
            pl.BlockSpec((1, d), lambda i: (0, 0)),
        ],
        out_specs=pl.BlockSpec((tm, d), lambda i: (i, 0)),
        out_shape=jax.ShapeDtypeStruct((n, d), F32),
        compiler_params=pltpu.CompilerParams(
            dimension_semantics=("arbitrary",), vmem_limit_bytes=VMEM_LIMIT),
        name="attn_out",
    )(ya, yb, w, x, g)


def _mlp_kernel(x_ref, g_in_ref, wu_ref, wd_ref, g_out_ref, o_ref, h_ref, acc_ref):
    f = pl.program_id(1)

    @pl.when(f == 0)
    def _():
        h_ref[...] = _rms(x_ref[...], g_in_ref[...]).astype(BF16)
        acc_ref[...] = jnp.zeros_like(acc_ref)

    u = jnp.maximum(_dot(h_ref[...], wu_ref[...]), 0.0)
    acc_ref[...] += _dot((u * u).astype(BF16), wd_ref[...])

    @pl.when(f == pl.num_programs(1) - 1)
    def _():
        o_ref[...] = x_ref[...] + _rms(acc_ref[...], g_out_ref[...])


def _mlp(x, g_in, wu, wd, g_out, *, tm=1024, tf=1024):
    n, d = x.shape
    ff = wu.shape[1]
    return pl.pallas_call(
        _mlp_kernel,
        grid=(n // tm, ff // tf),
        in_specs=[
            pl.BlockSpec((tm, d), lambda i, f: (i, 0)),
            pl.BlockSpec((1, d), lambda i, f: (0, 0)),
            pl.BlockSpec((d, tf), lambda i, f: (0, f)),
            pl.BlockSpec((tf, d), lambda i, f: (f, 0)),
            pl.BlockSpec((1, d), lambda i, f: (0, 0)),
        ],
        out_specs=pl.BlockSpec((tm, d), lambda i, f: (i, 0)),
        out_shape=jax.ShapeDtypeStruct((n, d), F32),
        scratch_shapes=[pltpu.VMEM((tm, d), BF16), pltpu.VMEM((tm, d), F32)],
        compiler_params=pltpu.CompilerParams(
            dimension_semantics=("arbitrary", "arbitrary"), vmem_limit_bytes=VMEM_LIMIT),
        name="mlp",
    )(x, g_in, wu, wd, g_out)


CONV_ROWS = 64
U_HIST = 32
Z_HIST = 8


def _conv_kernel(x_ref, g_in_ref, w_in_ref, dw_w_ref, dw_b_ref, ln_g_ref, ln_b_ref, sc_w_ref,
                 w_out_ref, g_out_ref, o_ref, ubuf_ref, zbuf_ref, shift_ref, conv_ref, *, ts):
    @pl.when(pl.program_id(1) == 0)
    def _():
        ubuf_ref[0:U_HIST, :] = jnp.zeros((U_HIST, CONV_CH), F32)
        zbuf_ref[0:Z_HIST, :] = jnp.zeros((Z_HIST, SC_CH), F32)

    x = x_ref[0]
    hb = _rms(x, g_in_ref[...]).astype(BF16)
    c = CONV_CH
    ca = _dot(hb, w_in_ref[:, 0:c])
    cg = _dot(hb, w_in_ref[:, c:2 * c])
    ubuf_ref[U_HIST:U_HIST + ts, :] = ca * jax.nn.sigmoid(cg)
    dc = _dot(hb, w_in_ref[:, 3 * c:4 * c])
    dh = _dot(hb, w_in_ref[:, 4 * c:5 * c])
    zbuf_ref[Z_HIST:Z_HIST + ts, :] = dc * dh

    first_off = U_HIST - (CONV_WIDTH - 1)
    shift_rows = shift_ref.shape[1]
    for r in range(1, SUBLANES):
        shift_ref[r - 1] = ubuf_ref[r:r + shift_rows, :]

    def conv_rows(rb, carry):
        base = pl.multiple_of(rb * CONV_ROWS, CONV_ROWS)
        acc = jnp.zeros((CONV_ROWS, c), F32)
        for j in range(CONV_WIDTH):
            r, a = (first_off + j) % SUBLANES, (first_off + j) // SUBLANES
            rows = pl.ds(base + SUBLANES * a, CONV_ROWS)
            taps = ubuf_ref[rows, :] if r == 0 else shift_ref[r - 1, rows, :]
            acc = acc + dw_w_ref[j:j + 1, :] * taps
        conv_ref[pl.ds(base, CONV_ROWS), :] = acc
        return carry

    lax.fori_loop(0, ts // CONV_ROWS, conv_rows, 0)
    u = conv_ref[...] + dw_b_ref[...]
    mu = jnp.mean(u, axis=-1, keepdims=True)
    ctr = u - mu
    var = jnp.mean(ctr * ctr, axis=-1, keepdims=True)
    u = ctr * lax.rsqrt(var + EPS) * ln_g_ref[...] + ln_b_ref[...]
    u = u * jax.nn.sigmoid(u)

    z = jnp.zeros((ts, c), F32)
    for j in range(SC_WIDTH):
        off = Z_HIST - (SC_WIDTH - 1) + j
        z = z + sc_w_ref[j:j + 1, :] * zbuf_ref[off:off + ts, :]
    z = _dot(hb, w_in_ref[:, 2 * c:3 * c]) * z

    ubuf_ref[0:U_HIST, :] = ubuf_ref[ts:ts + U_HIST, :]
    zbuf_ref[0:Z_HIST, :] = zbuf_ref[ts:ts + Z_HIST, :]

    m = _dot(u.astype(BF16), w_out_ref[0:c, :]) + _dot(z.astype(BF16), w_out_ref[c:2 * c, :])
    o_ref[0] = x + _rms(m, g_out_ref[...])


def _conv_mixer(x, g_in, w_in, dw_w, dw_b, ln_g, ln_b, sc_w, w_out, g_out, *, ts=512):
    bsz, s, d = x.shape
    full = lambda a: pl.BlockSpec(a.shape, lambda b, i: (0,) * a.ndim)
    return pl.pallas_call(
        functools.partial(_conv_kernel, ts=ts),
        grid=(bsz, s // ts),
        in_specs=[pl.BlockSpec((1, ts, d), lambda b, i: (b, i, 0)),
                  full(g_in), full(w_in), full(dw_w), full(dw_b), full(ln_g), full(ln_b),
                  full(sc_w), full(w_out), full(g_out)],
        out_specs=pl.BlockSpec((1, ts, d), lambda b, i: (b, i, 0)),
        out_shape=jax.ShapeDtypeStruct((bsz, s, d), F32),
        scratch_shapes=[pltpu.VMEM((U_HIST + ts, CONV_CH), F32),
                        pltpu.VMEM((Z_HIST + ts, SC_CH), F32),
                        pltpu.VMEM((SUBLANES - 1, ts + U_HIST - SUBLANES, CONV_CH), F32),
                        pltpu.VMEM((ts, CONV_CH), F32)],
        compiler_params=pltpu.CompilerParams(
            dimension_semantics=("arbitrary", "arbitrary"), vmem_limit_bytes=VMEM_LIMIT),
        name="conv_mixer",
    )(x, g_in, w_in, dw_w, dw_b, ln_g, ln_b, sc_w, w_out, g_out)


def _t5_bucket(rel):
    nb = NUM_BUCKETS // 2
    ret = jnp.where(rel > 0, nb, 0)
    n = jnp.abs(rel)
    max_exact = nb // 2
    nf = jnp.maximum(n, 1).astype(jnp.float32)
    large = max_exact + (jnp.log(nf / max_exact) / math.log(MAX_DISTANCE / max_exact)
                         * (nb - max_exact)).astype(jnp.int32)
    large = jnp.minimum(large, nb - 1)
    return ret + jnp.where(n < max_exact, n, large)


def _toeplitz(f, off):
    heads, length = f.shape
    g = jnp.tile(jnp.pad(f, ((0, 0), (0, 1))), (1, TQ))[:, :TQ * length]
    return g.reshape(heads, TQ, length)[:, :, off:off + TK]


def _bias_tables(rel_bias):
    lo = -(TQ + TK - 1)
    rel = jnp.arange(lo, TK, dtype=jnp.int32)
    far_bucket = NUM_BUCKETS // 2 - 1
    f = (rel_bias[_t5_bucket(rel)].astype(F32) - rel_bias[far_bucket].astype(F32)[None, :]).T
    f = f * LOG2E
    diag = jnp.swapaxes(_toeplitz(f, -lo), 1, 2)
    prev = jnp.swapaxes(_toeplitz(f, -lo - TK), 1, 2)
    ki = np.arange(TK)[:, None]
    qi = np.arange(TQ)[None, :]
    adm = jnp.asarray((ki // CHUNK) <= (qi // CHUNK))
    none = jnp.zeros_like(diag)
    a_diag = jnp.where(adm[None], diag[:A_HEADS], NEG)
    a_bias = jnp.stack([none[:A_HEADS], prev[:A_HEADS], a_diag], axis=0)
    b_bias = jnp.stack([none[A_HEADS:], prev[A_HEADS:], diag[A_HEADS:]], axis=0)
    return a_bias, b_bias


def _attn_weights(w_in):
    o = np.cumsum([0, 512, 512, 512, 512, 64, 64, 256, 32, 8])
    aq, ak, av, bq, bk, bv, iq, ik, iw = [w_in[:, o[n]:o[n + 1]] for n in range(9)]
    aq = aq * (A_HEAD_DIM ** -0.5 * LOG2E)
    bq = bq * (B_HEAD_DIM ** -0.5 * LOG2E)
    w = jnp.concatenate([aq, ak, bq, iq, bk, bk, ik, ik, ik, ik], axis=1).astype(BF16)
    pad = jnp.zeros((w_in.shape[0], VT_ROWS - ROW_IW - IDX_HEADS), w_in.dtype)
    wt = jnp.concatenate([av, bv, iw, pad], axis=1).T.astype(BF16)
    return w, wt


def kernel(x, rel_bias, norm_g, w_mlp_up, w_mlp_down, attn_w_in, attn_w_out, diff_lambda,
           diff_subln_g, conv_w_in, conv_w_out, conv_dw_w, conv_dw_b, conv_ln_g, conv_ln_b,
           sconv_w):
    bsz, s, d = x.shape
    depth = norm_g.shape[0]
    top_k = min(TOPK_MAX, s // 4)
    row = lambda v: v.reshape(1, -1)
    a_bias, b_bias = _bias_tables(rel_bias)
    for i in range(depth):
        j = i // 2
        if i % 2 == 0:
            lambda_init = 0.8 - 0.6 * math.exp(-0.3 * i)
            w, wt = _attn_weights(attn_w_in[j])
            proj, avt, bvt, iwt = _attn_in(x.reshape(bsz, s, d), row(norm_g[i, 0]), w, wt)
            ya = _diff_attn(proj, avt, diff_lambda[j], a_bias, diff_subln_g[j].reshape(-1, 1),
                            lambda_init=lambda_init)
            yb = _dsa_attn(proj, bvt, iwt, b_bias, top_k=top_k)
            x = _attn_out(ya.reshape(bsz * s, -1), yb.reshape(bsz * s, -1),
                          attn_w_out[j].astype(BF16), x.reshape(bsz * s, d),
                          row(norm_g[i, 1]))
        else:
            x = _conv_mixer(x.reshape(bsz, s, d), row(norm_g[i, 0]), conv_w_in[j].astype(BF16),
                            conv_dw_w[j], row(conv_dw_b[j]), row(conv_ln_g[j]),
                            row(conv_ln_b[j]), sconv_w[j], conv_w_out[j].astype(BF16),
                            row(norm_g[i, 1])).reshape(bsz * s, d)
        x = _mlp(x, row(norm_g[i, 2]), w_mlp_up[i].astype(BF16), w_mlp_down[i].astype(BF16),
                 row(norm_g[i, 3]))
    return x.reshape(bsz, s, d)
```

```python
import functools
import math

import numpy as np
import jax
import jax.numpy as jnp
from jax import lax
from jax.experimental import pallas as pl
from jax.experimental.pallas import tpu as pltpu

F32 = jnp.float32
BF16 = jnp.bfloat16
I32 = jnp.int32

D_MODEL = 1024
CHUNK = 64
NUM_BUCKETS = 32
MAX_DISTANCE = 128
EPS = 1e-6
NEG = -1e30
LOG2E = math.log2(math.e)
A_HEADS = 4
A_HEAD_DIM = 64
A_V_DIM = 2 * A_HEAD_DIM
B_HEADS = 8
B_HEAD_DIM = 64
IDX_HEADS = 8
IDX_DIM = 32
TOPK_MAX = 256
CONV_CH = 512
CONV_WIDTH = 31
SC_CH = 512
SC_WIDTH = 3
D_FF = 4 * D_MODEL

LANES = 128
SUBLANES = 8
VMEM_LIMIT = 56 * 1024 * 1024

TQ = 256
TK = 256
COL_AQ, COL_AK, COL_BQ, COL_IQ, COL_KB, COL_IK, PROJ_W = 0, 512, 1024, 1536, 1792, 1920, 2048
ROW_AV, ROW_BV, ROW_IW, VT_ROWS = 0, 512, 576, 640
ONES_ROWS = 16

INT_MIN = -2 ** 31
KEY_NEG_INF = INT_MIN + 0x7FFFFF


def _rms(x, g):
    return x * lax.rsqrt(jnp.mean(x * x, axis=-1, keepdims=True) + EPS) * g


def _dot_nt(a, b):
    return lax.dot_general(a, b, (((1,), (1,)), ((), ())), preferred_element_type=F32)


def _dot(a, b):
    return jnp.dot(a, b, preferred_element_type=F32)


def _lane_band(x, lo, width):
    lane = lax.broadcasted_iota(I32, x.shape, 1)
    return jnp.where((lane >= lo) & (lane < lo + width), x, jnp.zeros_like(x))


KEY_BITS = 32


def _bit_planes(keys):
    assert keys.shape[0] == KEY_BITS * SUBLANES
    u = keys ^ INT_MIN
    a = [u[SUBLANES * r:SUBLANES * (r + 1), :] for r in range(KEY_BITS)]
    j, m = KEY_BITS // 2, (1 << (KEY_BITS // 2)) - 1
    while j:
        mask = np.int32(np.uint32(m))
        k = 0
        while k < KEY_BITS:
            t = (a[k] ^ lax.shift_right_logical(a[k + j], jnp.int32(j))) & mask
            a[k] = a[k] ^ t
            a[k + j] = a[k + j] ^ lax.shift_left(t, jnp.int32(j))
            k = (k + j + 1) & ~j
        j >>= 1
        m = (m ^ (m << j)) & 0xFFFFFFFF
    return a


def _index_bits(seq):
    bits = seq.bit_length() - 1
    assert seq == 1 << bits and seq >= KEY_BITS * SUBLANES
    return bits


def _index_planes(nkb):
    shape = (nkb, SUBLANES, TQ)
    blk = lax.broadcasted_iota(I32, shape, 0)
    sub = lax.broadcasted_iota(I32, shape, 1)
    sub_bits = SUBLANES.bit_length() - 1
    word_bits = KEY_BITS.bit_length() - 1
    planes = []
    for b in range(_index_bits(nkb * TK) - 1, -1, -1):
        if b >= sub_bits + word_bits:
            on = ((nkb - 1 - blk) >> (b - sub_bits - word_bits)) & 1
            planes.append(jnp.where(on == 1, -1, 0))
        elif b >= sub_bits:
            t = b - sub_bits
            pattern = sum(1 << k for k in range(KEY_BITS) if (k >> t) & 1)
            planes.append(jnp.full(shape, np.int32(np.uint32(pattern)), I32))
        else:
            on = ((SUBLANES - 1 - sub) >> b) & 1
            planes.append(jnp.where(on == 1, -1, 0))
    return planes


def _bias_kind(j, i):
    return jnp.clip(j - i + 2, 0, 2)


def _stream_key_blocks(i, n_chains, s_ref, acc_ref, logits, v_aug):
    chains = range(n_chains)

    def stage(j, c, kind):
        s = logits(j, c, kind)
        s_ref[c] = s
        return jnp.max(s, axis=0, keepdims=True)

    def absorb(j, c, m_old, block_max):
        m_new = jnp.maximum(m_old, block_max)
        alpha = jnp.exp2(m_old - m_new)
        p = jnp.exp2(s_ref[c] - m_new).astype(BF16)
        acc_ref[c] = acc_ref[c] * alpha + _dot(v_aug(j, c), p)
        return m_new

    def step(j, carry, kind):
        ms, bms = carry
        out = [(absorb(j, c, ms[c], bms[c]), stage(j + 1, c, kind)) for c in chains]
        return tuple(o[0] for o in out), tuple(o[1] for o in out)

    acc_ref[...] = jnp.zeros_like(acc_ref)
    carry = ((jnp.full((1, TQ), NEG, F32),) * n_chains, tuple(stage(0, c, 'any') for c in chains))
    n_far = jnp.maximum(i - 2, 0)
    carry = lax.fori_loop(0, n_far, lambda j, cr: step(j, cr, 'far'), carry)
    ms, bms = lax.fori_loop(n_far, i, lambda j, cr: step(j, cr, 'near'), carry)
    for c in chains:
        absorb(i, c, ms[c], bms[c])


def _attn_in_kernel(x_ref, g_ref, w_ref, wt_ref, proj_ref, avt_ref, bvt_ref, iwt_ref, *, tm):
    hb = _rms(x_ref[0], g_ref[...]).astype(BF16)
    proj_ref[0] = _dot(hb, w_ref[...]).astype(BF16)
    t = _dot_nt(wt_ref[...], hb)
    for c in range(tm // TK):
        avt_ref[0, c] = t[ROW_AV:ROW_BV, c * TK:(c + 1) * TK].astype(BF16)
        bvt_ref[0, c] = t[ROW_BV:ROW_IW, c * TK:(c + 1) * TK].astype(BF16)
    iwt_ref[0] = t[ROW_IW:ROW_IW + IDX_HEADS, :] * (IDX_HEADS ** -0.5 * IDX_DIM ** -0.5)


def _attn_in(x, g, w, wt, *, tm=512):
    bsz, s, d = x.shape
    av_rows = ROW_BV - ROW_AV
    return pl.pallas_call(
        functools.partial(_attn_in_kernel, tm=tm),
        grid=(bsz, s // tm),
        in_specs=[
            pl.BlockSpec((1, tm, d), lambda b, i: (b, i, 0)),
            pl.BlockSpec((1, d), lambda b, i: (0, 0)),
            pl.BlockSpec((d, PROJ_W), lambda b, i: (0, 0)),
            pl.BlockSpec((VT_ROWS, d), lambda b, i: (0, 0)),
        ],
        out_specs=[
            pl.BlockSpec((1, tm, PROJ_W), lambda b, i: (b, i, 0)),
            pl.BlockSpec((1, tm // TK, av_rows, TK), lambda b, i: (b, i, 0, 0)),
            pl.BlockSpec((1, tm // TK, B_HEAD_DIM, TK), lambda b, i: (b, i, 0, 0)),
            pl.BlockSpec((1, IDX_HEADS, tm), lambda b, i: (b, 0, i)),
        ],
        out_shape=[
            jax.ShapeDtypeStruct((bsz, s, PROJ_W), BF16),
            jax.ShapeDtypeStruct((bsz, s // TK, av_rows, TK), BF16),
            jax.ShapeDtypeStruct((bsz, s // TK, B_HEAD_DIM, TK), BF16),
            jax.ShapeDtypeStruct((bsz, IDX_HEADS, s), F32),
        ],
        compiler_params=pltpu.CompilerParams(
            dimension_semantics=("arbitrary", "arbitrary"), vmem_limit_bytes=VMEM_LIMIT),
        name="attn_in",
    )(x, g, w, wt)


def _diff_attn_kernel(lam_ref, q_ref, k_ref, vt_ref, bias_ref, g_ref, o_ref,
                      vaug_ref, s_ref, acc_ref, ot_ref, *, lambda_init, seq):
    i = pl.program_id(1)
    nkb = seq // TK
    n_maps = 2 * A_HEADS

    @pl.when(i == 0)
    def _():
        for c in range(nkb):
            for h in range(A_HEADS):
                vaug_ref[h, c, 0:A_V_DIM, :] = vt_ref[0, c, h * A_V_DIM:(h + 1) * A_V_DIM, :]
                vaug_ref[h, c, A_V_DIM:A_V_DIM + ONES_ROWS, :] = jnp.ones((ONES_ROWS, TK), BF16)

    lv = lam_ref[...]
    lam = (jnp.exp(jnp.sum(lv[0:1] * lv[1:2], axis=-1, keepdims=True))
           - jnp.exp(jnp.sum(lv[2:3] * lv[3:4], axis=-1, keepdims=True)) + lambda_init)

    q = q_ref[0]
    qm = []
    for h in range(A_HEADS):
        qh = q[:, h * LANES:(h + 1) * LANES]
        qm += [_lane_band(qh, 0, A_HEAD_DIM), _lane_band(qh, A_HEAD_DIM, A_HEAD_DIM)]

    def logits(j, c, kind):
        h = c // 2
        kb = k_ref[0, pl.ds(pl.multiple_of(j * TK, TK), TK), h * LANES:(h + 1) * LANES]
        s = _dot_nt(kb, qm[c])
        return s if kind == 'far' else s + bias_ref[_bias_kind(j, i), h]

    _stream_key_blocks(i, n_maps, s_ref, acc_ref, logits, lambda j, c: vaug_ref[c // 2, j])

    for h in range(A_HEADS):
        a1 = acc_ref[2 * h]
        a2 = acc_ref[2 * h + 1]
        a = (a1[0:A_V_DIM] / a1[A_V_DIM:A_V_DIM + 1]
             - lam * (a2[0:A_V_DIM] / a2[A_V_DIM:A_V_DIM + 1]))
        y = a * lax.rsqrt(jnp.mean(a * a, axis=0, keepdims=True) + EPS) * g_ref[...]
        ot_ref[h * A_V_DIM:(h + 1) * A_V_DIM, :] = y * (1.0 - lambda_init)
    o_ref[0] = ot_ref[...].T.astype(BF16)


def _diff_attn(proj, avt, lam_vecs, bias, subln_g_col, *, lambda_init):
    bsz, s, _ = proj.shape
    nq = s // TQ
    nkb = s // TK
    width = A_HEADS * A_V_DIM
    return pl.pallas_call(
        functools.partial(_diff_attn_kernel, lambda_init=lambda_init, seq=s),
        grid=(bsz, nq),
        in_specs=[
            pl.BlockSpec((4, A_HEAD_DIM), lambda b, i: (0, 0)),
            pl.BlockSpec((1, TQ, width), lambda b, i: (b, i, COL_AQ // width)),
            pl.BlockSpec((1, s, width), lambda b, i: (b, 0, COL_AK // width)),
            pl.BlockSpec((1, nkb, width, TK), lambda b, i: (b, 0, 0, 0)),
            pl.BlockSpec((3, A_HEADS, TK, TQ), lambda b, i: (0, 0, 0, 0)),
            pl.BlockSpec((A_V_DIM, 1), lambda b, i: (0, 0)),
        ],
        out_specs=pl.BlockSpec((1, TQ, width), lambda b, i: (b, i, 0)),
        out_shape=jax.ShapeDtypeStruct((bsz, s, width), BF16),
        scratch_shapes=[
            pltpu.VMEM((A_HEADS, nkb, A_V_DIM + ONES_ROWS, TK), BF16),
            pltpu.VMEM((2 * A_HEADS, TK, TQ), F32),
            pltpu.VMEM((2 * A_HEADS, A_V_DIM + ONES_ROWS, TQ), F32),
            pltpu.VMEM((width, TQ), F32),
        ],
        compiler_params=pltpu.CompilerParams(
            dimension_semantics=("arbitrary", "arbitrary"), vmem_limit_bytes=VMEM_LIMIT),
        name="diff_attn",
    )(lam_vecs, proj, proj, avt, bias, subln_g_col)


def _dsa_kernel(qb_ref, iq_ref, kb_ref, ik_ref, vt_ref, iwt_ref, bias_ref, o_ref,
                score_ref, key_ref, plane_ref, mask_ref, vaug_ref, s_ref, acc_ref, ot_ref,
                *, top_k, seq):
    i = pl.program_id(1)
    nkb = seq // TK

    @pl.when(i == 0)
    def _():
        for c in range(nkb):
            vaug_ref[c, 0:B_HEAD_DIM, :] = vt_ref[0, c]
            vaug_ref[c, B_HEAD_DIM:B_HEAD_DIM + ONES_ROWS, :] = jnp.ones((ONES_ROWS, TK), BF16)

    @pl.when((i == 0) & (pl.program_id(0) == 0))
    def _():
        plane_ref[0:KEY_BITS] = jnp.zeros((KEY_BITS, nkb, SUBLANES, TQ), I32)
        for t, plane in enumerate(_index_planes(nkb)):
            plane_ref[KEY_BITS + t] = plane

    def rows(ref, j):
        return ref[0, pl.ds(pl.multiple_of(j * TK, TK), TK), :]

    krow = lax.broadcasted_iota(I32, (TK, TQ), 0)
    qcol = lax.broadcasted_iota(I32, (TK, TQ), 1)

    iq = iq_ref[0]
    per_blk = LANES // IDX_DIM
    iqh = [_lane_band(iq[:, (h // per_blk) * LANES:(h // per_blk + 1) * LANES],
                      (h % per_blk) * IDX_DIM, IDX_DIM) for h in range(IDX_HEADS)]
    w = iwt_ref[0]

    def stage_scores(j):
        ikb = rows(ik_ref, j)
        half = TK // 2
        for r in (0, half):
            acc = jnp.zeros((half, TQ), F32)
            for h in range(IDX_HEADS):
                acc = acc + jnp.maximum(_dot_nt(ikb[r:r + half], iqh[h]), 0.0) * w[h:h + 1, :]
            score_ref[r:r + half, :] = acc

    def staged_keys():
        acc = score_ref[...]
        acc = jnp.where(acc == 0.0, 0.0, acc)
        bits = lax.bitcast_convert_type(acc, I32)
        return bits ^ ((bits >> 31) & 0x7FFFFFFF)

    def store_keys(j, keys):
        key_ref[j] = keys
        for p, plane in enumerate(_bit_planes(keys)):
            plane_ref[p, j] = plane

    def score_body(j, c):
        keys = staged_keys()
        stage_scores(j + 1)
        store_keys(j, keys)
        return c

    stage_scores(0)
    lax.fori_loop(0, i, score_body, 0)
    admissible = (krow // CHUNK) <= (qcol // CHUNK)
    store_keys(i, jnp.where(admissible, staged_keys(), KEY_NEG_INF))

    blk = lax.broadcasted_iota(I32, (nkb, SUBLANES, TQ), 0)
    alive0 = jnp.where(blk <= i, -1, 0)

    def bit_body(p, carry, n_bits):
        alive, n_gt, thr_u = carry
        hit = alive & plane_ref[p]
        pc = jnp.sum(lax.population_count(hit), axis=0)
        n_hit = jnp.sum(pc.astype(F32), axis=0, keepdims=True)
        take = (n_gt + n_hit) >= top_k
        thr_u = thr_u | jnp.where(take, lax.shift_left(jnp.int32(1), n_bits - 1 - p), 0)
        n_gt = n_gt + jnp.where(take, 0.0, n_hit)
        alive = jnp.where(take, hit, alive ^ hit)
        return alive, n_gt, thr_u

    zero = jnp.zeros((1, TQ), I32)
    alive, n_gt, thr_u = lax.fori_loop(
        0, KEY_BITS, lambda p, c: bit_body(p, c, KEY_BITS), (alive0, jnp.zeros((1, TQ), F32), zero))
    idx_bits = _index_bits(seq)
    _, _, rev_idx = lax.fori_loop(
        KEY_BITS, KEY_BITS + idx_bits,
        lambda p, c: bit_body(p, c, KEY_BITS + idx_bits), (alive, n_gt, zero))
    thr = thr_u ^ INT_MIN
    lim = seq - rev_idx

    def mask_body(j, c):
        kb = key_ref[j]
        tie_ok = jnp.where((krow + j * TK) < lim, 0.0, NEG)
        sel = jnp.where(kb > thr, 0.0, jnp.where(kb == thr, tie_ok, NEG))
        mask_ref[j] = jnp.where(kb > KEY_NEG_INF, sel, NEG)
        return c

    lax.fori_loop(0, i + 1, mask_body, 0)

    qb = qb_ref[0]
    qh = [_lane_band(qb[:, (h // 2) * LANES:(h // 2 + 1) * LANES],
                     (h % 2) * B_HEAD_DIM, B_HEAD_DIM) for h in range(B_HEADS)]

    def logits(j, h, kind):
        s = _dot_nt(rows(kb_ref, j), qh[h]) + mask_ref[j]
        return s if kind == 'far' else s + bias_ref[_bias_kind(j, i), h]

    _stream_key_blocks(i, B_HEADS, s_ref, acc_ref, logits, lambda j, h: vaug_ref[j])

    for h in range(B_HEADS):
        a = acc_ref[h]
        ot_ref[h * B_HEAD_DIM:(h + 1) * B_HEAD_DIM, :] = (
            a[0:B_HEAD_DIM] / a[B_HEAD_DIM:B_HEAD_DIM + 1])
    o_ref[0] = ot_ref[...].T.astype(BF16)


def _dsa_attn(proj, bvt, iwt, bias, *, top_k):
    bsz, s, _ = proj.shape
    nq = s // TQ
    nkb = s // TK
    width = B_HEADS * B_HEAD_DIM
    iq_w = IDX_HEADS * IDX_DIM
    return pl.pallas_call(
        functools.partial(_dsa_kernel, top_k=top_k, seq=s),
        grid=(bsz, nq),
        in_specs=[
            pl.BlockSpec((1, TQ, width), lambda b, i: (b, i, COL_BQ // width)),
            pl.BlockSpec((1, TQ, iq_w), lambda b, i: (b, i, COL_IQ // iq_w)),
            pl.BlockSpec((1, s, LANES), lambda b, i: (b, 0, COL_KB // LANES)),
            pl.BlockSpec((1, s, LANES), lambda b, i: (b, 0, COL_IK // LANES)),
            pl.BlockSpec((1, nkb, B_HEAD_DIM, TK), lambda b, i: (b, 0, 0, 0)),
            pl.BlockSpec((1, IDX_HEADS, TQ), lambda b, i: (b, 0, i)),
            pl.BlockSpec((3, B_HEADS, TK, TQ), lambda b, i: (0, 0, 0, 0)),
        ],
        out_specs=pl.BlockSpec((1, TQ, width), lambda b, i: (b, i, 0)),
        out_shape=jax.ShapeDtypeStruct((bsz, s, width), BF16),
        scratch_shapes=[
            pltpu.VMEM((TK, TQ), F32),
            pltpu.VMEM((nkb, TK, TQ), I32),
            pltpu.VMEM((KEY_BITS + _index_bits(s), nkb, SUBLANES, TQ), I32),
            pltpu.VMEM((nkb, TK, TQ), F32),
            pltpu.VMEM((nkb, B_HEAD_DIM + ONES_ROWS, TK), BF16),
            pltpu.VMEM((B_HEADS, TK, TQ), F32),
            pltpu.VMEM((B_HEADS, B_HEAD_DIM + ONES_ROWS, TQ), F32),
            pltpu.VMEM((width, TQ), F32),
        ],
        compiler_params=pltpu.CompilerParams(
            dimension_semantics=("arbitrary", "arbitrary"), vmem_limit_bytes=VMEM_LIMIT),
        name="dsa_attn",
    )(proj, proj, proj, proj, bvt, iwt, bias)


def _attn_out_kernel(ya_ref, yb_ref, w_ref, x_ref, g_ref, o_ref):
    half = ya_ref.shape[-1]
    m = _dot(ya_ref[...], w_ref[0:half, :]) + _dot(yb_ref[...], w_ref[half:2 * half, :])
    o_ref[...] = x_ref[...] + _rms(m, g_ref[...])


def _attn_out(ya, yb, w, x, g, *, tm=512):
    n, d = x.shape
    half = ya.shape[-1]
    return pl.pallas_call(
        _attn_out_kernel,
        grid=(n // tm,),
        in_specs=[
            pl.BlockSpec((tm, half), lambda i: (i, 0)),
            pl.BlockSpec((tm, half), lambda i: (i, 0)),
            pl.BlockSpec((2 * half, d), lambda i: (0, 0)),
            pl.BlockSpec((tm, d), lambda i: (i, 0)),
            pl.BlockSpec((1, d), lambda i: (0, 0)),
        ],
        out_specs=pl.BlockSpec((tm, d), lambda i: (i, 0)),
        out_shape=jax.ShapeDtypeStruct((n, d), F32),
        compiler_params=pltpu.CompilerParams(
            dimension_semantics=("arbitrary",), vmem_limit_bytes=VMEM_LIMIT),
        name="attn_out",
    )(ya, yb, w, x, g)


def _mlp_kernel(x_ref, g_in_ref, wu_ref, wd_ref, g_out_ref, o_ref, h_ref, acc_ref):
    f = pl.program_id(1)

    @pl.when(f == 0)
    def _():
        h_ref[...] = _rms(x_ref[...], g_in_ref[...]).astype(BF16)
        acc_ref[...] = jnp.zeros_like(acc_ref)

    u = jnp.maximum(_dot(h_ref[...], wu_ref[...]), 0.0)
    acc_ref[...] += _dot((u * u).astype(BF16), wd_ref[...])

    @pl.when(f == pl.num_programs(1) - 1)
    def _():
        o_ref[...] = x_ref[...] + _rms(acc_ref[...], g_out_ref[...])


def _mlp(x, g_in, wu, wd, g_out, *, tm=1024, tf=1024):
    n, d = x.shape
    ff = wu.shape[1]
    return pl.pallas_call(
        _mlp_kernel,
        grid=(n // tm, ff // tf),
        in_specs=[
            pl.BlockSpec((tm, d), lambda i, f: (i, 0)),
            pl.BlockSpec((1, d), lambda i, f: (0, 0)),
            pl.BlockSpec((d, tf), lambda i, f: (0, f)),
            pl.BlockSpec((tf, d), lambda i, f: (f, 0)),
            pl.BlockSpec((1, d), lambda i, f: (0, 0)),
        ],
        out_specs=pl.BlockSpec((tm, d), lambda i, f: (i, 0)),
        out_shape=jax.ShapeDtypeStruct((n, d), F32),
        scratch_shapes=[pltpu.VMEM((tm, d), BF16), pltpu.VMEM((tm, d), F32)],
        compiler_params=pltpu.CompilerParams(
            dimension_semantics=("arbitrary", "arbitrary"), vmem_limit_bytes=VMEM_LIMIT),
        name="mlp",
    )(x, g_in, wu, wd, g_out)


CONV_ROWS = 64
U_HIST = 32
Z_HIST = 8


def _conv_kernel(x_ref, g_in_ref, w_in_ref, dw_w_ref, dw_b_ref, ln_g_ref, ln_b_ref, sc_w_ref,
                 w_out_ref, g_out_ref, o_ref, ubuf_ref, zbuf_ref, shift_ref, conv_ref, *, ts):
    @pl.when(pl.program_id(1) == 0)
    def _():
        ubuf_ref[0:U_HIST, :] = jnp.zeros((U_HIST, CONV_CH), F32)
        zbuf_ref[0:Z_HIST, :] = jnp.zeros((Z_HIST, SC_CH), F32)

    x = x_ref[0]
    hb = _rms(x, g_in_ref[...]).astype(BF16)
    c = CONV_CH
    ca = _dot(hb, w_in_ref[:, 0:c])
    cg = _dot(hb, w_in_ref[:, c:2 * c])
    ubuf_ref[U_HIST:U_HIST + ts, :] = ca * jax.nn.sigmoid(cg)
    dc = _dot(hb, w_in_ref[:, 3 * c:4 * c])
    dh = _dot(hb, w_in_ref[:, 4 * c:5 * c])
    zbuf_ref[Z_HIST:Z_HIST + ts, :] = dc * dh

    first_off = U_HIST - (CONV_WIDTH - 1)
    shift_rows = shift_ref.shape[1]
    for r in range(1, SUBLANES):
        shift_ref[r - 1] = ubuf_ref[r:r + shift_rows, :]

    def conv_rows(rb, carry):
        base = pl.multiple_of(rb * CONV_ROWS, CONV_ROWS)
        acc = jnp.zeros((CONV_ROWS, c), F32)
        for j in range(CONV_WIDTH):
            r, a = (first_off + j) % SUBLANES, (first_off + j) // SUBLANES
            rows = pl.ds(base + SUBLANES * a, CONV_ROWS)
            taps = ubuf_ref[rows, :] if r == 0 else shift_ref[r - 1, rows, :]
            acc = acc + dw_w_ref[j:j + 1, :] * taps
        conv_ref[pl.ds(base, CONV_ROWS), :] = acc
        return carry

    lax.fori_loop(0, ts // CONV_ROWS, conv_rows, 0)
    u = conv_ref[...] + dw_b_ref[...]
    mu = jnp.mean(u, axis=-1, keepdims=True)
    ctr = u - mu
    var = jnp.mean(ctr * ctr, axis=-1, keepdims=True)
    u = ctr * lax.rsqrt(var + EPS) * ln_g_ref[...] + ln_b_ref[...]
    u = u * jax.nn.sigmoid(u)

    z = jnp.zeros((ts, c), F32)
    for j in range(SC_WIDTH):
        off = Z_HIST - (SC_WIDTH - 1) + j
        z = z + sc_w_ref[j:j + 1, :] * zbuf_ref[off:off + ts, :]
    z = _dot(hb, w_in_ref[:, 2 * c:3 * c]) * z

    ubuf_ref[0:U_HIST, :] = ubuf_ref[ts:ts + U_HIST, :]
    zbuf_ref[0:Z_HIST, :] = zbuf_ref[ts:ts + Z_HIST, :]

    m = _dot(u.astype(BF16), w_out_ref[0:c, :]) + _dot(z.astype(BF16), w_out_ref[c:2 * c, :])
    o_ref[0] = x + _rms(m, g_out_ref[...])


def _conv_mixer(x, g_in, w_in, dw_w, dw_b, ln_g, ln_b, sc_w, w_out, g_out, *, ts=512):
    bsz, s, d = x.shape
    full = lambda a: pl.BlockSpec(a.shape, lambda b, i: (0,) * a.ndim)
    return pl.pallas_call(
        functools.partial(_conv_kernel, ts=ts),
        grid=(bsz, s // ts),
        in_specs=[pl.BlockSpec((1, ts, d), lambda b, i: (b, i, 0)),
                  full(g_in), full(w_in), full(dw_w), full(dw_b), full(ln_g), full(ln_b),
                  full(sc_w), full(w_out), full(g_out)],
        out_specs=pl.BlockSpec((1, ts, d), lambda b, i: (b, i, 0)),
        out_shape=jax.ShapeDtypeStruct((bsz, s, d), F32),
        scratch_shapes=[pltpu.VMEM((U_HIST + ts, CONV_CH), F32),
                        pltpu.VMEM((Z_HIST + ts, SC_CH), F32),
                        pltpu.VMEM((SUBLANES - 1, ts + U_HIST - SUBLANES, CONV_CH), F32),
                        pltpu.VMEM((ts, CONV_CH), F32)],
        compiler_params=pltpu.CompilerParams(
            dimension_semantics=("arbitrary", "arbitrary"), vmem_limit_bytes=VMEM_LIMIT),
        name="conv_mixer",
    )(x, g_in, w_in, dw_w, dw_b, ln_g, ln_b, sc_w, w_out, g_out)


def _t5_bucket(rel):
    nb = NUM_BUCKETS // 2
    ret = jnp.where(rel > 0, nb, 0)
    n = jnp.abs(rel)
    max_exact = nb // 2
    nf = jnp.maximum(n, 1).astype(jnp.float32)
    large = max_exact + (jnp.log(nf / max_exact) / math.log(MAX_DISTANCE / max_exact)
                         * (nb - max_exact)).astype(jnp.int32)
    large = jnp.minimum(large, nb - 1)
    return ret + jnp.where(n < max_exact, n, large)


def _toeplitz(f, off):
    heads, length = f.shape
    g = jnp.tile(jnp.pad(f, ((0, 0), (0, 1))), (1, TQ))[:, :TQ * length]
    return g.reshape(heads, TQ, length)[:, :, off:off + TK]


def _bias_tables(rel_bias):
    lo = -(TQ + TK - 1)
    rel = jnp.arange(lo, TK, dtype=jnp.int32)
    far_bucket = NUM_BUCKETS // 2 - 1
    f = (rel_bias[_t5_bucket(rel)].astype(F32) - rel_bias[far_bucket].astype(F32)[None, :]).T
    f = f * LOG2E
    diag = jnp.swapaxes(_toeplitz(f, -lo), 1, 2)
    prev = jnp.swapaxes(_toeplitz(f, -lo - TK), 1, 2)
    ki = np.arange(TK)[:, None]
    qi = np.arange(TQ)[None, :]
    adm = jnp.asarray((ki // CHUNK) <= (qi // CHUNK))
    none = jnp.zeros_like(diag)
    a_diag = jnp.where(adm[None], diag[:A_HEADS], NEG)
    a_bias = jnp.stack([none[:A_HEADS], prev[:A_HEADS], a_diag], axis=0)
    b_bias = jnp.stack([none[A_HEADS:], prev[A_HEADS:], diag[A_HEADS:]], axis=0)
    return a_bias, b_bias


def _attn_weights(w_in):
    o = np.cumsum([0, 512, 512, 512, 512, 64, 64, 256, 32, 8])
    aq, ak, av, bq, bk, bv, iq, ik, iw = [w_in[:, o[n]:o[n + 1]] for n in range(9)]
    aq = aq * (A_HEAD_DIM ** -0.5 * LOG2E)
    bq = bq * (B_HEAD_DIM ** -0.5 * LOG2E)
    w = jnp.concatenate([aq, ak, bq, iq, bk, bk, ik, ik, ik, ik], axis=1).astype(BF16)
    pad = jnp.zeros((w_in.shape[0], VT_ROWS - ROW_IW - IDX_HEADS), w_in.dtype)
    wt = jnp.concatenate([av, bv, iw, pad], axis=1).T.astype(BF16)
    return w, wt


def kernel(x, rel_bias, norm_g, w_mlp_up, w_mlp_down, attn_w_in, attn_w_out, diff_lambda,
           diff_subln_g, conv_w_in, conv_w_out, conv_dw_w, conv_dw_b, conv_ln_g, conv_ln_b,
           sconv_w):
    bsz, s, d = x.shape
    depth = norm_g.shape[0]
    top_k = min(TOPK_MAX, s // 4)
    row = lambda v: v.reshape(1, -1)
    a_bias, b_bias = _bias_tables(rel_bias)
    for i in range(depth):
        j = i // 2
        if i % 2 == 0:
            lambda_init = 0.8 - 0.6 * math.exp(-0.3 * i)
            w, wt = _attn_weights(attn_w_in[j])
            proj, avt, bvt, iwt = _attn_in(x.reshape(bsz, s, d), row(norm_g[i, 0]), w, wt)
            ya = _diff_attn(proj, avt, diff_lambda[j], a_bias, diff_subln_g[j].reshape(-1, 1),
                            lambda_init=lambda_init)
            yb = _dsa_attn(proj, bvt, iwt, b_bias, top_k=top_k)
            x = _attn_out(ya.reshape(bsz * s, -1), yb.reshape(bsz * s, -1),
                          attn_w_out[j].astype(BF16), x.reshape(bsz * s, d),
                          row(norm_g[i, 1]))
        else:
            x = _conv_mixer(x.reshape(bsz, s, d), row(norm_g[i, 0]), conv_w_in[j].astype(BF16),
                            conv_dw_w[j], row(conv_dw_b[j]), row(conv_ln_g[j]),
                            row(conv_ln_b[j]), sconv_w[j], conv_w_out[j].astype(BF16),
                            row(norm_g[i, 1])).reshape(bsz * s, d)
        x = _mlp(x, row(norm_g[i, 2]), w_mlp_up[i].astype(BF16), w_mlp_down[i].astype(BF16),
                 row(norm_g[i, 3]))
    return x.reshape(bsz, s, d)
```

```python
import functools
import math

import numpy as np
import jax
import jax.numpy as jnp
from jax import lax
from jax.experimental import pallas as pl
from jax.experimental.pallas import tpu as pltpu

F32 = jnp.float32
BF16 = jnp.bfloat16
I32 = jnp.int32

D_MODEL = 1024
CHUNK = 64
NUM_BUCKETS = 32
MAX_DISTANCE = 128
EPS = 1e-6
NEG = -1e30
LOG2E = math.log2(math.e)
A_HEADS = 4
A_HEAD_DIM = 64
A_V_DIM = 2 * A_HEAD_DIM
B_HEADS = 8
B_HEAD_DIM = 64
IDX_HEADS = 8
IDX_DIM = 32
TOPK_MAX = 256
CONV_CH = 512
CONV_WIDTH = 31
SC_CH = 512
SC_WIDTH = 3
D_FF = 4 * D_MODEL

LANES = 128
SUBLANES = 8
VMEM_LIMIT = 56 * 1024 * 1024

TQ = 256
TK = 256
COL_AQ, COL_AK, COL_BQ, COL_IQ, COL_KB, COL_IK, PROJ_W = 0, 512, 1024, 1536, 1792, 1920, 2048
ROW_AV, ROW_BV, ROW_IW, VT_ROWS = 0, 512, 576, 640
ONES_ROWS = 16
INT_MIN = -2 ** 31
KEY_NEG_INF = INT_MIN + 0x7FFFFF


def _rms(x, g):
    return x * lax.rsqrt(jnp.mean(x * x, axis=-1, keepdims=True) + EPS) * g


def _dot_nt(a, b):
    return lax.dot_general(a, b, (((1,), (1,)), ((), ())), preferred_element_type=F32)


def _dot(a, b):
    return jnp.dot(a, b, preferred_element_type=F32)


def _lane_band(x, lo, width):
    lane = lax.broadcasted_iota(I32, x.shape, 1)
    return jnp.where((lane >= lo) & (lane < lo + width), x, jnp.zeros_like(x))


KEY_BITS = 32


def _bit_planes(keys):
    assert keys.shape[0] == KEY_BITS * SUBLANES
    u = keys ^ INT_MIN
    a = [u[SUBLANES * r:SUBLANES * (r + 1), :] for r in range(KEY_BITS)]
    j, m = KEY_BITS // 2, (1 << (KEY_BITS // 2)) - 1
    while j:
        mask = np.int32(np.uint32(m))
        k = 0
        while k < KEY_BITS:
            t = (a[k] ^ lax.shift_right_logical(a[k + j], jnp.int32(j))) & mask
            a[k] = a[k] ^ t
            a[k + j] = a[k + j] ^ lax.shift_left(t, jnp.int32(j))
            k = (k + j + 1) & ~j
        j >>= 1
        m = (m ^ (m << j)) & 0xFFFFFFFF
    return a


def _index_bits(seq):
    bits = seq.bit_length() - 1
    assert seq == 1 << bits and seq >= KEY_BITS * SUBLANES
    return bits


def _index_planes(nkb):
    shape = (nkb, SUBLANES, TQ)
    blk = lax.broadcasted_iota(I32, shape, 0)
    sub = lax.broadcasted_iota(I32, shape, 1)
    sub_bits = SUBLANES.bit_length() - 1
    word_bits = KEY_BITS.bit_length() - 1
    planes = []
    for b in range(_index_bits(nkb * TK) - 1, -1, -1):
        if b >= sub_bits + word_bits:
            on = ((nkb - 1 - blk) >> (b - sub_bits - word_bits)) & 1
            planes.append(jnp.where(on == 1, -1, 0))
        elif b >= sub_bits:
            t = b - sub_bits
            pattern = sum(1 << k for k in range(KEY_BITS) if (k >> t) & 1)
            planes.append(jnp.full(shape, np.int32(np.uint32(pattern)), I32))
        else:
            on = ((SUBLANES - 1 - sub) >> b) & 1
            planes.append(jnp.where(on == 1, -1, 0))
    return planes


def _bias_kind(j, i):
    return jnp.clip(j - i + 2, 0, 2)


def _stream_key_blocks(i, n_chains, s_ref, acc_ref, logits, v_aug):
    chains = range(n_chains)

    def stage(j, c, kind):
        s = logits(j, c, kind)
        s_ref[c] = s
        return jnp.max(s, axis=0, keepdims=True)

    def absorb(j, c, m_old, block_max):
        m_new = jnp.maximum(m_old, block_max)
        alpha = jnp.exp2(m_old - m_new)
        p = jnp.exp2(s_ref[c] - m_new).astype(BF16)
        acc_ref[c] = acc_ref[c] * alpha + _dot(v_aug(j, c), p)
        return m_new

    def step(j, carry, kind):
        ms, bms = carry
        out = [(absorb(j, c, ms[c], bms[c]), stage(j + 1, c, kind)) for c in chains]
        return tuple(o[0] for o in out), tuple(o[1] for o in out)

    acc_ref[...] = jnp.zeros_like(acc_ref)
    carry = ((jnp.full((1, TQ), NEG, F32),) * n_chains, tuple(stage(0, c, 'any') for c in chains))
    n_far = jnp.maximum(i - 2, 0)
    carry = lax.fori_loop(0, n_far, lambda j, cr: step(j, cr, 'far'), carry)
    ms, bms = lax.fori_loop(n_far, i, lambda j, cr: step(j, cr, 'near'), carry)
    for c in chains:
        absorb(i, c, ms[c], bms[c])


def _attn_in_kernel(x_ref, g_ref, w_ref, wt_ref, proj_ref, avt_ref, bvt_ref, iwt_ref, *, tm):
    hb = _rms(x_ref[0], g_ref[...]).astype(BF16)
    proj_ref[0] = _dot(hb, w_ref[...]).astype(BF16)
    t = _dot_nt(wt_ref[...], hb)
    for c in range(tm // TK):
        avt_ref[0, c] = t[ROW_AV:ROW_BV, c * TK:(c + 1) * TK].astype(BF16)
        bvt_ref[0, c] = t[ROW_BV:ROW_IW, c * TK:(c + 1) * TK].astype(BF16)
    iwt_ref[0] = t[ROW_IW:ROW_IW + IDX_HEADS, :] * (IDX_HEADS ** -0.5 * IDX_DIM ** -0.5)


def _attn_in(x, g, w, wt, *, tm=512):
    bsz, s, d = x.shape
    av_rows = ROW_BV - ROW_AV
    return pl.pallas_call(
        functools.partial(_attn_in_kernel, tm=tm),
        grid=(bsz, s // tm),
        in_specs=[
            pl.BlockSpec((1, tm, d), lambda b, i: (b, i, 0)),
            pl.BlockSpec((1, d), lambda b, i: (0, 0)),
            pl.BlockSpec((d, PROJ_W), lambda b, i: (0, 0)),
            pl.BlockSpec((VT_ROWS, d), lambda b, i: (0, 0)),
        ],
        out_specs=[
            pl.BlockSpec((1, tm, PROJ_W), lambda b, i: (b, i, 0)),
            pl.BlockSpec((1, tm // TK, av_rows, TK), lambda b, i: (b, i, 0, 0)),
            pl.BlockSpec((1, tm // TK, B_HEAD_DIM, TK), lambda b, i: (b, i, 0, 0)),
            pl.BlockSpec((1, IDX_HEADS, tm), lambda b, i: (b, 0, i)),
        ],
        out_shape=[
            jax.ShapeDtypeStruct((bsz, s, PROJ_W), BF16),
            jax.ShapeDtypeStruct((bsz, s // TK, av_rows, TK), BF16),
            jax.ShapeDtypeStruct((bsz, s // TK, B_HEAD_DIM, TK), BF16),
            jax.ShapeDtypeStruct((bsz, IDX_HEADS, s), F32),
        ],
        compiler_params=pltpu.CompilerParams(
            dimension_semantics=("arbitrary", "arbitrary"), vmem_limit_bytes=VMEM_LIMIT),
        name="attn_in",
    )(x, g, w, wt)


def _diff_attn_kernel(lam_ref, q_ref, k_ref, vt_ref, bias_ref, g_ref, o_ref,
                      vaug_ref, s_ref, acc_ref, ot_ref, *, lambda_init, seq):
    i = pl.program_id(1)
    nkb = seq // TK
    n_maps = 2 * A_HEADS

    @pl.when(i == 0)
    def _():
        for c in range(nkb):
            for h in range(A_HEADS):
                vaug_ref[h, c, 0:A_V_DIM, :] = vt_ref[0, c, h * A_V_DIM:(h + 1) * A_V_DIM, :]
                vaug_ref[h, c, A_V_DIM:A_V_DIM + ONES_ROWS, :] = jnp.ones((ONES_ROWS, TK), BF16)

    lv = lam_ref[...]
    lam = (jnp.exp(jnp.sum(lv[0:1] * lv[1:2], axis=-1, keepdims=True))
           - jnp.exp(jnp.sum(lv[2:3] * lv[3:4], axis=-1, keepdims=True)) + lambda_init)

    q = q_ref[0]
    qm = []
    for h in range(A_HEADS):
        qh = q[:, h * LANES:(h + 1) * LANES]
        qm += [_lane_band(qh, 0, A_HEAD_DIM), _lane_band(qh, A_HEAD_DIM, A_HEAD_DIM)]

    def logits(j, c, kind):
        h = c // 2
        kb = k_ref[0, pl.ds(pl.multiple_of(j * TK, TK), TK), h * LANES:(h + 1) * LANES]
        s = _dot_nt(kb, qm[c])
        return s if kind == 'far' else s + bias_ref[_bias_kind(j, i), h]

    _stream_key_blocks(i, n_maps, s_ref, acc_ref, logits, lambda j, c: vaug_ref[c // 2, j])

    for h in range(A_HEADS):
        a1 = acc_ref[2 * h]
        a2 = acc_ref[2 * h + 1]
        a = (a1[0:A_V_DIM] / a1[A_V_DIM:A_V_DIM + 1]
             - lam * (a2[0:A_V_DIM] / a2[A_V_DIM:A_V_DIM + 1]))
        y = a * lax.rsqrt(jnp.mean(a * a, axis=0, keepdims=True) + EPS) * g_ref[...]
        ot_ref[h * A_V_DIM:(h + 1) * A_V_DIM, :] = y * (1.0 - lambda_init)
    o_ref[0] = ot_ref[...].T.astype(BF16)


def _diff_attn(proj, avt, lam_vecs, bias, subln_g_col, *, lambda_init):
    bsz, s, _ = proj.shape
    nq = s // TQ
    nkb = s // TK
    width = A_HEADS * A_V_DIM
    return pl.pallas_call(
        functools.partial(_diff_attn_kernel, lambda_init=lambda_init, seq=s),
        grid=(bsz, nq),
        in_specs=[
            pl.BlockSpec((4, A_HEAD_DIM), lambda b, i: (0, 0)),
            pl.BlockSpec((1, TQ, width), lambda b, i: (b, i, COL_AQ // width)),
            pl.BlockSpec((1, s, width), lambda b, i: (b, 0, COL_AK // width)),
            pl.BlockSpec((1, nkb, width, TK), lambda b, i: (b, 0, 0, 0)),
            pl.BlockSpec((3, A_HEADS, TK, TQ), lambda b, i: (0, 0, 0, 0)),
            pl.BlockSpec((A_V_DIM, 1), lambda b, i: (0, 0)),
        ],
        out_specs=pl.BlockSpec((1, TQ, width), lambda b, i: (b, i, 0)),
        out_shape=jax.ShapeDtypeStruct((bsz, s, width), BF16),
        scratch_shapes=[
            pltpu.VMEM((A_HEADS, nkb, A_V_DIM + ONES_ROWS, TK), BF16),
            pltpu.VMEM((2 * A_HEADS, TK, TQ), F32),
            pltpu.VMEM((2 * A_HEADS, A_V_DIM + ONES_ROWS, TQ), F32),
            pltpu.VMEM((width, TQ), F32),
        ],
        compiler_params=pltpu.CompilerParams(
            dimension_semantics=("arbitrary", "arbitrary"), vmem_limit_bytes=VMEM_LIMIT),
        name="diff_attn",
    )(lam_vecs, proj, proj, avt, bias, subln_g_col)


def _dsa_kernel(qb_ref, iq_ref, kb_ref, ik_ref, vt_ref, iwt_ref, bias_ref, o_ref,
                score_ref, key_ref, plane_ref, mask_ref, vaug_ref, s_ref, acc_ref, ot_ref,
                *, top_k, seq):
    i = pl.program_id(1)
    nkb = seq // TK

    @pl.when(i == 0)
    def _():
        for c in range(nkb):
            vaug_ref[c, 0:B_HEAD_DIM, :] = vt_ref[0, c]
            vaug_ref[c, B_HEAD_DIM:B_HEAD_DIM + ONES_ROWS, :] = jnp.ones((ONES_ROWS, TK), BF16)

    @pl.when((i == 0) & (pl.program_id(0) == 0))
    def _():
        plane_ref[0:KEY_BITS] = jnp.zeros((KEY_BITS, nkb, SUBLANES, TQ), I32)
        for t, plane in enumerate(_index_planes(nkb)):
            plane_ref[KEY_BITS + t] = plane

    def rows(ref, j):
        return ref[0, pl.ds(pl.multiple_of(j * TK, TK), TK), :]

    krow = lax.broadcasted_iota(I32, (TK, TQ), 0)
    qcol = lax.broadcasted_iota(I32, (TK, TQ), 1)

    iq = iq_ref[0]
    per_blk = LANES // IDX_DIM
    iqh = [_lane_band(iq[:, (h // per_blk) * LANES:(h // per_blk + 1) * LANES],
                      (h % per_blk) * IDX_DIM, IDX_DIM) for h in range(IDX_HEADS)]
    w = iwt_ref[0]

    def stage_scores(j):
        ikb = rows(ik_ref, j)
        half = TK // 2
        for r in (0, half):
            acc = jnp.zeros((half, TQ), F32)
            for h in range(IDX_HEADS):
                acc = acc + jnp.maximum(_dot_nt(ikb[r:r + half], iqh[h]), 0.0) * w[h:h + 1, :]
            score_ref[r:r + half, :] = acc

    def staged_keys():
        acc = score_ref[...]
        acc = jnp.where(acc == 0.0, 0.0, acc)
        bits = lax.bitcast_convert_type(acc, I32)
        return bits ^ ((bits >> 31) & 0x7FFFFFFF)

    def store_keys(j, keys):
        key_ref[j] = keys
        for p, plane in enumerate(_bit_planes(keys)):
            plane_ref[p, j] = plane

    def score_body(j, c):
        keys = staged_keys()
        stage_scores(j + 1)
        store_keys(j, keys)
        return c

    stage_scores(0)
    lax.fori_loop(0, i, score_body, 0)
    admissible = (krow // CHUNK) <= (qcol // CHUNK)
    store_keys(i, jnp.where(admissible, staged_keys(), KEY_NEG_INF))

    blk = lax.broadcasted_iota(I32, (nkb, SUBLANES, TQ), 0)
    alive0 = jnp.where(blk <= i, -1, 0)

    def bit_body(p, carry, n_bits):
        alive, n_gt, thr_u = carry
        hit = alive & plane_ref[p]
        pc = jnp.sum(lax.population_count(hit), axis=0)
        n_hit = jnp.sum(pc.astype(F32), axis=0, keepdims=True)
        take = (n_gt + n_hit) >= top_k
        thr_u = thr_u | jnp.where(take, lax.shift_left(jnp.int32(1), n_bits - 1 - p), 0)
        n_gt = n_gt + jnp.where(take, 0.0, n_hit)
        alive = jnp.where(take, hit, alive ^ hit)
        return alive, n_gt, thr_u

    zero = jnp.zeros((1, TQ), I32)
    alive, n_gt, thr_u = lax.fori_loop(
        0, KEY_BITS, lambda p, c: bit_body(p, c, KEY_BITS), (alive0, jnp.zeros((1, TQ), F32), zero))
    idx_bits = _index_bits(seq)
    _, _, rev_idx = lax.fori_loop(
        KEY_BITS, KEY_BITS + idx_bits,
        lambda p, c: bit_body(p, c, KEY_BITS + idx_bits), (alive, n_gt, zero))
    thr = thr_u ^ INT_MIN
    lim = seq - rev_idx

    def mask_body(j, c):
        kb = key_ref[j]
        tie_ok = jnp.where((krow + j * TK) < lim, 0.0, NEG)
        sel = jnp.where(kb > thr, 0.0, jnp.where(kb == thr, tie_ok, NEG))
        mask_ref[j] = jnp.where(kb > KEY_NEG_INF, sel, NEG)
        return c

    lax.fori_loop(0, i + 1, mask_body, 0)

    qb = qb_ref[0]
    qh = [_lane_band(qb[:, (h // 2) * LANES:(h // 2 + 1) * LANES],
                     (h % 2) * B_HEAD_DIM, B_HEAD_DIM) for h in range(B_HEADS)]

    def logits(j, h, kind):
        s = _dot_nt(rows(kb_ref, j), qh[h]) + mask_ref[j]
        return s if kind == 'far' else s + bias_ref[_bias_kind(j, i), h]

    _stream_key_blocks(i, B_HEADS, s_ref, acc_ref, logits, lambda j, h: vaug_ref[j])

    for h in range(B_HEADS):
        a = acc_ref[h]
        ot_ref[h * B_HEAD_DIM:(h + 1) * B_HEAD_DIM, :] = (
            a[0:B_HEAD_DIM] / a[B_HEAD_DIM:B_HEAD_DIM + 1])
    o_ref[0] = ot_ref[...].T.astype(BF16)


def _dsa_attn(proj, bvt, iwt, bias, *, top_k):
    bsz, s, _ = proj.shape
    nq = s // TQ
    nkb = s // TK
    width = B_HEADS * B_HEAD_DIM
    iq_w = IDX_HEADS * IDX_DIM
    return pl.pallas_call(
        functools.partial(_dsa_kernel, top_k=top_k, seq=s),
        grid=(bsz, nq),
        in_specs=[
            pl.BlockSpec((1, TQ, width), lambda b, i: (b, i, COL_BQ // width)),
            pl.BlockSpec((1, TQ, iq_w), lambda b, i: (b, i, COL_IQ // iq_w)),
            pl.BlockSpec((1, s, LANES), lambda b, i: (b, 0, COL_KB // LANES)),
            pl.BlockSpec((1, s, LANES), lambda b, i: (b, 0, COL_IK // LANES)),
            pl.BlockSpec((1, nkb, B_HEAD_DIM, TK), lambda b, i: (b, 0, 0, 0)),
            pl.BlockSpec((1, IDX_HEADS, TQ), lambda b, i: (b, 0, i)),
            pl.BlockSpec((3, B_HEADS, TK, TQ), lambda b, i: (0, 0, 0, 0)),
        ],
        out_specs=pl.BlockSpec((1, TQ, width), lambda b, i: (b, i, 0)),
        out_shape=jax.ShapeDtypeStruct((bsz, s, width), BF16),
        scratch_shapes=[
            pltpu.VMEM((TK, TQ), F32),
            pltpu.VMEM((nkb, TK, TQ), I32),
            pltpu.VMEM((KEY_BITS + _index_bits(s), nkb, SUBLANES, TQ), I32),
            pltpu.VMEM((nkb, TK, TQ), F32),
            pltpu.VMEM((nkb, B_HEAD_DIM + ONES_ROWS, TK), BF16),
            pltpu.VMEM((B_HEADS, TK, TQ), F32),
            pltpu.VMEM((B_HEADS, B_HEAD_DIM + ONES_ROWS, TQ), F32),
            pltpu.VMEM((width, TQ), F32),
        ],
        compiler_params=pltpu.CompilerParams(
            dimension_semantics=("arbitrary", "arbitrary"), vmem_limit_bytes=VMEM_LIMIT),
        name="dsa_attn",
    )(proj, proj, proj, proj, bvt, iwt, bias)


def _tail_kernel(ya_ref, yb_ref, wo_ref, x_ref, g_mix_ref, g_in_ref, wu_ref, wd_ref, g_out_ref,
                 o_ref, x1_ref, h_ref, acc_ref):
    f = pl.program_id(1)

    @pl.when(f == 0)
    def _():
        half = ya_ref.shape[-1]
        m = _dot(ya_ref[...], wo_ref[0:half, :]) + _dot(yb_ref[...], wo_ref[half:2 * half, :])
        x1 = x_ref[...] + _rms(m, g_mix_ref[...])
        x1_ref[...] = x1
        h_ref[...] = _rms(x1, g_in_ref[...]).astype(BF16)
        acc_ref[...] = jnp.zeros_like(acc_ref)

    u = jnp.maximum(_dot(h_ref[...], wu_ref[...]), 0.0)
    acc_ref[...] += _dot((u * u).astype(BF16), wd_ref[...])

    @pl.when(f == pl.num_programs(1) - 1)
    def _():
        o_ref[...] = x1_ref[...] + _rms(acc_ref[...], g_out_ref[...])


def _layer_tail(ya, yb, wo, x, g_mix, g_in, wu, wd, g_out, *, tm=1024, tf=1024):
    n, d = x.shape
    half = ya.shape[-1]
    ff = wu.shape[1]
    return pl.pallas_call(
        _tail_kernel,
        grid=(n // tm, ff // tf),
        in_specs=[
            pl.BlockSpec((tm, half), lambda i, f: (i, 0)),
            pl.BlockSpec((tm, half), lambda i, f: (i, 0)),
            pl.BlockSpec((2 * half, d), lambda i, f: (0, 0)),
            pl.BlockSpec((tm, d), lambda i, f: (i, 0)),
            pl.BlockSpec((1, d), lambda i, f: (0, 0)),
            pl.BlockSpec((1, d), lambda i, f: (0, 0)),
            pl.BlockSpec((d, tf), lambda i, f: (0, f)),
            pl.BlockSpec((tf, d), lambda i, f: (f, 0)),
            pl.BlockSpec((1, d), lambda i, f: (0, 0)),
        ],
        out_specs=pl.BlockSpec((tm, d), lambda i, f: (i, 0)),
        out_shape=jax.ShapeDtypeStruct((n, d), F32),
        scratch_shapes=[pltpu.VMEM((tm, d), F32), pltpu.VMEM((tm, d), BF16),
                        pltpu.VMEM((tm, d), F32)],
        compiler_params=pltpu.CompilerParams(
            dimension_semantics=("arbitrary", "arbitrary"), vmem_limit_bytes=VMEM_LIMIT),
        name="layer_tail",
    )(ya, yb, wo, x, g_mix, g_in, wu, wd, g_out)


CONV_ROWS = 64
U_HIST = 32
Z_HIST = 8


def _conv_kernel(x_ref, g_in_ref, w_in_ref, dw_w_ref, dw_b_ref, ln_g_ref, ln_b_ref, sc_w_ref,
                 u_ref, z_ref, ubuf_ref, zbuf_ref, shift_ref, conv_ref, *, ts):
    @pl.when(pl.program_id(1) == 0)
    def _():
        ubuf_ref[0:U_HIST, :] = jnp.zeros((U_HIST, CONV_CH), F32)
        zbuf_ref[0:Z_HIST, :] = jnp.zeros((Z_HIST, SC_CH), F32)

    x = x_ref[0]
    hb = _rms(x, g_in_ref[...]).astype(BF16)
    c = CONV_CH
    ca = _dot(hb, w_in_ref[:, 0:c])
    cg = _dot(hb, w_in_ref[:, c:2 * c])
    ubuf_ref[U_HIST:U_HIST + ts, :] = ca * jax.nn.sigmoid(cg)
    dc = _dot(hb, w_in_ref[:, 3 * c:4 * c])
    dh = _dot(hb, w_in_ref[:, 4 * c:5 * c])
    zbuf_ref[Z_HIST:Z_HIST + ts, :] = dc * dh

    first_off = U_HIST - (CONV_WIDTH - 1)
    shift_rows = shift_ref.shape[1]
    for r in range(1, SUBLANES):
        shift_ref[r - 1] = ubuf_ref[r:r + shift_rows, :]

    def conv_rows(rb, carry):
        base = pl.multiple_of(rb * CONV_ROWS, CONV_ROWS)
        acc = jnp.zeros((CONV_ROWS, c), F32)
        for j in range(CONV_WIDTH):
            r, a = (first_off + j) % SUBLANES, (first_off + j) // SUBLANES
            rows = pl.ds(base + SUBLANES * a, CONV_ROWS)
            taps = ubuf_ref[rows, :] if r == 0 else shift_ref[r - 1, rows, :]
            acc = acc + dw_w_ref[j:j + 1, :] * taps
        conv_ref[pl.ds(base, CONV_ROWS), :] = acc
        return carry

    lax.fori_loop(0, ts // CONV_ROWS, conv_rows, 0)
    u = conv_ref[...] + dw_b_ref[...]
    mu = jnp.mean(u, axis=-1, keepdims=True)
    ctr = u - mu
    var = jnp.mean(ctr * ctr, axis=-1, keepdims=True)
    u = ctr * lax.rsqrt(var + EPS) * ln_g_ref[...] + ln_b_ref[...]
    u = u * jax.nn.sigmoid(u)

    z = jnp.zeros((ts, c), F32)
    for j in range(SC_WIDTH):
        off = Z_HIST - (SC_WIDTH - 1) + j
        z = z + sc_w_ref[j:j + 1, :] * zbuf_ref[off:off + ts, :]
    z = _dot(hb, w_in_ref[:, 2 * c:3 * c]) * z

    ubuf_ref[0:U_HIST, :] = ubuf_ref[ts:ts + U_HIST, :]
    zbuf_ref[0:Z_HIST, :] = zbuf_ref[ts:ts + Z_HIST, :]

    u_ref[0] = u.astype(BF16)
    z_ref[0] = z.astype(BF16)


def _conv_mixer(x, g_in, w_in, dw_w, dw_b, ln_g, ln_b, sc_w, *, ts=512):
    bsz, s, d = x.shape
    full = lambda a: pl.BlockSpec(a.shape, lambda b, i: (0,) * a.ndim)
    return pl.pallas_call(
        functools.partial(_conv_kernel, ts=ts),
        grid=(bsz, s // ts),
        in_specs=[pl.BlockSpec((1, ts, d), lambda b, i: (b, i, 0)),
                  full(g_in), full(w_in), full(dw_w), full(dw_b), full(ln_g), full(ln_b),
                  full(sc_w)],
        out_specs=[pl.BlockSpec((1, ts, CONV_CH), lambda b, i: (b, i, 0)),
                   pl.BlockSpec((1, ts, SC_CH), lambda b, i: (b, i, 0))],
        out_shape=[jax.ShapeDtypeStruct((bsz, s, CONV_CH), BF16),
                   jax.ShapeDtypeStruct((bsz, s, SC_CH), BF16)],
        scratch_shapes=[pltpu.VMEM((U_HIST + ts, CONV_CH), F32),
                        pltpu.VMEM((Z_HIST + ts, SC_CH), F32),
                        pltpu.VMEM((SUBLANES - 1, ts + U_HIST - SUBLANES, CONV_CH), F32),
                        pltpu.VMEM((ts, CONV_CH), F32)],
        compiler_params=pltpu.CompilerParams(
            dimension_semantics=("arbitrary", "arbitrary"), vmem_limit_bytes=VMEM_LIMIT),
        name="conv_mixer",
    )(x, g_in, w_in, dw_w, dw_b, ln_g, ln_b, sc_w)


def _t5_bucket(rel):
    nb = NUM_BUCKETS // 2
    ret = jnp.where(rel > 0, nb, 0)
    n = jnp.abs(rel)
    max_exact = nb // 2
    nf = jnp.maximum(n, 1).astype(jnp.float32)
    large = max_exact + (jnp.log(nf / max_exact) / math.log(MAX_DISTANCE / max_exact)
                         * (nb - max_exact)).astype(jnp.int32)
    large = jnp.minimum(large, nb - 1)
    return ret + jnp.where(n < max_exact, n, large)


def _toeplitz(f, off):
    heads, length = f.shape
    g = jnp.tile(jnp.pad(f, ((0, 0), (0, 1))), (1, TQ))[:, :TQ * length]
    return g.reshape(heads, TQ, length)[:, :, off:off + TK]


def _bias_tables(rel_bias):
    lo = -(TQ + TK - 1)
    rel = jnp.arange(lo, TK, dtype=jnp.int32)
    far_bucket = NUM_BUCKETS // 2 - 1
    f = (rel_bias[_t5_bucket(rel)].astype(F32) - rel_bias[far_bucket].astype(F32)[None, :]).T
    f = f * LOG2E
    diag = jnp.swapaxes(_toeplitz(f, -lo), 1, 2)
    prev = jnp.swapaxes(_toeplitz(f, -lo - TK), 1, 2)
    ki = np.arange(TK)[:, None]
    qi = np.arange(TQ)[None, :]
    adm = jnp.asarray((ki // CHUNK) <= (qi // CHUNK))
    none = jnp.zeros_like(diag)
    a_diag = jnp.where(adm[None], diag[:A_HEADS], NEG)
    a_bias = jnp.stack([none[:A_HEADS], prev[:A_HEADS], a_diag], axis=0)
    b_bias = jnp.stack([none[A_HEADS:], prev[A_HEADS:], diag[A_HEADS:]], axis=0)
    return a_bias, b_bias


def _attn_weights(w_in):
    o = np.cumsum([0, 512, 512, 512, 512, 64, 64, 256, 32, 8])
    aq, ak, av, bq, bk, bv, iq, ik, iw = [w_in[:, o[n]:o[n + 1]] for n in range(9)]
    aq = aq * (A_HEAD_DIM ** -0.5 * LOG2E)
    bq = bq * (B_HEAD_DIM ** -0.5 * LOG2E)
    w = jnp.concatenate([aq, ak, bq, iq, bk, bk, ik, ik, ik, ik], axis=1).astype(BF16)
    pad = jnp.zeros((w_in.shape[0], VT_ROWS - ROW_IW - IDX_HEADS), w_in.dtype)
    wt = jnp.concatenate([av, bv, iw, pad], axis=1).T.astype(BF16)
    return w, wt


def kernel(x, rel_bias, norm_g, w_mlp_up, w_mlp_down, attn_w_in, attn_w_out, diff_lambda,
           diff_subln_g, conv_w_in, conv_w_out, conv_dw_w, conv_dw_b, conv_ln_g, conv_ln_b,
           sconv_w):
    bsz, s, d = x.shape
    depth = norm_g.shape[0]
    top_k = min(TOPK_MAX, s // 4)
    row = lambda v: v.reshape(1, -1)
    a_bias, b_bias = _bias_tables(rel_bias)
    for i in range(depth):
        j = i // 2
        if i % 2 == 0:
            lambda_init = 0.8 - 0.6 * math.exp(-0.3 * i)
            w, wt = _attn_weights(attn_w_in[j])
            proj, avt, bvt, iwt = _attn_in(x.reshape(bsz, s, d), row(norm_g[i, 0]), w, wt)
            ya = _diff_attn(proj, avt, diff_lambda[j], a_bias, diff_subln_g[j].reshape(-1, 1),
                            lambda_init=lambda_init)
            yb = _dsa_attn(proj, bvt, iwt, b_bias, top_k=top_k)
            w_out = attn_w_out[j]
        else:
            ya, yb = _conv_mixer(x.reshape(bsz, s, d), row(norm_g[i, 0]),
                                 conv_w_in[j].astype(BF16), conv_dw_w[j], row(conv_dw_b[j]),
                                 row(conv_ln_g[j]), row(conv_ln_b[j]), sconv_w[j])
            w_out = conv_w_out[j]
        x = _layer_tail(ya.reshape(bsz * s, -1), yb.reshape(bsz * s, -1), w_out.astype(BF16),
                        x.reshape(bsz * s, d), row(norm_g[i, 1]), row(norm_g[i, 2]),
                        w_mlp_up[i].astype(BF16), w_mlp_down[i].astype(BF16), row(norm_g[i, 3]))
    return x.reshape(bsz, s, d)
```

```python
import functools
import math

import numpy as np
import jax
import jax.numpy as jnp
from jax import lax
from jax.experimental import pallas as pl
from jax.experimental.pallas import tpu as pltpu

F32 = jnp.float32
BF16 = jnp.bfloat16
I32 = jnp.int32

D_MODEL = 1024
CHUNK = 64
NUM_BUCKETS = 32
MAX_DISTANCE = 128
EPS = 1e-6
NEG = -1e30
LOG2E = math.log2(math.e)
A_HEADS = 4
A_HEAD_DIM = 64
A_V_DIM = 2 * A_HEAD_DIM
B_HEADS = 8
B_HEAD_DIM = 64
IDX_HEADS = 8
IDX_DIM = 32
TOPK_MAX = 256
CONV_CH = 512
CONV_WIDTH = 31
SC_CH = 512
SC_WIDTH = 3
D_FF = 4 * D_MODEL

LANES = 128
SUBLANES = 8
VMEM_LIMIT = 56 * 1024 * 1024

TQ = 256
TK = 256
COL_AQ, COL_AK, COL_BQ, COL_IQ, COL_KB, COL_IK, PROJ_W = 0, 512, 1024, 1536, 1792, 1920, 2048
ROW_AV, ROW_BV, ROW_IW, VT_ROWS = 0, 512, 576, 640
ONES_ROWS = 16
INT_MIN = -2 ** 31
KEY_NEG_INF = INT_MIN + 0x7FFFFF


def _rms(x, g):
    return x * lax.rsqrt(jnp.mean(x * x, axis=-1, keepdims=True) + EPS) * g


def _dot_nt(a, b):
    return lax.dot_general(a, b, (((1,), (1,)), ((), ())), preferred_element_type=F32)


def _dot(a, b):
    return jnp.dot(a, b, preferred_element_type=F32)


def _lane_band(x, lo, width):
    lane = lax.broadcasted_iota(I32, x.shape, 1)
    return jnp.where((lane >= lo) & (lane < lo + width), x, jnp.zeros_like(x))


KEY_BITS = 32


def _bit_planes(keys):
    assert keys.shape[0] == KEY_BITS * SUBLANES
    u = keys ^ INT_MIN
    a = [u[SUBLANES * r:SUBLANES * (r + 1), :] for r in range(KEY_BITS)]
    j, m = KEY_BITS // 2, (1 << (KEY_BITS // 2)) - 1
    while j:
        mask = np.int32(np.uint32(m))
        k = 0
        while k < KEY_BITS:
            t = (a[k] ^ lax.shift_right_logical(a[k + j], jnp.int32(j))) & mask
            a[k] = a[k] ^ t
            a[k + j] = a[k + j] ^ lax.shift_left(t, jnp.int32(j))
            k = (k + j + 1) & ~j
        j >>= 1
        m = (m ^ (m << j)) & 0xFFFFFFFF
    return a


def _index_bits(seq):
    bits = seq.bit_length() - 1
    assert seq == 1 << bits and seq >= KEY_BITS * SUBLANES
    return bits


def _index_planes(nkb):
    shape = (nkb, SUBLANES, TQ)
    blk = lax.broadcasted_iota(I32, shape, 0)
    sub = lax.broadcasted_iota(I32, shape, 1)
    sub_bits = SUBLANES.bit_length() - 1
    word_bits = KEY_BITS.bit_length() - 1
    planes = []
    for b in range(_index_bits(nkb * TK) - 1, -1, -1):
        if b >= sub_bits + word_bits:
            on = ((nkb - 1 - blk) >> (b - sub_bits - word_bits)) & 1
            planes.append(jnp.where(on == 1, -1, 0))
        elif b >= sub_bits:
            t = b - sub_bits
            pattern = sum(1 << k for k in range(KEY_BITS) if (k >> t) & 1)
            planes.append(jnp.full(shape, np.int32(np.uint32(pattern)), I32))
        else:
            on = ((SUBLANES - 1 - sub) >> b) & 1
            planes.append(jnp.where(on == 1, -1, 0))
    return planes


def _bias_kind(j, i):
    return jnp.clip(j - i + 2, 0, 2)


def _stream_key_blocks(i, n_chains, s_ref, acc_ref, logits, v_aug):
    chains = range(n_chains)

    def stage(j, c, kind):
        s = logits(j, c, kind)
        s_ref[c] = s
        return jnp.max(s, axis=0, keepdims=True)

    def absorb(j, c, m_old, block_max):
        m_new = jnp.maximum(m_old, block_max)
        alpha = jnp.exp2(m_old - m_new)
        p = jnp.exp2(s_ref[c] - m_new).astype(BF16)
        acc_ref[c] = acc_ref[c] * alpha + _dot(v_aug(j, c), p)
        return m_new

    def step(j, carry, kind):
        ms, bms = carry
        out = [(absorb(j, c, ms[c], bms[c]), stage(j + 1, c, kind)) for c in chains]
        return tuple(o[0] for o in out), tuple(o[1] for o in out)

    acc_ref[...] = jnp.zeros_like(acc_ref)
    carry = ((jnp.full((1, TQ), NEG, F32),) * n_chains, tuple(stage(0, c, 'any') for c in chains))
    n_far = jnp.maximum(i - 2, 0)
    carry = lax.fori_loop(0, n_far, lambda j, cr: step(j, cr, 'far'), carry)
    ms, bms = lax.fori_loop(n_far, i, lambda j, cr: step(j, cr, 'near'), carry)
    for c in chains:
        absorb(i, c, ms[c], bms[c])


def _attn_in_kernel(x_ref, g_ref, w_ref, wt_ref, proj_ref, avt_ref, bvt_ref, iwt_ref, *, tm):
    hb = _rms(x_ref[0], g_ref[...]).astype(BF16)
    proj_ref[0] = _dot(hb, w_ref[...]).astype(BF16)
    t = _dot_nt(wt_ref[...], hb)
    for c in range(tm // TK):
        avt_ref[0, c] = t[ROW_AV:ROW_BV, c * TK:(c + 1) * TK].astype(BF16)
        bvt_ref[0, c] = t[ROW_BV:ROW_IW, c * TK:(c + 1) * TK].astype(BF16)
    iwt_ref[0] = t[ROW_IW:ROW_IW + IDX_HEADS, :] * (IDX_HEADS ** -0.5 * IDX_DIM ** -0.5)


def _attn_in(x, g, w, wt, *, tm=512):
    bsz, s, d = x.shape
    av_rows = ROW_BV - ROW_AV
    return pl.pallas_call(
        functools.partial(_attn_in_kernel, tm=tm),
        grid=(bsz, s // tm),
        in_specs=[
            pl.BlockSpec((1, tm, d), lambda b, i: (b, i, 0)),
            pl.BlockSpec((1, d), lambda b, i: (0, 0)),
            pl.BlockSpec((d, PROJ_W), lambda b, i: (0, 0)),
            pl.BlockSpec((VT_ROWS, d), lambda b, i: (0, 0)),
        ],
        out_specs=[
            pl.BlockSpec((1, tm, PROJ_W), lambda b, i: (b, i, 0)),
            pl.BlockSpec((1, tm // TK, av_rows, TK), lambda b, i: (b, i, 0, 0)),
            pl.BlockSpec((1, tm // TK, B_HEAD_DIM, TK), lambda b, i: (b, i, 0, 0)),
            pl.BlockSpec((1, IDX_HEADS, tm), lambda b, i: (b, 0, i)),
        ],
        out_shape=[
            jax.ShapeDtypeStruct((bsz, s, PROJ_W), BF16),
            jax.ShapeDtypeStruct((bsz, s // TK, av_rows, TK), BF16),
            jax.ShapeDtypeStruct((bsz, s // TK, B_HEAD_DIM, TK), BF16),
            jax.ShapeDtypeStruct((bsz, IDX_HEADS, s), F32),
        ],
        compiler_params=pltpu.CompilerParams(
            dimension_semantics=("arbitrary", "arbitrary"), vmem_limit_bytes=VMEM_LIMIT),
        name="attn_in",
    )(x, g, w, wt)


def _diff_attn_kernel(lam_ref, q_ref, k_ref, vt_ref, bias_ref, g_ref, o_ref,
                      vaug_ref, s_ref, acc_ref, ot_ref, *, lambda_init, seq):
    i = pl.program_id(1)
    nkb = seq // TK
    n_maps = 2 * A_HEADS

    @pl.when(i == 0)
    def _():
        for c in range(nkb):
            for h in range(A_HEADS):
                vaug_ref[h, c, 0:A_V_DIM, :] = vt_ref[0, c, h * A_V_DIM:(h + 1) * A_V_DIM, :]
                vaug_ref[h, c, A_V_DIM:A_V_DIM + ONES_ROWS, :] = jnp.ones((ONES_ROWS, TK), BF16)

    lv = lam_ref[...]
    lam = (jnp.exp(jnp.sum(lv[0:1] * lv[1:2], axis=-1, keepdims=True))
           - jnp.exp(jnp.sum(lv[2:3] * lv[3:4], axis=-1, keepdims=True)) + lambda_init)

    q = q_ref[0]
    qm = []
    for h in range(A_HEADS):
        qh = q[:, h * LANES:(h + 1) * LANES]
        qm += [_lane_band(qh, 0, A_HEAD_DIM), _lane_band(qh, A_HEAD_DIM, A_HEAD_DIM)]

    def logits(j, c, kind):
        h = c // 2
        kb = k_ref[0, pl.ds(pl.multiple_of(j * TK, TK), TK), h * LANES:(h + 1) * LANES]
        s = _dot_nt(kb, qm[c])
        return s if kind == 'far' else s + bias_ref[_bias_kind(j, i), h]

    _stream_key_blocks(i, n_maps, s_ref, acc_ref, logits, lambda j, c: vaug_ref[c // 2, j])

    for h in range(A_HEADS):
        a1 = acc_ref[2 * h]
        a2 = acc_ref[2 * h + 1]
        a = (a1[0:A_V_DIM] / a1[A_V_DIM:A_V_DIM + 1]
             - lam * (a2[0:A_V_DIM] / a2[A_V_DIM:A_V_DIM + 1]))
        y = a * lax.rsqrt(jnp.mean(a * a, axis=0, keepdims=True) + EPS) * g_ref[...]
        ot_ref[h * A_V_DIM:(h + 1) * A_V_DIM, :] = y * (1.0 - lambda_init)
    o_ref[0] = ot_ref[...].T.astype(BF16)


def _diff_attn(proj, avt, lam_vecs, bias, subln_g_col, *, lambda_init):
    bsz, s, _ = proj.shape
    nq = s // TQ
    nkb = s // TK
    width = A_HEADS * A_V_DIM
    return pl.pallas_call(
        functools.partial(_diff_attn_kernel, lambda_init=lambda_init, seq=s),
        grid=(bsz, nq),
        in_specs=[
            pl.BlockSpec((4, A_HEAD_DIM), lambda b, i: (0, 0)),
            pl.BlockSpec((1, TQ, width), lambda b, i: (b, i, COL_AQ // width)),
            pl.BlockSpec((1, s, width), lambda b, i: (b, 0, COL_AK // width)),
            pl.BlockSpec((1, nkb, width, TK), lambda b, i: (b, 0, 0, 0)),
            pl.BlockSpec((3, A_HEADS, TK, TQ), lambda b, i: (0, 0, 0, 0)),
            pl.BlockSpec((A_V_DIM, 1), lambda b, i: (0, 0)),
        ],
        out_specs=pl.BlockSpec((1, TQ, width), lambda b, i: (b, i, 0)),
        out_shape=jax.ShapeDtypeStruct((bsz, s, width), BF16),
        scratch_shapes=[
            pltpu.VMEM((A_HEADS, nkb, A_V_DIM + ONES_ROWS, TK), BF16),
            pltpu.VMEM((2 * A_HEADS, TK, TQ), F32),
            pltpu.VMEM((2 * A_HEADS, A_V_DIM + ONES_ROWS, TQ), F32),
            pltpu.VMEM((width, TQ), F32),
        ],
        compiler_params=pltpu.CompilerParams(
            dimension_semantics=("arbitrary", "arbitrary"), vmem_limit_bytes=VMEM_LIMIT),
        name="diff_attn",
    )(lam_vecs, proj, proj, avt, bias, subln_g_col)


def _dsa_kernel(qb_ref, iq_ref, kb_ref, ik_ref, vt_ref, iwt_ref, bias_ref, o_ref,
                score_ref, key_ref, plane_ref, mask_ref, vaug_ref, s_ref, acc_ref, ot_ref,
                *, top_k, seq):
    i = pl.program_id(1)
    nkb = seq // TK

    @pl.when(i == 0)
    def _():
        for c in range(nkb):
            vaug_ref[c, 0:B_HEAD_DIM, :] = vt_ref[0, c]
            vaug_ref[c, B_HEAD_DIM:B_HEAD_DIM + ONES_ROWS, :] = jnp.ones((ONES_ROWS, TK), BF16)

    @pl.when((i == 0) & (pl.program_id(0) == 0))
    def _():
        plane_ref[0:KEY_BITS] = jnp.zeros((KEY_BITS, nkb, SUBLANES, TQ), I32)
        for t, plane in enumerate(_index_planes(nkb)):
            plane_ref[KEY_BITS + t] = plane

    def rows(ref, j):
        return ref[0, pl.ds(pl.multiple_of(j * TK, TK), TK), :]

    krow = lax.broadcasted_iota(I32, (TK, TQ), 0)
    qcol = lax.broadcasted_iota(I32, (TK, TQ), 1)

    iq = iq_ref[0]
    per_blk = LANES // IDX_DIM
    iqh = [_lane_band(iq[:, (h // per_blk) * LANES:(h // per_blk + 1) * LANES],
                      (h % per_blk) * IDX_DIM, IDX_DIM) for h in range(IDX_HEADS)]
    w = iwt_ref[0]

    def stage_scores(j):
        ikb = rows(ik_ref, j)
        half = TK // 2
        for r in (0, half):
            acc = jnp.zeros((half, TQ), F32)
            for h in range(IDX_HEADS):
                acc = acc + jnp.maximum(_dot_nt(ikb[r:r + half], iqh[h]), 0.0) * w[h:h + 1, :]
            score_ref[r:r + half, :] = acc

    def staged_keys():
        acc = score_ref[...]
        acc = jnp.where(acc == 0.0, 0.0, acc)
        bits = lax.bitcast_convert_type(acc, I32)
        return bits ^ ((bits >> 31) & 0x7FFFFFFF)

    def store_keys(j, keys):
        key_ref[j] = keys
        for p, plane in enumerate(_bit_planes(keys)):
            plane_ref[p, j] = plane

    def score_body(j, c):
        keys = staged_keys()
        stage_scores(j + 1)
        store_keys(j, keys)
        return c

    stage_scores(0)
    lax.fori_loop(0, i, score_body, 0)
    admissible = (krow // CHUNK) <= (qcol // CHUNK)
    store_keys(i, jnp.where(admissible, staged_keys(), KEY_NEG_INF))

    blk = lax.broadcasted_iota(I32, (nkb, SUBLANES, TQ), 0)
    alive0 = jnp.where(blk <= i, -1, 0)

    def bit_body(p, carry, n_bits):
        alive, n_gt, thr_u = carry
        hit = alive & plane_ref[p]
        pc = jnp.sum(lax.population_count(hit), axis=0)
        n_hit = jnp.sum(pc.astype(F32), axis=0, keepdims=True)
        take = (n_gt + n_hit) >= top_k
        thr_u = thr_u | jnp.where(take, lax.shift_left(jnp.int32(1), n_bits - 1 - p), 0)
        n_gt = n_gt + jnp.where(take, 0.0, n_hit)
        alive = jnp.where(take, hit, alive ^ hit)
        return alive, n_gt, thr_u

    zero = jnp.zeros((1, TQ), I32)
    alive, n_gt, thr_u = lax.fori_loop(
        0, KEY_BITS, lambda p, c: bit_body(p, c, KEY_BITS), (alive0, jnp.zeros((1, TQ), F32), zero))
    idx_bits = _index_bits(seq)
    _, _, rev_idx = lax.fori_loop(
        KEY_BITS, KEY_BITS + idx_bits,
        lambda p, c: bit_body(p, c, KEY_BITS + idx_bits), (alive, n_gt, zero))
    thr = thr_u ^ INT_MIN
    lim = seq - rev_idx

    def mask_body(j, c):
        kb = key_ref[j]
        tie_ok = jnp.where((krow + j * TK) < lim, 0.0, NEG)
        sel = jnp.where(kb > thr, 0.0, jnp.where(kb == thr, tie_ok, NEG))
        mask_ref[j] = jnp.where(kb > KEY_NEG_INF, sel, NEG)
        return c

    lax.fori_loop(0, i + 1, mask_body, 0)

    qb = qb_ref[0]
    qh = [_lane_band(qb[:, (h // 2) * LANES:(h // 2 + 1) * LANES],
                     (h % 2) * B_HEAD_DIM, B_HEAD_DIM) for h in range(B_HEADS)]

    def logits(j, h, kind):
        s = _dot_nt(rows(kb_ref, j), qh[h]) + mask_ref[j]
        return s if kind == 'far' else s + bias_ref[_bias_kind(j, i), h]

    _stream_key_blocks(i, B_HEADS, s_ref, acc_ref, logits, lambda j, h: vaug_ref[j])

    for h in range(B_HEADS):
        a = acc_ref[h]
        ot_ref[h * B_HEAD_DIM:(h + 1) * B_HEAD_DIM, :] = (
            a[0:B_HEAD_DIM] / a[B_HEAD_DIM:B_HEAD_DIM + 1])
    o_ref[0] = ot_ref[...].T.astype(BF16)


def _dsa_attn(proj, bvt, iwt, bias, *, top_k):
    bsz, s, _ = proj.shape
    nq = s // TQ
    nkb = s // TK
    width = B_HEADS * B_HEAD_DIM
    iq_w = IDX_HEADS * IDX_DIM
    return pl.pallas_call(
        functools.partial(_dsa_kernel, top_k=top_k, seq=s),
        grid=(bsz, nq),
        in_specs=[
            pl.BlockSpec((1, TQ, width), lambda b, i: (b, i, COL_BQ // width)),
            pl.BlockSpec((1, TQ, iq_w), lambda b, i: (b, i, COL_IQ // iq_w)),
            pl.BlockSpec((1, s, LANES), lambda b, i: (b, 0, COL_KB // LANES)),
            pl.BlockSpec((1, s, LANES), lambda b, i: (b, 0, COL_IK // LANES)),
            pl.BlockSpec((1, nkb, B_HEAD_DIM, TK), lambda b, i: (b, 0, 0, 0)),
            pl.BlockSpec((1, IDX_HEADS, TQ), lambda b, i: (b, 0, i)),
            pl.BlockSpec((3, B_HEADS, TK, TQ), lambda b, i: (0, 0, 0, 0)),
        ],
        out_specs=pl.BlockSpec((1, TQ, width), lambda b, i: (b, i, 0)),
        out_shape=jax.ShapeDtypeStruct((bsz, s, width), BF16),
        scratch_shapes=[
            pltpu.VMEM((TK, TQ), F32),
            pltpu.VMEM((nkb, TK, TQ), I32),
            pltpu.VMEM((KEY_BITS + _index_bits(s), nkb, SUBLANES, TQ), I32),
            pltpu.VMEM((nkb, TK, TQ), F32),
            pltpu.VMEM((nkb, B_HEAD_DIM + ONES_ROWS, TK), BF16),
            pltpu.VMEM((B_HEADS, TK, TQ), F32),
            pltpu.VMEM((B_HEADS, B_HEAD_DIM + ONES_ROWS, TQ), F32),
            pltpu.VMEM((width, TQ), F32),
        ],
        compiler_params=pltpu.CompilerParams(
            dimension_semantics=("arbitrary", "arbitrary"), vmem_limit_bytes=VMEM_LIMIT),
        name="dsa_attn",
    )(proj, proj, proj, proj, bvt, iwt, bias)


def _tail_kernel(ya_ref, yb_ref, wo_ref, x_ref, g_mix_ref, g_in_ref, wu_ref, wd_ref, g_out_ref,
                 o_ref, x1_ref, h_ref, acc_ref):
    f = pl.program_id(1)

    @pl.when(f == 0)
    def _():
        half = ya_ref.shape[-1]
        m = _dot(ya_ref[...], wo_ref[0:half, :]) + _dot(yb_ref[...], wo_ref[half:2 * half, :])
        x1 = x_ref[...] + _rms(m, g_mix_ref[...])
        x1_ref[...] = x1
        h_ref[...] = _rms(x1, g_in_ref[...]).astype(BF16)
        acc_ref[...] = jnp.zeros_like(acc_ref)

    u = jnp.maximum(_dot(h_ref[...], wu_ref[...]), 0.0)
    acc_ref[...] += _dot((u * u).astype(BF16), wd_ref[...])

    @pl.when(f == pl.num_programs(1) - 1)
    def _():
        o_ref[...] = x1_ref[...] + _rms(acc_ref[...], g_out_ref[...])


def _layer_tail(ya, yb, wo, x, g_mix, g_in, wu, wd, layer, g_out, *, tm=1024, tf=1024):
    n, d = x.shape
    half = ya.shape[-1]
    ff = wu.shape[-1]
    return pl.pallas_call(
        _tail_kernel,
        grid=(n // tm, ff // tf),
        in_specs=[
            pl.BlockSpec((tm, half), lambda i, f: (i, 0)),
            pl.BlockSpec((tm, half), lambda i, f: (i, 0)),
            pl.BlockSpec((2 * half, d), lambda i, f: (0, 0)),
            pl.BlockSpec((tm, d), lambda i, f: (i, 0)),
            pl.BlockSpec((1, d), lambda i, f: (0, 0)),
            pl.BlockSpec((1, d), lambda i, f: (0, 0)),
            pl.BlockSpec((None, d, tf), lambda i, f: (layer, 0, f)),
            pl.BlockSpec((None, tf, d), lambda i, f: (layer, f, 0)),
            pl.BlockSpec((1, d), lambda i, f: (0, 0)),
        ],
        out_specs=pl.BlockSpec((tm, d), lambda i, f: (i, 0)),
        out_shape=jax.ShapeDtypeStruct((n, d), F32),
        scratch_shapes=[pltpu.VMEM((tm, d), F32), pltpu.VMEM((tm, d), BF16),
                        pltpu.VMEM((tm, d), F32)],
        compiler_params=pltpu.CompilerParams(
            dimension_semantics=("arbitrary", "arbitrary"), vmem_limit_bytes=VMEM_LIMIT),
        name="layer_tail",
    )(ya, yb, wo, x, g_mix, g_in, wu, wd, g_out)


CONV_ROWS = 64
U_HIST = 32
Z_HIST = 8


def _conv_kernel(x_ref, g_in_ref, w_in_ref, dw_w_ref, dw_b_ref, ln_g_ref, ln_b_ref, sc_w_ref,
                 u_ref, z_ref, ubuf_ref, zbuf_ref, shift_ref, conv_ref, *, ts):
    @pl.when(pl.program_id(1) == 0)
    def _():
        ubuf_ref[0:U_HIST, :] = jnp.zeros((U_HIST, CONV_CH), F32)
        zbuf_ref[0:Z_HIST, :] = jnp.zeros((Z_HIST, SC_CH), F32)

    x = x_ref[0]
    hb = _rms(x, g_in_ref[...]).astype(BF16)
    c = CONV_CH
    ca = _dot(hb, w_in_ref[:, 0:c])
    cg = _dot(hb, w_in_ref[:, c:2 * c])
    ubuf_ref[U_HIST:U_HIST + ts, :] = ca * jax.nn.sigmoid(cg)
    dc = _dot(hb, w_in_ref[:, 3 * c:4 * c])
    dh = _dot(hb, w_in_ref[:, 4 * c:5 * c])
    zbuf_ref[Z_HIST:Z_HIST + ts, :] = dc * dh

    first_off = U_HIST - (CONV_WIDTH - 1)
    shift_rows = shift_ref.shape[1]
    for r in range(1, SUBLANES):
        shift_ref[r - 1] = ubuf_ref[r:r + shift_rows, :]

    def conv_rows(rb, carry):
        base = pl.multiple_of(rb * CONV_ROWS, CONV_ROWS)
        acc = jnp.zeros((CONV_ROWS, c), F32)
        for j in range(CONV_WIDTH):
            r, a = (first_off + j) % SUBLANES, (first_off + j) // SUBLANES
            rows = pl.ds(base + SUBLANES * a, CONV_ROWS)
            taps = ubuf_ref[rows, :] if r == 0 else shift_ref[r - 1, rows, :]
            acc = acc + dw_w_ref[j:j + 1, :] * taps
        conv_ref[pl.ds(base, CONV_ROWS), :] = acc
        return carry

    lax.fori_loop(0, ts // CONV_ROWS, conv_rows, 0)
    u = conv_ref[...] + dw_b_ref[...]
    mu = jnp.mean(u, axis=-1, keepdims=True)
    ctr = u - mu
    var = jnp.mean(ctr * ctr, axis=-1, keepdims=True)
    u = ctr * lax.rsqrt(var + EPS) * ln_g_ref[...] + ln_b_ref[...]
    u = u * jax.nn.sigmoid(u)

    z = jnp.zeros((ts, c), F32)
    for j in range(SC_WIDTH):
        off = Z_HIST - (SC_WIDTH - 1) + j
        z = z + sc_w_ref[j:j + 1, :] * zbuf_ref[off:off + ts, :]
    z = _dot(hb, w_in_ref[:, 2 * c:3 * c]) * z

    ubuf_ref[0:U_HIST, :] = ubuf_ref[ts:ts + U_HIST, :]
    zbuf_ref[0:Z_HIST, :] = zbuf_ref[ts:ts + Z_HIST, :]

    u_ref[0] = u.astype(BF16)
    z_ref[0] = z.astype(BF16)


def _conv_mixer(x, g_in, w_in, dw_w, dw_b, ln_g, ln_b, sc_w, *, ts=512):
    bsz, s, d = x.shape
    full = lambda a: pl.BlockSpec(a.shape, lambda b, i: (0,) * a.ndim)
    return pl.pallas_call(
        functools.partial(_conv_kernel, ts=ts),
        grid=(bsz, s // ts),
        in_specs=[pl.BlockSpec((1, ts, d), lambda b, i: (b, i, 0)),
                  full(g_in), full(w_in), full(dw_w), full(dw_b), full(ln_g), full(ln_b),
                  full(sc_w)],
        out_specs=[pl.BlockSpec((1, ts, CONV_CH), lambda b, i: (b, i, 0)),
                   pl.BlockSpec((1, ts, SC_CH), lambda b, i: (b, i, 0))],
        out_shape=[jax.ShapeDtypeStruct((bsz, s, CONV_CH), BF16),
                   jax.ShapeDtypeStruct((bsz, s, SC_CH), BF16)],
        scratch_shapes=[pltpu.VMEM((U_HIST + ts, CONV_CH), F32),
                        pltpu.VMEM((Z_HIST + ts, SC_CH), F32),
                        pltpu.VMEM((SUBLANES - 1, ts + U_HIST - SUBLANES, CONV_CH), F32),
                        pltpu.VMEM((ts, CONV_CH), F32)],
        compiler_params=pltpu.CompilerParams(
            dimension_semantics=("arbitrary", "arbitrary"), vmem_limit_bytes=VMEM_LIMIT),
        name="conv_mixer",
    )(x, g_in, w_in, dw_w, dw_b, ln_g, ln_b, sc_w)


def _t5_bucket(rel):
    nb = NUM_BUCKETS // 2
    ret = jnp.where(rel > 0, nb, 0)
    n = jnp.abs(rel)
    max_exact = nb // 2
    nf = jnp.maximum(n, 1).astype(jnp.float32)
    large = max_exact + (jnp.log(nf / max_exact) / math.log(MAX_DISTANCE / max_exact)
                         * (nb - max_exact)).astype(jnp.int32)
    large = jnp.minimum(large, nb - 1)
    return ret + jnp.where(n < max_exact, n, large)


REL_LO = -(TQ + TK - 1)
REL_LEN = TK - REL_LO
REL_LANES = -(-REL_LEN // LANES) * LANES


def _bias_kernel(fr_ref, a_ref, b_ref):
    krow = lax.broadcasted_iota(I32, (TK, TQ), 0)
    qcol = lax.broadcasted_iota(I32, (TK, TQ), 1)
    admissible = (krow // CHUNK) <= (qcol // CHUNK)
    zeros = jnp.zeros((TK, TQ), F32)
    for h in range(A_HEADS + B_HEADS):
        table = jnp.broadcast_to(fr_ref[h], (TK, REL_LANES))
        blocks = []
        for shift in (TK, 0):
            s0 = (-shift - REL_LO - (REL_LEN - 1)) % REL_LANES
            blocks.append(pltpu.roll(table, s0, 1, stride=1, stride_axis=0)[:, 0:TQ])
        prev, diag = blocks
        if h < A_HEADS:
            a_ref[0, h], a_ref[1, h], a_ref[2, h] = zeros, prev, jnp.where(admissible, diag, NEG)
        else:
            g = h - A_HEADS
            b_ref[0, g], b_ref[1, g], b_ref[2, g] = zeros, prev, diag


def _bias_tables(rel_bias):
    rel = jnp.arange(REL_LO, TK, dtype=jnp.int32)
    far_bucket = NUM_BUCKETS // 2 - 1
    f = (rel_bias[_t5_bucket(rel)].astype(F32) - rel_bias[far_bucket].astype(F32)[None, :]).T
    fr = jnp.pad(f[:, ::-1] * LOG2E, ((0, 0), (0, REL_LANES - REL_LEN)))[:, None, :]
    heads = A_HEADS + B_HEADS
    return pl.pallas_call(
        _bias_kernel,
        grid=(1,),
        in_specs=[pl.BlockSpec((heads, 1, REL_LANES), lambda i: (0, 0, 0))],
        out_specs=[pl.BlockSpec((3, A_HEADS, TK, TQ), lambda i: (0, 0, 0, 0)),
                   pl.BlockSpec((3, B_HEADS, TK, TQ), lambda i: (0, 0, 0, 0))],
        out_shape=[jax.ShapeDtypeStruct((3, A_HEADS, TK, TQ), F32),
                   jax.ShapeDtypeStruct((3, B_HEADS, TK, TQ), F32)],
        compiler_params=pltpu.CompilerParams(
            dimension_semantics=("arbitrary",), vmem_limit_bytes=VMEM_LIMIT),
        name="bias_tables",
    )(fr)


def _attn_weights(w_in):
    o = np.cumsum([0, 512, 512, 512, 512, 64, 64, 256, 32, 8])
    aq, ak, av, bq, bk, bv, iq, ik, iw = [w_in[:, o[n]:o[n + 1]] for n in range(9)]
    aq = aq * (A_HEAD_DIM ** -0.5 * LOG2E)
    bq = bq * (B_HEAD_DIM ** -0.5 * LOG2E)
    w = jnp.concatenate([aq, ak, bq, iq, bk, bk, ik, ik, ik, ik], axis=1).astype(BF16)
    pad = jnp.zeros((w_in.shape[0], VT_ROWS - ROW_IW - IDX_HEADS), w_in.dtype)
    wt = jnp.concatenate([av, bv, iw, pad], axis=1).T.astype(BF16)
    return w, wt


def kernel(x, rel_bias, norm_g, w_mlp_up, w_mlp_down, attn_w_in, attn_w_out, diff_lambda,
           diff_subln_g, conv_w_in, conv_w_out, conv_dw_w, conv_dw_b, conv_ln_g, conv_ln_b,
           sconv_w):
    bsz, s, d = x.shape
    depth = norm_g.shape[0]
    top_k = min(TOPK_MAX, s // 4)
    row = lambda v: v.reshape(1, -1)
    a_bias, b_bias = _bias_tables(rel_bias)
    w_up, w_down = w_mlp_up.astype(BF16), w_mlp_down.astype(BF16)
    for i in range(depth):
        j = i // 2
        if i % 2 == 0:
            lambda_init = 0.8 - 0.6 * math.exp(-0.3 * i)
            w, wt = _attn_weights(attn_w_in[j])
            proj, avt, bvt, iwt = _attn_in(x.reshape(bsz, s, d), row(norm_g[i, 0]), w, wt)
            ya = _diff_attn(proj, avt, diff_lambda[j], a_bias, diff_subln_g[j].reshape(-1, 1),
                            lambda_init=lambda_init)
            yb = _dsa_attn(proj, bvt, iwt, b_bias, top_k=top_k)
            w_out = attn_w_out[j]
        else:
            ya, yb = _conv_mixer(x.reshape(bsz, s, d), row(norm_g[i, 0]),
                                 conv_w_in[j].astype(BF16), conv_dw_w[j], row(conv_dw_b[j]),
                                 row(conv_ln_g[j]), row(conv_ln_b[j]), sconv_w[j])
            w_out = conv_w_out[j]
        x = _layer_tail(ya.reshape(bsz * s, -1), yb.reshape(bsz * s, -1), w_out.astype(BF16),
                        x.reshape(bsz * s, d), row(norm_g[i, 1]), row(norm_g[i, 2]),
                        w_up, w_down, i, row(norm_g[i, 3]))
    return x.reshape(bsz, s, d)
```

```python
import functools
import math

import numpy as np
import jax
import jax.numpy as jnp
from jax import lax
from jax.experimental import pallas as pl
from jax.experimental.pallas import tpu as pltpu

F32 = jnp.float32
BF16 = jnp.bfloat16
I32 = jnp.int32

D_MODEL = 1024
CHUNK = 64
NUM_BUCKETS = 32
MAX_DISTANCE = 128
EPS = 1e-6
NEG = -1e30
LOG2E = math.log2(math.e)
A_HEADS = 4
A_HEAD_DIM = 64
A_V_DIM = 2 * A_HEAD_DIM
B_HEADS = 8
B_HEAD_DIM = 64
IDX_HEADS = 8
IDX_DIM = 32
TOPK_MAX = 256
CONV_CH = 512
CONV_WIDTH = 31
SC_CH = 512
SC_WIDTH = 3
D_FF = 4 * D_MODEL

LANES = 128
SUBLANES = 8
VMEM_LIMIT = 56 * 1024 * 1024

TQ = 256
TK = 256
COL_AQ, COL_AK, COL_BQ, COL_IQ, COL_KB, COL_IK, PROJ_W = 0, 512, 1024, 1536, 1792, 1920, 2048
ROW_AV, ROW_BV, ROW_IW, VT_ROWS = 0, 512, 576, 640
ONES_ROWS = 16
INT_MIN = -2 ** 31
KEY_NEG_INF = INT_MIN + 0x7FFFFF


def _rms(x, g):
    return x * lax.rsqrt(jnp.mean(x * x, axis=-1, keepdims=True) + EPS) * g


def _dot_nt(a, b):
    return lax.dot_general(a, b, (((1,), (1,)), ((), ())), preferred_element_type=F32)


def _dot(a, b):
    return jnp.dot(a, b, preferred_element_type=F32)


def _lane_band(x, lo, width):
    lane = lax.broadcasted_iota(I32, x.shape, 1)
    return jnp.where((lane >= lo) & (lane < lo + width), x, jnp.zeros_like(x))


KEY_BITS = 32


def _bit_planes(keys):
    assert keys.shape[0] == KEY_BITS * SUBLANES
    u = keys ^ INT_MIN
    a = [u[SUBLANES * r:SUBLANES * (r + 1), :] for r in range(KEY_BITS)]
    j, m = KEY_BITS // 2, (1 << (KEY_BITS // 2)) - 1
    while j:
        mask = np.int32(np.uint32(m))
        k = 0
        while k < KEY_BITS:
            t = (a[k] ^ lax.shift_right_logical(a[k + j], jnp.int32(j))) & mask
            a[k] = a[k] ^ t
            a[k + j] = a[k + j] ^ lax.shift_left(t, jnp.int32(j))
            k = (k + j + 1) & ~j
        j >>= 1
        m = (m ^ (m << j)) & 0xFFFFFFFF
    return a


def _index_bits(seq):
    bits = seq.bit_length() - 1
    assert seq == 1 << bits and seq >= KEY_BITS * SUBLANES
    return bits


def _index_planes(nkb):
    shape = (nkb, SUBLANES, TQ)
    blk = lax.broadcasted_iota(I32, shape, 0)
    sub = lax.broadcasted_iota(I32, shape, 1)
    sub_bits = SUBLANES.bit_length() - 1
    word_bits = KEY_BITS.bit_length() - 1
    planes = []
    for b in range(_index_bits(nkb * TK) - 1, -1, -1):
        if b >= sub_bits + word_bits:
            on = ((nkb - 1 - blk) >> (b - sub_bits - word_bits)) & 1
            planes.append(jnp.where(on == 1, -1, 0))
        elif b >= sub_bits:
            t = b - sub_bits
            pattern = sum(1 << k for k in range(KEY_BITS) if (k >> t) & 1)
            planes.append(jnp.full(shape, np.int32(np.uint32(pattern)), I32))
        else:
            on = ((SUBLANES - 1 - sub) >> b) & 1
            planes.append(jnp.where(on == 1, -1, 0))
    return planes


def _bias_kind(j, i):
    return jnp.clip(j - i + 2, 0, 2)


def _stream_key_blocks(i, n_chains, s_ref, acc_ref, logits, v_aug):
    chains = range(n_chains)

    def stage(j, c, kind):
        s = logits(j, c, kind)
        s_ref[c] = s
        return jnp.max(s, axis=0, keepdims=True)

    def absorb(j, c, m_old, block_max):
        m_new = jnp.maximum(m_old, block_max)
        alpha = jnp.exp2(m_old - m_new)
        p = jnp.exp2(s_ref[c] - m_new).astype(BF16)
        acc_ref[c] = acc_ref[c] * alpha + _dot(v_aug(j, c), p)
        return m_new

    def step(j, carry, kind):
        ms, bms = carry
        out = [(absorb(j, c, ms[c], bms[c]), stage(j + 1, c, kind)) for c in chains]
        return tuple(o[0] for o in out), tuple(o[1] for o in out)

    acc_ref[...] = jnp.zeros_like(acc_ref)
    carry = ((jnp.full((1, TQ), NEG, F32),) * n_chains, tuple(stage(0, c, 'any') for c in chains))
    n_far = jnp.maximum(i - 2, 0)
    carry = lax.fori_loop(0, n_far, lambda j, cr: step(j, cr, 'far'), carry)
    ms, bms = lax.fori_loop(n_far, i, lambda j, cr: step(j, cr, 'near'), carry)
    for c in chains:
        absorb(i, c, ms[c], bms[c])


def _attn_in_kernel(x_ref, g_ref, w_ref, wt_ref, proj_ref, avt_ref, bvt_ref, iwt_ref, *, tm):
    hb = _rms(x_ref[0], g_ref[...]).astype(BF16)
    proj_ref[0] = _dot(hb, w_ref[...]).astype(BF16)
    t = _dot_nt(wt_ref[...], hb)
    for c in range(tm // TK):
        avt_ref[0, c] = t[ROW_AV:ROW_BV, c * TK:(c + 1) * TK].astype(BF16)
        bvt_ref[0, c] = t[ROW_BV:ROW_IW, c * TK:(c + 1) * TK].astype(BF16)
    iwt_ref[0] = t[ROW_IW:ROW_IW + IDX_HEADS, :] * (IDX_HEADS ** -0.5 * IDX_DIM ** -0.5)


def _attn_in(x, g, w, wt, *, tm=512):
    bsz, s, d = x.shape
    av_rows = ROW_BV - ROW_AV
    return pl.pallas_call(
        functools.partial(_attn_in_kernel, tm=tm),
        grid=(bsz, s // tm),
        in_specs=[
            pl.BlockSpec((1, tm, d), lambda b, i: (b, i, 0)),
            pl.BlockSpec((1, d), lambda b, i: (0, 0)),
            pl.BlockSpec((d, PROJ_W), lambda b, i: (0, 0)),
            pl.BlockSpec((VT_ROWS, d), lambda b, i: (0, 0)),
        ],
        out_specs=[
            pl.BlockSpec((1, tm, PROJ_W), lambda b, i: (b, i, 0)),
            pl.BlockSpec((1, tm // TK, av_rows, TK), lambda b, i: (b, i, 0, 0)),
            pl.BlockSpec((1, tm // TK, B_HEAD_DIM, TK), lambda b, i: (b, i, 0, 0)),
            pl.BlockSpec((1, IDX_HEADS, tm), lambda b, i: (b, 0, i)),
        ],
        out_shape=[
            jax.ShapeDtypeStruct((bsz, s, PROJ_W), BF16),
            jax.ShapeDtypeStruct((bsz, s // TK, av_rows, TK), BF16),
            jax.ShapeDtypeStruct((bsz, s // TK, B_HEAD_DIM, TK), BF16),
            jax.ShapeDtypeStruct((bsz, IDX_HEADS, s), F32),
        ],
        compiler_params=pltpu.CompilerParams(
            dimension_semantics=("arbitrary", "arbitrary"), vmem_limit_bytes=VMEM_LIMIT),
        name="attn_in",
    )(x, g, w, wt)


def _diff_attn_kernel(lam_ref, q_ref, k_ref, vt_ref, bias_ref, g_ref, o_ref,
                      vaug_ref, s_ref, acc_ref, ot_ref, *, lambda_init, seq):
    i = pl.program_id(1)
    nkb = seq // TK
    n_maps = 2 * A_HEADS

    @pl.when(i == 0)
    def _():
        for c in range(nkb):
            for h in range(A_HEADS):
                vaug_ref[h, c, 0:A_V_DIM, :] = vt_ref[0, c, h * A_V_DIM:(h + 1) * A_V_DIM, :]
                vaug_ref[h, c, A_V_DIM:A_V_DIM + ONES_ROWS, :] = jnp.ones((ONES_ROWS, TK), BF16)

    lv = lam_ref[...]
    lam = (jnp.exp(jnp.sum(lv[0:1] * lv[1:2], axis=-1, keepdims=True))
           - jnp.exp(jnp.sum(lv[2:3] * lv[3:4], axis=-1, keepdims=True)) + lambda_init)

    q = q_ref[0]
    qm = []
    for h in range(A_HEADS):
        qh = q[:, h * LANES:(h + 1) * LANES]
        qm += [_lane_band(qh, 0, A_HEAD_DIM), _lane_band(qh, A_HEAD_DIM, A_HEAD_DIM)]

    def logits(j, c, kind):
        h = c // 2
        kb = k_ref[0, pl.ds(pl.multiple_of(j * TK, TK), TK), h * LANES:(h + 1) * LANES]
        s = _dot_nt(kb, qm[c])
        return s if kind == 'far' else s + bias_ref[_bias_kind(j, i), h]

    _stream_key_blocks(i, n_maps, s_ref, acc_ref, logits, lambda j, c: vaug_ref[c // 2, j])

    for h in range(A_HEADS):
        a1 = acc_ref[2 * h]
        a2 = acc_ref[2 * h + 1]
        a = (a1[0:A_V_DIM] / a1[A_V_DIM:A_V_DIM + 1]
             - lam * (a2[0:A_V_DIM] / a2[A_V_DIM:A_V_DIM + 1]))
        y = a * lax.rsqrt(jnp.mean(a * a, axis=0, keepdims=True) + EPS) * g_ref[...]
        ot_ref[h * A_V_DIM:(h + 1) * A_V_DIM, :] = y * (1.0 - lambda_init)
    o_ref[0] = ot_ref[...].T.astype(BF16)


def _diff_attn(proj, avt, lam_vecs, bias, subln_g_col, *, lambda_init):
    bsz, s, _ = proj.shape
    nq = s // TQ
    nkb = s // TK
    width = A_HEADS * A_V_DIM
    return pl.pallas_call(
        functools.partial(_diff_attn_kernel, lambda_init=lambda_init, seq=s),
        grid=(bsz, nq),
        in_specs=[
            pl.BlockSpec((4, A_HEAD_DIM), lambda b, i: (0, 0)),
            pl.BlockSpec((1, TQ, width), lambda b, i: (b, i, COL_AQ // width)),
            pl.BlockSpec((1, s, width), lambda b, i: (b, 0, COL_AK // width)),
            pl.BlockSpec((1, nkb, width, TK), lambda b, i: (b, 0, 0, 0)),
            pl.BlockSpec((3, A_HEADS, TK, TQ), lambda b, i: (0, 0, 0, 0)),
            pl.BlockSpec((A_V_DIM, 1), lambda b, i: (0, 0)),
        ],
        out_specs=pl.BlockSpec((1, TQ, width), lambda b, i: (b, i, 0)),
        out_shape=jax.ShapeDtypeStruct((bsz, s, width), BF16),
        scratch_shapes=[
            pltpu.VMEM((A_HEADS, nkb, A_V_DIM + ONES_ROWS, TK), BF16),
            pltpu.VMEM((2 * A_HEADS, TK, TQ), F32),
            pltpu.VMEM((2 * A_HEADS, A_V_DIM + ONES_ROWS, TQ), F32),
            pltpu.VMEM((width, TQ), F32),
        ],
        compiler_params=pltpu.CompilerParams(
            dimension_semantics=("arbitrary", "arbitrary"), vmem_limit_bytes=VMEM_LIMIT),
        name="diff_attn",
    )(lam_vecs, proj, proj, avt, bias, subln_g_col)


def _dsa_kernel(qb_ref, iq_ref, kb_ref, ik_ref, vt_ref, iwt_ref, bias_ref, o_ref,
                score_ref, key_ref, plane_ref, mask_ref, vaug_ref, s_ref, acc_ref, ot_ref,
                *, top_k, seq):
    i = pl.program_id(1)
    nkb = seq // TK

    @pl.when(i == 0)
    def _():
        for c in range(nkb):
            vaug_ref[c, 0:B_HEAD_DIM, :] = vt_ref[0, c]
            vaug_ref[c, B_HEAD_DIM:B_HEAD_DIM + ONES_ROWS, :] = jnp.ones((ONES_ROWS, TK), BF16)

    @pl.when((i == 0) & (pl.program_id(0) == 0))
    def _():
        plane_ref[0:KEY_BITS] = jnp.zeros((KEY_BITS, nkb, SUBLANES, TQ), I32)
        for t, plane in enumerate(_index_planes(nkb)):
            plane_ref[KEY_BITS + t] = plane

    def rows(ref, j):
        return ref[0, pl.ds(pl.multiple_of(j * TK, TK), TK), :]

    krow = lax.broadcasted_iota(I32, (TK, TQ), 0)
    qcol = lax.broadcasted_iota(I32, (TK, TQ), 1)

    iq = iq_ref[0]
    per_blk = LANES // IDX_DIM
    iqh = [_lane_band(iq[:, (h // per_blk) * LANES:(h // per_blk + 1) * LANES],
                      (h % per_blk) * IDX_DIM, IDX_DIM) for h in range(IDX_HEADS)]
    w = iwt_ref[0]

    def stage_scores(j):
        ikb = rows(ik_ref, j)
        half = TK // 2
        for r in (0, half):
            acc = jnp.zeros((half, TQ), F32)
            for h in range(IDX_HEADS):
                acc = acc + jnp.maximum(_dot_nt(ikb[r:r + half], iqh[h]), 0.0) * w[h:h + 1, :]
            score_ref[r:r + half, :] = acc

    def staged_keys():
        acc = score_ref[...]
        acc = jnp.where(acc == 0.0, 0.0, acc)
        bits = lax.bitcast_convert_type(acc, I32)
        return bits ^ ((bits >> 31) & 0x7FFFFFFF)

    def store_keys(j, keys):
        key_ref[j] = keys
        for p, plane in enumerate(_bit_planes(keys)):
            plane_ref[p, j] = plane

    def score_body(j, c):
        keys = staged_keys()
        stage_scores(j + 1)
        store_keys(j, keys)
        return c

    stage_scores(0)
    lax.fori_loop(0, i, score_body, 0)
    admissible = (krow // CHUNK) <= (qcol // CHUNK)
    store_keys(i, jnp.where(admissible, staged_keys(), KEY_NEG_INF))

    blk = lax.broadcasted_iota(I32, (nkb, SUBLANES, TQ), 0)
    alive0 = jnp.where(blk <= i, -1, 0)

    def bit_body(p, carry, n_bits):
        alive, n_gt, thr_u = carry
        hit = alive & plane_ref[p]
        pc = jnp.sum(lax.population_count(hit), axis=0)
        n_hit = jnp.sum(pc.astype(F32), axis=0, keepdims=True)
        take = (n_gt + n_hit) >= top_k
        thr_u = thr_u | jnp.where(take, lax.shift_left(jnp.int32(1), n_bits - 1 - p), 0)
        n_gt = n_gt + jnp.where(take, 0.0, n_hit)
        alive = jnp.where(take, hit, alive ^ hit)
        return alive, n_gt, thr_u

    zero = jnp.zeros((1, TQ), I32)
    alive, n_gt, thr_u = lax.fori_loop(
        0, KEY_BITS, lambda p, c: bit_body(p, c, KEY_BITS), (alive0, jnp.zeros((1, TQ), F32), zero))
    idx_bits = _index_bits(seq)
    _, _, rev_idx = lax.fori_loop(
        KEY_BITS, KEY_BITS + idx_bits,
        lambda p, c: bit_body(p, c, KEY_BITS + idx_bits), (alive, n_gt, zero))
    thr = thr_u ^ INT_MIN
    lim = seq - rev_idx
    lim = jnp.where(thr > KEY_NEG_INF, lim, seq)
    thr = jnp.maximum(thr, KEY_NEG_INF + 1)

    def mask_body(j, c):
        late = jnp.where(krow >= lim - j * TK, 1, 0)
        mask_ref[j] = jnp.where(key_ref[j] - late >= thr, 0.0, NEG)
        return c

    lax.fori_loop(0, i + 1, mask_body, 0)

    qb = qb_ref[0]
    qh = [_lane_band(qb[:, (h // 2) * LANES:(h // 2 + 1) * LANES],
                     (h % 2) * B_HEAD_DIM, B_HEAD_DIM) for h in range(B_HEADS)]

    def logits(j, h, kind):
        s = _dot_nt(rows(kb_ref, j), qh[h]) + mask_ref[j]
        return s if kind == 'far' else s + bias_ref[_bias_kind(j, i), h]

    _stream_key_blocks(i, B_HEADS, s_ref, acc_ref, logits, lambda j, h: vaug_ref[j])

    for h in range(B_HEADS):
        a = acc_ref[h]
        ot_ref[h * B_HEAD_DIM:(h + 1) * B_HEAD_DIM, :] = (
            a[0:B_HEAD_DIM] / a[B_HEAD_DIM:B_HEAD_DIM + 1])
    o_ref[0] = ot_ref[...].T.astype(BF16)


def _dsa_attn(proj, bvt, iwt, bias, *, top_k):
    bsz, s, _ = proj.shape
    nq = s // TQ
    nkb = s // TK
    width = B_HEADS * B_HEAD_DIM
    iq_w = IDX_HEADS * IDX_DIM
    return pl.pallas_call(
        functools.partial(_dsa_kernel, top_k=top_k, seq=s),
        grid=(bsz, nq),
        in_specs=[
            pl.BlockSpec((1, TQ, width), lambda b, i: (b, i, COL_BQ // width)),
            pl.BlockSpec((1, TQ, iq_w), lambda b, i: (b, i, COL_IQ // iq_w)),
            pl.BlockSpec((1, s, LANES), lambda b, i: (b, 0, COL_KB // LANES)),
            pl.BlockSpec((1, s, LANES), lambda b, i: (b, 0, COL_IK // LANES)),
            pl.BlockSpec((1, nkb, B_HEAD_DIM, TK), lambda b, i: (b, 0, 0, 0)),
            pl.BlockSpec((1, IDX_HEADS, TQ), lambda b, i: (b, 0, i)),
            pl.BlockSpec((3, B_HEADS, TK, TQ), lambda b, i: (0, 0, 0, 0)),
        ],
        out_specs=pl.BlockSpec((1, TQ, width), lambda b, i: (b, i, 0)),
        out_shape=jax.ShapeDtypeStruct((bsz, s, width), BF16),
        scratch_shapes=[
            pltpu.VMEM((TK, TQ), F32),
            pltpu.VMEM((nkb, TK, TQ), I32),
            pltpu.VMEM((KEY_BITS + _index_bits(s), nkb, SUBLANES, TQ), I32),
            pltpu.VMEM((nkb, TK, TQ), F32),
            pltpu.VMEM((nkb, B_HEAD_DIM + ONES_ROWS, TK), BF16),
            pltpu.VMEM((B_HEADS, TK, TQ), F32),
            pltpu.VMEM((B_HEADS, B_HEAD_DIM + ONES_ROWS, TQ), F32),
            pltpu.VMEM((width, TQ), F32),
        ],
        compiler_params=pltpu.CompilerParams(
            dimension_semantics=("arbitrary", "arbitrary"), vmem_limit_bytes=VMEM_LIMIT),
        name="dsa_attn",
    )(proj, proj, proj, proj, bvt, iwt, bias)


def _tail_kernel(ya_ref, yb_ref, wo_ref, x_ref, g_mix_ref, g_in_ref, wu_ref, wd_ref, g_out_ref,
                 o_ref, x1_ref, h_ref, acc_ref):
    f = pl.program_id(1)

    @pl.when(f == 0)
    def _():
        half = ya_ref.shape[-1]
        m = _dot(ya_ref[...], wo_ref[0:half, :]) + _dot(yb_ref[...], wo_ref[half:2 * half, :])
        x1 = x_ref[...] + _rms(m, g_mix_ref[...])
        x1_ref[...] = x1
        h_ref[...] = _rms(x1, g_in_ref[...]).astype(BF16)
        acc_ref[...] = jnp.zeros_like(acc_ref)

    u = jnp.maximum(_dot(h_ref[...], wu_ref[...]), 0.0)
    acc_ref[...] += _dot((u * u).astype(BF16), wd_ref[...])

    @pl.when(f == pl.num_programs(1) - 1)
    def _():
        o_ref[...] = x1_ref[...] + _rms(acc_ref[...], g_out_ref[...])


def _layer_tail(ya, yb, wo, x, g_mix, g_in, wu, wd, layer, g_out, *, tm=1024, tf=1024):
    n, d = x.shape
    half = ya.shape[-1]
    ff = wu.shape[-1]
    return pl.pallas_call(
        _tail_kernel,
        grid=(n // tm, ff // tf),
        in_specs=[
            pl.BlockSpec((tm, half), lambda i, f: (i, 0)),
            pl.BlockSpec((tm, half), lambda i, f: (i, 0)),
            pl.BlockSpec((2 * half, d), lambda i, f: (0, 0)),
            pl.BlockSpec((tm, d), lambda i, f: (i, 0)),
            pl.BlockSpec((1, d), lambda i, f: (0, 0)),
            pl.BlockSpec((1, d), lambda i, f: (0, 0)),
            pl.BlockSpec((None, d, tf), lambda i, f: (layer, 0, f)),
            pl.BlockSpec((None, tf, d), lambda i, f: (layer, f, 0)),
            pl.BlockSpec((1, d), lambda i, f: (0, 0)),
        ],
        out_specs=pl.BlockSpec((tm, d), lambda i, f: (i, 0)),
        out_shape=jax.ShapeDtypeStruct((n, d), F32),
        scratch_shapes=[pltpu.VMEM((tm, d), F32), pltpu.VMEM((tm, d), BF16),
                        pltpu.VMEM((tm, d), F32)],
        compiler_params=pltpu.CompilerParams(
            dimension_semantics=("arbitrary", "arbitrary"), vmem_limit_bytes=VMEM_LIMIT),
        name="layer_tail",
    )(ya, yb, wo, x, g_mix, g_in, wu, wd, g_out)


CONV_ROWS = 64
U_HIST = 32
Z_HIST = 8


def _conv_kernel(x_ref, g_in_ref, w_in_ref, dw_w_ref, dw_b_ref, ln_g_ref, ln_b_ref, sc_w_ref,
                 u_ref, z_ref, ubuf_ref, zbuf_ref, shift_ref, conv_ref, *, ts):
    @pl.when(pl.program_id(1) == 0)
    def _():
        ubuf_ref[0:U_HIST, :] = jnp.zeros((U_HIST, CONV_CH), F32)
        zbuf_ref[0:Z_HIST, :] = jnp.zeros((Z_HIST, SC_CH), F32)

    x = x_ref[0]
    hb = _rms(x, g_in_ref[...]).astype(BF16)
    c = CONV_CH
    ca = _dot(hb, w_in_ref[:, 0:c])
    cg = _dot(hb, w_in_ref[:, c:2 * c])
    ubuf_ref[U_HIST:U_HIST + ts, :] = ca * jax.nn.sigmoid(cg)
    dc = _dot(hb, w_in_ref[:, 3 * c:4 * c])
    dh = _dot(hb, w_in_ref[:, 4 * c:5 * c])
    zbuf_ref[Z_HIST:Z_HIST + ts, :] = dc * dh

    first_off = U_HIST - (CONV_WIDTH - 1)
    shift_rows = shift_ref.shape[1]
    for r in range(1, SUBLANES):
        shift_ref[r - 1] = ubuf_ref[r:r + shift_rows, :]

    for base in range(0, ts, CONV_ROWS):
        acc = jnp.zeros((CONV_ROWS, c), F32)
        for j in range(CONV_WIDTH):
            r, a = (first_off + j) % SUBLANES, (first_off + j) // SUBLANES
            rows = pl.ds(base + SUBLANES * a, CONV_ROWS)
            taps = ubuf_ref[rows, :] if r == 0 else shift_ref[r - 1, rows, :]
            acc = acc + dw_w_ref[j:j + 1, :] * taps
        conv_ref[base:base + CONV_ROWS, :] = acc
    u = conv_ref[...] + dw_b_ref[...]
    mu = jnp.mean(u, axis=-1, keepdims=True)
    ctr = u - mu
    var = jnp.mean(ctr * ctr, axis=-1, keepdims=True)
    u = ctr * lax.rsqrt(var + EPS) * ln_g_ref[...] + ln_b_ref[...]
    u = u * jax.nn.sigmoid(u)

    z = jnp.zeros((ts, c), F32)
    for j in range(SC_WIDTH):
        off = Z_HIST - (SC_WIDTH - 1) + j
        z = z + sc_w_ref[j:j + 1, :] * zbuf_ref[off:off + ts, :]
    z = _dot(hb, w_in_ref[:, 2 * c:3 * c]) * z

    ubuf_ref[0:U_HIST, :] = ubuf_ref[ts:ts + U_HIST, :]
    zbuf_ref[0:Z_HIST, :] = zbuf_ref[ts:ts + Z_HIST, :]

    u_ref[0] = u.astype(BF16)
    z_ref[0] = z.astype(BF16)


def _conv_mixer(x, g_in, w_in, dw_w, dw_b, ln_g, ln_b, sc_w, *, ts=512):
    bsz, s, d = x.shape
    full = lambda a: pl.BlockSpec(a.shape, lambda b, i: (0,) * a.ndim)
    return pl.pallas_call(
        functools.partial(_conv_kernel, ts=ts),
        grid=(bsz, s // ts),
        in_specs=[pl.BlockSpec((1, ts, d), lambda b, i: (b, i, 0)),
                  full(g_in), full(w_in), full(dw_w), full(dw_b), full(ln_g), full(ln_b),
                  full(sc_w)],
        out_specs=[pl.BlockSpec((1, ts, CONV_CH), lambda b, i: (b, i, 0)),
                   pl.BlockSpec((1, ts, SC_CH), lambda b, i: (b, i, 0))],
        out_shape=[jax.ShapeDtypeStruct((bsz, s, CONV_CH), BF16),
                   jax.ShapeDtypeStruct((bsz, s, SC_CH), BF16)],
        scratch_shapes=[pltpu.VMEM((U_HIST + ts, CONV_CH), F32),
                        pltpu.VMEM((Z_HIST + ts, SC_CH), F32),
                        pltpu.VMEM((SUBLANES - 1, ts + U_HIST - SUBLANES, CONV_CH), F32),
                        pltpu.VMEM((ts, CONV_CH), F32)],
        compiler_params=pltpu.CompilerParams(
            dimension_semantics=("arbitrary", "arbitrary"), vmem_limit_bytes=VMEM_LIMIT),
        name="conv_mixer",
    )(x, g_in, w_in, dw_w, dw_b, ln_g, ln_b, sc_w)


def _t5_bucket(rel):
    nb = NUM_BUCKETS // 2
    ret = jnp.where(rel > 0, nb, 0)
    n = jnp.abs(rel)
    max_exact = nb // 2
    nf = jnp.maximum(n, 1).astype(jnp.float32)
    large = max_exact + (jnp.log(nf / max_exact) / math.log(MAX_DISTANCE / max_exact)
                         * (nb - max_exact)).astype(jnp.int32)
    large = jnp.minimum(large, nb - 1)
    return ret + jnp.where(n < max_exact, n, large)


REL_LO = -(TQ + TK - 1)
REL_LEN = TK - REL_LO
REL_LANES = -(-REL_LEN // LANES) * LANES


def _bias_kernel(fr_ref, a_ref, b_ref):
    krow = lax.broadcasted_iota(I32, (TK, TQ), 0)
    qcol = lax.broadcasted_iota(I32, (TK, TQ), 1)
    admissible = (krow // CHUNK) <= (qcol // CHUNK)
    zeros = jnp.zeros((TK, TQ), F32)
    for h in range(A_HEADS + B_HEADS):
        table = jnp.broadcast_to(fr_ref[h], (TK, REL_LANES))
        blocks = []
        for shift in (TK, 0):
            s0 = (-shift - REL_LO - (REL_LEN - 1)) % REL_LANES
            blocks.append(pltpu.roll(table, s0, 1, stride=1, stride_axis=0)[:, 0:TQ])
        prev, diag = blocks
        if h < A_HEADS:
            a_ref[0, h], a_ref[1, h], a_ref[2, h] = zeros, prev, jnp.where(admissible, diag, NEG)
        else:
            g = h - A_HEADS
            b_ref[0, g], b_ref[1, g], b_ref[2, g] = zeros, prev, diag


def _bias_tables(rel_bias):
    rel = jnp.arange(REL_LO, TK, dtype=jnp.int32)
    far_bucket = NUM_BUCKETS // 2 - 1
    f = (rel_bias[_t5_bucket(rel)].astype(F32) - rel_bias[far_bucket].astype(F32)[None, :]).T
    fr = jnp.pad(f[:, ::-1] * LOG2E, ((0, 0), (0, REL_LANES - REL_LEN)))[:, None, :]
    heads = A_HEADS + B_HEADS
    return pl.pallas_call(
        _bias_kernel,
        grid=(1,),
        in_specs=[pl.BlockSpec((heads, 1, REL_LANES), lambda i: (0, 0, 0))],
        out_specs=[pl.BlockSpec((3, A_HEADS, TK, TQ), lambda i: (0, 0, 0, 0)),
                   pl.BlockSpec((3, B_HEADS, TK, TQ), lambda i: (0, 0, 0, 0))],
        out_shape=[jax.ShapeDtypeStruct((3, A_HEADS, TK, TQ), F32),
                   jax.ShapeDtypeStruct((3, B_HEADS, TK, TQ), F32)],
        compiler_params=pltpu.CompilerParams(
            dimension_semantics=("arbitrary",), vmem_limit_bytes=VMEM_LIMIT),
        name="bias_tables",
    )(fr)


def _attn_weights(w_in):
    o = np.cumsum([0, 512, 512, 512, 512, 64, 64, 256, 32, 8])
    aq, ak, av, bq, bk, bv, iq, ik, iw = [w_in[:, o[n]:o[n + 1]] for n in range(9)]
    aq = aq * (A_HEAD_DIM ** -0.5 * LOG2E)
    bq = bq * (B_HEAD_DIM ** -0.5 * LOG2E)
    w = jnp.concatenate([aq, ak, bq, iq, bk, bk, ik, ik, ik, ik], axis=1).astype(BF16)
    pad = jnp.zeros((w_in.shape[0], VT_ROWS - ROW_IW - IDX_HEADS), w_in.dtype)
    wt = jnp.concatenate([av, bv, iw, pad], axis=1).T.astype(BF16)
    return w, wt


def kernel(x, rel_bias, norm_g, w_mlp_up, w_mlp_down, attn_w_in, attn_w_out, diff_lambda,
           diff_subln_g, conv_w_in, conv_w_out, conv_dw_w, conv_dw_b, conv_ln_g, conv_ln_b,
           sconv_w):
    bsz, s, d = x.shape
    depth = norm_g.shape[0]
    top_k = min(TOPK_MAX, s // 4)
    row = lambda v: v.reshape(1, -1)
    a_bias, b_bias = _bias_tables(rel_bias)
    w_up, w_down = w_mlp_up.astype(BF16), w_mlp_down.astype(BF16)
    for i in range(depth):
        j = i // 2
        if i % 2 == 0:
            lambda_init = 0.8 - 0.6 * math.exp(-0.3 * i)
            w, wt = _attn_weights(attn_w_in[j])
            proj, avt, bvt, iwt = _attn_in(x.reshape(bsz, s, d), row(norm_g[i, 0]), w, wt)
            ya = _diff_attn(proj, avt, diff_lambda[j], a_bias, diff_subln_g[j].reshape(-1, 1),
                            lambda_init=lambda_init)
            yb = _dsa_attn(proj, bvt, iwt, b_bias, top_k=top_k)
            w_out = attn_w_out[j]
        else:
            ya, yb = _conv_mixer(x.reshape(bsz, s, d), row(norm_g[i, 0]),
                                 conv_w_in[j].astype(BF16), conv_dw_w[j], row(conv_dw_b[j]),
                                 row(conv_ln_g[j]), row(conv_ln_b[j]), sconv_w[j])
            w_out = conv_w_out[j]
        x = _layer_tail(ya.reshape(bsz * s, -1), yb.reshape(bsz * s, -1), w_out.astype(BF16),
                        x.reshape(bsz * s, d), row(norm_g[i, 1]), row(norm_g[i, 2]),
                        w_up, w_down, i, row(norm_g[i, 3]))
    return x.reshape(bsz, s, d)
```

```python
import functools
import math

import numpy as np
import jax
import jax.numpy as jnp
from jax import lax
from jax.experimental import pallas as pl
from jax.experimental.pallas import tpu as pltpu

F32 = jnp.float32
BF16 = jnp.bfloat16
I32 = jnp.int32

D_MODEL = 1024
CHUNK = 64
NUM_BUCKETS = 32
MAX_DISTANCE = 128
EPS = 1e-6
NEG = -1e30
LOG2E = math.log2(math.e)
A_HEADS = 4
A_HEAD_DIM = 64
A_V_DIM = 2 * A_HEAD_DIM
B_HEADS = 8
B_HEAD_DIM = 64
IDX_HEADS = 8
IDX_DIM = 32
TOPK_MAX = 256
CONV_CH = 512
CONV_WIDTH = 31
SC_CH = 512
SC_WIDTH = 3
D_FF = 4 * D_MODEL

LANES = 128
SUBLANES = 8
VMEM_LIMIT = 56 * 1024 * 1024

TQ = 256
TK = 256
COL_AQ, COL_AK, COL_BQ, COL_IQ, COL_KB, COL_IK, PROJ_W = 0, 512, 1024, 1536, 1792, 1920, 2048
ROW_AV, ROW_BV, ROW_IW, VT_ROWS = 0, 512, 576, 640
ONES_ROWS = 16
INT_MIN = -2 ** 31
KEY_NEG_INF = INT_MIN + 0x7FFFFF


def _rms(x, g):
    return x * lax.rsqrt(jnp.mean(x * x, axis=-1, keepdims=True) + EPS) * g


def _dot_nt(a, b):
    return lax.dot_general(a, b, (((1,), (1,)), ((), ())), preferred_element_type=F32)


def _dot(a, b):
    return jnp.dot(a, b, preferred_element_type=F32)


def _lane_band(x, lo, width):
    lane = lax.broadcasted_iota(I32, x.shape, 1)
    return jnp.where((lane >= lo) & (lane < lo + width), x, jnp.zeros_like(x))


KEY_BITS = 32


def _bit_planes(keys):
    assert keys.shape[0] == KEY_BITS * SUBLANES
    u = keys ^ INT_MIN
    a = [u[SUBLANES * r:SUBLANES * (r + 1), :] for r in range(KEY_BITS)]
    j, m = KEY_BITS // 2, (1 << (KEY_BITS // 2)) - 1
    while j:
        mask = np.int32(np.uint32(m))
        k = 0
        while k < KEY_BITS:
            t = (a[k] ^ lax.shift_right_logical(a[k + j], jnp.int32(j))) & mask
            a[k] = a[k] ^ t
            a[k + j] = a[k + j] ^ lax.shift_left(t, jnp.int32(j))
            k = (k + j + 1) & ~j
        j >>= 1
        m = (m ^ (m << j)) & 0xFFFFFFFF
    return a


def _index_bits(seq):
    bits = seq.bit_length() - 1
    assert seq == 1 << bits and seq >= KEY_BITS * SUBLANES
    return bits


def _index_planes(nkb):
    shape = (nkb, SUBLANES, TQ)
    blk = lax.broadcasted_iota(I32, shape, 0)
    sub = lax.broadcasted_iota(I32, shape, 1)
    sub_bits = SUBLANES.bit_length() - 1
    word_bits = KEY_BITS.bit_length() - 1
    planes = []
    for b in range(_index_bits(nkb * TK) - 1, -1, -1):
        if b >= sub_bits + word_bits:
            on = ((nkb - 1 - blk) >> (b - sub_bits - word_bits)) & 1
            planes.append(jnp.where(on == 1, -1, 0))
        elif b >= sub_bits:
            t = b - sub_bits
            pattern = sum(1 << k for k in range(KEY_BITS) if (k >> t) & 1)
            planes.append(jnp.full(shape, np.int32(np.uint32(pattern)), I32))
        else:
            on = ((SUBLANES - 1 - sub) >> b) & 1
            planes.append(jnp.where(on == 1, -1, 0))
    return planes


def _bias_kind(j, i):
    return jnp.clip(j - i + 2, 0, 2)


def _stream_key_blocks(i, n_chains, s_ref, acc_ref, logits, v_aug):
    chains = range(n_chains)

    def stage(j, c, kind):
        s = logits(j, c, kind)
        s_ref[c] = s
        return jnp.max(s, axis=0, keepdims=True)

    def absorb(j, c, m_old, block_max):
        m_new = jnp.maximum(m_old, block_max)
        alpha = jnp.exp2(m_old - m_new)
        p = jnp.exp2(s_ref[c] - m_new).astype(BF16)
        acc_ref[c] = acc_ref[c] * alpha + _dot(v_aug(j, c), p)
        return m_new

    def step(j, carry, kind):
        ms, bms = carry
        out = [(absorb(j, c, ms[c], bms[c]), stage(j + 1, c, kind)) for c in chains]
        return tuple(o[0] for o in out), tuple(o[1] for o in out)

    acc_ref[...] = jnp.zeros_like(acc_ref)
    carry = ((jnp.full((1, TQ), NEG, F32),) * n_chains, tuple(stage(0, c, 'any') for c in chains))
    n_far = jnp.maximum(i - 2, 0)
    carry = lax.fori_loop(0, n_far, lambda j, cr: step(j, cr, 'far'), carry)
    ms, bms = lax.fori_loop(n_far, i, lambda j, cr: step(j, cr, 'near'), carry)
    for c in chains:
        absorb(i, c, ms[c], bms[c])


def _attn_in_kernel(x_ref, g_ref, w_ref, wt_ref, proj_ref, avt_ref, bvt_ref, iwt_ref, *, tm):
    hb = _rms(x_ref[0], g_ref[...]).astype(BF16)
    proj_ref[0] = _dot(hb, w_ref[...]).astype(BF16)
    t = _dot_nt(wt_ref[...], hb)
    for c in range(tm // TK):
        avt_ref[0, c] = t[ROW_AV:ROW_BV, c * TK:(c + 1) * TK].astype(BF16)
        bvt_ref[0, c] = t[ROW_BV:ROW_IW, c * TK:(c + 1) * TK].astype(BF16)
    iwt_ref[0] = t[ROW_IW:ROW_IW + IDX_HEADS, :] * (IDX_HEADS ** -0.5 * IDX_DIM ** -0.5)


def _attn_in(x, g, w, wt, *, tm=512):
    bsz, s, d = x.shape
    av_rows = ROW_BV - ROW_AV
    return pl.pallas_call(
        functools.partial(_attn_in_kernel, tm=tm),
        grid=(bsz, s // tm),
        in_specs=[
            pl.BlockSpec((1, tm, d), lambda b, i: (b, i, 0)),
            pl.BlockSpec((1, d), lambda b, i: (0, 0)),
            pl.BlockSpec((d, PROJ_W), lambda b, i: (0, 0)),
            pl.BlockSpec((VT_ROWS, d), lambda b, i: (0, 0)),
        ],
        out_specs=[
            pl.BlockSpec((1, tm, PROJ_W), lambda b, i: (b, i, 0)),
            pl.BlockSpec((1, tm // TK, av_rows, TK), lambda b, i: (b, i, 0, 0)),
            pl.BlockSpec((1, tm // TK, B_HEAD_DIM, TK), lambda b, i: (b, i, 0, 0)),
            pl.BlockSpec((1, IDX_HEADS, tm), lambda b, i: (b, 0, i)),
        ],
        out_shape=[
            jax.ShapeDtypeStruct((bsz, s, PROJ_W), BF16),
            jax.ShapeDtypeStruct((bsz, s // TK, av_rows, TK), BF16),
            jax.ShapeDtypeStruct((bsz, s // TK, B_HEAD_DIM, TK), BF16),
            jax.ShapeDtypeStruct((bsz, IDX_HEADS, s), F32),
        ],
        compiler_params=pltpu.CompilerParams(
            dimension_semantics=("arbitrary", "arbitrary"), vmem_limit_bytes=VMEM_LIMIT),
        name="attn_in",
    )(x, g, w, wt)


def _diff_attn_kernel(lam_ref, q_ref, k_ref, vt_ref, bias_ref, g_ref, o_ref,
                      vaug_ref, s_ref, acc_ref, ot_ref, *, lambda_init, seq):
    i = pl.program_id(1)
    nkb = seq // TK
    n_maps = 2 * A_HEADS

    @pl.when(i == 0)
    def _():
        for c in range(nkb):
            for h in range(A_HEADS):
                vaug_ref[h, c, 0:A_V_DIM, :] = vt_ref[0, c, h * A_V_DIM:(h + 1) * A_V_DIM, :]
                vaug_ref[h, c, A_V_DIM:A_V_DIM + ONES_ROWS, :] = jnp.ones((ONES_ROWS, TK), BF16)

    lv = lam_ref[...]
    lam = (jnp.exp(jnp.sum(lv[0:1] * lv[1:2], axis=-1, keepdims=True))
           - jnp.exp(jnp.sum(lv[2:3] * lv[3:4], axis=-1, keepdims=True)) + lambda_init)

    q = q_ref[0]
    qm = []
    for h in range(A_HEADS):
        qh = q[:, h * LANES:(h + 1) * LANES]
        qm += [_lane_band(qh, 0, A_HEAD_DIM), _lane_band(qh, A_HEAD_DIM, A_HEAD_DIM)]

    def logits(j, c, kind):
        h = c // 2
        kb = k_ref[0, pl.ds(pl.multiple_of(j * TK, TK), TK), h * LANES:(h + 1) * LANES]
        s = _dot_nt(kb, qm[c])
        return s if kind == 'far' else s + bias_ref[_bias_kind(j, i), h]

    _stream_key_blocks(i, n_maps, s_ref, acc_ref, logits, lambda j, c: vaug_ref[c // 2, j])

    for h in range(A_HEADS):
        a1 = acc_ref[2 * h]
        a2 = acc_ref[2 * h + 1]
        a = (a1[0:A_V_DIM] / a1[A_V_DIM:A_V_DIM + 1]
             - lam * (a2[0:A_V_DIM] / a2[A_V_DIM:A_V_DIM + 1]))
        y = a * lax.rsqrt(jnp.mean(a * a, axis=0, keepdims=True) + EPS) * g_ref[...]
        ot_ref[h * A_V_DIM:(h + 1) * A_V_DIM, :] = y * (1.0 - lambda_init)
    o_ref[0] = ot_ref[...].T.astype(BF16)


def _diff_attn(proj, avt, lam_vecs, bias, subln_g_col, *, lambda_init):
    bsz, s, _ = proj.shape
    nq = s // TQ
    nkb = s // TK
    width = A_HEADS * A_V_DIM
    return pl.pallas_call(
        functools.partial(_diff_attn_kernel, lambda_init=lambda_init, seq=s),
        grid=(bsz, nq),
        in_specs=[
            pl.BlockSpec((4, A_HEAD_DIM), lambda b, i: (0, 0)),
            pl.BlockSpec((1, TQ, width), lambda b, i: (b, i, COL_AQ // width)),
            pl.BlockSpec((1, s, width), lambda b, i: (b, 0, COL_AK // width)),
            pl.BlockSpec((1, nkb, width, TK), lambda b, i: (b, 0, 0, 0)),
            pl.BlockSpec((3, A_HEADS, TK, TQ), lambda b, i: (0, 0, 0, 0)),
            pl.BlockSpec((A_V_DIM, 1), lambda b, i: (0, 0)),
        ],
        out_specs=pl.BlockSpec((1, TQ, width), lambda b, i: (b, i, 0)),
        out_shape=jax.ShapeDtypeStruct((bsz, s, width), BF16),
        scratch_shapes=[
            pltpu.VMEM((A_HEADS, nkb, A_V_DIM + ONES_ROWS, TK), BF16),
            pltpu.VMEM((2 * A_HEADS, TK, TQ), F32),
            pltpu.VMEM((2 * A_HEADS, A_V_DIM + ONES_ROWS, TQ), F32),
            pltpu.VMEM((width, TQ), F32),
        ],
        compiler_params=pltpu.CompilerParams(
            dimension_semantics=("arbitrary", "arbitrary"), vmem_limit_bytes=VMEM_LIMIT),
        name="diff_attn",
    )(lam_vecs, proj, proj, avt, bias, subln_g_col)


def _dsa_kernel(qb_ref, iq_ref, kb_ref, ik_ref, vt_ref, iwt_ref, bias_ref, o_ref,
                score_ref, key_ref, plane_ref, mask_ref, vaug_ref, s_ref, acc_ref, ot_ref,
                *, top_k, seq):
    i = pl.program_id(1)
    nkb = seq // TK

    @pl.when(i == 0)
    def _():
        for c in range(nkb):
            vaug_ref[c, 0:B_HEAD_DIM, :] = vt_ref[0, c]
            vaug_ref[c, B_HEAD_DIM:B_HEAD_DIM + ONES_ROWS, :] = jnp.ones((ONES_ROWS, TK), BF16)

    @pl.when((i == 0) & (pl.program_id(0) == 0))
    def _():
        plane_ref[0:KEY_BITS] = jnp.zeros((KEY_BITS, nkb, SUBLANES, TQ), I32)
        for t, plane in enumerate(_index_planes(nkb)):
            plane_ref[KEY_BITS + t] = plane

    def rows(ref, j):
        return ref[0, pl.ds(pl.multiple_of(j * TK, TK), TK), :]

    krow = lax.broadcasted_iota(I32, (TK, TQ), 0)
    qcol = lax.broadcasted_iota(I32, (TK, TQ), 1)

    iq = iq_ref[0]
    per_blk = LANES // IDX_DIM
    iqh = [_lane_band(iq[:, (h // per_blk) * LANES:(h // per_blk + 1) * LANES],
                      (h % per_blk) * IDX_DIM, IDX_DIM) for h in range(IDX_HEADS)]
    w = iwt_ref[0]

    def stage_scores(j):
        ikb = rows(ik_ref, j)
        half = TK // 2
        for r in (0, half):
            acc = jnp.zeros((half, TQ), F32)
            for h in range(IDX_HEADS):
                acc = acc + jnp.maximum(_dot_nt(ikb[r:r + half], iqh[h]), 0.0) * w[h:h + 1, :]
            score_ref[r:r + half, :] = acc

    def staged_keys():
        acc = score_ref[...]
        acc = jnp.where(acc == 0.0, 0.0, acc)
        bits = lax.bitcast_convert_type(acc, I32)
        return bits ^ ((bits >> 31) & 0x7FFFFFFF)

    def store_keys(j, keys):
        key_ref[j] = keys
        for p, plane in enumerate(_bit_planes(keys)):
            plane_ref[p, j] = plane

    def score_body(j, c):
        keys = staged_keys()
        stage_scores(j + 1)
        store_keys(j, keys)
        return c

    stage_scores(0)
    lax.fori_loop(0, i, score_body, 0)
    admissible = (krow // CHUNK) <= (qcol // CHUNK)
    store_keys(i, jnp.where(admissible, staged_keys(), KEY_NEG_INF))

    blk = lax.broadcasted_iota(I32, (nkb, SUBLANES, TQ), 0)
    alive0 = jnp.where(blk <= i, -1, 0)

    def bit_body(p, carry, n_bits):
        alive, n_gt, thr_u = carry
        hit = alive & plane_ref[p]
        pc = jnp.sum(lax.population_count(hit), axis=0)
        n_hit = jnp.sum(pc.astype(F32), axis=0, keepdims=True)
        take = (n_gt + n_hit) >= top_k
        thr_u = thr_u | jnp.where(take, lax.shift_left(jnp.int32(1), n_bits - 1 - p), 0)
        n_gt = n_gt + jnp.where(take, 0.0, n_hit)
        alive = jnp.where(take, hit, alive ^ hit)
        return alive, n_gt, thr_u

    zero = jnp.zeros((1, TQ), I32)
    alive, n_gt, thr_u = lax.fori_loop(
        0, KEY_BITS, lambda p, c: bit_body(p, c, KEY_BITS), (alive0, jnp.zeros((1, TQ), F32), zero))
    idx_bits = _index_bits(seq)
    _, _, rev_idx = lax.fori_loop(
        KEY_BITS, KEY_BITS + idx_bits,
        lambda p, c: bit_body(p, c, KEY_BITS + idx_bits), (alive, n_gt, zero))
    thr = thr_u ^ INT_MIN
    lim = seq - rev_idx
    lim = jnp.where(thr > KEY_NEG_INF, lim, seq)
    thr = jnp.maximum(thr, KEY_NEG_INF + 1)

    def mask_body(j, c):
        late = jnp.where(krow >= lim - j * TK, 1, 0)
        mask_ref[j] = jnp.where(key_ref[j] - late >= thr, 0.0, NEG)
        return c

    lax.fori_loop(0, i + 1, mask_body, 0)

    qb = qb_ref[0]
    qh = [_lane_band(qb[:, (h // 2) * LANES:(h // 2 + 1) * LANES],
                     (h % 2) * B_HEAD_DIM, B_HEAD_DIM) for h in range(B_HEADS)]

    def logits(j, h, kind):
        s = _dot_nt(rows(kb_ref, j), qh[h]) + mask_ref[j]
        return s if kind == 'far' else s + bias_ref[_bias_kind(j, i), h]

    _stream_key_blocks(i, B_HEADS, s_ref, acc_ref, logits, lambda j, h: vaug_ref[j])

    for h in range(B_HEADS):
        a = acc_ref[h]
        ot_ref[h * B_HEAD_DIM:(h + 1) * B_HEAD_DIM, :] = (
            a[0:B_HEAD_DIM] / a[B_HEAD_DIM:B_HEAD_DIM + 1])
    o_ref[0] = ot_ref[...].T.astype(BF16)


def _dsa_attn(proj, bvt, iwt, bias, *, top_k):
    bsz, s, _ = proj.shape
    nq = s // TQ
    nkb = s // TK
    width = B_HEADS * B_HEAD_DIM
    iq_w = IDX_HEADS * IDX_DIM
    return pl.pallas_call(
        functools.partial(_dsa_kernel, top_k=top_k, seq=s),
        grid=(bsz, nq),
        in_specs=[
            pl.BlockSpec((1, TQ, width), lambda b, i: (b, i, COL_BQ // width)),
            pl.BlockSpec((1, TQ, iq_w), lambda b, i: (b, i, COL_IQ // iq_w)),
            pl.BlockSpec((1, s, LANES), lambda b, i: (b, 0, COL_KB // LANES)),
            pl.BlockSpec((1, s, LANES), lambda b, i: (b, 0, COL_IK // LANES)),
            pl.BlockSpec((1, nkb, B_HEAD_DIM, TK), lambda b, i: (b, 0, 0, 0)),
            pl.BlockSpec((1, IDX_HEADS, TQ), lambda b, i: (b, 0, i)),
            pl.BlockSpec((3, B_HEADS, TK, TQ), lambda b, i: (0, 0, 0, 0)),
        ],
        out_specs=pl.BlockSpec((1, TQ, width), lambda b, i: (b, i, 0)),
        out_shape=jax.ShapeDtypeStruct((bsz, s, width), BF16),
        scratch_shapes=[
            pltpu.VMEM((TK, TQ), F32),
            pltpu.VMEM((nkb, TK, TQ), I32),
            pltpu.VMEM((KEY_BITS + _index_bits(s), nkb, SUBLANES, TQ), I32),
            pltpu.VMEM((nkb, TK, TQ), F32),
            pltpu.VMEM((nkb, B_HEAD_DIM + ONES_ROWS, TK), BF16),
            pltpu.VMEM((B_HEADS, TK, TQ), F32),
            pltpu.VMEM((B_HEADS, B_HEAD_DIM + ONES_ROWS, TQ), F32),
            pltpu.VMEM((width, TQ), F32),
        ],
        compiler_params=pltpu.CompilerParams(
            dimension_semantics=("arbitrary", "arbitrary"), vmem_limit_bytes=VMEM_LIMIT),
        name="dsa_attn",
    )(proj, proj, proj, proj, bvt, iwt, bias)


TAIL_ROWS = 256


def _tail_kernel(ya_ref, yb_ref, wo_ref, x_ref, g_mix_ref, g_in_ref, wu_ref, wd_ref, g_out_ref,
                 o_ref, x1_ref, h_ref, acc_ref):
    f = pl.program_id(1)

    @pl.when(f == 0)
    def _():
        half = ya_ref.shape[-1]
        for r in range(0, x_ref.shape[0], TAIL_ROWS):
            rows = slice(r, r + TAIL_ROWS)
            m = (_dot(ya_ref[rows, :], wo_ref[0:half, :])
                 + _dot(yb_ref[rows, :], wo_ref[half:2 * half, :]))
            x1 = x_ref[rows, :] + _rms(m, g_mix_ref[...])
            x1_ref[rows, :] = x1
            h_ref[rows, :] = _rms(x1, g_in_ref[...]).astype(BF16)
        acc_ref[...] = jnp.zeros_like(acc_ref)

    u = jnp.maximum(_dot(h_ref[...], wu_ref[...]), 0.0)
    acc_ref[...] += _dot((u * u).astype(BF16), wd_ref[...])

    @pl.when(f == pl.num_programs(1) - 1)
    def _():
        o_ref[...] = x1_ref[...] + _rms(acc_ref[...], g_out_ref[...])


def _layer_tail(ya, yb, wo, x, g_mix, g_in, wu, wd, layer, g_out, *, tm=1024, tf=1024):
    n, d = x.shape
    half = ya.shape[-1]
    ff = wu.shape[-1]
    return pl.pallas_call(
        _tail_kernel,
        grid=(n // tm, ff // tf),
        in_specs=[
            pl.BlockSpec((tm, half), lambda i, f: (i, 0)),
            pl.BlockSpec((tm, half), lambda i, f: (i, 0)),
            pl.BlockSpec((2 * half, d), lambda i, f: (0, 0)),
            pl.BlockSpec((tm, d), lambda i, f: (i, 0)),
            pl.BlockSpec((1, d), lambda i, f: (0, 0)),
            pl.BlockSpec((1, d), lambda i, f: (0, 0)),
            pl.BlockSpec((None, d, tf), lambda i, f: (layer, 0, f)),
            pl.BlockSpec((None, tf, d), lambda i, f: (layer, f, 0)),
            pl.BlockSpec((1, d), lambda i, f: (0, 0)),
        ],
        out_specs=pl.BlockSpec((tm, d), lambda i, f: (i, 0)),
        out_shape=jax.ShapeDtypeStruct((n, d), F32),
        scratch_shapes=[pltpu.VMEM((tm, d), F32), pltpu.VMEM((tm, d), BF16),
                        pltpu.VMEM((tm, d), F32)],
        compiler_params=pltpu.CompilerParams(
            dimension_semantics=("arbitrary", "arbitrary"), vmem_limit_bytes=VMEM_LIMIT),
        name="layer_tail",
    )(ya, yb, wo, x, g_mix, g_in, wu, wd, g_out)


CONV_ROWS = 64
U_HIST = 32
Z_HIST = 8


def _conv_kernel(x_ref, g_in_ref, w_in_ref, dw_w_ref, dw_b_ref, ln_g_ref, ln_b_ref, sc_w_ref,
                 u_ref, z_ref, ubuf_ref, zbuf_ref, shift_ref, conv_ref, *, ts):
    @pl.when(pl.program_id(1) == 0)
    def _():
        ubuf_ref[0:U_HIST, :] = jnp.zeros((U_HIST, CONV_CH), F32)
        zbuf_ref[0:Z_HIST, :] = jnp.zeros((Z_HIST, SC_CH), F32)

    x = x_ref[0]
    hb = _rms(x, g_in_ref[...]).astype(BF16)
    c = CONV_CH
    ca = _dot(hb, w_in_ref[:, 0:c])
    cg = _dot(hb, w_in_ref[:, c:2 * c])
    ubuf_ref[U_HIST:U_HIST + ts, :] = ca * jax.nn.sigmoid(cg)
    dc = _dot(hb, w_in_ref[:, 3 * c:4 * c])
    dh = _dot(hb, w_in_ref[:, 4 * c:5 * c])
    zbuf_ref[Z_HIST:Z_HIST + ts, :] = dc * dh

    first_off = U_HIST - (CONV_WIDTH - 1)
    shift_rows = shift_ref.shape[1]
    for r in range(1, SUBLANES):
        shift_ref[r - 1] = ubuf_ref[r:r + shift_rows, :]

    for base in range(0, ts, CONV_ROWS):
        acc = jnp.zeros((CONV_ROWS, c), F32)
        for j in range(CONV_WIDTH):
            r, a = (first_off + j) % SUBLANES, (first_off + j) // SUBLANES
            rows = pl.ds(base + SUBLANES * a, CONV_ROWS)
            taps = ubuf_ref[rows, :] if r == 0 else shift_ref[r - 1, rows, :]
            acc = acc + dw_w_ref[j:j + 1, :] * taps
        conv_ref[base:base + CONV_ROWS, :] = acc
    u = conv_ref[...] + dw_b_ref[...]
    mu = jnp.mean(u, axis=-1, keepdims=True)
    ctr = u - mu
    var = jnp.mean(ctr * ctr, axis=-1, keepdims=True)
    u = ctr * lax.rsqrt(var + EPS) * ln_g_ref[...] + ln_b_ref[...]
    u = u * jax.nn.sigmoid(u)

    z = jnp.zeros((ts, c), F32)
    for j in range(SC_WIDTH):
        off = Z_HIST - (SC_WIDTH - 1) + j
        z = z + sc_w_ref[j:j + 1, :] * zbuf_ref[off:off + ts, :]
    z = _dot(hb, w_in_ref[:, 2 * c:3 * c]) * z

    ubuf_ref[0:U_HIST, :] = ubuf_ref[ts:ts + U_HIST, :]
    zbuf_ref[0:Z_HIST, :] = zbuf_ref[ts:ts + Z_HIST, :]

    u_ref[0] = u.astype(BF16)
    z_ref[0] = z.astype(BF16)


def _conv_mixer(x, g_in, w_in, dw_w, dw_b, ln_g, ln_b, sc_w, *, ts=512):
    bsz, s, d = x.shape
    full = lambda a: pl.BlockSpec(a.shape, lambda b, i: (0,) * a.ndim)
    return pl.pallas_call(
        functools.partial(_conv_kernel, ts=ts),
        grid=(bsz, s // ts),
        in_specs=[pl.BlockSpec((1, ts, d), lambda b, i: (b, i, 0)),
                  full(g_in), full(w_in), full(dw_w), full(dw_b), full(ln_g), full(ln_b),
                  full(sc_w)],
        out_specs=[pl.BlockSpec((1, ts, CONV_CH), lambda b, i: (b, i, 0)),
                   pl.BlockSpec((1, ts, SC_CH), lambda b, i: (b, i, 0))],
        out_shape=[jax.ShapeDtypeStruct((bsz, s, CONV_CH), BF16),
                   jax.ShapeDtypeStruct((bsz, s, SC_CH), BF16)],
        scratch_shapes=[pltpu.VMEM((U_HIST + ts, CONV_CH), F32),
                        pltpu.VMEM((Z_HIST + ts, SC_CH), F32),
                        pltpu.VMEM((SUBLANES - 1, ts + U_HIST - SUBLANES, CONV_CH), F32),
                        pltpu.VMEM((ts, CONV_CH), F32)],
        compiler_params=pltpu.CompilerParams(
            dimension_semantics=("arbitrary", "arbitrary"), vmem_limit_bytes=VMEM_LIMIT),
        name="conv_mixer",
    )(x, g_in, w_in, dw_w, dw_b, ln_g, ln_b, sc_w)


def _t5_bucket(rel):
    nb = NUM_BUCKETS // 2
    ret = jnp.where(rel > 0, nb, 0)
    n = jnp.abs(rel)
    max_exact = nb // 2
    nf = jnp.maximum(n, 1).astype(jnp.float32)
    large = max_exact + (jnp.log(nf / max_exact) / math.log(MAX_DISTANCE / max_exact)
                         * (nb - max_exact)).astype(jnp.int32)
    large = jnp.minimum(large, nb - 1)
    return ret + jnp.where(n < max_exact, n, large)


REL_LO = -(TQ + TK - 1)
REL_LEN = TK - REL_LO
REL_LANES = -(-REL_LEN // LANES) * LANES


def _bias_kernel(fr_ref, a_ref, b_ref):
    krow = lax.broadcasted_iota(I32, (TK, TQ), 0)
    qcol = lax.broadcasted_iota(I32, (TK, TQ), 1)
    admissible = (krow // CHUNK) <= (qcol // CHUNK)
    zeros = jnp.zeros((TK, TQ), F32)
    for h in range(A_HEADS + B_HEADS):
        table = jnp.broadcast_to(fr_ref[h], (TK, REL_LANES))
        blocks = []
        for shift in (TK, 0):
            s0 = (-shift - REL_LO - (REL_LEN - 1)) % REL_LANES
            blocks.append(pltpu.roll(table, s0, 1, stride=1, stride_axis=0)[:, 0:TQ])
        prev, diag = blocks
        if h < A_HEADS:
            a_ref[0, h], a_ref[1, h], a_ref[2, h] = zeros, prev, jnp.where(admissible, diag, NEG)
        else:
            g = h - A_HEADS
            b_ref[0, g], b_ref[1, g], b_ref[2, g] = zeros, prev, diag


def _bias_tables(rel_bias):
    rel = jnp.arange(REL_LO, TK, dtype=jnp.int32)
    far_bucket = NUM_BUCKETS // 2 - 1
    f = (rel_bias[_t5_bucket(rel)].astype(F32) - rel_bias[far_bucket].astype(F32)[None, :]).T
    fr = jnp.pad(f[:, ::-1] * LOG2E, ((0, 0), (0, REL_LANES - REL_LEN)))[:, None, :]
    heads = A_HEADS + B_HEADS
    return pl.pallas_call(
        _bias_kernel,
        grid=(1,),
        in_specs=[pl.BlockSpec((heads, 1, REL_LANES), lambda i: (0, 0, 0))],
        out_specs=[pl.BlockSpec((3, A_HEADS, TK, TQ), lambda i: (0, 0, 0, 0)),
                   pl.BlockSpec((3, B_HEADS, TK, TQ), lambda i: (0, 0, 0, 0))],
        out_shape=[jax.ShapeDtypeStruct((3, A_HEADS, TK, TQ), F32),
                   jax.ShapeDtypeStruct((3, B_HEADS, TK, TQ), F32)],
        compiler_params=pltpu.CompilerParams(
            dimension_semantics=("arbitrary",), vmem_limit_bytes=VMEM_LIMIT),
        name="bias_tables",
    )(fr)


def _attn_weights(w_in):
    o = np.cumsum([0, 512, 512, 512, 512, 64, 64, 256, 32, 8])
    aq, ak, av, bq, bk, bv, iq, ik, iw = [w_in[:, o[n]:o[n + 1]] for n in range(9)]
    aq = aq * (A_HEAD_DIM ** -0.5 * LOG2E)
    bq = bq * (B_HEAD_DIM ** -0.5 * LOG2E)
    w = jnp.concatenate([aq, ak, bq, iq, bk, bk, ik, ik, ik, ik], axis=1).astype(BF16)
    pad = jnp.zeros((w_in.shape[0], VT_ROWS - ROW_IW - IDX_HEADS), w_in.dtype)
    wt = jnp.concatenate([av, bv, iw, pad], axis=1).T.astype(BF16)
    return w, wt


def kernel(x, rel_bias, norm_g, w_mlp_up, w_mlp_down, attn_w_in, attn_w_out, diff_lambda,
           diff_subln_g, conv_w_in, conv_w_out, conv_dw_w, conv_dw_b, conv_ln_g, conv_ln_b,
           sconv_w):
    bsz, s, d = x.shape
    depth = norm_g.shape[0]
    top_k = min(TOPK_MAX, s // 4)
    row = lambda v: v.reshape(1, -1)
    a_bias, b_bias = _bias_tables(rel_bias)
    w_up, w_down = w_mlp_up.astype(BF16), w_mlp_down.astype(BF16)
    for i in range(depth):
        j = i // 2
        if i % 2 == 0:
            lambda_init = 0.8 - 0.6 * math.exp(-0.3 * i)
            w, wt = _attn_weights(attn_w_in[j])
            proj, avt, bvt, iwt = _attn_in(x.reshape(bsz, s, d), row(norm_g[i, 0]), w, wt)
            ya = _diff_attn(proj, avt, diff_lambda[j], a_bias, diff_subln_g[j].reshape(-1, 1),
                            lambda_init=lambda_init)
            yb = _dsa_attn(proj, bvt, iwt, b_bias, top_k=top_k)
            w_out = attn_w_out[j]
        else:
            ya, yb = _conv_mixer(x.reshape(bsz, s, d), row(norm_g[i, 0]),
                                 conv_w_in[j].astype(BF16), conv_dw_w[j], row(conv_dw_b[j]),
                                 row(conv_ln_g[j]), row(conv_ln_b[j]), sconv_w[j])
            w_out = conv_w_out[j]
        x = _layer_tail(ya.reshape(bsz * s, -1), yb.reshape(bsz * s, -1), w_out.astype(BF16),
                        x.reshape(bsz * s, d), row(norm_g[i, 1]), row(norm_g[i, 2]),
                        w_up, w_down, i, row(norm_g[i, 3]))
    return x.reshape(bsz, s, d)
```

```python
import functools
import math

import numpy as np
import jax
import jax.numpy as jnp
from jax import lax
from jax.experimental import pallas as pl
from jax.experimental.pallas import tpu as pltpu

F32 = jnp.float32
BF16 = jnp.bfloat16
I32 = jnp.int32

D_MODEL = 1024
CHUNK = 64
NUM_BUCKETS = 32
MAX_DISTANCE = 128
EPS = 1e-6
NEG = -1e30
LOG2E = math.log2(math.e)
A_HEADS = 4
A_HEAD_DIM = 64
A_V_DIM = 2 * A_HEAD_DIM
B_HEADS = 8
B_HEAD_DIM = 64
IDX_HEADS = 8
IDX_DIM = 32
TOPK_MAX = 256
CONV_CH = 512
CONV_WIDTH = 31
SC_CH = 512
SC_WIDTH = 3
D_FF = 4 * D_MODEL

LANES = 128
SUBLANES = 8
VMEM_LIMIT = 56 * 1024 * 1024

TQ = 256
TK = 256
COL_AQ, COL_AK, COL_BQ, COL_IQ, COL_KB, COL_IK, PROJ_W = 0, 512, 1024, 1536, 1792, 1920, 2048
ROW_AV, ROW_BV, ROW_IW, VT_ROWS = 0, 512, 576, 640
ONES_ROWS = 16
INT_MIN = -2 ** 31
KEY_NEG_INF = INT_MIN + 0x7FFFFF


def _rms(x, g):
    return x * lax.rsqrt(jnp.mean(x * x, axis=-1, keepdims=True) + EPS) * g


def _dot_nt(a, b):
    return lax.dot_general(a, b, (((1,), (1,)), ((), ())), preferred_element_type=F32)


def _dot(a, b):
    return jnp.dot(a, b, preferred_element_type=F32)


def _lane_band(x, lo, width):
    lane = lax.broadcasted_iota(I32, x.shape, 1)
    return jnp.where((lane >= lo) & (lane < lo + width), x, jnp.zeros_like(x))


KEY_BITS = 32


def _bit_planes(keys):
    assert keys.shape[0] == KEY_BITS * SUBLANES
    u = keys ^ INT_MIN
    a = [u[SUBLANES * r:SUBLANES * (r + 1), :] for r in range(KEY_BITS)]
    j, m = KEY_BITS // 2, (1 << (KEY_BITS // 2)) - 1
    while j:
        mask = np.int32(np.uint32(m))
        k = 0
        while k < KEY_BITS:
            t = (a[k] ^ lax.shift_right_logical(a[k + j], jnp.int32(j))) & mask
            a[k] = a[k] ^ t
            a[k + j] = a[k + j] ^ lax.shift_left(t, jnp.int32(j))
            k = (k + j + 1) & ~j
        j >>= 1
        m = (m ^ (m << j)) & 0xFFFFFFFF
    return a


def _index_bits(seq):
    bits = seq.bit_length() - 1
    assert seq == 1 << bits and seq >= KEY_BITS * SUBLANES
    return bits


def _index_planes(nkb):
    shape = (nkb, SUBLANES, TQ)
    blk = lax.broadcasted_iota(I32, shape, 0)
    sub = lax.broadcasted_iota(I32, shape, 1)
    sub_bits = SUBLANES.bit_length() - 1
    word_bits = KEY_BITS.bit_length() - 1
    planes = []
    for b in range(_index_bits(nkb * TK) - 1, -1, -1):
        if b >= sub_bits + word_bits:
            on = ((nkb - 1 - blk) >> (b - sub_bits - word_bits)) & 1
            planes.append(jnp.where(on == 1, -1, 0))
        elif b >= sub_bits:
            t = b - sub_bits
            pattern = sum(1 << k for k in range(KEY_BITS) if (k >> t) & 1)
            planes.append(jnp.full(shape, np.int32(np.uint32(pattern)), I32))
        else:
            on = ((SUBLANES - 1 - sub) >> b) & 1
            planes.append(jnp.where(on == 1, -1, 0))
    return planes


def _bias_kind(j, i):
    return jnp.clip(j - i + 2, 0, 2)


def _stream_key_blocks(i, n_chains, s_ref, acc_ref, logits, v_aug, beside=None):
    chains = range(n_chains)

    def stage(j, c, kind):
        s = logits(j, c, kind)
        s_ref[c] = s
        return jnp.max(s, axis=0, keepdims=True)

    def absorb(j, c, m_old, block_max):
        m_new = jnp.maximum(m_old, block_max)
        alpha = jnp.exp2(m_old - m_new)
        p = jnp.exp2(s_ref[c] - m_new).astype(BF16)
        acc_ref[c] = acc_ref[c] * alpha + _dot(v_aug(j, c), p)
        return m_new

    def step(j, carry, kind):
        ms, bms = carry
        out = [(absorb(j, c, ms[c], bms[c]), stage(j + 1, c, kind)) for c in chains]
        if beside is not None:
            beside(j)
        return tuple(o[0] for o in out), tuple(o[1] for o in out)

    acc_ref[...] = jnp.zeros_like(acc_ref)
    carry = ((jnp.full((1, TQ), NEG, F32),) * n_chains, tuple(stage(0, c, 'any') for c in chains))
    n_far = jnp.maximum(i - 2, 0)
    carry = lax.fori_loop(0, n_far, lambda j, cr: step(j, cr, 'far'), carry)
    ms, bms = lax.fori_loop(n_far, i, lambda j, cr: step(j, cr, 'near'), carry)
    for c in chains:
        absorb(i, c, ms[c], bms[c])


def _attn_in_kernel(x_ref, g_ref, w_ref, wt_ref, proj_ref, avt_ref, bvt_ref, iwt_ref, *, tm):
    hb = _rms(x_ref[0], g_ref[...]).astype(BF16)
    proj_ref[0] = _dot(hb, w_ref[...]).astype(BF16)
    t = _dot_nt(wt_ref[...], hb)
    for c in range(tm // TK):
        avt_ref[0, c] = t[ROW_AV:ROW_BV, c * TK:(c + 1) * TK].astype(BF16)
        bvt_ref[0, c] = t[ROW_BV:ROW_IW, c * TK:(c + 1) * TK].astype(BF16)
    iwt_ref[0] = t[ROW_IW:ROW_IW + IDX_HEADS, :] * (IDX_HEADS ** -0.5 * IDX_DIM ** -0.5)


def _attn_in(x, g, w, wt, *, tm=512):
    bsz, s, d = x.shape
    av_rows = ROW_BV - ROW_AV
    return pl.pallas_call(
        functools.partial(_attn_in_kernel, tm=tm),
        grid=(bsz, s // tm),
        in_specs=[
            pl.BlockSpec((1, tm, d), lambda b, i: (b, i, 0)),
            pl.BlockSpec((1, d), lambda b, i: (0, 0)),
            pl.BlockSpec((d, PROJ_W), lambda b, i: (0, 0)),
            pl.BlockSpec((VT_ROWS, d), lambda b, i: (0, 0)),
        ],
        out_specs=[
            pl.BlockSpec((1, tm, PROJ_W), lambda b, i: (b, i, 0)),
            pl.BlockSpec((1, tm // TK, av_rows, TK), lambda b, i: (b, i, 0, 0)),
            pl.BlockSpec((1, tm // TK, B_HEAD_DIM, TK), lambda b, i: (b, i, 0, 0)),
            pl.BlockSpec((1, IDX_HEADS, tm), lambda b, i: (b, 0, i)),
        ],
        out_shape=[
            jax.ShapeDtypeStruct((bsz, s, PROJ_W), BF16),
            jax.ShapeDtypeStruct((bsz, s // TK, av_rows, TK), BF16),
            jax.ShapeDtypeStruct((bsz, s // TK, B_HEAD_DIM, TK), BF16),
            jax.ShapeDtypeStruct((bsz, IDX_HEADS, s), F32),
        ],
        compiler_params=pltpu.CompilerParams(
            dimension_semantics=("arbitrary", "arbitrary"), vmem_limit_bytes=VMEM_LIMIT),
        name="attn_in",
    )(x, g, w, wt)


def _key_rows(ref, j):
    return ref[0, pl.ds(pl.multiple_of(j * TK, TK), TK), :]


def _indexer(i, iq_ref, ik_ref, iwt_ref, score_ref, key_ref, plane_ref):
    iq = iq_ref[0]
    per_blk = LANES // IDX_DIM
    iqh = [_lane_band(iq[:, (h // per_blk) * LANES:(h // per_blk + 1) * LANES],
                      (h % per_blk) * IDX_DIM, IDX_DIM) for h in range(IDX_HEADS)]
    w = iwt_ref[0]

    def stage_scores(j):
        ikb = _key_rows(ik_ref, j)
        half = TK // 2
        for r in (0, half):
            acc = jnp.zeros((half, TQ), F32)
            for h in range(IDX_HEADS):
                acc = acc + jnp.maximum(_dot_nt(ikb[r:r + half], iqh[h]), 0.0) * w[h:h + 1, :]
            score_ref[r:r + half, :] = acc

    def staged_keys():
        acc = score_ref[...]
        acc = jnp.where(acc == 0.0, 0.0, acc)
        bits = lax.bitcast_convert_type(acc, I32)
        return bits ^ ((bits >> 31) & 0x7FFFFFFF)

    def store_keys(j, keys):
        key_ref[j] = keys
        for p, plane in enumerate(_bit_planes(keys)):
            plane_ref[p, j] = plane

    def per_block(j):
        keys = staged_keys()
        stage_scores(j + 1)
        store_keys(j, keys)

    def finish():
        krow = lax.broadcasted_iota(I32, (TK, TQ), 0)
        qcol = lax.broadcasted_iota(I32, (TK, TQ), 1)
        admissible = (krow // CHUNK) <= (qcol // CHUNK)
        store_keys(i, jnp.where(admissible, staged_keys(), KEY_NEG_INF))

    stage_scores(0)
    return per_block, finish


def _select_mask(i, key_ref, plane_ref, mask_ref, *, top_k, seq):
    nkb = seq // TK
    blk = lax.broadcasted_iota(I32, (nkb, SUBLANES, TQ), 0)
    alive0 = jnp.where(blk <= i, -1, 0)

    def bit_body(p, carry, n_bits):
        alive, n_gt, thr_u = carry
        hit = alive & plane_ref[p]
        pc = jnp.sum(lax.population_count(hit), axis=0)
        n_hit = jnp.sum(pc.astype(F32), axis=0, keepdims=True)
        take = (n_gt + n_hit) >= top_k
        thr_u = thr_u | jnp.where(take, lax.shift_left(jnp.int32(1), n_bits - 1 - p), 0)
        n_gt = n_gt + jnp.where(take, 0.0, n_hit)
        alive = jnp.where(take, hit, alive ^ hit)
        return alive, n_gt, thr_u

    zero = jnp.zeros((1, TQ), I32)
    alive, n_gt, thr_u = lax.fori_loop(
        0, KEY_BITS, lambda p, c: bit_body(p, c, KEY_BITS), (alive0, jnp.zeros((1, TQ), F32), zero))
    idx_bits = _index_bits(seq)
    _, _, rev_idx = lax.fori_loop(
        KEY_BITS, KEY_BITS + idx_bits,
        lambda p, c: bit_body(p, c, KEY_BITS + idx_bits), (alive, n_gt, zero))
    thr = thr_u ^ INT_MIN
    lim = seq - rev_idx
    lim = jnp.where(thr > KEY_NEG_INF, lim, seq)
    thr = jnp.maximum(thr, KEY_NEG_INF + 1)
    krow = lax.broadcasted_iota(I32, (TK, TQ), 0)

    def mask_body(j, c):
        late = jnp.where(krow >= lim - j * TK, 1, 0)
        mask_ref[j] = jnp.where(key_ref[j] - late >= thr, 0.0, NEG)
        return c

    lax.fori_loop(0, i + 1, mask_body, 0)


def _attn_kernel(lam_ref, aq_ref, ak_ref, avt_ref, abias_ref, g_ref,
                 bq_ref, iq_ref, kb_ref, ik_ref, bvt_ref, iwt_ref, bbias_ref,
                 ya_ref, yb_ref,
                 avaug_ref, bvaug_ref, s_ref, aacc_ref, bacc_ref, ot_ref,
                 score_ref, key_ref, plane_ref, mask_ref, *, lambda_init, top_k, seq):
    i = pl.program_id(1)
    nkb = seq // TK
    n_maps = 2 * A_HEADS

    @pl.when(i == 0)
    def _():
        for c in range(nkb):
            for h in range(A_HEADS):
                avaug_ref[h, c, 0:A_V_DIM, :] = avt_ref[0, c, h * A_V_DIM:(h + 1) * A_V_DIM, :]
                avaug_ref[h, c, A_V_DIM:A_V_DIM + ONES_ROWS, :] = jnp.ones((ONES_ROWS, TK), BF16)
            bvaug_ref[c, 0:B_HEAD_DIM, :] = bvt_ref[0, c]
            bvaug_ref[c, B_HEAD_DIM:B_HEAD_DIM + ONES_ROWS, :] = jnp.ones((ONES_ROWS, TK), BF16)

    @pl.when((i == 0) & (pl.program_id(0) == 0))
    def _():
        plane_ref[0:KEY_BITS] = jnp.zeros((KEY_BITS, nkb, SUBLANES, TQ), I32)
        for t, plane in enumerate(_index_planes(nkb)):
            plane_ref[KEY_BITS + t] = plane

    lv = lam_ref[...]
    lam = (jnp.exp(jnp.sum(lv[0:1] * lv[1:2], axis=-1, keepdims=True))
           - jnp.exp(jnp.sum(lv[2:3] * lv[3:4], axis=-1, keepdims=True)) + lambda_init)

    q = aq_ref[0]
    qm = []
    for h in range(A_HEADS):
        qh = q[:, h * LANES:(h + 1) * LANES]
        qm += [_lane_band(qh, 0, A_HEAD_DIM), _lane_band(qh, A_HEAD_DIM, A_HEAD_DIM)]

    def a_logits(j, c, kind):
        h = c // 2
        kb = ak_ref[0, pl.ds(pl.multiple_of(j * TK, TK), TK), h * LANES:(h + 1) * LANES]
        s = _dot_nt(kb, qm[c])
        return s if kind == 'far' else s + abias_ref[_bias_kind(j, i), h]

    index_block, finish_index = _indexer(i, iq_ref, ik_ref, iwt_ref, score_ref, key_ref, plane_ref)
    _stream_key_blocks(i, n_maps, s_ref, aacc_ref, a_logits, lambda j, c: avaug_ref[c // 2, j],
                       beside=index_block)
    finish_index()

    for h in range(A_HEADS):
        a1 = aacc_ref[2 * h]
        a2 = aacc_ref[2 * h + 1]
        a = (a1[0:A_V_DIM] / a1[A_V_DIM:A_V_DIM + 1]
             - lam * (a2[0:A_V_DIM] / a2[A_V_DIM:A_V_DIM + 1]))
        y = a * lax.rsqrt(jnp.mean(a * a, axis=0, keepdims=True) + EPS) * g_ref[...]
        ot_ref[h * A_V_DIM:(h + 1) * A_V_DIM, :] = y * (1.0 - lambda_init)
    ya_ref[0] = ot_ref[...].T.astype(BF16)

    _select_mask(i, key_ref, plane_ref, mask_ref, top_k=top_k, seq=seq)

    qb = bq_ref[0]
    qh = [_lane_band(qb[:, (h // 2) * LANES:(h // 2 + 1) * LANES],
                     (h % 2) * B_HEAD_DIM, B_HEAD_DIM) for h in range(B_HEADS)]

    def b_logits(j, h, kind):
        s = _dot_nt(_key_rows(kb_ref, j), qh[h]) + mask_ref[j]
        return s if kind == 'far' else s + bbias_ref[_bias_kind(j, i), h]

    _stream_key_blocks(i, B_HEADS, s_ref, bacc_ref, b_logits, lambda j, h: bvaug_ref[j])

    for h in range(B_HEADS):
        a = bacc_ref[h]
        ot_ref[h * B_HEAD_DIM:(h + 1) * B_HEAD_DIM, :] = (
            a[0:B_HEAD_DIM] / a[B_HEAD_DIM:B_HEAD_DIM + 1])
    yb_ref[0] = ot_ref[...].T.astype(BF16)


def _attention(proj, avt, bvt, iwt, lam_vecs, a_bias, b_bias, subln_g_col, *, lambda_init, top_k):
    bsz, s, _ = proj.shape
    nq = s // TQ
    nkb = s // TK
    a_width = A_HEADS * A_V_DIM
    b_width = B_HEADS * B_HEAD_DIM
    assert a_width == b_width and 2 * A_HEADS == B_HEADS
    iq_w = IDX_HEADS * IDX_DIM
    once = pl.Buffered(1)
    return pl.pallas_call(
        functools.partial(_attn_kernel, lambda_init=lambda_init, top_k=top_k, seq=s),
        grid=(bsz, nq),
        in_specs=[
            pl.BlockSpec((4, A_HEAD_DIM), lambda b, i: (0, 0)),
            pl.BlockSpec((1, TQ, a_width), lambda b, i: (b, i, COL_AQ // a_width)),
            pl.BlockSpec((1, s, a_width), lambda b, i: (b, 0, COL_AK // a_width), once),
            pl.BlockSpec((1, nkb, a_width, TK), lambda b, i: (b, 0, 0, 0), once),
            pl.BlockSpec((3, A_HEADS, TK, TQ), lambda b, i: (0, 0, 0, 0), once),
            pl.BlockSpec((A_V_DIM, 1), lambda b, i: (0, 0)),
            pl.BlockSpec((1, TQ, b_width), lambda b, i: (b, i, COL_BQ // b_width)),
            pl.BlockSpec((1, TQ, iq_w), lambda b, i: (b, i, COL_IQ // iq_w)),
            pl.BlockSpec((1, s, LANES), lambda b, i: (b, 0, COL_KB // LANES), once),
            pl.BlockSpec((1, s, LANES), lambda b, i: (b, 0, COL_IK // LANES), once),
            pl.BlockSpec((1, nkb, B_HEAD_DIM, TK), lambda b, i: (b, 0, 0, 0), once),
            pl.BlockSpec((1, IDX_HEADS, TQ), lambda b, i: (b, 0, i)),
            pl.BlockSpec((3, B_HEADS, TK, TQ), lambda b, i: (0, 0, 0, 0), once),
        ],
        out_specs=[pl.BlockSpec((1, TQ, a_width), lambda b, i: (b, i, 0)),
                   pl.BlockSpec((1, TQ, b_width), lambda b, i: (b, i, 0))],
        out_shape=[jax.ShapeDtypeStruct((bsz, s, a_width), BF16),
                   jax.ShapeDtypeStruct((bsz, s, b_width), BF16)],
        scratch_shapes=[
            pltpu.VMEM((A_HEADS, nkb, A_V_DIM + ONES_ROWS, TK), BF16),
            pltpu.VMEM((nkb, B_HEAD_DIM + ONES_ROWS, TK), BF16),
            pltpu.VMEM((B_HEADS, TK, TQ), F32),
            pltpu.VMEM((2 * A_HEADS, A_V_DIM + ONES_ROWS, TQ), F32),
            pltpu.VMEM((B_HEADS, B_HEAD_DIM + ONES_ROWS, TQ), F32),
            pltpu.VMEM((a_width, TQ), F32),
            pltpu.VMEM((TK, TQ), F32),
            pltpu.VMEM((nkb, TK, TQ), I32),
            pltpu.VMEM((KEY_BITS + _index_bits(s), nkb, SUBLANES, TQ), I32),
            pltpu.VMEM((nkb, TK, TQ), F32),
        ],
        compiler_params=pltpu.CompilerParams(
            dimension_semantics=("arbitrary", "arbitrary"), vmem_limit_bytes=VMEM_LIMIT),
        name="attention",
    )(lam_vecs, proj, proj, avt, a_bias, subln_g_col, proj, proj, proj, proj, bvt, iwt, b_bias)


TAIL_ROWS = 256


def _tail_kernel(ya_ref, yb_ref, wo_ref, x_ref, g_mix_ref, g_in_ref, wu_ref, wd_ref, g_out_ref,
                 o_ref, x1_ref, h_ref, acc_ref):
    f = pl.program_id(1)

    @pl.when(f == 0)
    def _():
        half = ya_ref.shape[-1]
        for r in range(0, x_ref.shape[0], TAIL_ROWS):
            rows = slice(r, r + TAIL_ROWS)
            m = (_dot(ya_ref[rows, :], wo_ref[0:half, :])
                 + _dot(yb_ref[rows, :], wo_ref[half:2 * half, :]))
            x1 = x_ref[rows, :] + _rms(m, g_mix_ref[...])
            x1_ref[rows, :] = x1
            h_ref[rows, :] = _rms(x1, g_in_ref[...]).astype(BF16)
        acc_ref[...] = jnp.zeros_like(acc_ref)

    u = jnp.maximum(_dot(h_ref[...], wu_ref[...]), 0.0)
    acc_ref[...] += _dot((u * u).astype(BF16), wd_ref[...])

    @pl.when(f == pl.num_programs(1) - 1)
    def _():
        o_ref[...] = x1_ref[...] + _rms(acc_ref[...], g_out_ref[...])


def _layer_tail(ya, yb, wo, x, g_mix, g_in, wu, wd, layer, g_out, *, tm=1024, tf=1024):
    n, d = x.shape
    half = ya.shape[-1]
    ff = wu.shape[-1]
    return pl.pallas_call(
        _tail_kernel,
        grid=(n // tm, ff // tf),
        in_specs=[
            pl.BlockSpec((tm, half), lambda i, f: (i, 0)),
            pl.BlockSpec((tm, half), lambda i, f: (i, 0)),
            pl.BlockSpec((2 * half, d), lambda i, f: (0, 0)),
            pl.BlockSpec((tm, d), lambda i, f: (i, 0)),
            pl.BlockSpec((1, d), lambda i, f: (0, 0)),
            pl.BlockSpec((1, d), lambda i, f: (0, 0)),
            pl.BlockSpec((None, d, tf), lambda i, f: (layer, 0, f)),
            pl.BlockSpec((None, tf, d), lambda i, f: (layer, f, 0)),
            pl.BlockSpec((1, d), lambda i, f: (0, 0)),
        ],
        out_specs=pl.BlockSpec((tm, d), lambda i, f: (i, 0)),
        out_shape=jax.ShapeDtypeStruct((n, d), F32),
        scratch_shapes=[pltpu.VMEM((tm, d), F32), pltpu.VMEM((tm, d), BF16),
                        pltpu.VMEM((tm, d), F32)],
        compiler_params=pltpu.CompilerParams(
            dimension_semantics=("arbitrary", "arbitrary"), vmem_limit_bytes=VMEM_LIMIT),
        name="layer_tail",
    )(ya, yb, wo, x, g_mix, g_in, wu, wd, g_out)


CONV_ROWS = 64
U_HIST = 32
Z_HIST = 8


def _conv_kernel(x_ref, g_in_ref, w_in_ref, dw_w_ref, dw_b_ref, ln_g_ref, ln_b_ref, sc_w_ref,
                 u_ref, z_ref, ubuf_ref, zbuf_ref, shift_ref, conv_ref, *, ts):
    @pl.when(pl.program_id(1) == 0)
    def _():
        ubuf_ref[0:U_HIST, :] = jnp.zeros((U_HIST, CONV_CH), F32)
        zbuf_ref[0:Z_HIST, :] = jnp.zeros((Z_HIST, SC_CH), F32)

    x = x_ref[0]
    hb = _rms(x, g_in_ref[...]).astype(BF16)
    c = CONV_CH
    ca = _dot(hb, w_in_ref[:, 0:c])
    cg = _dot(hb, w_in_ref[:, c:2 * c])
    ubuf_ref[U_HIST:U_HIST + ts, :] = ca * jax.nn.sigmoid(cg)
    dc = _dot(hb, w_in_ref[:, 3 * c:4 * c])
    dh = _dot(hb, w_in_ref[:, 4 * c:5 * c])
    zbuf_ref[Z_HIST:Z_HIST + ts, :] = dc * dh

    first_off = U_HIST - (CONV_WIDTH - 1)
    shift_rows = shift_ref.shape[1]
    for r in range(1, SUBLANES):
        shift_ref[r - 1] = ubuf_ref[r:r + shift_rows, :]

    for base in range(0, ts, CONV_ROWS):
        acc = jnp.zeros((CONV_ROWS, c), F32)
        for j in range(CONV_WIDTH):
            r, a = (first_off + j) % SUBLANES, (first_off + j) // SUBLANES
            rows = pl.ds(base + SUBLANES * a, CONV_ROWS)
            taps = ubuf_ref[rows, :] if r == 0 else shift_ref[r - 1, rows, :]
            acc = acc + dw_w_ref[j:j + 1, :] * taps
        conv_ref[base:base + CONV_ROWS, :] = acc
    u = conv_ref[...] + dw_b_ref[...]
    mu = jnp.mean(u, axis=-1, keepdims=True)
    ctr = u - mu
    var = jnp.mean(ctr * ctr, axis=-1, keepdims=True)
    u = ctr * lax.rsqrt(var + EPS) * ln_g_ref[...] + ln_b_ref[...]
    u = u * jax.nn.sigmoid(u)

    z = jnp.zeros((ts, c), F32)
    for j in range(SC_WIDTH):
        off = Z_HIST - (SC_WIDTH - 1) + j
        z = z + sc_w_ref[j:j + 1, :] * zbuf_ref[off:off + ts, :]
    z = _dot(hb, w_in_ref[:, 2 * c:3 * c]) * z

    ubuf_ref[0:U_HIST, :] = ubuf_ref[ts:ts + U_HIST, :]
    zbuf_ref[0:Z_HIST, :] = zbuf_ref[ts:ts + Z_HIST, :]

    u_ref[0] = u.astype(BF16)
    z_ref[0] = z.astype(BF16)


def _conv_mixer(x, g_in, w_in, dw_w, dw_b, ln_g, ln_b, sc_w, *, ts=512):
    bsz, s, d = x.shape
    full = lambda a: pl.BlockSpec(a.shape, lambda b, i: (0,) * a.ndim)
    return pl.pallas_call(
        functools.partial(_conv_kernel, ts=ts),
        grid=(bsz, s // ts),
        in_specs=[pl.BlockSpec((1, ts, d), lambda b, i: (b, i, 0)),
                  full(g_in), full(w_in), full(dw_w), full(dw_b), full(ln_g), full(ln_b),
                  full(sc_w)],
        out_specs=[pl.BlockSpec((1, ts, CONV_CH), lambda b, i: (b, i, 0)),
                   pl.BlockSpec((1, ts, SC_CH), lambda b, i: (b, i, 0))],
        out_shape=[jax.ShapeDtypeStruct((bsz, s, CONV_CH), BF16),
                   jax.ShapeDtypeStruct((bsz, s, SC_CH), BF16)],
        scratch_shapes=[pltpu.VMEM((U_HIST + ts, CONV_CH), F32),
                        pltpu.VMEM((Z_HIST + ts, SC_CH), F32),
                        pltpu.VMEM((SUBLANES - 1, ts + U_HIST - SUBLANES, CONV_CH), F32),
                        pltpu.VMEM((ts, CONV_CH), F32)],
        compiler_params=pltpu.CompilerParams(
            dimension_semantics=("arbitrary", "arbitrary"), vmem_limit_bytes=VMEM_LIMIT),
        name="conv_mixer",
    )(x, g_in, w_in, dw_w, dw_b, ln_g, ln_b, sc_w)


def _t5_bucket(rel):
    nb = NUM_BUCKETS // 2
    ret = jnp.where(rel > 0, nb, 0)
    n = jnp.abs(rel)
    max_exact = nb // 2
    nf = jnp.maximum(n, 1).astype(jnp.float32)
    large = max_exact + (jnp.log(nf / max_exact) / math.log(MAX_DISTANCE / max_exact)
                         * (nb - max_exact)).astype(jnp.int32)
    large = jnp.minimum(large, nb - 1)
    return ret + jnp.where(n < max_exact, n, large)


REL_LO = -(TQ + TK - 1)
REL_LEN = TK - REL_LO
REL_LANES = -(-REL_LEN // LANES) * LANES


def _bias_kernel(fr_ref, a_ref, b_ref):
    krow = lax.broadcasted_iota(I32, (TK, TQ), 0)
    qcol = lax.broadcasted_iota(I32, (TK, TQ), 1)
    admissible = (krow // CHUNK) <= (qcol // CHUNK)
    zeros = jnp.zeros((TK, TQ), F32)
    for h in range(A_HEADS + B_HEADS):
        table = jnp.broadcast_to(fr_ref[h], (TK, REL_LANES))
        blocks = []
        for shift in (TK, 0):
            s0 = (-shift - REL_LO - (REL_LEN - 1)) % REL_LANES
            blocks.append(pltpu.roll(table, s0, 1, stride=1, stride_axis=0)[:, 0:TQ])
        prev, diag = blocks
        if h < A_HEADS:
            a_ref[0, h], a_ref[1, h], a_ref[2, h] = zeros, prev, jnp.where(admissible, diag, NEG)
        else:
            g = h - A_HEADS
            b_ref[0, g], b_ref[1, g], b_ref[2, g] = zeros, prev, diag


def _bias_tables(rel_bias):
    rel = jnp.arange(REL_LO, TK, dtype=jnp.int32)
    far_bucket = NUM_BUCKETS // 2 - 1
    f = (rel_bias[_t5_bucket(rel)].astype(F32) - rel_bias[far_bucket].astype(F32)[None, :]).T
    fr = jnp.pad(f[:, ::-1] * LOG2E, ((0, 0), (0, REL_LANES - REL_LEN)))[:, None, :]
    heads = A_HEADS + B_HEADS
    return pl.pallas_call(
        _bias_kernel,
        grid=(1,),
        in_specs=[pl.BlockSpec((heads, 1, REL_LANES), lambda i: (0, 0, 0))],
        out_specs=[pl.BlockSpec((3, A_HEADS, TK, TQ), lambda i: (0, 0, 0, 0)),
                   pl.BlockSpec((3, B_HEADS, TK, TQ), lambda i: (0, 0, 0, 0))],
        out_shape=[jax.ShapeDtypeStruct((3, A_HEADS, TK, TQ), F32),
                   jax.ShapeDtypeStruct((3, B_HEADS, TK, TQ), F32)],
        compiler_params=pltpu.CompilerParams(
            dimension_semantics=("arbitrary",), vmem_limit_bytes=VMEM_LIMIT),
        name="bias_tables",
    )(fr)


def _attn_weights(w_in):
    o = np.cumsum([0, 512, 512, 512, 512, 64, 64, 256, 32, 8])
    aq, ak, av, bq, bk, bv, iq, ik, iw = [w_in[:, o[n]:o[n + 1]] for n in range(9)]
    aq = aq * (A_HEAD_DIM ** -0.5 * LOG2E)
    bq = bq * (B_HEAD_DIM ** -0.5 * LOG2E)
    w = jnp.concatenate([aq, ak, bq, iq, bk, bk, ik, ik, ik, ik], axis=1).astype(BF16)
    pad = jnp.zeros((w_in.shape[0], VT_ROWS - ROW_IW - IDX_HEADS), w_in.dtype)
    wt = jnp.concatenate([av, bv, iw, pad], axis=1).T.astype(BF16)
    return w, wt


def kernel(x, rel_bias, norm_g, w_mlp_up, w_mlp_down, attn_w_in, attn_w_out, diff_lambda,
           diff_subln_g, conv_w_in, conv_w_out, conv_dw_w, conv_dw_b, conv_ln_g, conv_ln_b,
           sconv_w):
    bsz, s, d = x.shape
    depth = norm_g.shape[0]
    top_k = min(TOPK_MAX, s // 4)
    row = lambda v: v.reshape(1, -1)
    a_bias, b_bias = _bias_tables(rel_bias)
    w_up, w_down = w_mlp_up.astype(BF16), w_mlp_down.astype(BF16)
    for i in range(depth):
        j = i // 2
        if i % 2 == 0:
            lambda_init = 0.8 - 0.6 * math.exp(-0.3 * i)
            w, wt = _attn_weights(attn_w_in[j])
            proj, avt, bvt, iwt = _attn_in(x.reshape(bsz, s, d), row(norm_g[i, 0]), w, wt)
            ya, yb = _attention(proj, avt, bvt, iwt, diff_lambda[j], a_bias, b_bias,
                                diff_subln_g[j].reshape(-1, 1), lambda_init=lambda_init,
                                top_k=top_k)
            w_out = attn_w_out[j]
        else:
            ya, yb = _conv_mixer(x.reshape(bsz, s, d), row(norm_g[i, 0]),
                                 conv_w_in[j].astype(BF16), conv_dw_w[j], row(conv_dw_b[j]),
                                 row(conv_ln_g[j]), row(conv_ln_b[j]), sconv_w[j])
            w_out = conv_w_out[j]
        x = _layer_tail(ya.reshape(bsz * s, -1), yb.reshape(bsz * s, -1), w_out.astype(BF16),
                        x.reshape(bsz * s, d), row(norm_g[i, 1]), row(norm_g[i, 2]),
                        w_up, w_down, i, row(norm_g[i, 3]))
    return x.reshape(bsz, s, d)
```

```python
import functools
import math

import numpy as np
import jax
import jax.numpy as jnp
from jax import lax
from jax.experimental import pallas as pl
from jax.experimental.pallas import tpu as pltpu

F32 = jnp.float32
BF16 = jnp.bfloat16
I32 = jnp.int32

D_MODEL = 1024
CHUNK = 64
NUM_BUCKETS = 32
MAX_DISTANCE = 128
EPS = 1e-6
NEG = -1e30
LOG2E = math.log2(math.e)
A_HEADS = 4
A_HEAD_DIM = 64
A_V_DIM = 2 * A_HEAD_DIM
B_HEADS = 8
B_HEAD_DIM = 64
IDX_HEADS = 8
IDX_DIM = 32
TOPK_MAX = 256
CONV_CH = 512
CONV_WIDTH = 31
SC_CH = 512
SC_WIDTH = 3
D_FF = 4 * D_MODEL

LANES = 128
SUBLANES = 8
VMEM_LIMIT = 56 * 1024 * 1024

TQ = 256
TK = 256
COL_AQ, COL_AK, COL_BQ, COL_IQ, COL_KB, COL_IK, PROJ_W = 0, 512, 1024, 1536, 1792, 1920, 2048
ROW_AV, ROW_BV, ROW_IW, VT_ROWS = 0, 512, 576, 640
ONES_ROWS = 16
INT_MIN = -2 ** 31
KEY_NEG_INF = INT_MIN + 0x7FFFFF


def _rms(x, g):
    return x * lax.rsqrt(jnp.mean(x * x, axis=-1, keepdims=True) + EPS) * g


def _dot_nt(a, b):
    return lax.dot_general(a, b, (((1,), (1,)), ((), ())), preferred_element_type=F32)


def _dot(a, b):
    return jnp.dot(a, b, preferred_element_type=F32)


def _lane_band(x, lo, width):
    lane = lax.broadcasted_iota(I32, x.shape, 1)
    return jnp.where((lane >= lo) & (lane < lo + width), x, jnp.zeros_like(x))


KEY_BITS = 32


def _bit_planes(keys):
    assert keys.shape[0] == KEY_BITS * SUBLANES
    u = keys ^ INT_MIN
    a = [u[SUBLANES * r:SUBLANES * (r + 1), :] for r in range(KEY_BITS)]
    j, m = KEY_BITS // 2, (1 << (KEY_BITS // 2)) - 1
    while j:
        mask = np.int32(np.uint32(m))
        k = 0
        while k < KEY_BITS:
            t = (a[k] ^ lax.shift_right_logical(a[k + j], jnp.int32(j))) & mask
            a[k] = a[k] ^ t
            a[k + j] = a[k + j] ^ lax.shift_left(t, jnp.int32(j))
            k = (k + j + 1) & ~j
        j >>= 1
        m = (m ^ (m << j)) & 0xFFFFFFFF
    return a


def _index_bits(seq):
    bits = seq.bit_length() - 1
    assert seq == 1 << bits and seq >= KEY_BITS * SUBLANES
    return bits


def _index_planes(nkb):
    shape = (nkb, SUBLANES, TQ)
    blk = lax.broadcasted_iota(I32, shape, 0)
    sub = lax.broadcasted_iota(I32, shape, 1)
    sub_bits = SUBLANES.bit_length() - 1
    word_bits = KEY_BITS.bit_length() - 1
    planes = []
    for b in range(_index_bits(nkb * TK) - 1, -1, -1):
        if b >= sub_bits + word_bits:
            on = ((nkb - 1 - blk) >> (b - sub_bits - word_bits)) & 1
            planes.append(jnp.where(on == 1, -1, 0))
        elif b >= sub_bits:
            t = b - sub_bits
            pattern = sum(1 << k for k in range(KEY_BITS) if (k >> t) & 1)
            planes.append(jnp.full(shape, np.int32(np.uint32(pattern)), I32))
        else:
            on = ((SUBLANES - 1 - sub) >> b) & 1
            planes.append(jnp.where(on == 1, -1, 0))
    return planes


def _bias_kind(j, i):
    return jnp.clip(j - i + 2, 0, 2)


def _stream_key_blocks(i, n_chains, s_ref, acc_ref, logits, v_aug, beside=None):
    chains = range(n_chains)

    def stage(j, c, kind):
        s = logits(j, c, kind)
        s_ref[c] = s
        return jnp.max(s, axis=0, keepdims=True)

    def absorb(j, c, m_old, block_max):
        m_new = jnp.maximum(m_old, block_max)
        alpha = jnp.exp2(m_old - m_new)
        p = jnp.exp2(s_ref[c] - m_new).astype(BF16)
        acc_ref[c] = acc_ref[c] * alpha + _dot(v_aug(j, c), p)
        return m_new

    def step(j, carry, kind):
        ms, bms = carry
        out = [(absorb(j, c, ms[c], bms[c]), stage(j + 1, c, kind)) for c in chains]
        if beside is not None:
            beside(j)
        return tuple(o[0] for o in out), tuple(o[1] for o in out)

    acc_ref[...] = jnp.zeros_like(acc_ref)
    carry = ((jnp.full((1, TQ), NEG, F32),) * n_chains, tuple(stage(0, c, 'any') for c in chains))
    n_far = jnp.maximum(i - 2, 0)
    carry = lax.fori_loop(0, n_far, lambda j, cr: step(j, cr, 'far'), carry)
    ms, bms = lax.fori_loop(n_far, i, lambda j, cr: step(j, cr, 'near'), carry)
    for c in chains:
        absorb(i, c, ms[c], bms[c])


def _attn_in_kernel(x_ref, g_ref, w_ref, wt_ref, proj_ref, avt_ref, bvt_ref, iwt_ref, *, tm):
    hb = _rms(x_ref[0], g_ref[...]).astype(BF16)
    proj_ref[0] = _dot(hb, w_ref[...]).astype(BF16)
    t = _dot_nt(wt_ref[...], hb)
    for c in range(tm // TK):
        avt_ref[0, c] = t[ROW_AV:ROW_BV, c * TK:(c + 1) * TK].astype(BF16)
        bvt_ref[0, c] = t[ROW_BV:ROW_IW, c * TK:(c + 1) * TK].astype(BF16)
    iwt_ref[0] = t[ROW_IW:ROW_IW + IDX_HEADS, :] * (IDX_HEADS ** -0.5 * IDX_DIM ** -0.5)


def _attn_in(x, g, w, wt, *, tm=512):
    bsz, s, d = x.shape
    av_rows = ROW_BV - ROW_AV
    return pl.pallas_call(
        functools.partial(_attn_in_kernel, tm=tm),
        grid=(bsz, s // tm),
        in_specs=[
            pl.BlockSpec((1, tm, d), lambda b, i: (b, i, 0)),
            pl.BlockSpec((1, d), lambda b, i: (0, 0)),
            pl.BlockSpec((d, PROJ_W), lambda b, i: (0, 0)),
            pl.BlockSpec((VT_ROWS, d), lambda b, i: (0, 0)),
        ],
        out_specs=[
            pl.BlockSpec((1, tm, PROJ_W), lambda b, i: (b, i, 0)),
            pl.BlockSpec((1, tm // TK, av_rows, TK), lambda b, i: (b, i, 0, 0)),
            pl.BlockSpec((1, tm // TK, B_HEAD_DIM, TK), lambda b, i: (b, i, 0, 0)),
            pl.BlockSpec((1, IDX_HEADS, tm), lambda b, i: (b, 0, i)),
        ],
        out_shape=[
            jax.ShapeDtypeStruct((bsz, s, PROJ_W), BF16),
            jax.ShapeDtypeStruct((bsz, s // TK, av_rows, TK), BF16),
            jax.ShapeDtypeStruct((bsz, s // TK, B_HEAD_DIM, TK), BF16),
            jax.ShapeDtypeStruct((bsz, IDX_HEADS, s), F32),
        ],
        compiler_params=pltpu.CompilerParams(
            dimension_semantics=("arbitrary", "arbitrary"), vmem_limit_bytes=VMEM_LIMIT),
        name="attn_in",
    )(x, g, w, wt)


def _key_rows(ref, j):
    return ref[0, pl.ds(pl.multiple_of(j * TK, TK), TK), :]


def _indexer(i, iq_ref, ik_ref, iwt_ref, score_ref, key_ref, plane_ref):
    iq = iq_ref[0]
    per_blk = LANES // IDX_DIM
    iqh = [_lane_band(iq[:, (h // per_blk) * LANES:(h // per_blk + 1) * LANES],
                      (h % per_blk) * IDX_DIM, IDX_DIM) for h in range(IDX_HEADS)]
    w = iwt_ref[0]

    def stage_scores(j):
        ikb = _key_rows(ik_ref, j)
        half = TK // 2
        for r in (0, half):
            acc = jnp.zeros((half, TQ), F32)
            for h in range(IDX_HEADS):
                acc = acc + jnp.maximum(_dot_nt(ikb[r:r + half], iqh[h]), 0.0) * w[h:h + 1, :]
            score_ref[r:r + half, :] = acc

    def staged_keys():
        acc = score_ref[...]
        acc = jnp.where(acc == 0.0, 0.0, acc)
        bits = lax.bitcast_convert_type(acc, I32)
        return bits ^ ((bits >> 31) & 0x7FFFFFFF)

    def store_keys(j, keys):
        key_ref[j] = keys
        for p, plane in enumerate(_bit_planes(keys)):
            plane_ref[p, j] = plane

    def per_block(j):
        keys = staged_keys()
        stage_scores(j + 1)
        store_keys(j, keys)

    def finish():
        krow = lax.broadcasted_iota(I32, (TK, TQ), 0)
        qcol = lax.broadcasted_iota(I32, (TK, TQ), 1)
        admissible = (krow // CHUNK) <= (qcol // CHUNK)
        store_keys(i, jnp.where(admissible, staged_keys(), KEY_NEG_INF))

    stage_scores(0)
    return per_block, finish


RADIX_CLASSES = 4


def _select_mask(i, key_ref, plane_ref, mask_ref, thr_ref, *, top_k, seq):
    nkb = seq // TK
    idx_bits = _index_bits(seq)

    def radix_select(n_blk):
        blk = lax.broadcasted_iota(I32, (n_blk, SUBLANES, TQ), 0)
        alive0 = jnp.where(blk <= i, -1, 0)

        def bit_body(p, carry, n_bits):
            alive, n_gt, thr_u = carry
            hit = alive & plane_ref[p, 0:n_blk]
            pc = jnp.sum(lax.population_count(hit), axis=0)
            n_hit = jnp.sum(pc.astype(F32), axis=0, keepdims=True)
            take = (n_gt + n_hit) >= top_k
            thr_u = thr_u | jnp.where(take, lax.shift_left(jnp.int32(1), n_bits - 1 - p), 0)
            n_gt = n_gt + jnp.where(take, 0.0, n_hit)
            alive = jnp.where(take, hit, alive ^ hit)
            return alive, n_gt, thr_u

        zero = jnp.zeros((1, TQ), I32)
        alive, n_gt, thr_u = lax.fori_loop(
            0, KEY_BITS, lambda p, c: bit_body(p, c, KEY_BITS),
            (alive0, jnp.zeros((1, TQ), F32), zero))
        _, _, rev_idx = lax.fori_loop(
            KEY_BITS, KEY_BITS + idx_bits,
            lambda p, c: bit_body(p, c, KEY_BITS + idx_bits), (alive, n_gt, zero))
        thr_ref[0:1, :] = thr_u
        thr_ref[1:2, :] = rev_idx

    per_class = nkb // RADIX_CLASSES
    for n_blk in range(per_class, nkb + 1, per_class):
        pl.when((i < n_blk) & (i >= n_blk - per_class))(functools.partial(radix_select, n_blk))
    thr_u = thr_ref[0:1, :]
    rev_idx = thr_ref[1:2, :]
    thr = thr_u ^ INT_MIN
    lim = seq - rev_idx
    lim = jnp.where(thr > KEY_NEG_INF, lim, seq)
    thr = jnp.maximum(thr, KEY_NEG_INF + 1)
    krow = lax.broadcasted_iota(I32, (TK, TQ), 0)

    def mask_body(j, c):
        late = jnp.where(krow >= lim - j * TK, 1, 0)
        mask_ref[j] = jnp.where(key_ref[j] - late >= thr, 0.0, NEG)
        return c

    lax.fori_loop(0, i + 1, mask_body, 0)


def _attn_kernel(lam_ref, aq_ref, ak_ref, avt_ref, abias_ref, g_ref,
                 bq_ref, iq_ref, kb_ref, ik_ref, bvt_ref, iwt_ref, bbias_ref,
                 ya_ref, yb_ref,
                 avaug_ref, bvaug_ref, s_ref, aacc_ref, bacc_ref, ot_ref,
                 score_ref, key_ref, plane_ref, mask_ref, thr_ref, *, lambda_init, top_k, seq):
    i = pl.program_id(1)
    nkb = seq // TK
    n_maps = 2 * A_HEADS

    @pl.when(i == 0)
    def _():
        for c in range(nkb):
            for h in range(A_HEADS):
                avaug_ref[h, c, 0:A_V_DIM, :] = avt_ref[0, c, h * A_V_DIM:(h + 1) * A_V_DIM, :]
                avaug_ref[h, c, A_V_DIM:A_V_DIM + ONES_ROWS, :] = jnp.ones((ONES_ROWS, TK), BF16)
            bvaug_ref[c, 0:B_HEAD_DIM, :] = bvt_ref[0, c]
            bvaug_ref[c, B_HEAD_DIM:B_HEAD_DIM + ONES_ROWS, :] = jnp.ones((ONES_ROWS, TK), BF16)

    @pl.when((i == 0) & (pl.program_id(0) == 0))
    def _():
        plane_ref[0:KEY_BITS] = jnp.zeros((KEY_BITS, nkb, SUBLANES, TQ), I32)
        for t, plane in enumerate(_index_planes(nkb)):
            plane_ref[KEY_BITS + t] = plane

    lv = lam_ref[...]
    lam = (jnp.exp(jnp.sum(lv[0:1] * lv[1:2], axis=-1, keepdims=True))
           - jnp.exp(jnp.sum(lv[2:3] * lv[3:4], axis=-1, keepdims=True)) + lambda_init)

    q = aq_ref[0]
    qm = []
    for h in range(A_HEADS):
        qh = q[:, h * LANES:(h + 1) * LANES]
        qm += [_lane_band(qh, 0, A_HEAD_DIM), _lane_band(qh, A_HEAD_DIM, A_HEAD_DIM)]

    def a_logits(j, c, kind):
        h = c // 2
        kb = ak_ref[0, pl.ds(pl.multiple_of(j * TK, TK), TK), h * LANES:(h + 1) * LANES]
        s = _dot_nt(kb, qm[c])
        return s if kind == 'far' else s + abias_ref[_bias_kind(j, i), h]

    index_block, finish_index = _indexer(i, iq_ref, ik_ref, iwt_ref, score_ref, key_ref, plane_ref)
    _stream_key_blocks(i, n_maps, s_ref, aacc_ref, a_logits, lambda j, c: avaug_ref[c // 2, j],
                       beside=index_block)
    finish_index()

    for h in range(A_HEADS):
        a1 = aacc_ref[2 * h]
        a2 = aacc_ref[2 * h + 1]
        a = (a1[0:A_V_DIM] / a1[A_V_DIM:A_V_DIM + 1]
             - lam * (a2[0:A_V_DIM] / a2[A_V_DIM:A_V_DIM + 1]))
        y = a * lax.rsqrt(jnp.mean(a * a, axis=0, keepdims=True) + EPS) * g_ref[...]
        ot_ref[h * A_V_DIM:(h + 1) * A_V_DIM, :] = y * (1.0 - lambda_init)
    ya_ref[0] = ot_ref[...].T.astype(BF16)

    _select_mask(i, key_ref, plane_ref, mask_ref, thr_ref, top_k=top_k, seq=seq)

    qb = bq_ref[0]
    qh = [_lane_band(qb[:, (h // 2) * LANES:(h // 2 + 1) * LANES],
                     (h % 2) * B_HEAD_DIM, B_HEAD_DIM) for h in range(B_HEADS)]

    def b_logits(j, h, kind):
        s = _dot_nt(_key_rows(kb_ref, j), qh[h]) + mask_ref[j]
        return s if kind == 'far' else s + bbias_ref[_bias_kind(j, i), h]

    _stream_key_blocks(i, B_HEADS, s_ref, bacc_ref, b_logits, lambda j, h: bvaug_ref[j])

    for h in range(B_HEADS):
        a = bacc_ref[h]
        ot_ref[h * B_HEAD_DIM:(h + 1) * B_HEAD_DIM, :] = (
            a[0:B_HEAD_DIM] / a[B_HEAD_DIM:B_HEAD_DIM + 1])
    yb_ref[0] = ot_ref[...].T.astype(BF16)


def _attention(proj, avt, bvt, iwt, lam_vecs, a_bias, b_bias, subln_g_col, *, lambda_init, top_k):
    bsz, s, _ = proj.shape
    nq = s // TQ
    nkb = s // TK
    a_width = A_HEADS * A_V_DIM
    b_width = B_HEADS * B_HEAD_DIM
    assert a_width == b_width and 2 * A_HEADS == B_HEADS
    iq_w = IDX_HEADS * IDX_DIM
    once = pl.Buffered(1)
    return pl.pallas_call(
        functools.partial(_attn_kernel, lambda_init=lambda_init, top_k=top_k, seq=s),
        grid=(bsz, nq),
        in_specs=[
            pl.BlockSpec((4, A_HEAD_DIM), lambda b, i: (0, 0)),
            pl.BlockSpec((1, TQ, a_width), lambda b, i: (b, i, COL_AQ // a_width)),
            pl.BlockSpec((1, s, a_width), lambda b, i: (b, 0, COL_AK // a_width), once),
            pl.BlockSpec((1, nkb, a_width, TK), lambda b, i: (b, 0, 0, 0), once),
            pl.BlockSpec((3, A_HEADS, TK, TQ), lambda b, i: (0, 0, 0, 0), once),
            pl.BlockSpec((A_V_DIM, 1), lambda b, i: (0, 0)),
            pl.BlockSpec((1, TQ, b_width), lambda b, i: (b, i, COL_BQ // b_width)),
            pl.BlockSpec((1, TQ, iq_w), lambda b, i: (b, i, COL_IQ // iq_w)),
            pl.BlockSpec((1, s, LANES), lambda b, i: (b, 0, COL_KB // LANES), once),
            pl.BlockSpec((1, s, LANES), lambda b, i: (b, 0, COL_IK // LANES), once),
            pl.BlockSpec((1, nkb, B_HEAD_DIM, TK), lambda b, i: (b, 0, 0, 0), once),
            pl.BlockSpec((1, IDX_HEADS, TQ), lambda b, i: (b, 0, i)),
            pl.BlockSpec((3, B_HEADS, TK, TQ), lambda b, i: (0, 0, 0, 0), once),
        ],
        out_specs=[pl.BlockSpec((1, TQ, a_width), lambda b, i: (b, i, 0)),
                   pl.BlockSpec((1, TQ, b_width), lambda b, i: (b, i, 0))],
        out_shape=[jax.ShapeDtypeStruct((bsz, s, a_width), BF16),
                   jax.ShapeDtypeStruct((bsz, s, b_width), BF16)],
        scratch_shapes=[
            pltpu.VMEM((A_HEADS, nkb, A_V_DIM + ONES_ROWS, TK), BF16),
            pltpu.VMEM((nkb, B_HEAD_DIM + ONES_ROWS, TK), BF16),
            pltpu.VMEM((B_HEADS, TK, TQ), F32),
            pltpu.VMEM((2 * A_HEADS, A_V_DIM + ONES_ROWS, TQ), F32),
            pltpu.VMEM((B_HEADS, B_HEAD_DIM + ONES_ROWS, TQ), F32),
            pltpu.VMEM((a_width, TQ), F32),
            pltpu.VMEM((TK, TQ), F32),
            pltpu.VMEM((nkb, TK, TQ), I32),
            pltpu.VMEM((KEY_BITS + _index_bits(s), nkb, SUBLANES, TQ), I32),
            pltpu.VMEM((nkb, TK, TQ), F32),
            pltpu.VMEM((SUBLANES, TQ), I32),
        ],
        compiler_params=pltpu.CompilerParams(
            dimension_semantics=("arbitrary", "arbitrary"), vmem_limit_bytes=VMEM_LIMIT),
        name="attention",
    )(lam_vecs, proj, proj, avt, a_bias, subln_g_col, proj, proj, proj, proj, bvt, iwt, b_bias)


TAIL_ROWS = 256


def _tail_kernel(ya_ref, yb_ref, wo_ref, x_ref, g_mix_ref, g_in_ref, wu_ref, wd_ref, g_out_ref,
                 o_ref, x1_ref, h_ref, acc_ref):
    f = pl.program_id(1)

    @pl.when(f == 0)
    def _():
        half = ya_ref.shape[-1]
        for r in range(0, x_ref.shape[0], TAIL_ROWS):
            rows = slice(r, r + TAIL_ROWS)
            m = (_dot(ya_ref[rows, :], wo_ref[0:half, :])
                 + _dot(yb_ref[rows, :], wo_ref[half:2 * half, :]))
            x1 = x_ref[rows, :] + _rms(m, g_mix_ref[...])
            x1_ref[rows, :] = x1
            h_ref[rows, :] = _rms(x1, g_in_ref[...]).astype(BF16)
        acc_ref[...] = jnp.zeros_like(acc_ref)

    u = jnp.maximum(_dot(h_ref[...], wu_ref[...]), 0.0)
    acc_ref[...] += _dot((u * u).astype(BF16), wd_ref[...])

    @pl.when(f == pl.num_programs(1) - 1)
    def _():
        o_ref[...] = x1_ref[...] + _rms(acc_ref[...], g_out_ref[...])


def _layer_tail(ya, yb, wo, x, g_mix, g_in, wu, wd, layer, g_out, *, tm=1024, tf=1024):
    n, d = x.shape
    half = ya.shape[-1]
    ff = wu.shape[-1]
    return pl.pallas_call(
        _tail_kernel,
        grid=(n // tm, ff // tf),
        in_specs=[
            pl.BlockSpec((tm, half), lambda i, f: (i, 0)),
            pl.BlockSpec((tm, half), lambda i, f: (i, 0)),
            pl.BlockSpec((2 * half, d), lambda i, f: (0, 0)),
            pl.BlockSpec((tm, d), lambda i, f: (i, 0)),
            pl.BlockSpec((1, d), lambda i, f: (0, 0)),
            pl.BlockSpec((1, d), lambda i, f: (0, 0)),
            pl.BlockSpec((None, d, tf), lambda i, f: (layer, 0, f)),
            pl.BlockSpec((None, tf, d), lambda i, f: (layer, f, 0)),
            pl.BlockSpec((1, d), lambda i, f: (0, 0)),
        ],
        out_specs=pl.BlockSpec((tm, d), lambda i, f: (i, 0)),
        out_shape=jax.ShapeDtypeStruct((n, d), F32),
        scratch_shapes=[pltpu.VMEM((tm, d), F32), pltpu.VMEM((tm, d), BF16),
                        pltpu.VMEM((tm, d), F32)],
        compiler_params=pltpu.CompilerParams(
            dimension_semantics=("arbitrary", "arbitrary"), vmem_limit_bytes=VMEM_LIMIT),
        name="layer_tail",
    )(ya, yb, wo, x, g_mix, g_in, wu, wd, g_out)


CONV_ROWS = 64
U_HIST = 32
Z_HIST = 8


def _conv_kernel(x_ref, g_in_ref, w_in_ref, dw_w_ref, dw_b_ref, ln_g_ref, ln_b_ref, sc_w_ref,
                 u_ref, z_ref, ubuf_ref, zbuf_ref, shift_ref, conv_ref, *, ts):
    @pl.when(pl.program_id(1) == 0)
    def _():
        ubuf_ref[0:U_HIST, :] = jnp.zeros((U_HIST, CONV_CH), F32)
        zbuf_ref[0:Z_HIST, :] = jnp.zeros((Z_HIST, SC_CH), F32)

    x = x_ref[0]
    hb = _rms(x, g_in_ref[...]).astype(BF16)
    c = CONV_CH
    ca = _dot(hb, w_in_ref[:, 0:c])
    cg = _dot(hb, w_in_ref[:, c:2 * c])
    ubuf_ref[U_HIST:U_HIST + ts, :] = ca * jax.nn.sigmoid(cg)
    dc = _dot(hb, w_in_ref[:, 3 * c:4 * c])
    dh = _dot(hb, w_in_ref[:, 4 * c:5 * c])
    zbuf_ref[Z_HIST:Z_HIST + ts, :] = dc * dh

    first_off = U_HIST - (CONV_WIDTH - 1)
    shift_rows = shift_ref.shape[1]
    for r in range(1, SUBLANES):
        shift_ref[r - 1] = ubuf_ref[r:r + shift_rows, :]

    for base in range(0, ts, CONV_ROWS):
        acc = jnp.zeros((CONV_ROWS, c), F32)
        for j in range(CONV_WIDTH):
            r, a = (first_off + j) % SUBLANES, (first_off + j) // SUBLANES
            rows = pl.ds(base + SUBLANES * a, CONV_ROWS)
            taps = ubuf_ref[rows, :] if r == 0 else shift_ref[r - 1, rows, :]
            acc = acc + dw_w_ref[j:j + 1, :] * taps
        conv_ref[base:base + CONV_ROWS, :] = acc
    u = conv_ref[...] + dw_b_ref[...]
    mu = jnp.mean(u, axis=-1, keepdims=True)
    ctr = u - mu
    var = jnp.mean(ctr * ctr, axis=-1, keepdims=True)
    u = ctr * lax.rsqrt(var + EPS) * ln_g_ref[...] + ln_b_ref[...]
    u = u * jax.nn.sigmoid(u)

    z = jnp.zeros((ts, c), F32)
    for j in range(SC_WIDTH):
        off = Z_HIST - (SC_WIDTH - 1) + j
        z = z + sc_w_ref[j:j + 1, :] * zbuf_ref[off:off + ts, :]
    z = _dot(hb, w_in_ref[:, 2 * c:3 * c]) * z

    ubuf_ref[0:U_HIST, :] = ubuf_ref[ts:ts + U_HIST, :]
    zbuf_ref[0:Z_HIST, :] = zbuf_ref[ts:ts + Z_HIST, :]

    u_ref[0] = u.astype(BF16)
    z_ref[0] = z.astype(BF16)


def _conv_mixer(x, g_in, w_in, dw_w, dw_b, ln_g, ln_b, sc_w, *, ts=512):
    bsz, s, d = x.shape
    full = lambda a: pl.BlockSpec(a.shape, lambda b, i: (0,) * a.ndim)
    return pl.pallas_call(
        functools.partial(_conv_kernel, ts=ts),
        grid=(bsz, s // ts),
        in_specs=[pl.BlockSpec((1, ts, d), lambda b, i: (b, i, 0)),
                  full(g_in), full(w_in), full(dw_w), full(dw_b), full(ln_g), full(ln_b),
                  full(sc_w)],
        out_specs=[pl.BlockSpec((1, ts, CONV_CH), lambda b, i: (b, i, 0)),
                   pl.BlockSpec((1, ts, SC_CH), lambda b, i: (b, i, 0))],
        out_shape=[jax.ShapeDtypeStruct((bsz, s, CONV_CH), BF16),
                   jax.ShapeDtypeStruct((bsz, s, SC_CH), BF16)],
        scratch_shapes=[pltpu.VMEM((U_HIST + ts, CONV_CH), F32),
                        pltpu.VMEM((Z_HIST + ts, SC_CH), F32),
                        pltpu.VMEM((SUBLANES - 1, ts + U_HIST - SUBLANES, CONV_CH), F32),
                        pltpu.VMEM((ts, CONV_CH), F32)],
        compiler_params=pltpu.CompilerParams(
            dimension_semantics=("arbitrary", "arbitrary"), vmem_limit_bytes=VMEM_LIMIT),
        name="conv_mixer",
    )(x, g_in, w_in, dw_w, dw_b, ln_g, ln_b, sc_w)


def _t5_bucket(rel):
    nb = NUM_BUCKETS // 2
    ret = jnp.where(rel > 0, nb, 0)
    n = jnp.abs(rel)
    max_exact = nb // 2
    nf = jnp.maximum(n, 1).astype(jnp.float32)
    large = max_exact + (jnp.log(nf / max_exact) / math.log(MAX_DISTANCE / max_exact)
                         * (nb - max_exact)).astype(jnp.int32)
    large = jnp.minimum(large, nb - 1)
    return ret + jnp.where(n < max_exact, n, large)


REL_LO = -(TQ + TK - 1)
REL_LEN = TK - REL_LO
REL_LANES = -(-REL_LEN // LANES) * LANES


def _bias_kernel(fr_ref, a_ref, b_ref):
    krow = lax.broadcasted_iota(I32, (TK, TQ), 0)
    qcol = lax.broadcasted_iota(I32, (TK, TQ), 1)
    admissible = (krow // CHUNK) <= (qcol // CHUNK)
    zeros = jnp.zeros((TK, TQ), F32)
    for h in range(A_HEADS + B_HEADS):
        table = jnp.broadcast_to(fr_ref[h], (TK, REL_LANES))
        blocks = []
        for shift in (TK, 0):
            s0 = (-shift - REL_LO - (REL_LEN - 1)) % REL_LANES
            blocks.append(pltpu.roll(table, s0, 1, stride=1, stride_axis=0)[:, 0:TQ])
        prev, diag = blocks
        if h < A_HEADS:
            a_ref[0, h], a_ref[1, h], a_ref[2, h] = zeros, prev, jnp.where(admissible, diag, NEG)
        else:
            g = h - A_HEADS
            b_ref[0, g], b_ref[1, g], b_ref[2, g] = zeros, prev, diag


def _bias_tables(rel_bias):
    rel = jnp.arange(REL_LO, TK, dtype=jnp.int32)
    far_bucket = NUM_BUCKETS // 2 - 1
    f = (rel_bias[_t5_bucket(rel)].astype(F32) - rel_bias[far_bucket].astype(F32)[None, :]).T
    fr = jnp.pad(f[:, ::-1] * LOG2E, ((0, 0), (0, REL_LANES - REL_LEN)))[:, None, :]
    heads = A_HEADS + B_HEADS
    return pl.pallas_call(
        _bias_kernel,
        grid=(1,),
        in_specs=[pl.BlockSpec((heads, 1, REL_LANES), lambda i: (0, 0, 0))],
        out_specs=[pl.BlockSpec((3, A_HEADS, TK, TQ), lambda i: (0, 0, 0, 0)),
                   pl.BlockSpec((3, B_HEADS, TK, TQ), lambda i: (0, 0, 0, 0))],
        out_shape=[jax.ShapeDtypeStruct((3, A_HEADS, TK, TQ), F32),
                   jax.ShapeDtypeStruct((3, B_HEADS, TK, TQ), F32)],
        compiler_params=pltpu.CompilerParams(
            dimension_semantics=("arbitrary",), vmem_limit_bytes=VMEM_LIMIT),
        name="bias_tables",
    )(fr)


def _attn_weights(w_in):
    o = np.cumsum([0, 512, 512, 512, 512, 64, 64, 256, 32, 8])
    aq, ak, av, bq, bk, bv, iq, ik, iw = [w_in[:, o[n]:o[n + 1]] for n in range(9)]
    aq = aq * (A_HEAD_DIM ** -0.5 * LOG2E)
    bq = bq * (B_HEAD_DIM ** -0.5 * LOG2E)
    w = jnp.concatenate([aq, ak, bq, iq, bk, bk, ik, ik, ik, ik], axis=1).astype(BF16)
    pad = jnp.zeros((w_in.shape[0], VT_ROWS - ROW_IW - IDX_HEADS), w_in.dtype)
    wt = jnp.concatenate([av, bv, iw, pad], axis=1).T.astype(BF16)
    return w, wt


def kernel(x, rel_bias, norm_g, w_mlp_up, w_mlp_down, attn_w_in, attn_w_out, diff_lambda,
           diff_subln_g, conv_w_in, conv_w_out, conv_dw_w, conv_dw_b, conv_ln_g, conv_ln_b,
           sconv_w):
    bsz, s, d = x.shape
    depth = norm_g.shape[0]
    top_k = min(TOPK_MAX, s // 4)
    row = lambda v: v.reshape(1, -1)
    a_bias, b_bias = _bias_tables(rel_bias)
    w_up, w_down = w_mlp_up.astype(BF16), w_mlp_down.astype(BF16)
    for i in range(depth):
        j = i // 2
        if i % 2 == 0:
            lambda_init = 0.8 - 0.6 * math.exp(-0.3 * i)
            w, wt = _attn_weights(attn_w_in[j])
            proj, avt, bvt, iwt = _attn_in(x.reshape(bsz, s, d), row(norm_g[i, 0]), w, wt)
            ya, yb = _attention(proj, avt, bvt, iwt, diff_lambda[j], a_bias, b_bias,
                                diff_subln_g[j].reshape(-1, 1), lambda_init=lambda_init,
                                top_k=top_k)
            w_out = attn_w_out[j]
        else:
            ya, yb = _conv_mixer(x.reshape(bsz, s, d), row(norm_g[i, 0]),
                                 conv_w_in[j].astype(BF16), conv_dw_w[j], row(conv_dw_b[j]),
                                 row(conv_ln_g[j]), row(conv_ln_b[j]), sconv_w[j])
            w_out = conv_w_out[j]
        x = _layer_tail(ya.reshape(bsz * s, -1), yb.reshape(bsz * s, -1), w_out.astype(BF16),
                        x.reshape(bsz * s, d), row(norm_g[i, 1]), row(norm_g[i, 2]),
                        w_up, w_down, i, row(norm_g[i, 3]))
    return x.reshape(bsz, s, d)
```

```python
import functools
import math

import numpy as np
import jax
import jax.numpy as jnp
from jax import lax
from jax.experimental import pallas as pl
from jax.experimental.pallas import tpu as pltpu

F32 = jnp.float32
BF16 = jnp.bfloat16
I32 = jnp.int32

D_MODEL = 1024
CHUNK = 64
NUM_BUCKETS = 32
MAX_DISTANCE = 128
EPS = 1e-6
NEG = -1e30
LOG2E = math.log2(math.e)
A_HEADS = 4
A_HEAD_DIM = 64
A_V_DIM = 2 * A_HEAD_DIM
B_HEADS = 8
B_HEAD_DIM = 64
IDX_HEADS = 8
IDX_DIM = 32
TOPK_MAX = 256
CONV_CH = 512
CONV_WIDTH = 31
SC_CH = 512
SC_WIDTH = 3
D_FF = 4 * D_MODEL

LANES = 128
SUBLANES = 8
VMEM_LIMIT = 56 * 1024 * 1024

TQ = 256
TK = 256
COL_AQ, COL_AK, COL_BQ, COL_IQ, COL_KB, COL_IK, PROJ_W = 0, 512, 1024, 1536, 1792, 1920, 2048
ROW_AV, ROW_BV, ROW_IW, VT_ROWS = 0, 512, 576, 640
ONES_ROWS = 16
INT_MIN = -2 ** 31
KEY_NEG_INF = INT_MIN + 0x7FFFFF


def _rms(x, g):
    return x * lax.rsqrt(jnp.mean(x * x, axis=-1, keepdims=True) + EPS) * g


def _dot_nt(a, b):
    return lax.dot_general(a, b, (((1,), (1,)), ((), ())), preferred_element_type=F32)


def _dot(a, b):
    return jnp.dot(a, b, preferred_element_type=F32)


def _lane_band(x, lo, width):
    lane = lax.broadcasted_iota(I32, x.shape, 1)
    return jnp.where((lane >= lo) & (lane < lo + width), x, jnp.zeros_like(x))


KEY_BITS = 32


def _bit_planes(keys):
    assert keys.shape[0] == KEY_BITS * SUBLANES
    u = keys ^ INT_MIN
    a = [u[SUBLANES * r:SUBLANES * (r + 1), :] for r in range(KEY_BITS)]
    j, m = KEY_BITS // 2, (1 << (KEY_BITS // 2)) - 1
    while j:
        mask = np.int32(np.uint32(m))
        k = 0
        while k < KEY_BITS:
            t = (a[k] ^ lax.shift_right_logical(a[k + j], jnp.int32(j))) & mask
            a[k] = a[k] ^ t
            a[k + j] = a[k + j] ^ lax.shift_left(t, jnp.int32(j))
            k = (k + j + 1) & ~j
        j >>= 1
        m = (m ^ (m << j)) & 0xFFFFFFFF
    return a


def _index_bits(seq):
    bits = seq.bit_length() - 1
    assert seq == 1 << bits and seq >= KEY_BITS * SUBLANES
    return bits


def _index_planes(nkb):
    shape = (nkb, SUBLANES, TQ)
    blk = lax.broadcasted_iota(I32, shape, 0)
    sub = lax.broadcasted_iota(I32, shape, 1)
    sub_bits = SUBLANES.bit_length() - 1
    word_bits = KEY_BITS.bit_length() - 1
    planes = []
    for b in range(_index_bits(nkb * TK) - 1, -1, -1):
        if b >= sub_bits + word_bits:
            on = ((nkb - 1 - blk) >> (b - sub_bits - word_bits)) & 1
            planes.append(jnp.where(on == 1, -1, 0))
        elif b >= sub_bits:
            t = b - sub_bits
            pattern = sum(1 << k for k in range(KEY_BITS) if (k >> t) & 1)
            planes.append(jnp.full(shape, np.int32(np.uint32(pattern)), I32))
        else:
            on = ((SUBLANES - 1 - sub) >> b) & 1
            planes.append(jnp.where(on == 1, -1, 0))
    return planes


def _bias_kind(j, i):
    return jnp.clip(j - i + 2, 0, 2)


def _key_block_stream(i, n_chains, s_ref, acc_ref, logits, v_aug):
    chains = range(n_chains)

    def stage(j, c, kind):
        s = logits(j, c, kind)
        s_ref[c] = s
        return jnp.max(s, axis=0, keepdims=True)

    def absorb(j, c, m_old, block_max):
        m_new = jnp.maximum(m_old, block_max)
        alpha = jnp.exp2(m_old - m_new)
        p = jnp.exp2(s_ref[c] - m_new).astype(BF16)
        acc_ref[c] = acc_ref[c] * alpha + _dot(v_aug(j, c), p)
        return m_new

    def start():
        acc_ref[...] = jnp.zeros_like(acc_ref)
        return (jnp.full((1, TQ), NEG, F32),) * n_chains, tuple(stage(0, c, 'any') for c in chains)

    def step(j, carry, kind):
        ms, bms = carry
        out = [(absorb(j, c, ms[c], bms[c]), stage(j + 1, c, kind)) for c in chains]
        return tuple(o[0] for o in out), tuple(o[1] for o in out)

    def finish(carry):
        ms, bms = carry
        for c in chains:
            absorb(i, c, ms[c], bms[c])

    return start, step, finish


def _run_key_block_streams(i, streams):
    def body(j, carries, kind):
        return tuple(step(j, c, kind) for (_, step, _), c in zip(streams, carries))

    carries = tuple(start() for start, _, _ in streams)
    n_far = jnp.maximum(i - 2, 0)
    carries = lax.fori_loop(0, n_far, functools.partial(body, kind='far'), carries)
    carries = lax.fori_loop(n_far, i, functools.partial(body, kind='near'), carries)
    for (_, _, finish), c in zip(streams, carries):
        finish(c)


def _attn_in_kernel(x_ref, g_ref, w_ref, wt_ref, proj_ref, avt_ref, bvt_ref, iwt_ref, *, tm):
    hb = _rms(x_ref[0], g_ref[...]).astype(BF16)
    proj_ref[0] = _dot(hb, w_ref[...]).astype(BF16)
    t = _dot_nt(wt_ref[...], hb)
    for c in range(tm // TK):
        avt_ref[0, c] = t[ROW_AV:ROW_BV, c * TK:(c + 1) * TK].astype(BF16)
        bvt_ref[0, c] = t[ROW_BV:ROW_IW, c * TK:(c + 1) * TK].astype(BF16)
    iwt_ref[0] = t[ROW_IW:ROW_IW + IDX_HEADS, :] * (IDX_HEADS ** -0.5 * IDX_DIM ** -0.5)


def _attn_in(x, g, w, wt, *, tm=512):
    bsz, s, d = x.shape
    av_rows = ROW_BV - ROW_AV
    return pl.pallas_call(
        functools.partial(_attn_in_kernel, tm=tm),
        grid=(bsz, s // tm),
        in_specs=[
            pl.BlockSpec((1, tm, d), lambda b, i: (b, i, 0)),
            pl.BlockSpec((1, d), lambda b, i: (0, 0)),
            pl.BlockSpec((d, PROJ_W), lambda b, i: (0, 0)),
            pl.BlockSpec((VT_ROWS, d), lambda b, i: (0, 0)),
        ],
        out_specs=[
            pl.BlockSpec((1, tm, PROJ_W), lambda b, i: (b, i, 0)),
            pl.BlockSpec((1, tm // TK, av_rows, TK), lambda b, i: (b, i, 0, 0)),
            pl.BlockSpec((1, tm // TK, B_HEAD_DIM, TK), lambda b, i: (b, i, 0, 0)),
            pl.BlockSpec((1, IDX_HEADS, tm), lambda b, i: (b, 0, i)),
        ],
        out_shape=[
            jax.ShapeDtypeStruct((bsz, s, PROJ_W), BF16),
            jax.ShapeDtypeStruct((bsz, s // TK, av_rows, TK), BF16),
            jax.ShapeDtypeStruct((bsz, s // TK, B_HEAD_DIM, TK), BF16),
            jax.ShapeDtypeStruct((bsz, IDX_HEADS, s), F32),
        ],
        compiler_params=pltpu.CompilerParams(
            dimension_semantics=("arbitrary", "arbitrary"), vmem_limit_bytes=VMEM_LIMIT),
        name="attn_in",
    )(x, g, w, wt)


def _key_rows(ref, j):
    return ref[0, pl.ds(pl.multiple_of(j * TK, TK), TK), :]


def _indexer(i, iq_ref, ik_ref, iwt_ref, score_ref, key_ref, plane_ref):
    iq = iq_ref[0]
    per_blk = LANES // IDX_DIM
    iqh = [_lane_band(iq[:, (h // per_blk) * LANES:(h // per_blk + 1) * LANES],
                      (h % per_blk) * IDX_DIM, IDX_DIM) for h in range(IDX_HEADS)]
    w = iwt_ref[0]

    def stage_scores(j):
        ikb = _key_rows(ik_ref, j)
        half = TK // 2
        for r in (0, half):
            acc = jnp.zeros((half, TQ), F32)
            for h in range(IDX_HEADS):
                acc = acc + jnp.maximum(_dot_nt(ikb[r:r + half], iqh[h]), 0.0) * w[h:h + 1, :]
            score_ref[r:r + half, :] = acc

    def staged_keys():
        acc = score_ref[...]
        acc = jnp.where(acc == 0.0, 0.0, acc)
        bits = lax.bitcast_convert_type(acc, I32)
        return bits ^ ((bits >> 31) & 0x7FFFFFFF)

    def store_keys(j, keys):
        key_ref[j] = keys
        for p, plane in enumerate(_bit_planes(keys)):
            plane_ref[p, j] = plane

    def per_block(j):
        keys = staged_keys()
        stage_scores(j + 1)
        store_keys(j, keys)

    def finish():
        krow = lax.broadcasted_iota(I32, (TK, TQ), 0)
        qcol = lax.broadcasted_iota(I32, (TK, TQ), 1)
        admissible = (krow // CHUNK) <= (qcol // CHUNK)
        store_keys(i, jnp.where(admissible, staged_keys(), KEY_NEG_INF))

    stage_scores(0)
    return per_block, finish


def _select_mask(i, key_ref, plane_ref, mask_ref, *, top_k, seq):
    nkb = seq // TK
    blk = lax.broadcasted_iota(I32, (nkb, SUBLANES, TQ), 0)
    alive0 = jnp.where(blk <= i, -1, 0)

    def bit_body(p, carry, n_bits):
        alive, n_gt, thr_u = carry
        hit = alive & plane_ref[p]
        pc = jnp.sum(lax.population_count(hit), axis=0)
        n_hit = jnp.sum(pc.astype(F32), axis=0, keepdims=True)
        take = (n_gt + n_hit) >= top_k
        thr_u = thr_u | jnp.where(take, lax.shift_left(jnp.int32(1), n_bits - 1 - p), 0)
        n_gt = n_gt + jnp.where(take, 0.0, n_hit)
        alive = jnp.where(take, hit, alive ^ hit)
        return alive, n_gt, thr_u

    zero = jnp.zeros((1, TQ), I32)
    alive, n_gt, thr_u = lax.fori_loop(
        0, KEY_BITS, lambda p, c: bit_body(p, c, KEY_BITS), (alive0, jnp.zeros((1, TQ), F32), zero))
    idx_bits = _index_bits(seq)
    _, _, rev_idx = lax.fori_loop(
        KEY_BITS, KEY_BITS + idx_bits,
        lambda p, c: bit_body(p, c, KEY_BITS + idx_bits), (alive, n_gt, zero))
    thr = thr_u ^ INT_MIN
    lim = seq - rev_idx
    lim = jnp.where(thr > KEY_NEG_INF, lim, seq)
    thr = jnp.maximum(thr, KEY_NEG_INF + 1)
    krow = lax.broadcasted_iota(I32, (TK, TQ), 0)

    def mask_body(j, c):
        late = jnp.where(krow >= lim - j * TK, 1, 0)
        mask_ref[j] = jnp.where(key_ref[j] - late >= thr, 0.0, NEG)
        return c

    lax.fori_loop(0, i + 1, mask_body, 0)


def _attn_kernel(lam_ref, aq_ref, ak_ref, avt_ref, abias_ref, g_ref,
                 bq_ref, iq_ref, kb_ref, ik_ref, bvt_ref, iwt_ref, bbias_ref,
                 ya_ref, yb_ref,
                 avaug_ref, bvaug_ref, sa_ref, sb_ref, aacc_ref, bacc_ref, ot_ref,
                 score_ref, key_ref, plane_ref, mask_ref, *, lambda_init, top_k, seq):
    i = pl.program_id(1)
    nkb = seq // TK
    n_maps = 2 * A_HEADS

    @pl.when(i == 0)
    def _():
        for c in range(nkb):
            for h in range(A_HEADS):
                avaug_ref[h, c, 0:A_V_DIM, :] = avt_ref[0, c, h * A_V_DIM:(h + 1) * A_V_DIM, :]
                avaug_ref[h, c, A_V_DIM:A_V_DIM + ONES_ROWS, :] = jnp.ones((ONES_ROWS, TK), BF16)
            bvaug_ref[c, 0:B_HEAD_DIM, :] = bvt_ref[0, c]
            bvaug_ref[c, B_HEAD_DIM:B_HEAD_DIM + ONES_ROWS, :] = jnp.ones((ONES_ROWS, TK), BF16)

    @pl.when((i == 0) & (pl.program_id(0) == 0))
    def _():
        plane_ref[0:KEY_BITS] = jnp.zeros((KEY_BITS, nkb, SUBLANES, TQ), I32)
        for t, plane in enumerate(_index_planes(nkb)):
            plane_ref[KEY_BITS + t] = plane

    index_block, finish_index = _indexer(i, iq_ref, ik_ref, iwt_ref, score_ref, key_ref, plane_ref)

    def index_body(j, c):
        index_block(j)
        return c

    lax.fori_loop(0, i, index_body, 0)
    finish_index()
    _select_mask(i, key_ref, plane_ref, mask_ref, top_k=top_k, seq=seq)

    lv = lam_ref[...]
    lam = (jnp.exp(jnp.sum(lv[0:1] * lv[1:2], axis=-1, keepdims=True))
           - jnp.exp(jnp.sum(lv[2:3] * lv[3:4], axis=-1, keepdims=True)) + lambda_init)

    q = aq_ref[0]
    qm = []
    for h in range(A_HEADS):
        q12 = q[:, h * LANES:(h + 1) * LANES]
        qm += [_lane_band(q12, 0, A_HEAD_DIM), _lane_band(q12, A_HEAD_DIM, A_HEAD_DIM)]

    def a_logits(j, c, kind):
        h = c // 2
        kb = ak_ref[0, pl.ds(pl.multiple_of(j * TK, TK), TK), h * LANES:(h + 1) * LANES]
        s = _dot_nt(kb, qm[c])
        return s if kind == 'far' else s + abias_ref[_bias_kind(j, i), h]

    qb = bq_ref[0]
    qh = [_lane_band(qb[:, (h // 2) * LANES:(h // 2 + 1) * LANES],
                     (h % 2) * B_HEAD_DIM, B_HEAD_DIM) for h in range(B_HEADS)]

    def b_logits(j, h, kind):
        s = _dot_nt(_key_rows(kb_ref, j), qh[h]) + mask_ref[j]
        return s if kind == 'far' else s + bbias_ref[_bias_kind(j, i), h]

    _run_key_block_streams(i, [
        _key_block_stream(i, n_maps, sa_ref, aacc_ref, a_logits, lambda j, c: avaug_ref[c // 2, j]),
        _key_block_stream(i, B_HEADS, sb_ref, bacc_ref, b_logits, lambda j, h: bvaug_ref[j])])

    for h in range(A_HEADS):
        a1 = aacc_ref[2 * h]
        a2 = aacc_ref[2 * h + 1]
        a = (a1[0:A_V_DIM] / a1[A_V_DIM:A_V_DIM + 1]
             - lam * (a2[0:A_V_DIM] / a2[A_V_DIM:A_V_DIM + 1]))
        y = a * lax.rsqrt(jnp.mean(a * a, axis=0, keepdims=True) + EPS) * g_ref[...]
        ot_ref[h * A_V_DIM:(h + 1) * A_V_DIM, :] = y * (1.0 - lambda_init)
    ya_ref[0] = ot_ref[...].T.astype(BF16)

    for h in range(B_HEADS):
        a = bacc_ref[h]
        ot_ref[h * B_HEAD_DIM:(h + 1) * B_HEAD_DIM, :] = (
            a[0:B_HEAD_DIM] / a[B_HEAD_DIM:B_HEAD_DIM + 1])
    yb_ref[0] = ot_ref[...].T.astype(BF16)


def _attention(proj, avt, bvt, iwt, lam_vecs, a_bias, b_bias, subln_g_col, *, lambda_init, top_k):
    bsz, s, _ = proj.shape
    nq = s // TQ
    nkb = s // TK
    a_width = A_HEADS * A_V_DIM
    b_width = B_HEADS * B_HEAD_DIM
    assert a_width == b_width
    iq_w = IDX_HEADS * IDX_DIM
    once = pl.Buffered(1)
    return pl.pallas_call(
        functools.partial(_attn_kernel, lambda_init=lambda_init, top_k=top_k, seq=s),
        grid=(bsz, nq),
        in_specs=[
            pl.BlockSpec((4, A_HEAD_DIM), lambda b, i: (0, 0)),
            pl.BlockSpec((1, TQ, a_width), lambda b, i: (b, i, COL_AQ // a_width)),
            pl.BlockSpec((1, s, a_width), lambda b, i: (b, 0, COL_AK // a_width), once),
            pl.BlockSpec((1, nkb, a_width, TK), lambda b, i: (b, 0, 0, 0), once),
            pl.BlockSpec((3, A_HEADS, TK, TQ), lambda b, i: (0, 0, 0, 0), once),
            pl.BlockSpec((A_V_DIM, 1), lambda b, i: (0, 0)),
            pl.BlockSpec((1, TQ, b_width), lambda b, i: (b, i, COL_BQ // b_width)),
            pl.BlockSpec((1, TQ, iq_w), lambda b, i: (b, i, COL_IQ // iq_w)),
            pl.BlockSpec((1, s, LANES), lambda b, i: (b, 0, COL_KB // LANES), once),
            pl.BlockSpec((1, s, LANES), lambda b, i: (b, 0, COL_IK // LANES), once),
            pl.BlockSpec((1, nkb, B_HEAD_DIM, TK), lambda b, i: (b, 0, 0, 0), once),
            pl.BlockSpec((1, IDX_HEADS, TQ), lambda b, i: (b, 0, i)),
            pl.BlockSpec((3, B_HEADS, TK, TQ), lambda b, i: (0, 0, 0, 0), once),
        ],
        out_specs=[pl.BlockSpec((1, TQ, a_width), lambda b, i: (b, i, 0)),
                   pl.BlockSpec((1, TQ, b_width), lambda b, i: (b, i, 0))],
        out_shape=[jax.ShapeDtypeStruct((bsz, s, a_width), BF16),
                   jax.ShapeDtypeStruct((bsz, s, b_width), BF16)],
        scratch_shapes=[
            pltpu.VMEM((A_HEADS, nkb, A_V_DIM + ONES_ROWS, TK), BF16),
            pltpu.VMEM((nkb, B_HEAD_DIM + ONES_ROWS, TK), BF16),
            pltpu.VMEM((2 * A_HEADS, TK, TQ), F32),
            pltpu.VMEM((B_HEADS, TK, TQ), F32),
            pltpu.VMEM((2 * A_HEADS, A_V_DIM + ONES_ROWS, TQ), F32),
            pltpu.VMEM((B_HEADS, B_HEAD_DIM + ONES_ROWS, TQ), F32),
            pltpu.VMEM((a_width, TQ), F32),
            pltpu.VMEM((TK, TQ), F32),
            pltpu.VMEM((nkb, TK, TQ), I32),
            pltpu.VMEM((KEY_BITS + _index_bits(s), nkb, SUBLANES, TQ), I32),
            pltpu.VMEM((nkb, TK, TQ), F32),
        ],
        compiler_params=pltpu.CompilerParams(
            dimension_semantics=("arbitrary", "arbitrary"), vmem_limit_bytes=VMEM_LIMIT),
        name="attention",
    )(lam_vecs, proj, proj, avt, a_bias, subln_g_col, proj, proj, proj, proj, bvt, iwt, b_bias)


TAIL_ROWS = 256


def _tail_kernel(ya_ref, yb_ref, wo_ref, x_ref, g_mix_ref, g_in_ref, wu_ref, wd_ref, g_out_ref,
                 o_ref, x1_ref, h_ref, acc_ref):
    f = pl.program_id(1)

    @pl.when(f == 0)
    def _():
        half = ya_ref.shape[-1]
        for r in range(0, x_ref.shape[0], TAIL_ROWS):
            rows = slice(r, r + TAIL_ROWS)
            m = (_dot(ya_ref[rows, :], wo_ref[0:half, :])
                 + _dot(yb_ref[rows, :], wo_ref[half:2 * half, :]))
            x1 = x_ref[rows, :] + _rms(m, g_mix_ref[...])
            x1_ref[rows, :] = x1
            h_ref[rows, :] = _rms(x1, g_in_ref[...]).astype(BF16)
        acc_ref[...] = jnp.zeros_like(acc_ref)

    u = jnp.maximum(_dot(h_ref[...], wu_ref[...]), 0.0)
    acc_ref[...] += _dot((u * u).astype(BF16), wd_ref[...])

    @pl.when(f == pl.num_programs(1) - 1)
    def _():
        o_ref[...] = x1_ref[...] + _rms(acc_ref[...], g_out_ref[...])


def _layer_tail(ya, yb, wo, x, g_mix, g_in, wu, wd, layer, g_out, *, tm=1024, tf=1024):
    n, d = x.shape
    half = ya.shape[-1]
    ff = wu.shape[-1]
    return pl.pallas_call(
        _tail_kernel,
        grid=(n // tm, ff // tf),
        in_specs=[
            pl.BlockSpec((tm, half), lambda i, f: (i, 0)),
            pl.BlockSpec((tm, half), lambda i, f: (i, 0)),
            pl.BlockSpec((2 * half, d), lambda i, f: (0, 0)),
            pl.BlockSpec((tm, d), lambda i, f: (i, 0)),
            pl.BlockSpec((1, d), lambda i, f: (0, 0)),
            pl.BlockSpec((1, d), lambda i, f: (0, 0)),
            pl.BlockSpec((None, d, tf), lambda i, f: (layer, 0, f)),
            pl.BlockSpec((None, tf, d), lambda i, f: (layer, f, 0)),
            pl.BlockSpec((1, d), lambda i, f: (0, 0)),
        ],
        out_specs=pl.BlockSpec((tm, d), lambda i, f: (i, 0)),
        out_shape=jax.ShapeDtypeStruct((n, d), F32),
        scratch_shapes=[pltpu.VMEM((tm, d), F32), pltpu.VMEM((tm, d), BF16),
                        pltpu.VMEM((tm, d), F32)],
        compiler_params=pltpu.CompilerParams(
            dimension_semantics=("arbitrary", "arbitrary"), vmem_limit_bytes=VMEM_LIMIT),
        name="layer_tail",
    )(ya, yb, wo, x, g_mix, g_in, wu, wd, g_out)


CONV_ROWS = 64
U_HIST = 32
Z_HIST = 8


def _conv_kernel(x_ref, g_in_ref, w_in_ref, dw_w_ref, dw_b_ref, ln_g_ref, ln_b_ref, sc_w_ref,
                 u_ref, z_ref, ubuf_ref, zbuf_ref, shift_ref, conv_ref, *, ts):
    @pl.when(pl.program_id(1) == 0)
    def _():
        ubuf_ref[0:U_HIST, :] = jnp.zeros((U_HIST, CONV_CH), F32)
        zbuf_ref[0:Z_HIST, :] = jnp.zeros((Z_HIST, SC_CH), F32)

    x = x_ref[0]
    hb = _rms(x, g_in_ref[...]).astype(BF16)
    c = CONV_CH
    ca = _dot(hb, w_in_ref[:, 0:c])
    cg = _dot(hb, w_in_ref[:, c:2 * c])
    ubuf_ref[U_HIST:U_HIST + ts, :] = ca * jax.nn.sigmoid(cg)
    dc = _dot(hb, w_in_ref[:, 3 * c:4 * c])
    dh = _dot(hb, w_in_ref[:, 4 * c:5 * c])
    zbuf_ref[Z_HIST:Z_HIST + ts, :] = dc * dh

    first_off = U_HIST - (CONV_WIDTH - 1)
    shift_rows = shift_ref.shape[1]
    for r in range(1, SUBLANES):
        shift_ref[r - 1] = ubuf_ref[r:r + shift_rows, :]

    for base in range(0, ts, CONV_ROWS):
        acc = jnp.zeros((CONV_ROWS, c), F32)
        for j in range(CONV_WIDTH):
            r, a = (first_off + j) % SUBLANES, (first_off + j) // SUBLANES
            rows = pl.ds(base + SUBLANES * a, CONV_ROWS)
            taps = ubuf_ref[rows, :] if r == 0 else shift_ref[r - 1, rows, :]
            acc = acc + dw_w_ref[j:j + 1, :] * taps
        conv_ref[base:base + CONV_ROWS, :] = acc
    u = conv_ref[...] + dw_b_ref[...]
    mu = jnp.mean(u, axis=-1, keepdims=True)
    ctr = u - mu
    var = jnp.mean(ctr * ctr, axis=-1, keepdims=True)
    u = ctr * lax.rsqrt(var + EPS) * ln_g_ref[...] + ln_b_ref[...]
    u = u * jax.nn.sigmoid(u)

    z = jnp.zeros((ts, c), F32)
    for j in range(SC_WIDTH):
        off = Z_HIST - (SC_WIDTH - 1) + j
        z = z + sc_w_ref[j:j + 1, :] * zbuf_ref[off:off + ts, :]
    z = _dot(hb, w_in_ref[:, 2 * c:3 * c]) * z

    ubuf_ref[0:U_HIST, :] = ubuf_ref[ts:ts + U_HIST, :]
    zbuf_ref[0:Z_HIST, :] = zbuf_ref[ts:ts + Z_HIST, :]

    u_ref[0] = u.astype(BF16)
    z_ref[0] = z.astype(BF16)


def _conv_mixer(x, g_in, w_in, dw_w, dw_b, ln_g, ln_b, sc_w, *, ts=512):
    bsz, s, d = x.shape
    full = lambda a: pl.BlockSpec(a.shape, lambda b, i: (0,) * a.ndim)
    return pl.pallas_call(
        functools.partial(_conv_kernel, ts=ts),
        grid=(bsz, s // ts),
        in_specs=[pl.BlockSpec((1, ts, d), lambda b, i: (b, i, 0)),
                  full(g_in), full(w_in), full(dw_w), full(dw_b), full(ln_g), full(ln_b),
                  full(sc_w)],
        out_specs=[pl.BlockSpec((1, ts, CONV_CH), lambda b, i: (b, i, 0)),
                   pl.BlockSpec((1, ts, SC_CH), lambda b, i: (b, i, 0))],
        out_shape=[jax.ShapeDtypeStruct((bsz, s, CONV_CH), BF16),
                   jax.ShapeDtypeStruct((bsz, s, SC_CH), BF16)],
        scratch_shapes=[pltpu.VMEM((U_HIST + ts, CONV_CH), F32),
                        pltpu.VMEM((Z_HIST + ts, SC_CH), F32),
                        pltpu.VMEM((SUBLANES - 1, ts + U_HIST - SUBLANES, CONV_CH), F32),
                        pltpu.VMEM((ts, CONV_CH), F32)],
        compiler_params=pltpu.CompilerParams(
            dimension_semantics=("arbitrary", "arbitrary"), vmem_limit_bytes=VMEM_LIMIT),
        name="conv_mixer",
    )(x, g_in, w_in, dw_w, dw_b, ln_g, ln_b, sc_w)


def _t5_bucket(rel):
    nb = NUM_BUCKETS // 2
    ret = jnp.where(rel > 0, nb, 0)
    n = jnp.abs(rel)
    max_exact = nb // 2
    nf = jnp.maximum(n, 1).astype(jnp.float32)
    large = max_exact + (jnp.log(nf / max_exact) / math.log(MAX_DISTANCE / max_exact)
                         * (nb - max_exact)).astype(jnp.int32)
    large = jnp.minimum(large, nb - 1)
    return ret + jnp.where(n < max_exact, n, large)


REL_LO = -(TQ + TK - 1)
REL_LEN = TK - REL_LO
REL_LANES = -(-REL_LEN // LANES) * LANES


def _bias_kernel(fr_ref, a_ref, b_ref):
    krow = lax.broadcasted_iota(I32, (TK, TQ), 0)
    qcol = lax.broadcasted_iota(I32, (TK, TQ), 1)
    admissible = (krow // CHUNK) <= (qcol // CHUNK)
    zeros = jnp.zeros((TK, TQ), F32)
    for h in range(A_HEADS + B_HEADS):
        table = jnp.broadcast_to(fr_ref[h], (TK, REL_LANES))
        blocks = []
        for shift in (TK, 0):
            s0 = (-shift - REL_LO - (REL_LEN - 1)) % REL_LANES
            blocks.append(pltpu.roll(table, s0, 1, stride=1, stride_axis=0)[:, 0:TQ])
        prev, diag = blocks
        if h < A_HEADS:
            a_ref[0, h], a_ref[1, h], a_ref[2, h] = zeros, prev, jnp.where(admissible, diag, NEG)
        else:
            g = h - A_HEADS
            b_ref[0, g], b_ref[1, g], b_ref[2, g] = zeros, prev, diag


def _bias_tables(rel_bias):
    rel = jnp.arange(REL_LO, TK, dtype=jnp.int32)
    far_bucket = NUM_BUCKETS // 2 - 1
    f = (rel_bias[_t5_bucket(rel)].astype(F32) - rel_bias[far_bucket].astype(F32)[None, :]).T
    fr = jnp.pad(f[:, ::-1] * LOG2E, ((0, 0), (0, REL_LANES - REL_LEN)))[:, None, :]
    heads = A_HEADS + B_HEADS
    return pl.pallas_call(
        _bias_kernel,
        grid=(1,),
        in_specs=[pl.BlockSpec((heads, 1, REL_LANES), lambda i: (0, 0, 0))],
        out_specs=[pl.BlockSpec((3, A_HEADS, TK, TQ), lambda i: (0, 0, 0, 0)),
                   pl.BlockSpec((3, B_HEADS, TK, TQ), lambda i: (0, 0, 0, 0))],
        out_shape=[jax.ShapeDtypeStruct((3, A_HEADS, TK, TQ), F32),
                   jax.ShapeDtypeStruct((3, B_HEADS, TK, TQ), F32)],
        compiler_params=pltpu.CompilerParams(
            dimension_semantics=("arbitrary",), vmem_limit_bytes=VMEM_LIMIT),
        name="bias_tables",
    )(fr)


def _attn_weights(w_in):
    o = np.cumsum([0, 512, 512, 512, 512, 64, 64, 256, 32, 8])
    aq, ak, av, bq, bk, bv, iq, ik, iw = [w_in[:, o[n]:o[n + 1]] for n in range(9)]
    aq = aq * (A_HEAD_DIM ** -0.5 * LOG2E)
    bq = bq * (B_HEAD_DIM ** -0.5 * LOG2E)
    w = jnp.concatenate([aq, ak, bq, iq, bk, bk, ik, ik, ik, ik], axis=1).astype(BF16)
    pad = jnp.zeros((w_in.shape[0], VT_ROWS - ROW_IW - IDX_HEADS), w_in.dtype)
    wt = jnp.concatenate([av, bv, iw, pad], axis=1).T.astype(BF16)
    return w, wt


def kernel(x, rel_bias, norm_g, w_mlp_up, w_mlp_down, attn_w_in, attn_w_out, diff_lambda,
           diff_subln_g, conv_w_in, conv_w_out, conv_dw_w, conv_dw_b, conv_ln_g, conv_ln_b,
           sconv_w):
    bsz, s, d = x.shape
    depth = norm_g.shape[0]
    top_k = min(TOPK_MAX, s // 4)
    row = lambda v: v.reshape(1, -1)
    a_bias, b_bias = _bias_tables(rel_bias)
    w_up, w_down = w_mlp_up.astype(BF16), w_mlp_down.astype(BF16)
    for i in range(depth):
        j = i // 2
        if i % 2 == 0:
            lambda_init = 0.8 - 0.6 * math.exp(-0.3 * i)
            w, wt = _attn_weights(attn_w_in[j])
            proj, avt, bvt, iwt = _attn_in(x.reshape(bsz, s, d), row(norm_g[i, 0]), w, wt)
            ya, yb = _attention(proj, avt, bvt, iwt, diff_lambda[j], a_bias, b_bias,
                                diff_subln_g[j].reshape(-1, 1), lambda_init=lambda_init,
                                top_k=top_k)
            w_out = attn_w_out[j]
        else:
            ya, yb = _conv_mixer(x.reshape(bsz, s, d), row(norm_g[i, 0]),
                                 conv_w_in[j].astype(BF16), conv_dw_w[j], row(conv_dw_b[j]),
                                 row(conv_ln_g[j]), row(conv_ln_b[j]), sconv_w[j])
            w_out = conv_w_out[j]
        x = _layer_tail(ya.reshape(bsz * s, -1), yb.reshape(bsz * s, -1), w_out.astype(BF16),
                        x.reshape(bsz * s, d), row(norm_g[i, 1]), row(norm_g[i, 2]),
                        w_up, w_down, i, row(norm_g[i, 3]))
    return x.reshape(bsz, s, d)
```

```python
import functools
import math

import numpy as np
import jax
import jax.numpy as jnp
from jax import lax
from jax.experimental import pallas as pl
from jax.experimental.pallas import tpu as pltpu

F32 = jnp.float32
BF16 = jnp.bfloat16
I32 = jnp.int32

D_MODEL = 1024
CHUNK = 64
NUM_BUCKETS = 32
MAX_DISTANCE = 128
EPS = 1e-6
NEG = -1e30
LOG2E = math.log2(math.e)
A_HEADS = 4
A_HEAD_DIM = 64
A_V_DIM = 2 * A_HEAD_DIM
B_HEADS = 8
B_HEAD_DIM = 64
IDX_HEADS = 8
IDX_DIM = 32
TOPK_MAX = 256
CONV_CH = 512
CONV_WIDTH = 31
SC_CH = 512
SC_WIDTH = 3
D_FF = 4 * D_MODEL

LANES = 128
SUBLANES = 8
VMEM_LIMIT = 56 * 1024 * 1024

TQ = 256
TK = 256
COL_AQ, COL_AK, COL_BQ, COL_IQ, COL_KB, COL_IK, PROJ_W = 0, 512, 1024, 1536, 1792, 1920, 2048
ROW_AV, ROW_BV, ROW_IW, VT_ROWS = 0, 512, 576, 640
ONES_ROWS = 16
INT_MIN = -2 ** 31
KEY_NEG_INF = INT_MIN + 0x7FFFFF


def _rms(x, g):
    return x * lax.rsqrt(jnp.mean(x * x, axis=-1, keepdims=True) + EPS) * g


def _dot_nt(a, b):
    return lax.dot_general(a, b, (((1,), (1,)), ((), ())), preferred_element_type=F32)


def _dot(a, b):
    return jnp.dot(a, b, preferred_element_type=F32)


def _lane_band(x, lo, width):
    lane = lax.broadcasted_iota(I32, x.shape, 1)
    return jnp.where((lane >= lo) & (lane < lo + width), x, jnp.zeros_like(x))


KEY_BITS = 32


def _bit_planes(keys):
    assert keys.shape[0] == KEY_BITS * SUBLANES
    u = keys ^ INT_MIN
    a = [u[SUBLANES * r:SUBLANES * (r + 1), :] for r in range(KEY_BITS)]
    j, m = KEY_BITS // 2, (1 << (KEY_BITS // 2)) - 1
    while j:
        mask = np.int32(np.uint32(m))
        k = 0
        while k < KEY_BITS:
            t = (a[k] ^ lax.shift_right_logical(a[k + j], jnp.int32(j))) & mask
            a[k] = a[k] ^ t
            a[k + j] = a[k + j] ^ lax.shift_left(t, jnp.int32(j))
            k = (k + j + 1) & ~j
        j >>= 1
        m = (m ^ (m << j)) & 0xFFFFFFFF
    return a


def _index_bits(seq):
    bits = seq.bit_length() - 1
    assert seq == 1 << bits and seq >= KEY_BITS * SUBLANES
    return bits


def _index_planes(nkb):
    shape = (nkb, SUBLANES, TQ)
    blk = lax.broadcasted_iota(I32, shape, 0)
    sub = lax.broadcasted_iota(I32, shape, 1)
    sub_bits = SUBLANES.bit_length() - 1
    word_bits = KEY_BITS.bit_length() - 1
    planes = []
    for b in range(_index_bits(nkb * TK) - 1, -1, -1):
        if b >= sub_bits + word_bits:
            on = ((nkb - 1 - blk) >> (b - sub_bits - word_bits)) & 1
            planes.append(jnp.where(on == 1, -1, 0))
        elif b >= sub_bits:
            t = b - sub_bits
            pattern = sum(1 << k for k in range(KEY_BITS) if (k >> t) & 1)
            planes.append(jnp.full(shape, np.int32(np.uint32(pattern)), I32))
        else:
            on = ((SUBLANES - 1 - sub) >> b) & 1
            planes.append(jnp.where(on == 1, -1, 0))
    return planes


def _bias_kind(j, i):
    return jnp.clip(j - i + 2, 0, 2)


def _key_block_stream(i, n_chains, s_ref, acc_ref, logits, v_aug):
    chains = range(n_chains)

    def stage(j, c, kind):
        s = logits(j, c, kind)
        s_ref[c] = s
        return jnp.max(s, axis=0, keepdims=True)

    def absorb(j, c, m_old, block_max):
        m_new = jnp.maximum(m_old, block_max)
        alpha = jnp.exp2(m_old - m_new)
        p = jnp.exp2(s_ref[c] - m_new).astype(BF16)
        acc_ref[c] = acc_ref[c] * alpha + _dot(v_aug(j, c), p)
        return m_new

    def start():
        acc_ref[...] = jnp.zeros_like(acc_ref)
        return (jnp.full((1, TQ), NEG, F32),) * n_chains, tuple(stage(0, c, 'any') for c in chains)

    def step(j, carry, kind):
        ms, bms = carry
        out = [(absorb(j, c, ms[c], bms[c]), stage(j + 1, c, kind)) for c in chains]
        return tuple(o[0] for o in out), tuple(o[1] for o in out)

    def finish(carry):
        ms, bms = carry
        for c in chains:
            absorb(i, c, ms[c], bms[c])

    return start, step, finish


def _run_key_block_streams(i, streams):
    def body(j, carries, kind):
        return tuple(step(j, c, kind) for (_, step, _), c in zip(streams, carries))

    carries = tuple(start() for start, _, _ in streams)
    n_far = jnp.maximum(i - 2, 0)
    carries = lax.fori_loop(0, n_far, functools.partial(body, kind='far'), carries)
    carries = lax.fori_loop(n_far, i, functools.partial(body, kind='near'), carries)
    for (_, _, finish), c in zip(streams, carries):
        finish(c)


def _attn_in_kernel(x_ref, g_ref, w_ref, wt_ref, proj_ref, avt_ref, bvt_ref, iwt_ref, *, tm):
    hb = _rms(x_ref[0], g_ref[...]).astype(BF16)
    proj_ref[0] = _dot(hb, w_ref[...]).astype(BF16)
    t = _dot_nt(wt_ref[...], hb)
    for c in range(tm // TK):
        avt_ref[0, c] = t[ROW_AV:ROW_BV, c * TK:(c + 1) * TK].astype(BF16)
        bvt_ref[0, c] = t[ROW_BV:ROW_IW, c * TK:(c + 1) * TK].astype(BF16)
    iwt_ref[0] = t[ROW_IW:ROW_IW + IDX_HEADS, :] * (IDX_HEADS ** -0.5 * IDX_DIM ** -0.5)


def _attn_in(x, g, w, wt, *, tm=512):
    bsz, s, d = x.shape
    av_rows = ROW_BV - ROW_AV
    return pl.pallas_call(
        functools.partial(_attn_in_kernel, tm=tm),
        grid=(bsz, s // tm),
        in_specs=[
            pl.BlockSpec((1, tm, d), lambda b, i: (b, i, 0)),
            pl.BlockSpec((1, d), lambda b, i: (0, 0)),
            pl.BlockSpec((d, PROJ_W), lambda b, i: (0, 0)),
            pl.BlockSpec((VT_ROWS, d), lambda b, i: (0, 0)),
        ],
        out_specs=[
            pl.BlockSpec((1, tm, PROJ_W), lambda b, i: (b, i, 0)),
            pl.BlockSpec((1, tm // TK, av_rows, TK), lambda b, i: (b, i, 0, 0)),
            pl.BlockSpec((1, tm // TK, B_HEAD_DIM, TK), lambda b, i: (b, i, 0, 0)),
            pl.BlockSpec((1, IDX_HEADS, tm), lambda b, i: (b, 0, i)),
        ],
        out_shape=[
            jax.ShapeDtypeStruct((bsz, s, PROJ_W), BF16),
            jax.ShapeDtypeStruct((bsz, s // TK, av_rows, TK), BF16),
            jax.ShapeDtypeStruct((bsz, s // TK, B_HEAD_DIM, TK), BF16),
            jax.ShapeDtypeStruct((bsz, IDX_HEADS, s), F32),
        ],
        compiler_params=pltpu.CompilerParams(
            dimension_semantics=("arbitrary", "arbitrary"), vmem_limit_bytes=VMEM_LIMIT),
        name="attn_in",
    )(x, g, w, wt)


def _key_rows(ref, j):
    return ref[0, pl.ds(pl.multiple_of(j * TK, TK), TK), :]


def _indexer(i, iq_ref, ik_ref, iwt_ref, score_ref, key_ref, plane_ref):
    iq = iq_ref[0]
    per_blk = LANES // IDX_DIM
    iqh = [_lane_band(iq[:, (h // per_blk) * LANES:(h // per_blk + 1) * LANES],
                      (h % per_blk) * IDX_DIM, IDX_DIM) for h in range(IDX_HEADS)]
    w = iwt_ref[0]

    def stage_scores(j):
        ikb = _key_rows(ik_ref, j)
        half = TK // 2
        for r in (0, half):
            acc = jnp.zeros((half, TQ), F32)
            for h in range(IDX_HEADS):
                acc = acc + jnp.maximum(_dot_nt(ikb[r:r + half], iqh[h]), 0.0) * w[h:h + 1, :]
            score_ref[r:r + half, :] = acc

    def staged_keys():
        acc = score_ref[...]
        acc = jnp.where(acc == 0.0, 0.0, acc)
        bits = lax.bitcast_convert_type(acc, I32)
        return bits ^ ((bits >> 31) & 0x7FFFFFFF)

    def store_keys(j, keys):
        key_ref[j] = keys
        for p, plane in enumerate(_bit_planes(keys)):
            plane_ref[p, j] = plane

    def per_block(j):
        keys = staged_keys()
        stage_scores(j + 1)
        store_keys(j, keys)

    def finish():
        krow = lax.broadcasted_iota(I32, (TK, TQ), 0)
        qcol = lax.broadcasted_iota(I32, (TK, TQ), 1)
        admissible = (krow // CHUNK) <= (qcol // CHUNK)
        store_keys(i, jnp.where(admissible, staged_keys(), KEY_NEG_INF))

    stage_scores(0)
    return per_block, finish


def _select_mask(i, key_ref, plane_ref, mask_ref, *, top_k, seq):
    nkb = seq // TK
    blk = lax.broadcasted_iota(I32, (nkb, SUBLANES, TQ), 0)
    alive0 = jnp.where(blk <= i, -1, 0)

    def bit_body(p, carry, n_bits):
        alive, n_gt, thr_u = carry
        hit = alive & plane_ref[p]
        pc = jnp.sum(lax.population_count(hit), axis=0)
        n_hit = jnp.sum(pc.astype(F32), axis=0, keepdims=True)
        take = (n_gt + n_hit) >= top_k
        thr_u = thr_u | jnp.where(take, lax.shift_left(jnp.int32(1), n_bits - 1 - p), 0)
        n_gt = n_gt + jnp.where(take, 0.0, n_hit)
        alive = jnp.where(take, hit, alive ^ hit)
        return alive, n_gt, thr_u

    zero = jnp.zeros((1, TQ), I32)
    alive, n_gt, thr_u = lax.fori_loop(
        0, KEY_BITS, lambda p, c: bit_body(p, c, KEY_BITS), (alive0, jnp.zeros((1, TQ), F32), zero))
    idx_bits = _index_bits(seq)
    _, _, rev_idx = lax.fori_loop(
        KEY_BITS, KEY_BITS + idx_bits,
        lambda p, c: bit_body(p, c, KEY_BITS + idx_bits), (alive, n_gt, zero))
    thr = thr_u ^ INT_MIN
    lim = seq - rev_idx
    lim = jnp.where(thr > KEY_NEG_INF, lim, seq)
    thr = jnp.maximum(thr, KEY_NEG_INF + 1)
    krow = lax.broadcasted_iota(I32, (TK, TQ), 0)

    def mask_body(j, c):
        late = jnp.where(krow >= lim - j * TK, 1, 0)
        mask_ref[j] = jnp.where(key_ref[j] - late >= thr, 0.0, NEG)
        return c

    lax.fori_loop(0, i + 1, mask_body, 0)


def _attn_kernel(lam_ref, aq_ref, ak_ref, avt_ref, abias_ref, g_ref,
                 bq_ref, iq_ref, kb_ref, ik_ref, bvt_ref, iwt_ref, bbias_ref,
                 ya_ref, yb_ref,
                 avaug_ref, bvaug_ref, sa_ref, sb_ref, aacc_ref, bacc_ref, ot_ref,
                 score_ref, key_ref, plane_ref, mask_ref, *, lambda_init, top_k, seq):
    i = pl.program_id(1)
    nkb = seq // TK
    n_maps = 2 * A_HEADS

    @pl.when(i == 0)
    def _():
        for c in range(nkb):
            for h in range(A_HEADS):
                avaug_ref[h, c, 0:A_V_DIM, :] = avt_ref[0, c, h * A_V_DIM:(h + 1) * A_V_DIM, :]
                avaug_ref[h, c, A_V_DIM:A_V_DIM + ONES_ROWS, :] = jnp.ones((ONES_ROWS, TK), BF16)
            bvaug_ref[c, 0:B_HEAD_DIM, :] = bvt_ref[0, c]
            bvaug_ref[c, B_HEAD_DIM:B_HEAD_DIM + ONES_ROWS, :] = jnp.ones((ONES_ROWS, TK), BF16)

    @pl.when((i == 0) & (pl.program_id(0) == 0))
    def _():
        plane_ref[0:KEY_BITS] = jnp.zeros((KEY_BITS, nkb, SUBLANES, TQ), I32)
        for t, plane in enumerate(_index_planes(nkb)):
            plane_ref[KEY_BITS + t] = plane

    index_block, finish_index = _indexer(i, iq_ref, ik_ref, iwt_ref, score_ref, key_ref, plane_ref)

    def index_body(j, c):
        index_block(j)
        return c

    lax.fori_loop(0, i, index_body, 0)
    finish_index()
    _select_mask(i, key_ref, plane_ref, mask_ref, top_k=top_k, seq=seq)

    lv = lam_ref[...]
    lam = (jnp.exp(jnp.sum(lv[0:1] * lv[1:2], axis=-1, keepdims=True))
           - jnp.exp(jnp.sum(lv[2:3] * lv[3:4], axis=-1, keepdims=True)) + lambda_init)

    q = aq_ref[0]
    qm = []
    for h in range(A_HEADS):
        q12 = q[:, h * LANES:(h + 1) * LANES]
        qm += [_lane_band(q12, 0, A_HEAD_DIM), _lane_band(q12, A_HEAD_DIM, A_HEAD_DIM)]

    def a_logits(j, c, kind):
        h = c // 2
        kb = ak_ref[0, pl.ds(pl.multiple_of(j * TK, TK), TK), h * LANES:(h + 1) * LANES]
        s = _dot_nt(kb, qm[c])
        return s if kind == 'far' else s + abias_ref[_bias_kind(j, i), h]

    qb = bq_ref[0]
    qh = [_lane_band(qb[:, (h // 2) * LANES:(h // 2 + 1) * LANES],
                     (h % 2) * B_HEAD_DIM, B_HEAD_DIM) for h in range(B_HEADS)]

    def b_logits(j, h, kind):
        s = _dot_nt(_key_rows(kb_ref, j), qh[h]) + mask_ref[j]
        return s if kind == 'far' else s + bbias_ref[_bias_kind(j, i), h]

    _run_key_block_streams(i, [
        _key_block_stream(i, n_maps, sa_ref, aacc_ref, a_logits, lambda j, c: avaug_ref[c // 2, j]),
        _key_block_stream(i, B_HEADS, sb_ref, bacc_ref, b_logits, lambda j, h: bvaug_ref[j])])

    for h in range(A_HEADS):
        a1 = aacc_ref[2 * h]
        a2 = aacc_ref[2 * h + 1]
        a = (a1[0:A_V_DIM] / a1[A_V_DIM:A_V_DIM + 1]
             - lam * (a2[0:A_V_DIM] / a2[A_V_DIM:A_V_DIM + 1]))
        y = a * lax.rsqrt(jnp.mean(a * a, axis=0, keepdims=True) + EPS) * g_ref[...]
        ot_ref[h * A_V_DIM:(h + 1) * A_V_DIM, :] = y * (1.0 - lambda_init)
    ya_ref[0] = ot_ref[...].T.astype(BF16)

    for h in range(B_HEADS):
        a = bacc_ref[h]
        ot_ref[h * B_HEAD_DIM:(h + 1) * B_HEAD_DIM, :] = (
            a[0:B_HEAD_DIM] / a[B_HEAD_DIM:B_HEAD_DIM + 1])
    yb_ref[0] = ot_ref[...].T.astype(BF16)


def _attention(proj, avt, bvt, iwt, lam_vecs, a_bias, b_bias, subln_g_col, *, lambda_init, top_k):
    bsz, s, _ = proj.shape
    nq = s // TQ
    nkb = s // TK
    a_width = A_HEADS * A_V_DIM
    b_width = B_HEADS * B_HEAD_DIM
    assert a_width == b_width
    iq_w = IDX_HEADS * IDX_DIM
    once = pl.Buffered(1)
    return pl.pallas_call(
        functools.partial(_attn_kernel, lambda_init=lambda_init, top_k=top_k, seq=s),
        grid=(bsz, nq),
        in_specs=[
            pl.BlockSpec((4, A_HEAD_DIM), lambda b, i: (0, 0)),
            pl.BlockSpec((1, TQ, a_width), lambda b, i: (b, i, COL_AQ // a_width)),
            pl.BlockSpec((1, s, a_width), lambda b, i: (b, 0, COL_AK // a_width), once),
            pl.BlockSpec((1, nkb, a_width, TK), lambda b, i: (b, 0, 0, 0), once),
            pl.BlockSpec((3, A_HEADS, TK, TQ), lambda b, i: (0, 0, 0, 0), once),
            pl.BlockSpec((A_V_DIM, 1), lambda b, i: (0, 0)),
            pl.BlockSpec((1, TQ, b_width), lambda b, i: (b, i, COL_BQ // b_width)),
            pl.BlockSpec((1, TQ, iq_w), lambda b, i: (b, i, COL_IQ // iq_w)),
            pl.BlockSpec((1, s, LANES), lambda b, i: (b, 0, COL_KB // LANES), once),
            pl.BlockSpec((1, s, LANES), lambda b, i: (b, 0, COL_IK // LANES), once),
            pl.BlockSpec((1, nkb, B_HEAD_DIM, TK), lambda b, i: (b, 0, 0, 0), once),
            pl.BlockSpec((1, IDX_HEADS, TQ), lambda b, i: (b, 0, i)),
            pl.BlockSpec((3, B_HEADS, TK, TQ), lambda b, i: (0, 0, 0, 0), once),
        ],
        out_specs=[pl.BlockSpec((1, TQ, a_width), lambda b, i: (b, i, 0)),
                   pl.BlockSpec((1, TQ, b_width), lambda b, i: (b, i, 0))],
        out_shape=[jax.ShapeDtypeStruct((bsz, s, a_width), BF16),
                   jax.ShapeDtypeStruct((bsz, s, b_width), BF16)],
        scratch_shapes=[
            pltpu.VMEM((A_HEADS, nkb, A_V_DIM + ONES_ROWS, TK), BF16),
            pltpu.VMEM((nkb, B_HEAD_DIM + ONES_ROWS, TK), BF16),
            pltpu.VMEM((2 * A_HEADS, TK, TQ), F32),
            pltpu.VMEM((B_HEADS, TK, TQ), F32),
            pltpu.VMEM((2 * A_HEADS, A_V_DIM + ONES_ROWS, TQ), F32),
            pltpu.VMEM((B_HEADS, B_HEAD_DIM + ONES_ROWS, TQ), F32),
            pltpu.VMEM((a_width, TQ), F32),
            pltpu.VMEM((TK, TQ), F32),
            pltpu.VMEM((nkb, TK, TQ), I32),
            pltpu.VMEM((KEY_BITS + _index_bits(s), nkb, SUBLANES, TQ), I32),
            pltpu.VMEM((nkb, TK, TQ), F32),
        ],
        compiler_params=pltpu.CompilerParams(
            dimension_semantics=("arbitrary", "arbitrary"), vmem_limit_bytes=VMEM_LIMIT),
        name="attention",
    )(lam_vecs, proj, proj, avt, a_bias, subln_g_col, proj, proj, proj, proj, bvt, iwt, b_bias)


TAIL_ROWS = 256


def _tail_kernel(ya_ref, yb_ref, wo_ref, x_ref, g_mix_ref, g_in_ref, wu_ref, wd_ref, g_out_ref,
                 o_ref, x1_ref, h_ref, acc_ref):
    f = pl.program_id(1)
    last = pl.num_programs(1) - 1
    chunks = [slice(r, r + TAIL_ROWS) for r in range(0, x_ref.shape[0], TAIL_ROWS)]

    def mlp_part(rows):
        u = jnp.maximum(_dot(h_ref[rows, :], wu_ref[...]), 0.0)
        return _dot((u * u).astype(BF16), wd_ref[...])

    @pl.when(f == 0)
    def _():
        half = ya_ref.shape[-1]
        for rows in chunks:
            m = (_dot(ya_ref[rows, :], wo_ref[0:half, :])
                 + _dot(yb_ref[rows, :], wo_ref[half:2 * half, :]))
            x1 = x_ref[rows, :] + _rms(m, g_mix_ref[...])
            x1_ref[rows, :] = x1
            h_ref[rows, :] = _rms(x1, g_in_ref[...]).astype(BF16)
        acc_ref[...] = mlp_part(slice(None))

    @pl.when((f > 0) & (f < last))
    def _():
        acc_ref[...] += mlp_part(slice(None))

    @pl.when(f == last)
    def _():
        for rows in chunks:
            y = acc_ref[rows, :] + mlp_part(rows)
            o_ref[rows, :] = x1_ref[rows, :] + _rms(y, g_out_ref[...])


def _layer_tail(ya, yb, wo, x, g_mix, g_in, wu, wd, layer, g_out, *, tm=1024, tf=1024):
    n, d = x.shape
    half = ya.shape[-1]
    ff = wu.shape[-1]
    assert ff // tf >= 2 and tm % TAIL_ROWS == 0
    return pl.pallas_call(
        _tail_kernel,
        grid=(n // tm, ff // tf),
        in_specs=[
            pl.BlockSpec((tm, half), lambda i, f: (i, 0)),
            pl.BlockSpec((tm, half), lambda i, f: (i, 0)),
            pl.BlockSpec((2 * half, d), lambda i, f: (0, 0)),
            pl.BlockSpec((tm, d), lambda i, f: (i, 0)),
            pl.BlockSpec((1, d), lambda i, f: (0, 0)),
            pl.BlockSpec((1, d), lambda i, f: (0, 0)),
            pl.BlockSpec((None, d, tf), lambda i, f: (layer, 0, f)),
            pl.BlockSpec((None, tf, d), lambda i, f: (layer, f, 0)),
            pl.BlockSpec((1, d), lambda i, f: (0, 0)),
        ],
        out_specs=pl.BlockSpec((tm, d), lambda i, f: (i, 0)),
        out_shape=jax.ShapeDtypeStruct((n, d), F32),
        scratch_shapes=[pltpu.VMEM((tm, d), F32), pltpu.VMEM((tm, d), BF16),
                        pltpu.VMEM((tm, d), F32)],
        compiler_params=pltpu.CompilerParams(
            dimension_semantics=("arbitrary", "arbitrary"), vmem_limit_bytes=VMEM_LIMIT),
        name="layer_tail",
    )(ya, yb, wo, x, g_mix, g_in, wu, wd, g_out)


CONV_ROWS = 64
U_HIST = 32
Z_HIST = 8


def _conv_kernel(x_ref, g_in_ref, w_in_ref, dw_w_ref, dw_b_ref, ln_g_ref, ln_b_ref, sc_w_ref,
                 u_ref, z_ref, ubuf_ref, zbuf_ref, shift_ref, conv_ref, *, ts):
    @pl.when(pl.program_id(1) == 0)
    def _():
        ubuf_ref[0:U_HIST, :] = jnp.zeros((U_HIST, CONV_CH), F32)
        zbuf_ref[0:Z_HIST, :] = jnp.zeros((Z_HIST, SC_CH), F32)

    x = x_ref[0]
    hb = _rms(x, g_in_ref[...]).astype(BF16)
    c = CONV_CH
    ca = _dot(hb, w_in_ref[:, 0:c])
    cg = _dot(hb, w_in_ref[:, c:2 * c])
    ubuf_ref[U_HIST:U_HIST + ts, :] = ca * jax.nn.sigmoid(cg)
    dc = _dot(hb, w_in_ref[:, 3 * c:4 * c])
    dh = _dot(hb, w_in_ref[:, 4 * c:5 * c])
    zbuf_ref[Z_HIST:Z_HIST + ts, :] = dc * dh

    first_off = U_HIST - (CONV_WIDTH - 1)
    shift_rows = shift_ref.shape[1]
    for r in range(1, SUBLANES):
        shift_ref[r - 1] = ubuf_ref[r:r + shift_rows, :]

    for base in range(0, ts, CONV_ROWS):
        acc = jnp.zeros((CONV_ROWS, c), F32)
        for j in range(CONV_WIDTH):
            r, a = (first_off + j) % SUBLANES, (first_off + j) // SUBLANES
            rows = pl.ds(base + SUBLANES * a, CONV_ROWS)
            taps = ubuf_ref[rows, :] if r == 0 else shift_ref[r - 1, rows, :]
            acc = acc + dw_w_ref[j:j + 1, :] * taps
        conv_ref[base:base + CONV_ROWS, :] = acc
    u = conv_ref[...] + dw_b_ref[...]
    mu = jnp.mean(u, axis=-1, keepdims=True)
    ctr = u - mu
    var = jnp.mean(ctr * ctr, axis=-1, keepdims=True)
    u = ctr * lax.rsqrt(var + EPS) * ln_g_ref[...] + ln_b_ref[...]
    u = u * jax.nn.sigmoid(u)

    z = jnp.zeros((ts, c), F32)
    for j in range(SC_WIDTH):
        off = Z_HIST - (SC_WIDTH - 1) + j
        z = z + sc_w_ref[j:j + 1, :] * zbuf_ref[off:off + ts, :]
    z = _dot(hb, w_in_ref[:, 2 * c:3 * c]) * z

    ubuf_ref[0:U_HIST, :] = ubuf_ref[ts:ts + U_HIST, :]
    zbuf_ref[0:Z_HIST, :] = zbuf_ref[ts:ts + Z_HIST, :]

    u_ref[0] = u.astype(BF16)
    z_ref[0] = z.astype(BF16)


def _conv_mixer(x, g_in, w_in, dw_w, dw_b, ln_g, ln_b, sc_w, *, ts=512):
    bsz, s, d = x.shape
    full = lambda a: pl.BlockSpec(a.shape, lambda b, i: (0,) * a.ndim)
    return pl.pallas_call(
        functools.partial(_conv_kernel, ts=ts),
        grid=(bsz, s // ts),
        in_specs=[pl.BlockSpec((1, ts, d), lambda b, i: (b, i, 0)),
                  full(g_in), full(w_in), full(dw_w), full(dw_b), full(ln_g), full(ln_b),
                  full(sc_w)],
        out_specs=[pl.BlockSpec((1, ts, CONV_CH), lambda b, i: (b, i, 0)),
                   pl.BlockSpec((1, ts, SC_CH), lambda b, i: (b, i, 0))],
        out_shape=[jax.ShapeDtypeStruct((bsz, s, CONV_CH), BF16),
                   jax.ShapeDtypeStruct((bsz, s, SC_CH), BF16)],
        scratch_shapes=[pltpu.VMEM((U_HIST + ts, CONV_CH), F32),
                        pltpu.VMEM((Z_HIST + ts, SC_CH), F32),
                        pltpu.VMEM((SUBLANES - 1, ts + U_HIST - SUBLANES, CONV_CH), F32),
                        pltpu.VMEM((ts, CONV_CH), F32)],
        compiler_params=pltpu.CompilerParams(
            dimension_semantics=("arbitrary", "arbitrary"), vmem_limit_bytes=VMEM_LIMIT),
        name="conv_mixer",
    )(x, g_in, w_in, dw_w, dw_b, ln_g, ln_b, sc_w)


def _t5_bucket(rel):
    nb = NUM_BUCKETS // 2
    ret = jnp.where(rel > 0, nb, 0)
    n = jnp.abs(rel)
    max_exact = nb // 2
    nf = jnp.maximum(n, 1).astype(jnp.float32)
    large = max_exact + (jnp.log(nf / max_exact) / math.log(MAX_DISTANCE / max_exact)
                         * (nb - max_exact)).astype(jnp.int32)
    large = jnp.minimum(large, nb - 1)
    return ret + jnp.where(n < max_exact, n, large)


REL_LO = -(TQ + TK - 1)
REL_LEN = TK - REL_LO
REL_LANES = -(-REL_LEN // LANES) * LANES


def _bias_kernel(fr_ref, a_ref, b_ref):
    krow = lax.broadcasted_iota(I32, (TK, TQ), 0)
    qcol = lax.broadcasted_iota(I32, (TK, TQ), 1)
    admissible = (krow // CHUNK) <= (qcol // CHUNK)
    zeros = jnp.zeros((TK, TQ), F32)
    for h in range(A_HEADS + B_HEADS):
        table = jnp.broadcast_to(fr_ref[h], (TK, REL_LANES))
        blocks = []
        for shift in (TK, 0):
            s0 = (-shift - REL_LO - (REL_LEN - 1)) % REL_LANES
            blocks.append(pltpu.roll(table, s0, 1, stride=1, stride_axis=0)[:, 0:TQ])
        prev, diag = blocks
        if h < A_HEADS:
            a_ref[0, h], a_ref[1, h], a_ref[2, h] = zeros, prev, jnp.where(admissible, diag, NEG)
        else:
            g = h - A_HEADS
            b_ref[0, g], b_ref[1, g], b_ref[2, g] = zeros, prev, diag


def _bias_tables(rel_bias):
    rel = jnp.arange(REL_LO, TK, dtype=jnp.int32)
    far_bucket = NUM_BUCKETS // 2 - 1
    f = (rel_bias[_t5_bucket(rel)].astype(F32) - rel_bias[far_bucket].astype(F32)[None, :]).T
    fr = jnp.pad(f[:, ::-1] * LOG2E, ((0, 0), (0, REL_LANES - REL_LEN)))[:, None, :]
    heads = A_HEADS + B_HEADS
    return pl.pallas_call(
        _bias_kernel,
        grid=(1,),
        in_specs=[pl.BlockSpec((heads, 1, REL_LANES), lambda i: (0, 0, 0))],
        out_specs=[pl.BlockSpec((3, A_HEADS, TK, TQ), lambda i: (0, 0, 0, 0)),
                   pl.BlockSpec((3, B_HEADS, TK, TQ), lambda i: (0, 0, 0, 0))],
        out_shape=[jax.ShapeDtypeStruct((3, A_HEADS, TK, TQ), F32),
                   jax.ShapeDtypeStruct((3, B_HEADS, TK, TQ), F32)],
        compiler_params=pltpu.CompilerParams(
            dimension_semantics=("arbitrary",), vmem_limit_bytes=VMEM_LIMIT),
        name="bias_tables",
    )(fr)


def _attn_weights(w_in):
    o = np.cumsum([0, 512, 512, 512, 512, 64, 64, 256, 32, 8])
    aq, ak, av, bq, bk, bv, iq, ik, iw = [w_in[:, o[n]:o[n + 1]] for n in range(9)]
    aq = aq * (A_HEAD_DIM ** -0.5 * LOG2E)
    bq = bq * (B_HEAD_DIM ** -0.5 * LOG2E)
    w = jnp.concatenate([aq, ak, bq, iq, bk, bk, ik, ik, ik, ik], axis=1).astype(BF16)
    pad = jnp.zeros((w_in.shape[0], VT_ROWS - ROW_IW - IDX_HEADS), w_in.dtype)
    wt = jnp.concatenate([av, bv, iw, pad], axis=1).T.astype(BF16)
    return w, wt


def kernel(x, rel_bias, norm_g, w_mlp_up, w_mlp_down, attn_w_in, attn_w_out, diff_lambda,
           diff_subln_g, conv_w_in, conv_w_out, conv_dw_w, conv_dw_b, conv_ln_g, conv_ln_b,
           sconv_w):
    bsz, s, d = x.shape
    depth = norm_g.shape[0]
    top_k = min(TOPK_MAX, s // 4)
    row = lambda v: v.reshape(1, -1)
    a_bias, b_bias = _bias_tables(rel_bias)
    w_up, w_down = w_mlp_up.astype(BF16), w_mlp_down.astype(BF16)
    for i in range(depth):
        j = i // 2
        if i % 2 == 0:
            lambda_init = 0.8 - 0.6 * math.exp(-0.3 * i)
            w, wt = _attn_weights(attn_w_in[j])
            proj, avt, bvt, iwt = _attn_in(x.reshape(bsz, s, d), row(norm_g[i, 0]), w, wt)
            ya, yb = _attention(proj, avt, bvt, iwt, diff_lambda[j], a_bias, b_bias,
                                diff_subln_g[j].reshape(-1, 1), lambda_init=lambda_init,
                                top_k=top_k)
            w_out = attn_w_out[j]
        else:
            ya, yb = _conv_mixer(x.reshape(bsz, s, d), row(norm_g[i, 0]),
                                 conv_w_in[j].astype(BF16), conv_dw_w[j], row(conv_dw_b[j]),
                                 row(conv_ln_g[j]), row(conv_ln_b[j]), sconv_w[j])
            w_out = conv_w_out[j]
        x = _layer_tail(ya.reshape(bsz * s, -1), yb.reshape(bsz * s, -1), w_out.astype(BF16),
                        x.reshape(bsz * s, d), row(norm_g[i, 1]), row(norm_g[i, 2]),
                        w_up, w_down, i, row(norm_g[i, 3]))
    return x.reshape(bsz, s, d)
```

```python
import functools
import math

import numpy as np
import jax
import jax.numpy as jnp
from jax import lax
from jax.experimental import pallas as pl
from jax.experimental.pallas import tpu as pltpu

F32 = jnp.float32
BF16 = jnp.bfloat16
I32 = jnp.int32

CHUNK = 64
NUM_BUCKETS = 32
MAX_DISTANCE = 128
EPS = 1e-6
NEG = -1e30
LOG2E = math.log2(math.e)
A_HEADS = 4
A_HEAD_DIM = 64
A_V_DIM = 2 * A_HEAD_DIM
B_HEADS = 8
B_HEAD_DIM = 64
IDX_HEADS = 8
IDX_DIM = 32
TOPK_MAX = 256
CONV_CH = 512
CONV_WIDTH = 31
SC_CH = 512
SC_WIDTH = 3

LANES = 128
SUBLANES = 8
VMEM_LIMIT = 56 * 1024 * 1024

TQ = 256
TK = 256
COL_AQ, COL_AK, COL_BQ, COL_IQ, COL_KB, COL_IK, PROJ_W = 0, 512, 1024, 1536, 1792, 1920, 2048
ROW_AV, ROW_BV, ROW_IW, VT_ROWS = 0, 512, 576, 640
ONES_ROWS = 16
INT_MIN = -2 ** 31
KEY_NEG_INF = INT_MIN + 0x7FFFFF


def _rms(x, g):
    return x * lax.rsqrt(jnp.mean(x * x, axis=-1, keepdims=True) + EPS) * g


def _dot_nt(a, b):
    return lax.dot_general(a, b, (((1,), (1,)), ((), ())), preferred_element_type=F32)


def _dot(a, b):
    return jnp.dot(a, b, preferred_element_type=F32)


def _lane_band(x, lo, width):
    lane = lax.broadcasted_iota(I32, x.shape, 1)
    return jnp.where((lane >= lo) & (lane < lo + width), x, jnp.zeros_like(x))


KEY_BITS = 32


def _bit_planes(keys):
    assert keys.shape[0] == KEY_BITS * SUBLANES
    u = keys ^ INT_MIN
    a = [u[SUBLANES * r:SUBLANES * (r + 1), :] for r in range(KEY_BITS)]
    j, m = KEY_BITS // 2, (1 << (KEY_BITS // 2)) - 1
    while j:
        mask = np.int32(np.uint32(m))
        k = 0
        while k < KEY_BITS:
            t = (a[k] ^ lax.shift_right_logical(a[k + j], jnp.int32(j))) & mask
            a[k] = a[k] ^ t
            a[k + j] = a[k + j] ^ lax.shift_left(t, jnp.int32(j))
            k = (k + j + 1) & ~j
        j >>= 1
        m = (m ^ (m << j)) & 0xFFFFFFFF
    return a


def _index_bits(seq):
    bits = seq.bit_length() - 1
    assert seq == 1 << bits and seq >= KEY_BITS * SUBLANES
    return bits


def _index_planes(nkb):
    shape = (nkb, SUBLANES, TQ)
    blk = lax.broadcasted_iota(I32, shape, 0)
    sub = lax.broadcasted_iota(I32, shape, 1)
    sub_bits = SUBLANES.bit_length() - 1
    word_bits = KEY_BITS.bit_length() - 1
    planes = []
    for b in range(_index_bits(nkb * TK) - 1, -1, -1):
        if b >= sub_bits + word_bits:
            on = ((nkb - 1 - blk) >> (b - sub_bits - word_bits)) & 1
            planes.append(jnp.where(on == 1, -1, 0))
        elif b >= sub_bits:
            t = b - sub_bits
            pattern = sum(1 << k for k in range(KEY_BITS) if (k >> t) & 1)
            planes.append(jnp.full(shape, np.int32(np.uint32(pattern)), I32))
        else:
            on = ((SUBLANES - 1 - sub) >> b) & 1
            planes.append(jnp.where(on == 1, -1, 0))
    return planes


def _bias_kind(j, i):
    return jnp.clip(j - i + 2, 0, 2)


def _key_block_stream(i, n_chains, s_ref, acc_ref, logits, v_aug):
    chains = range(n_chains)

    def stage(j, c, kind):
        s = logits(j, c, kind)
        s_ref[c] = s
        return jnp.max(s, axis=0, keepdims=True)

    def absorb(j, c, m_old, block_max):
        m_new = jnp.maximum(m_old, block_max)
        alpha = jnp.exp2(m_old - m_new)
        p = jnp.exp2(s_ref[c] - m_new).astype(BF16)
        acc_ref[c] = acc_ref[c] * alpha + _dot(v_aug(j, c), p)
        return m_new

    def start():
        acc_ref[...] = jnp.zeros_like(acc_ref)
        return (jnp.full((1, TQ), NEG, F32),) * n_chains, tuple(stage(0, c, 'any') for c in chains)

    def step(j, carry, kind):
        ms, bms = carry
        out = [(absorb(j, c, ms[c], bms[c]), stage(j + 1, c, kind)) for c in chains]
        return tuple(o[0] for o in out), tuple(o[1] for o in out)

    def finish(carry):
        ms, bms = carry
        for c in chains:
            absorb(i, c, ms[c], bms[c])

    return start, step, finish


def _run_key_block_streams(i, streams):
    def body(j, carries, kind):
        return tuple(step(j, c, kind) for (_, step, _), c in zip(streams, carries))

    carries = tuple(start() for start, _, _ in streams)
    n_far = jnp.maximum(i - 2, 0)
    carries = lax.fori_loop(0, n_far, functools.partial(body, kind='far'), carries)
    carries = lax.fori_loop(n_far, i, functools.partial(body, kind='near'), carries)
    for (_, _, finish), c in zip(streams, carries):
        finish(c)


def _attn_in_kernel(x_ref, g_ref, w_ref, wt_ref, proj_ref, avt_ref, bvt_ref, iwt_ref, *, tm):
    hb = _rms(x_ref[0], g_ref[...]).astype(BF16)
    proj_ref[0] = _dot(hb, w_ref[...]).astype(BF16)
    t = _dot_nt(wt_ref[...], hb)
    ones = jnp.ones((ONES_ROWS, TK), BF16)
    for c in range(tm // TK):
        cols = slice(c * TK, (c + 1) * TK)
        for h in range(A_HEADS):
            r = ROW_AV + h * A_V_DIM
            avt_ref[0, c, h, 0:A_V_DIM, :] = t[r:r + A_V_DIM, cols].astype(BF16)
            avt_ref[0, c, h, A_V_DIM:A_V_DIM + ONES_ROWS, :] = ones
        bvt_ref[0, c, 0:B_HEAD_DIM, :] = t[ROW_BV:ROW_IW, cols].astype(BF16)
        bvt_ref[0, c, B_HEAD_DIM:B_HEAD_DIM + ONES_ROWS, :] = ones
    iwt_ref[0] = t[ROW_IW:ROW_IW + IDX_HEADS, :] * (IDX_HEADS ** -0.5 * IDX_DIM ** -0.5)


def _attn_in(x, g, w, wt, *, tm=512):
    bsz, s, d = x.shape
    av_blk = (A_HEADS, A_V_DIM + ONES_ROWS, TK)
    bv_blk = (B_HEAD_DIM + ONES_ROWS, TK)
    return pl.pallas_call(
        functools.partial(_attn_in_kernel, tm=tm),
        grid=(bsz, s // tm),
        in_specs=[
            pl.BlockSpec((1, tm, d), lambda b, i: (b, i, 0)),
            pl.BlockSpec((1, d), lambda b, i: (0, 0)),
            pl.BlockSpec((d, PROJ_W), lambda b, i: (0, 0)),
            pl.BlockSpec((VT_ROWS, d), lambda b, i: (0, 0)),
        ],
        out_specs=[
            pl.BlockSpec((1, tm, PROJ_W), lambda b, i: (b, i, 0)),
            pl.BlockSpec((1, tm // TK) + av_blk, lambda b, i: (b, i, 0, 0, 0)),
            pl.BlockSpec((1, tm // TK) + bv_blk, lambda b, i: (b, i, 0, 0)),
            pl.BlockSpec((1, IDX_HEADS, tm), lambda b, i: (b, 0, i)),
        ],
        out_shape=[
            jax.ShapeDtypeStruct((bsz, s, PROJ_W), BF16),
            jax.ShapeDtypeStruct((bsz, s // TK) + av_blk, BF16),
            jax.ShapeDtypeStruct((bsz, s // TK) + bv_blk, BF16),
            jax.ShapeDtypeStruct((bsz, IDX_HEADS, s), F32),
        ],
        compiler_params=pltpu.CompilerParams(
            dimension_semantics=("arbitrary", "arbitrary"), vmem_limit_bytes=VMEM_LIMIT),
        name="attn_in",
    )(x, g, w, wt)


def _key_rows(ref, j):
    return ref[0, pl.ds(pl.multiple_of(j * TK, TK), TK), :]


def _indexer(i, iq_ref, ik_ref, iwt_ref, score_ref, key_ref, plane_ref):
    iq = iq_ref[0]
    per_blk = LANES // IDX_DIM
    iqh = [_lane_band(iq[:, (h // per_blk) * LANES:(h // per_blk + 1) * LANES],
                      (h % per_blk) * IDX_DIM, IDX_DIM) for h in range(IDX_HEADS)]
    w = iwt_ref[0]

    def stage_scores(j):
        ikb = _key_rows(ik_ref, j)
        half = TK // 2
        for r in (0, half):
            acc = jnp.zeros((half, TQ), F32)
            for h in range(IDX_HEADS):
                acc = acc + jnp.maximum(_dot_nt(ikb[r:r + half], iqh[h]), 0.0) * w[h:h + 1, :]
            score_ref[r:r + half, :] = acc

    def staged_keys():
        acc = score_ref[...]
        acc = jnp.where(acc == 0.0, 0.0, acc)
        bits = lax.bitcast_convert_type(acc, I32)
        return bits ^ ((bits >> 31) & 0x7FFFFFFF)

    def store_keys(j, keys):
        key_ref[j] = keys
        for p, plane in enumerate(_bit_planes(keys)):
            plane_ref[p, j] = plane

    def per_block(j):
        keys = staged_keys()
        stage_scores(j + 1)
        store_keys(j, keys)

    def finish():
        krow = lax.broadcasted_iota(I32, (TK, TQ), 0)
        qcol = lax.broadcasted_iota(I32, (TK, TQ), 1)
        admissible = (krow // CHUNK) <= (qcol // CHUNK)
        store_keys(i, jnp.where(admissible, staged_keys(), KEY_NEG_INF))

    stage_scores(0)
    return per_block, finish


def _select_mask(i, key_ref, plane_ref, mask_ref, *, top_k, seq):
    nkb = seq // TK
    blk = lax.broadcasted_iota(I32, (nkb, SUBLANES, TQ), 0)
    alive0 = jnp.where(blk <= i, -1, 0)

    def bit_body(p, carry, n_bits):
        alive, n_gt, thr_u = carry
        hit = alive & plane_ref[p]
        pc = jnp.sum(lax.population_count(hit), axis=0)
        n_hit = jnp.sum(pc.astype(F32), axis=0, keepdims=True)
        take = (n_gt + n_hit) >= top_k
        thr_u = thr_u | jnp.where(take, lax.shift_left(jnp.int32(1), n_bits - 1 - p), 0)
        n_gt = n_gt + jnp.where(take, 0.0, n_hit)
        alive = jnp.where(take, hit, alive ^ hit)
        return alive, n_gt, thr_u

    zero = jnp.zeros((1, TQ), I32)
    alive, n_gt, thr_u = lax.fori_loop(
        0, KEY_BITS, lambda p, c: bit_body(p, c, KEY_BITS), (alive0, jnp.zeros((1, TQ), F32), zero))
    idx_bits = _index_bits(seq)
    _, _, rev_idx = lax.fori_loop(
        KEY_BITS, KEY_BITS + idx_bits,
        lambda p, c: bit_body(p, c, KEY_BITS + idx_bits), (alive, n_gt, zero))
    thr = thr_u ^ INT_MIN
    lim = seq - rev_idx
    lim = jnp.where(thr > KEY_NEG_INF, lim, seq)
    thr = jnp.maximum(thr, KEY_NEG_INF + 1)
    krow = lax.broadcasted_iota(I32, (TK, TQ), 0)

    def mask_body(j, c):
        late = jnp.where(krow >= lim - j * TK, 1, 0)
        mask_ref[j] = jnp.where(key_ref[j] - late >= thr, 0.0, NEG)
        return c

    lax.fori_loop(0, i + 1, mask_body, 0)


def _attn_kernel(lam_ref, aq_ref, ak_ref, avt_ref, abias_ref, g_ref,
                 bq_ref, iq_ref, kb_ref, ik_ref, bvt_ref, iwt_ref, bbias_ref,
                 ya_ref, yb_ref,
                 sa_ref, sb_ref, aacc_ref, bacc_ref, ot_ref,
                 score_ref, key_ref, plane_ref, mask_ref, *, lambda_init, top_k, seq):
    i = pl.program_id(1)
    nkb = seq // TK
    n_maps = 2 * A_HEADS

    @pl.when((i == 0) & (pl.program_id(0) == 0))
    def _():
        plane_ref[0:KEY_BITS] = jnp.zeros((KEY_BITS, nkb, SUBLANES, TQ), I32)
        for t, plane in enumerate(_index_planes(nkb)):
            plane_ref[KEY_BITS + t] = plane

    index_block, finish_index = _indexer(i, iq_ref, ik_ref, iwt_ref, score_ref, key_ref, plane_ref)

    def index_body(j, c):
        index_block(j)
        return c

    lax.fori_loop(0, i, index_body, 0)
    finish_index()
    _select_mask(i, key_ref, plane_ref, mask_ref, top_k=top_k, seq=seq)

    lv = lam_ref[...]
    lam = (jnp.exp(jnp.sum(lv[0:1] * lv[1:2], axis=-1, keepdims=True))
           - jnp.exp(jnp.sum(lv[2:3] * lv[3:4], axis=-1, keepdims=True)) + lambda_init)

    q = aq_ref[0]
    qm = []
    for h in range(A_HEADS):
        q12 = q[:, h * LANES:(h + 1) * LANES]
        qm += [_lane_band(q12, 0, A_HEAD_DIM), _lane_band(q12, A_HEAD_DIM, A_HEAD_DIM)]

    def a_logits(j, c, kind):
        h = c // 2
        kb = ak_ref[0, pl.ds(pl.multiple_of(j * TK, TK), TK), h * LANES:(h + 1) * LANES]
        s = _dot_nt(kb, qm[c])
        return s if kind == 'far' else s + abias_ref[_bias_kind(j, i), h]

    qb = bq_ref[0]
    qh = [_lane_band(qb[:, (h // 2) * LANES:(h // 2 + 1) * LANES],
                     (h % 2) * B_HEAD_DIM, B_HEAD_DIM) for h in range(B_HEADS)]

    def b_logits(j, h, kind):
        s = _dot_nt(_key_rows(kb_ref, j), qh[h]) + mask_ref[j]
        return s if kind == 'far' else s + bbias_ref[_bias_kind(j, i), h]

    _run_key_block_streams(i, [
        _key_block_stream(i, n_maps, sa_ref, aacc_ref, a_logits, lambda j, c: avt_ref[0, j, c // 2]),
        _key_block_stream(i, B_HEADS, sb_ref, bacc_ref, b_logits, lambda j, h: bvt_ref[0, j])])

    for h in range(A_HEADS):
        a1 = aacc_ref[2 * h]
        a2 = aacc_ref[2 * h + 1]
        a = (a1[0:A_V_DIM] / a1[A_V_DIM:A_V_DIM + 1]
             - lam * (a2[0:A_V_DIM] / a2[A_V_DIM:A_V_DIM + 1]))
        y = a * lax.rsqrt(jnp.mean(a * a, axis=0, keepdims=True) + EPS) * g_ref[...]
        ot_ref[h * A_V_DIM:(h + 1) * A_V_DIM, :] = y * (1.0 - lambda_init)
    ya_ref[0] = ot_ref[...].T.astype(BF16)

    for h in range(B_HEADS):
        a = bacc_ref[h]
        ot_ref[h * B_HEAD_DIM:(h + 1) * B_HEAD_DIM, :] = (
            a[0:B_HEAD_DIM] / a[B_HEAD_DIM:B_HEAD_DIM + 1])
    yb_ref[0] = ot_ref[...].T.astype(BF16)


def _attention(proj, avt, bvt, iwt, lam_vecs, a_bias, b_bias, subln_g_col, *, lambda_init, top_k):
    bsz, s, _ = proj.shape
    nq = s // TQ
    nkb = s // TK
    a_width = A_HEADS * A_V_DIM
    b_width = B_HEADS * B_HEAD_DIM
    assert a_width == b_width
    iq_w = IDX_HEADS * IDX_DIM
    once = pl.Buffered(1)
    return pl.pallas_call(
        functools.partial(_attn_kernel, lambda_init=lambda_init, top_k=top_k, seq=s),
        grid=(bsz, nq),
        in_specs=[
            pl.BlockSpec((4, A_HEAD_DIM), lambda b, i: (0, 0)),
            pl.BlockSpec((1, TQ, a_width), lambda b, i: (b, i, COL_AQ // a_width)),
            pl.BlockSpec((1, s, a_width), lambda b, i: (b, 0, COL_AK // a_width)),
            pl.BlockSpec((1, nkb, A_HEADS, A_V_DIM + ONES_ROWS, TK), lambda b, i: (b, 0, 0, 0, 0)),
            pl.BlockSpec((3, A_HEADS, TK, TQ), lambda b, i: (0, 0, 0, 0), once),
            pl.BlockSpec((A_V_DIM, 1), lambda b, i: (0, 0)),
            pl.BlockSpec((1, TQ, b_width), lambda b, i: (b, i, COL_BQ // b_width)),
            pl.BlockSpec((1, TQ, iq_w), lambda b, i: (b, i, COL_IQ // iq_w)),
            pl.BlockSpec((1, s, LANES), lambda b, i: (b, 0, COL_KB // LANES), once),
            pl.BlockSpec((1, s, LANES), lambda b, i: (b, 0, COL_IK // LANES), once),
            pl.BlockSpec((1, nkb, B_HEAD_DIM + ONES_ROWS, TK), lambda b, i: (b, 0, 0, 0), once),
            pl.BlockSpec((1, IDX_HEADS, TQ), lambda b, i: (b, 0, i)),
            pl.BlockSpec((3, B_HEADS, TK, TQ), lambda b, i: (0, 0, 0, 0), once),
        ],
        out_specs=[pl.BlockSpec((1, TQ, a_width), lambda b, i: (b, i, 0)),
                   pl.BlockSpec((1, TQ, b_width), lambda b, i: (b, i, 0))],
        out_shape=[jax.ShapeDtypeStruct((bsz, s, a_width), BF16),
                   jax.ShapeDtypeStruct((bsz, s, b_width), BF16)],
        scratch_shapes=[
            pltpu.VMEM((2 * A_HEADS, TK, TQ), F32),
            pltpu.VMEM((B_HEADS, TK, TQ), F32),
            pltpu.VMEM((2 * A_HEADS, A_V_DIM + ONES_ROWS, TQ), F32),
            pltpu.VMEM((B_HEADS, B_HEAD_DIM + ONES_ROWS, TQ), F32),
            pltpu.VMEM((a_width, TQ), F32),
            pltpu.VMEM((TK, TQ), F32),
            pltpu.VMEM((nkb, TK, TQ), I32),
            pltpu.VMEM((KEY_BITS + _index_bits(s), nkb, SUBLANES, TQ), I32),
            pltpu.VMEM((nkb, TK, TQ), F32),
        ],
        compiler_params=pltpu.CompilerParams(
            dimension_semantics=("arbitrary", "arbitrary"), vmem_limit_bytes=VMEM_LIMIT),
        name="attention",
    )(lam_vecs, proj, proj, avt, a_bias, subln_g_col, proj, proj, proj, proj, bvt, iwt, b_bias)


TAIL_ROWS = 256


def _tail_kernel(ya_ref, yb_ref, wo_ref, x_ref, g_mix_ref, g_in_ref, wu_ref, wd_ref, g_out_ref,
                 o_ref, x1_ref, h_ref, acc_ref):
    f = pl.program_id(1)
    last = pl.num_programs(1) - 1
    chunks = [slice(r, r + TAIL_ROWS) for r in range(0, x_ref.shape[0], TAIL_ROWS)]

    def mlp_part(rows):
        u = jnp.maximum(_dot(h_ref[rows, :], wu_ref[...]), 0.0)
        return _dot((u * u).astype(BF16), wd_ref[...])

    @pl.when(f == 0)
    def _():
        half = ya_ref.shape[-1]
        for rows in chunks:
            m = (_dot(ya_ref[rows, :], wo_ref[0:half, :])
                 + _dot(yb_ref[rows, :], wo_ref[half:2 * half, :]))
            x1 = x_ref[rows, :] + _rms(m, g_mix_ref[...])
            x1_ref[rows, :] = x1
            h_ref[rows, :] = _rms(x1, g_in_ref[...]).astype(BF16)
        acc_ref[...] = mlp_part(slice(None))

    @pl.when((f > 0) & (f < last))
    def _():
        acc_ref[...] += mlp_part(slice(None))

    @pl.when(f == last)
    def _():
        for rows in chunks:
            y = acc_ref[rows, :] + mlp_part(rows)
            o_ref[rows, :] = x1_ref[rows, :] + _rms(y, g_out_ref[...])


def _layer_tail(ya, yb, wo, x, g_mix, g_in, wu, wd, layer, g_out, *, tm=1024, tf=1024):
    n, d = x.shape
    half = ya.shape[-1]
    ff = wu.shape[-1]
    assert ff // tf >= 2 and tm % TAIL_ROWS == 0
    return pl.pallas_call(
        _tail_kernel,
        grid=(n // tm, ff // tf),
        in_specs=[
            pl.BlockSpec((tm, half), lambda i, f: (i, 0)),
            pl.BlockSpec((tm, half), lambda i, f: (i, 0)),
            pl.BlockSpec((2 * half, d), lambda i, f: (0, 0)),
            pl.BlockSpec((tm, d), lambda i, f: (i, 0)),
            pl.BlockSpec((1, d), lambda i, f: (0, 0)),
            pl.BlockSpec((1, d), lambda i, f: (0, 0)),
            pl.BlockSpec((None, d, tf), lambda i, f: (layer, 0, f)),
            pl.BlockSpec((None, tf, d), lambda i, f: (layer, f, 0)),
            pl.BlockSpec((1, d), lambda i, f: (0, 0)),
        ],
        out_specs=pl.BlockSpec((tm, d), lambda i, f: (i, 0)),
        out_shape=jax.ShapeDtypeStruct((n, d), F32),
        scratch_shapes=[pltpu.VMEM((tm, d), F32), pltpu.VMEM((tm, d), BF16),
                        pltpu.VMEM((tm, d), F32)],
        compiler_params=pltpu.CompilerParams(
            dimension_semantics=("arbitrary", "arbitrary"), vmem_limit_bytes=VMEM_LIMIT),
        name="layer_tail",
    )(ya, yb, wo, x, g_mix, g_in, wu, wd, g_out)


CONV_ROWS = 64
U_HIST = 32
Z_HIST = 8


def _conv_kernel(x_ref, g_in_ref, w_in_ref, dw_w_ref, dw_b_ref, ln_g_ref, ln_b_ref, sc_w_ref,
                 u_ref, z_ref, ubuf_ref, zbuf_ref, shift_ref, conv_ref, *, ts):
    @pl.when(pl.program_id(1) == 0)
    def _():
        ubuf_ref[0:U_HIST, :] = jnp.zeros((U_HIST, CONV_CH), F32)
        zbuf_ref[0:Z_HIST, :] = jnp.zeros((Z_HIST, SC_CH), F32)

    x = x_ref[0]
    hb = _rms(x, g_in_ref[...]).astype(BF16)
    c = CONV_CH
    ca = _dot(hb, w_in_ref[:, 0:c])
    cg = _dot(hb, w_in_ref[:, c:2 * c])
    ubuf_ref[U_HIST:U_HIST + ts, :] = ca * jax.nn.sigmoid(cg)
    dc = _dot(hb, w_in_ref[:, 3 * c:4 * c])
    dh = _dot(hb, w_in_ref[:, 4 * c:5 * c])
    zbuf_ref[Z_HIST:Z_HIST + ts, :] = dc * dh

    first_off = U_HIST - (CONV_WIDTH - 1)
    shift_rows = shift_ref.shape[1]
    for r in range(1, SUBLANES):
        shift_ref[r - 1] = ubuf_ref[r:r + shift_rows, :]

    for base in range(0, ts, CONV_ROWS):
        acc = jnp.zeros((CONV_ROWS, c), F32)
        for j in range(CONV_WIDTH):
            r, a = (first_off + j) % SUBLANES, (first_off + j) // SUBLANES
            rows = pl.ds(base + SUBLANES * a, CONV_ROWS)
            taps = ubuf_ref[rows, :] if r == 0 else shift_ref[r - 1, rows, :]
            acc = acc + dw_w_ref[j:j + 1, :] * taps
        conv_ref[base:base + CONV_ROWS, :] = acc
    u = conv_ref[...] + dw_b_ref[...]
    mu = jnp.mean(u, axis=-1, keepdims=True)
    ctr = u - mu
    var = jnp.mean(ctr * ctr, axis=-1, keepdims=True)
    u = ctr * lax.rsqrt(var + EPS) * ln_g_ref[...] + ln_b_ref[...]
    u = u * jax.nn.sigmoid(u)

    z = jnp.zeros((ts, c), F32)
    for j in range(SC_WIDTH):
        off = Z_HIST - (SC_WIDTH - 1) + j
        z = z + sc_w_ref[j:j + 1, :] * zbuf_ref[off:off + ts, :]
    z = _dot(hb, w_in_ref[:, 2 * c:3 * c]) * z

    ubuf_ref[0:U_HIST, :] = ubuf_ref[ts:ts + U_HIST, :]
    zbuf_ref[0:Z_HIST, :] = zbuf_ref[ts:ts + Z_HIST, :]

    u_ref[0] = u.astype(BF16)
    z_ref[0] = z.astype(BF16)


def _conv_mixer(x, g_in, w_in, dw_w, dw_b, ln_g, ln_b, sc_w, *, ts=512):
    bsz, s, d = x.shape
    full = lambda a: pl.BlockSpec(a.shape, lambda b, i: (0,) * a.ndim)
    return pl.pallas_call(
        functools.partial(_conv_kernel, ts=ts),
        grid=(bsz, s // ts),
        in_specs=[pl.BlockSpec((1, ts, d), lambda b, i: (b, i, 0)),
                  full(g_in), full(w_in), full(dw_w), full(dw_b), full(ln_g), full(ln_b),
                  full(sc_w)],
        out_specs=[pl.BlockSpec((1, ts, CONV_CH), lambda b, i: (b, i, 0)),
                   pl.BlockSpec((1, ts, SC_CH), lambda b, i: (b, i, 0))],
        out_shape=[jax.ShapeDtypeStruct((bsz, s, CONV_CH), BF16),
                   jax.ShapeDtypeStruct((bsz, s, SC_CH), BF16)],
        scratch_shapes=[pltpu.VMEM((U_HIST + ts, CONV_CH), F32),
                        pltpu.VMEM((Z_HIST + ts, SC_CH), F32),
                        pltpu.VMEM((SUBLANES - 1, ts + U_HIST - SUBLANES, CONV_CH), F32),
                        pltpu.VMEM((ts, CONV_CH), F32)],
        compiler_params=pltpu.CompilerParams(
            dimension_semantics=("arbitrary", "arbitrary"), vmem_limit_bytes=VMEM_LIMIT),
        name="conv_mixer",
    )(x, g_in, w_in, dw_w, dw_b, ln_g, ln_b, sc_w)


def _t5_bucket(rel):
    nb = NUM_BUCKETS // 2
    ret = jnp.where(rel > 0, nb, 0)
    n = jnp.abs(rel)
    max_exact = nb // 2
    nf = jnp.maximum(n, 1).astype(jnp.float32)
    large = max_exact + (jnp.log(nf / max_exact) / math.log(MAX_DISTANCE / max_exact)
                         * (nb - max_exact)).astype(jnp.int32)
    large = jnp.minimum(large, nb - 1)
    return ret + jnp.where(n < max_exact, n, large)


REL_LO = -(TQ + TK - 1)
REL_LEN = TK - REL_LO
REL_LANES = -(-REL_LEN // LANES) * LANES


def _bias_kernel(fr_ref, a_ref, b_ref):
    krow = lax.broadcasted_iota(I32, (TK, TQ), 0)
    qcol = lax.broadcasted_iota(I32, (TK, TQ), 1)
    admissible = (krow // CHUNK) <= (qcol // CHUNK)
    zeros = jnp.zeros((TK, TQ), F32)
    for h in range(A_HEADS + B_HEADS):
        table = jnp.broadcast_to(fr_ref[h], (TK, REL_LANES))
        blocks = []
        for shift in (TK, 0):
            s0 = (-shift - REL_LO - (REL_LEN - 1)) % REL_LANES
            blocks.append(pltpu.roll(table, s0, 1, stride=1, stride_axis=0)[:, 0:TQ])
        prev, diag = blocks
        if h < A_HEADS:
            a_ref[0, h], a_ref[1, h], a_ref[2, h] = zeros, prev, jnp.where(admissible, diag, NEG)
        else:
            g = h - A_HEADS
            b_ref[0, g], b_ref[1, g], b_ref[2, g] = zeros, prev, diag


def _bias_tables(rel_bias):
    rel = jnp.arange(REL_LO, TK, dtype=jnp.int32)
    far_bucket = NUM_BUCKETS // 2 - 1
    f = (rel_bias[_t5_bucket(rel)].astype(F32) - rel_bias[far_bucket].astype(F32)[None, :]).T
    fr = jnp.pad(f[:, ::-1] * LOG2E, ((0, 0), (0, REL_LANES - REL_LEN)))[:, None, :]
    heads = A_HEADS + B_HEADS
    return pl.pallas_call(
        _bias_kernel,
        grid=(1,),
        in_specs=[pl.BlockSpec((heads, 1, REL_LANES), lambda i: (0, 0, 0))],
        out_specs=[pl.BlockSpec((3, A_HEADS, TK, TQ), lambda i: (0, 0, 0, 0)),
                   pl.BlockSpec((3, B_HEADS, TK, TQ), lambda i: (0, 0, 0, 0))],
        out_shape=[jax.ShapeDtypeStruct((3, A_HEADS, TK, TQ), F32),
                   jax.ShapeDtypeStruct((3, B_HEADS, TK, TQ), F32)],
        compiler_params=pltpu.CompilerParams(
            dimension_semantics=("arbitrary",), vmem_limit_bytes=VMEM_LIMIT),
        name="bias_tables",
    )(fr)


def _attn_weights(w_in):
    o = np.cumsum([0, 512, 512, 512, 512, 64, 64, 256, 32, 8])
    aq, ak, av, bq, bk, bv, iq, ik, iw = [w_in[:, o[n]:o[n + 1]] for n in range(9)]
    aq = aq * (A_HEAD_DIM ** -0.5 * LOG2E)
    bq = bq * (B_HEAD_DIM ** -0.5 * LOG2E)
    w = jnp.concatenate([aq, ak, bq, iq, bk, bk, ik, ik, ik, ik], axis=1).astype(BF16)
    pad = jnp.zeros((w_in.shape[0], VT_ROWS - ROW_IW - IDX_HEADS), w_in.dtype)
    wt = jnp.concatenate([av, bv, iw, pad], axis=1).T.astype(BF16)
    return w, wt


def kernel(x, rel_bias, norm_g, w_mlp_up, w_mlp_down, attn_w_in, attn_w_out, diff_lambda,
           diff_subln_g, conv_w_in, conv_w_out, conv_dw_w, conv_dw_b, conv_ln_g, conv_ln_b,
           sconv_w):
    bsz, s, d = x.shape
    depth = norm_g.shape[0]
    top_k = min(TOPK_MAX, s // 4)
    row = lambda v: v.reshape(1, -1)
    a_bias, b_bias = _bias_tables(rel_bias)
    w_up, w_down = w_mlp_up.astype(BF16), w_mlp_down.astype(BF16)
    for i in range(depth):
        j = i // 2
        if i % 2 == 0:
            lambda_init = 0.8 - 0.6 * math.exp(-0.3 * i)
            w, wt = _attn_weights(attn_w_in[j])
            proj, avt, bvt, iwt = _attn_in(x.reshape(bsz, s, d), row(norm_g[i, 0]), w, wt)
            ya, yb = _attention(proj, avt, bvt, iwt, diff_lambda[j], a_bias, b_bias,
                                diff_subln_g[j].reshape(-1, 1), lambda_init=lambda_init,
                                top_k=top_k)
            w_out = attn_w_out[j]
        else:
            ya, yb = _conv_mixer(x.reshape(bsz, s, d), row(norm_g[i, 0]),
                                 conv_w_in[j].astype(BF16), conv_dw_w[j], row(conv_dw_b[j]),
                                 row(conv_ln_g[j]), row(conv_ln_b[j]), sconv_w[j])
            w_out = conv_w_out[j]
        x = _layer_tail(ya.reshape(bsz * s, -1), yb.reshape(bsz * s, -1), w_out.astype(BF16),
                        x.reshape(bsz * s, d), row(norm_g[i, 1]), row(norm_g[i, 2]),
                        w_up, w_down, i, row(norm_g[i, 3]))
    return x.reshape(bsz, s, d)
```

```python
import functools
import math

import numpy as np
import jax
import jax.numpy as jnp
from jax import lax
from jax.experimental import pallas as pl
from jax.experimental.pallas import tpu as pltpu

F32 = jnp.float32
BF16 = jnp.bfloat16
I32 = jnp.int32

CHUNK = 64
NUM_BUCKETS = 32
MAX_DISTANCE = 128
EPS = 1e-6
NEG = -1e30
LOG2E = math.log2(math.e)
A_HEADS = 4
A_HEAD_DIM = 64
A_V_DIM = 2 * A_HEAD_DIM
B_HEADS = 8
B_HEAD_DIM = 64
IDX_HEADS = 8
IDX_DIM = 32
TOPK_MAX = 256
CONV_CH = 512
CONV_WIDTH = 31
SC_CH = 512
SC_WIDTH = 3

LANES = 128
SUBLANES = 8
VMEM_LIMIT = 56 * 1024 * 1024

TQ = 256
TK = 256
COL_AQ, COL_AK, COL_BQ, COL_IQ, COL_KB, COL_IK, PROJ_W = 0, 512, 1024, 1536, 1792, 1920, 2048
ROW_AV, ROW_BV, ROW_IW, VT_ROWS = 0, 512, 576, 640
ONES_ROWS = 16
INT_MIN = -2 ** 31
KEY_NEG_INF = INT_MIN + 0x7FFFFF


def _rms(x, g):
    return x * lax.rsqrt(jnp.mean(x * x, axis=-1, keepdims=True) + EPS) * g


def _dot_nt(a, b):
    return lax.dot_general(a, b, (((1,), (1,)), ((), ())), preferred_element_type=F32)


def _dot(a, b):
    return jnp.dot(a, b, preferred_element_type=F32)


def _lane_band(x, lo, width):
    lane = lax.broadcasted_iota(I32, x.shape, 1)
    return jnp.where((lane >= lo) & (lane < lo + width), x, jnp.zeros_like(x))


KEY_BITS = 32


def _bit_planes(keys):
    assert keys.shape[0] == KEY_BITS * SUBLANES
    u = keys ^ INT_MIN
    a = [u[SUBLANES * r:SUBLANES * (r + 1), :] for r in range(KEY_BITS)]
    j, m = KEY_BITS // 2, (1 << (KEY_BITS // 2)) - 1
    while j:
        mask = np.int32(np.uint32(m))
        k = 0
        while k < KEY_BITS:
            t = (a[k] ^ lax.shift_right_logical(a[k + j], jnp.int32(j))) & mask
            a[k] = a[k] ^ t
            a[k + j] = a[k + j] ^ lax.shift_left(t, jnp.int32(j))
            k = (k + j + 1) & ~j
        j >>= 1
        m = (m ^ (m << j)) & 0xFFFFFFFF
    return a


def _index_bits(seq):
    bits = seq.bit_length() - 1
    assert seq == 1 << bits and seq >= KEY_BITS * SUBLANES
    return bits


def _index_planes(nkb):
    shape = (nkb, SUBLANES, TQ)
    blk = lax.broadcasted_iota(I32, shape, 0)
    sub = lax.broadcasted_iota(I32, shape, 1)
    sub_bits = SUBLANES.bit_length() - 1
    word_bits = KEY_BITS.bit_length() - 1
    planes = []
    for b in range(_index_bits(nkb * TK) - 1, -1, -1):
        if b >= sub_bits + word_bits:
            on = ((nkb - 1 - blk) >> (b - sub_bits - word_bits)) & 1
            planes.append(jnp.where(on == 1, -1, 0))
        elif b >= sub_bits:
            t = b - sub_bits
            pattern = sum(1 << k for k in range(KEY_BITS) if (k >> t) & 1)
            planes.append(jnp.full(shape, np.int32(np.uint32(pattern)), I32))
        else:
            on = ((SUBLANES - 1 - sub) >> b) & 1
            planes.append(jnp.where(on == 1, -1, 0))
    return planes


def _bias_kind(j, i):
    return jnp.clip(j - i + 2, 0, 2)


def _key_block_stream(i, n_chains, s_ref, acc_ref, logits, v_aug):
    chains = range(n_chains)

    def stage(j, c, kind):
        s = logits(j, c, kind)
        s_ref[c] = s
        return jnp.max(s, axis=0, keepdims=True)

    def absorb(j, c, m_old, block_max):
        m_new = jnp.maximum(m_old, block_max)
        alpha = jnp.exp2(m_old - m_new)
        p = jnp.exp2(s_ref[c] - m_new).astype(BF16)
        acc_ref[c] = acc_ref[c] * alpha + _dot(v_aug(j, c), p)
        return m_new

    def start():
        acc_ref[...] = jnp.zeros_like(acc_ref)
        return (jnp.full((1, TQ), NEG, F32),) * n_chains, tuple(stage(0, c, 'any') for c in chains)

    def step(j, carry, kind):
        ms, bms = carry
        out = [(absorb(j, c, ms[c], bms[c]), stage(j + 1, c, kind)) for c in chains]
        return tuple(o[0] for o in out), tuple(o[1] for o in out)

    def finish(carry):
        ms, bms = carry
        for c in chains:
            absorb(i, c, ms[c], bms[c])

    return start, step, finish


def _run_key_block_streams(i, streams):
    def body(j, carries, kind):
        return tuple(step(j, c, kind) for (_, step, _), c in zip(streams, carries))

    carries = tuple(start() for start, _, _ in streams)
    n_far = jnp.maximum(i - 2, 0)
    carries = lax.fori_loop(0, n_far, functools.partial(body, kind='far'), carries)
    carries = lax.fori_loop(n_far, i, functools.partial(body, kind='near'), carries)
    for (_, _, finish), c in zip(streams, carries):
        finish(c)


def _attn_in_kernel(x_ref, g_ref, w_ref, wt_ref, proj_ref, avt_ref, bvt_ref, iwt_ref, *, tm):
    hb = _rms(x_ref[0], g_ref[...]).astype(BF16)
    proj_ref[0] = _dot(hb, w_ref[...]).astype(BF16)
    t = _dot_nt(wt_ref[...], hb)
    ones = jnp.ones((ONES_ROWS, TK), BF16)
    for c in range(tm // TK):
        cols = slice(c * TK, (c + 1) * TK)
        for h in range(A_HEADS):
            r = ROW_AV + h * A_V_DIM
            avt_ref[0, c, h, 0:A_V_DIM, :] = t[r:r + A_V_DIM, cols].astype(BF16)
            avt_ref[0, c, h, A_V_DIM:A_V_DIM + ONES_ROWS, :] = ones
        bvt_ref[0, c, 0:B_HEAD_DIM, :] = t[ROW_BV:ROW_IW, cols].astype(BF16)
        bvt_ref[0, c, B_HEAD_DIM:B_HEAD_DIM + ONES_ROWS, :] = ones
    iwt_ref[0] = t[ROW_IW:ROW_IW + IDX_HEADS, :] * (IDX_HEADS ** -0.5 * IDX_DIM ** -0.5)


def _attn_in(x, g, w, wt, *, tm=1024):
    bsz, s, d = x.shape
    av_blk = (A_HEADS, A_V_DIM + ONES_ROWS, TK)
    bv_blk = (B_HEAD_DIM + ONES_ROWS, TK)
    return pl.pallas_call(
        functools.partial(_attn_in_kernel, tm=tm),
        grid=(bsz, s // tm),
        in_specs=[
            pl.BlockSpec((1, tm, d), lambda b, i: (b, i, 0)),
            pl.BlockSpec((1, d), lambda b, i: (0, 0)),
            pl.BlockSpec((d, PROJ_W), lambda b, i: (0, 0)),
            pl.BlockSpec((VT_ROWS, d), lambda b, i: (0, 0)),
        ],
        out_specs=[
            pl.BlockSpec((1, tm, PROJ_W), lambda b, i: (b, i, 0)),
            pl.BlockSpec((1, tm // TK) + av_blk, lambda b, i: (b, i, 0, 0, 0)),
            pl.BlockSpec((1, tm // TK) + bv_blk, lambda b, i: (b, i, 0, 0)),
            pl.BlockSpec((1, IDX_HEADS, tm), lambda b, i: (b, 0, i)),
        ],
        out_shape=[
            jax.ShapeDtypeStruct((bsz, s, PROJ_W), BF16),
            jax.ShapeDtypeStruct((bsz, s // TK) + av_blk, BF16),
            jax.ShapeDtypeStruct((bsz, s // TK) + bv_blk, BF16),
            jax.ShapeDtypeStruct((bsz, IDX_HEADS, s), F32),
        ],
        compiler_params=pltpu.CompilerParams(
            dimension_semantics=("arbitrary", "arbitrary"), vmem_limit_bytes=VMEM_LIMIT),
        name="attn_in",
    )(x, g, w, wt)


def _key_rows(ref, j):
    return ref[0, pl.ds(pl.multiple_of(j * TK, TK), TK), :]


def _indexer(i, iq_ref, ik_ref, iwt_ref, score_ref, key_ref, plane_ref):
    iq = iq_ref[0]
    per_blk = LANES // IDX_DIM
    iqh = [_lane_band(iq[:, (h // per_blk) * LANES:(h // per_blk + 1) * LANES],
                      (h % per_blk) * IDX_DIM, IDX_DIM) for h in range(IDX_HEADS)]
    w = iwt_ref[0]

    def stage_scores(j):
        ikb = _key_rows(ik_ref, j)
        half = TK // 2
        for r in (0, half):
            acc = jnp.zeros((half, TQ), F32)
            for h in range(IDX_HEADS):
                acc = acc + jnp.maximum(_dot_nt(ikb[r:r + half], iqh[h]), 0.0) * w[h:h + 1, :]
            score_ref[r:r + half, :] = acc

    def staged_keys():
        acc = score_ref[...]
        acc = jnp.where(acc == 0.0, 0.0, acc)
        bits = lax.bitcast_convert_type(acc, I32)
        return bits ^ ((bits >> 31) & 0x7FFFFFFF)

    def store_keys(j, keys):
        key_ref[j] = keys
        for p, plane in enumerate(_bit_planes(keys)):
            plane_ref[p, j] = plane

    def per_block(j):
        keys = staged_keys()
        stage_scores(j + 1)
        store_keys(j, keys)

    def finish():
        krow = lax.broadcasted_iota(I32, (TK, TQ), 0)
        qcol = lax.broadcasted_iota(I32, (TK, TQ), 1)
        admissible = (krow // CHUNK) <= (qcol // CHUNK)
        store_keys(i, jnp.where(admissible, staged_keys(), KEY_NEG_INF))

    stage_scores(0)
    return per_block, finish


def _select_mask(i, key_ref, plane_ref, mask_ref, *, top_k, seq):
    nkb = seq // TK
    blk = lax.broadcasted_iota(I32, (nkb, SUBLANES, TQ), 0)
    alive0 = jnp.where(blk <= i, -1, 0)

    def bit_body(p, carry, n_bits):
        alive, n_gt, thr_u = carry
        hit = alive & plane_ref[p]
        pc = jnp.sum(lax.population_count(hit), axis=0)
        n_hit = jnp.sum(pc.astype(F32), axis=0, keepdims=True)
        take = (n_gt + n_hit) >= top_k
        thr_u = thr_u | jnp.where(take, lax.shift_left(jnp.int32(1), n_bits - 1 - p), 0)
        n_gt = n_gt + jnp.where(take, 0.0, n_hit)
        alive = jnp.where(take, hit, alive ^ hit)
        return alive, n_gt, thr_u

    zero = jnp.zeros((1, TQ), I32)
    alive, n_gt, thr_u = lax.fori_loop(
        0, KEY_BITS, lambda p, c: bit_body(p, c, KEY_BITS), (alive0, jnp.zeros((1, TQ), F32), zero))
    idx_bits = _index_bits(seq)
    _, _, rev_idx = lax.fori_loop(
        KEY_BITS, KEY_BITS + idx_bits,
        lambda p, c: bit_body(p, c, KEY_BITS + idx_bits), (alive, n_gt, zero))
    thr = thr_u ^ INT_MIN
    lim = seq - rev_idx
    lim = jnp.where(thr > KEY_NEG_INF, lim, seq)
    thr = jnp.maximum(thr, KEY_NEG_INF + 1)
    krow = lax.broadcasted_iota(I32, (TK, TQ), 0)

    def mask_body(j, c):
        late = jnp.where(krow >= lim - j * TK, 1, 0)
        mask_ref[j] = jnp.where(key_ref[j] - late >= thr, 0.0, NEG)
        return c

    lax.fori_loop(0, i + 1, mask_body, 0)


def _attn_kernel(lam_ref, aq_ref, ak_ref, avt_ref, abias_ref, g_ref,
                 bq_ref, iq_ref, kb_ref, ik_ref, bvt_ref, iwt_ref, bbias_ref,
                 ya_ref, yb_ref,
                 sa_ref, sb_ref, aacc_ref, bacc_ref, ot_ref,
                 score_ref, key_ref, plane_ref, mask_ref, *, lambda_init, top_k, seq):
    i = pl.program_id(1)
    nkb = seq // TK
    n_maps = 2 * A_HEADS

    @pl.when((i == 0) & (pl.program_id(0) == 0))
    def _():
        plane_ref[0:KEY_BITS] = jnp.zeros((KEY_BITS, nkb, SUBLANES, TQ), I32)
        for t, plane in enumerate(_index_planes(nkb)):
            plane_ref[KEY_BITS + t] = plane

    index_block, finish_index = _indexer(i, iq_ref, ik_ref, iwt_ref, score_ref, key_ref, plane_ref)

    def index_body(j, c):
        index_block(j)
        return c

    lax.fori_loop(0, i, index_body, 0)
    finish_index()
    _select_mask(i, key_ref, plane_ref, mask_ref, top_k=top_k, seq=seq)

    lv = lam_ref[...]
    lam = (jnp.exp(jnp.sum(lv[0:1] * lv[1:2], axis=-1, keepdims=True))
           - jnp.exp(jnp.sum(lv[2:3] * lv[3:4], axis=-1, keepdims=True)) + lambda_init)

    q = aq_ref[0]
    qm = []
    for h in range(A_HEADS):
        q12 = q[:, h * LANES:(h + 1) * LANES]
        qm += [_lane_band(q12, 0, A_HEAD_DIM), _lane_band(q12, A_HEAD_DIM, A_HEAD_DIM)]

    def a_logits(j, c, kind):
        h = c // 2
        kb = ak_ref[0, pl.ds(pl.multiple_of(j * TK, TK), TK), h * LANES:(h + 1) * LANES]
        s = _dot_nt(kb, qm[c])
        return s if kind == 'far' else s + abias_ref[_bias_kind(j, i), h]

    qb = bq_ref[0]
    qh = [_lane_band(qb[:, (h // 2) * LANES:(h // 2 + 1) * LANES],
                     (h % 2) * B_HEAD_DIM, B_HEAD_DIM) for h in range(B_HEADS)]

    def b_logits(j, h, kind):
        s = _dot_nt(_key_rows(kb_ref, j), qh[h]) + mask_ref[j]
        return s if kind == 'far' else s + bbias_ref[_bias_kind(j, i), h]

    _run_key_block_streams(i, [
        _key_block_stream(i, n_maps, sa_ref, aacc_ref, a_logits, lambda j, c: avt_ref[0, j, c // 2]),
        _key_block_stream(i, B_HEADS, sb_ref, bacc_ref, b_logits, lambda j, h: bvt_ref[0, j])])

    for h in range(A_HEADS):
        a1 = aacc_ref[2 * h]
        a2 = aacc_ref[2 * h + 1]
        a = (a1[0:A_V_DIM] / a1[A_V_DIM:A_V_DIM + 1]
             - lam * (a2[0:A_V_DIM] / a2[A_V_DIM:A_V_DIM + 1]))
        y = a * lax.rsqrt(jnp.mean(a * a, axis=0, keepdims=True) + EPS) * g_ref[...]
        ot_ref[h * A_V_DIM:(h + 1) * A_V_DIM, :] = y * (1.0 - lambda_init)
    ya_ref[0] = ot_ref[...].T.astype(BF16)

    for h in range(B_HEADS):
        a = bacc_ref[h]
        ot_ref[h * B_HEAD_DIM:(h + 1) * B_HEAD_DIM, :] = (
            a[0:B_HEAD_DIM] / a[B_HEAD_DIM:B_HEAD_DIM + 1])
    yb_ref[0] = ot_ref[...].T.astype(BF16)


def _attention(proj, avt, bvt, iwt, lam_vecs, a_bias, b_bias, subln_g_col, *, lambda_init, top_k):
    bsz, s, _ = proj.shape
    nq = s // TQ
    nkb = s // TK
    a_width = A_HEADS * A_V_DIM
    b_width = B_HEADS * B_HEAD_DIM
    assert a_width == b_width
    iq_w = IDX_HEADS * IDX_DIM
    once = pl.Buffered(1)
    return pl.pallas_call(
        functools.partial(_attn_kernel, lambda_init=lambda_init, top_k=top_k, seq=s),
        grid=(bsz, nq),
        in_specs=[
            pl.BlockSpec((4, A_HEAD_DIM), lambda b, i: (0, 0)),
            pl.BlockSpec((1, TQ, a_width), lambda b, i: (b, i, COL_AQ // a_width)),
            pl.BlockSpec((1, s, a_width), lambda b, i: (b, 0, COL_AK // a_width)),
            pl.BlockSpec((1, nkb, A_HEADS, A_V_DIM + ONES_ROWS, TK), lambda b, i: (b, 0, 0, 0, 0)),
            pl.BlockSpec((3, A_HEADS, TK, TQ), lambda b, i: (0, 0, 0, 0), once),
            pl.BlockSpec((A_V_DIM, 1), lambda b, i: (0, 0)),
            pl.BlockSpec((1, TQ, b_width), lambda b, i: (b, i, COL_BQ // b_width)),
            pl.BlockSpec((1, TQ, iq_w), lambda b, i: (b, i, COL_IQ // iq_w)),
            pl.BlockSpec((1, s, LANES), lambda b, i: (b, 0, COL_KB // LANES)),
            pl.BlockSpec((1, s, LANES), lambda b, i: (b, 0, COL_IK // LANES)),
            pl.BlockSpec((1, nkb, B_HEAD_DIM + ONES_ROWS, TK), lambda b, i: (b, 0, 0, 0)),
            pl.BlockSpec((1, IDX_HEADS, TQ), lambda b, i: (b, 0, i)),
            pl.BlockSpec((3, B_HEADS, TK, TQ), lambda b, i: (0, 0, 0, 0), once),
        ],
        out_specs=[pl.BlockSpec((1, TQ, a_width), lambda b, i: (b, i, 0)),
                   pl.BlockSpec((1, TQ, b_width), lambda b, i: (b, i, 0))],
        out_shape=[jax.ShapeDtypeStruct((bsz, s, a_width), BF16),
                   jax.ShapeDtypeStruct((bsz, s, b_width), BF16)],
        scratch_shapes=[
            pltpu.VMEM((2 * A_HEADS, TK, TQ), F32),
            pltpu.VMEM((B_HEADS, TK, TQ), F32),
            pltpu.VMEM((2 * A_HEADS, A_V_DIM + ONES_ROWS, TQ), F32),
            pltpu.VMEM((B_HEADS, B_HEAD_DIM + ONES_ROWS, TQ), F32),
            pltpu.VMEM((a_width, TQ), F32),
            pltpu.VMEM((TK, TQ), F32),
            pltpu.VMEM((nkb, TK, TQ), I32),
            pltpu.VMEM((KEY_BITS + _index_bits(s), nkb, SUBLANES, TQ), I32),
            pltpu.VMEM((nkb, TK, TQ), F32),
        ],
        compiler_params=pltpu.CompilerParams(
            dimension_semantics=("arbitrary", "arbitrary"), vmem_limit_bytes=VMEM_LIMIT),
        name="attention",
    )(lam_vecs, proj, proj, avt, a_bias, subln_g_col, proj, proj, proj, proj, bvt, iwt, b_bias)


TAIL_ROWS = 256


def _tail_kernel(ya_ref, yb_ref, wo_ref, x_ref, g_mix_ref, g_in_ref, wu_ref, wd_ref, g_out_ref,
                 o_ref, x1_ref, h_ref, acc_ref):
    f = pl.program_id(1)
    last = pl.num_programs(1) - 1
    chunks = [slice(r, r + TAIL_ROWS) for r in range(0, x_ref.shape[0], TAIL_ROWS)]

    def mlp_part(rows):
        u = jnp.maximum(_dot(h_ref[rows, :], wu_ref[...]), 0.0)
        return _dot((u * u).astype(BF16), wd_ref[...])

    @pl.when(f == 0)
    def _():
        half = ya_ref.shape[-1]
        for rows in chunks:
            m = (_dot(ya_ref[rows, :], wo_ref[0:half, :])
                 + _dot(yb_ref[rows, :], wo_ref[half:2 * half, :]))
            x1 = x_ref[rows, :] + _rms(m, g_mix_ref[...])
            x1_ref[rows, :] = x1
            h_ref[rows, :] = _rms(x1, g_in_ref[...]).astype(BF16)
        acc_ref[...] = mlp_part(slice(None))

    @pl.when((f > 0) & (f < last))
    def _():
        acc_ref[...] += mlp_part(slice(None))

    @pl.when(f == last)
    def _():
        for rows in chunks:
            y = acc_ref[rows, :] + mlp_part(rows)
            o_ref[rows, :] = x1_ref[rows, :] + _rms(y, g_out_ref[...])


def _layer_tail(ya, yb, wo, x, g_mix, g_in, wu, wd, layer, g_out, *, tm=1024, tf=1024):
    n, d = x.shape
    half = ya.shape[-1]
    ff = wu.shape[-1]
    assert ff // tf >= 2 and tm % TAIL_ROWS == 0
    return pl.pallas_call(
        _tail_kernel,
        grid=(n // tm, ff // tf),
        in_specs=[
            pl.BlockSpec((tm, half), lambda i, f: (i, 0)),
            pl.BlockSpec((tm, half), lambda i, f: (i, 0)),
            pl.BlockSpec((2 * half, d), lambda i, f: (0, 0)),
            pl.BlockSpec((tm, d), lambda i, f: (i, 0)),
            pl.BlockSpec((1, d), lambda i, f: (0, 0)),
            pl.BlockSpec((1, d), lambda i, f: (0, 0)),
            pl.BlockSpec((None, d, tf), lambda i, f: (layer, 0, f)),
            pl.BlockSpec((None, tf, d), lambda i, f: (layer, f, 0)),
            pl.BlockSpec((1, d), lambda i, f: (0, 0)),
        ],
        out_specs=pl.BlockSpec((tm, d), lambda i, f: (i, 0)),
        out_shape=jax.ShapeDtypeStruct((n, d), F32),
        scratch_shapes=[pltpu.VMEM((tm, d), F32), pltpu.VMEM((tm, d), BF16),
                        pltpu.VMEM((tm, d), F32)],
        compiler_params=pltpu.CompilerParams(
            dimension_semantics=("arbitrary", "arbitrary"), vmem_limit_bytes=VMEM_LIMIT),
        name="layer_tail",
    )(ya, yb, wo, x, g_mix, g_in, wu, wd, g_out)


CONV_ROWS = 64
U_HIST = 32
Z_HIST = 8


def _conv_kernel(x_ref, g_in_ref, w_in_ref, dw_w_ref, dw_b_ref, ln_g_ref, ln_b_ref, sc_w_ref,
                 u_ref, z_ref, ubuf_ref, zbuf_ref, shift_ref, conv_ref, *, ts):
    @pl.when(pl.program_id(1) == 0)
    def _():
        ubuf_ref[0:U_HIST, :] = jnp.zeros((U_HIST, CONV_CH), F32)
        zbuf_ref[0:Z_HIST, :] = jnp.zeros((Z_HIST, SC_CH), F32)

    x = x_ref[0]
    hb = _rms(x, g_in_ref[...]).astype(BF16)
    c = CONV_CH
    ca = _dot(hb, w_in_ref[:, 0:c])
    cg = _dot(hb, w_in_ref[:, c:2 * c])
    ubuf_ref[U_HIST:U_HIST + ts, :] = ca * jax.nn.sigmoid(cg)
    dc = _dot(hb, w_in_ref[:, 3 * c:4 * c])
    dh = _dot(hb, w_in_ref[:, 4 * c:5 * c])
    zbuf_ref[Z_HIST:Z_HIST + ts, :] = dc * dh

    first_off = U_HIST - (CONV_WIDTH - 1)
    shift_rows = shift_ref.shape[1]
    for r in range(1, SUBLANES):
        shift_ref[r - 1] = ubuf_ref[r:r + shift_rows, :]

    for base in range(0, ts, CONV_ROWS):
        acc = jnp.zeros((CONV_ROWS, c), F32)
        for j in range(CONV_WIDTH):
            r, a = (first_off + j) % SUBLANES, (first_off + j) // SUBLANES
            rows = pl.ds(base + SUBLANES * a, CONV_ROWS)
            taps = ubuf_ref[rows, :] if r == 0 else shift_ref[r - 1, rows, :]
            acc = acc + dw_w_ref[j:j + 1, :] * taps
        conv_ref[base:base + CONV_ROWS, :] = acc
    u = conv_ref[...] + dw_b_ref[...]
    mu = jnp.mean(u, axis=-1, keepdims=True)
    ctr = u - mu
    var = jnp.mean(ctr * ctr, axis=-1, keepdims=True)
    u = ctr * lax.rsqrt(var + EPS) * ln_g_ref[...] + ln_b_ref[...]
    u = u * jax.nn.sigmoid(u)

    z = jnp.zeros((ts, c), F32)
    for j in range(SC_WIDTH):
        off = Z_HIST - (SC_WIDTH - 1) + j
        z = z + sc_w_ref[j:j + 1, :] * zbuf_ref[off:off + ts, :]
    z = _dot(hb, w_in_ref[:, 2 * c:3 * c]) * z

    ubuf_ref[0:U_HIST, :] = ubuf_ref[ts:ts + U_HIST, :]
    zbuf_ref[0:Z_HIST, :] = zbuf_ref[ts:ts + Z_HIST, :]

    u_ref[0] = u.astype(BF16)
    z_ref[0] = z.astype(BF16)


def _conv_mixer(x, g_in, w_in, dw_w, dw_b, ln_g, ln_b, sc_w, *, ts=512):
    bsz, s, d = x.shape
    full = lambda a: pl.BlockSpec(a.shape, lambda b, i: (0,) * a.ndim)
    return pl.pallas_call(
        functools.partial(_conv_kernel, ts=ts),
        grid=(bsz, s // ts),
        in_specs=[pl.BlockSpec((1, ts, d), lambda b, i: (b, i, 0)),
                  full(g_in), full(w_in), full(dw_w), full(dw_b), full(ln_g), full(ln_b),
                  full(sc_w)],
        out_specs=[pl.BlockSpec((1, ts, CONV_CH), lambda b, i: (b, i, 0)),
                   pl.BlockSpec((1, ts, SC_CH), lambda b, i: (b, i, 0))],
        out_shape=[jax.ShapeDtypeStruct((bsz, s, CONV_CH), BF16),
                   jax.ShapeDtypeStruct((bsz, s, SC_CH), BF16)],
        scratch_shapes=[pltpu.VMEM((U_HIST + ts, CONV_CH), F32),
                        pltpu.VMEM((Z_HIST + ts, SC_CH), F32),
                        pltpu.VMEM((SUBLANES - 1, ts + U_HIST - SUBLANES, CONV_CH), F32),
                        pltpu.VMEM((ts, CONV_CH), F32)],
        compiler_params=pltpu.CompilerParams(
            dimension_semantics=("arbitrary", "arbitrary"), vmem_limit_bytes=VMEM_LIMIT),
        name="conv_mixer",
    )(x, g_in, w_in, dw_w, dw_b, ln_g, ln_b, sc_w)


def _t5_bucket(rel):
    nb = NUM_BUCKETS // 2
    ret = jnp.where(rel > 0, nb, 0)
    n = jnp.abs(rel)
    max_exact = nb // 2
    nf = jnp.maximum(n, 1).astype(jnp.float32)
    large = max_exact + (jnp.log(nf / max_exact) / math.log(MAX_DISTANCE / max_exact)
                         * (nb - max_exact)).astype(jnp.int32)
    large = jnp.minimum(large, nb - 1)
    return ret + jnp.where(n < max_exact, n, large)


REL_LO = -(TQ + TK - 1)
REL_LEN = TK - REL_LO
REL_LANES = -(-REL_LEN // LANES) * LANES


def _bias_kernel(fr_ref, a_ref, b_ref):
    krow = lax.broadcasted_iota(I32, (TK, TQ), 0)
    qcol = lax.broadcasted_iota(I32, (TK, TQ), 1)
    admissible = (krow // CHUNK) <= (qcol // CHUNK)
    zeros = jnp.zeros((TK, TQ), F32)
    for h in range(A_HEADS + B_HEADS):
        table = jnp.broadcast_to(fr_ref[h], (TK, REL_LANES))
        blocks = []
        for shift in (TK, 0):
            s0 = (-shift - REL_LO - (REL_LEN - 1)) % REL_LANES
            blocks.append(pltpu.roll(table, s0, 1, stride=1, stride_axis=0)[:, 0:TQ])
        prev, diag = blocks
        if h < A_HEADS:
            a_ref[0, h], a_ref[1, h], a_ref[2, h] = zeros, prev, jnp.where(admissible, diag, NEG)
        else:
            g = h - A_HEADS
            b_ref[0, g], b_ref[1, g], b_ref[2, g] = zeros, prev, diag


def _bias_tables(rel_bias):
    rel = jnp.arange(REL_LO, TK, dtype=jnp.int32)
    far_bucket = NUM_BUCKETS // 2 - 1
    f = (rel_bias[_t5_bucket(rel)].astype(F32) - rel_bias[far_bucket].astype(F32)[None, :]).T
    fr = jnp.pad(f[:, ::-1] * LOG2E, ((0, 0), (0, REL_LANES - REL_LEN)))[:, None, :]
    heads = A_HEADS + B_HEADS
    return pl.pallas_call(
        _bias_kernel,
        grid=(1,),
        in_specs=[pl.BlockSpec((heads, 1, REL_LANES), lambda i: (0, 0, 0))],
        out_specs=[pl.BlockSpec((3, A_HEADS, TK, TQ), lambda i: (0, 0, 0, 0)),
                   pl.BlockSpec((3, B_HEADS, TK, TQ), lambda i: (0, 0, 0, 0))],
        out_shape=[jax.ShapeDtypeStruct((3, A_HEADS, TK, TQ), F32),
                   jax.ShapeDtypeStruct((3, B_HEADS, TK, TQ), F32)],
        compiler_params=pltpu.CompilerParams(
            dimension_semantics=("arbitrary",), vmem_limit_bytes=VMEM_LIMIT),
        name="bias_tables",
    )(fr)


def _attn_weights(w_in):
    o = np.cumsum([0, 512, 512, 512, 512, 64, 64, 256, 32, 8])
    aq, ak, av, bq, bk, bv, iq, ik, iw = [w_in[:, o[n]:o[n + 1]] for n in range(9)]
    aq = aq * (A_HEAD_DIM ** -0.5 * LOG2E)
    bq = bq * (B_HEAD_DIM ** -0.5 * LOG2E)
    w = jnp.concatenate([aq, ak, bq, iq, bk, bk, ik, ik, ik, ik], axis=1).astype(BF16)
    pad = jnp.zeros((w_in.shape[0], VT_ROWS - ROW_IW - IDX_HEADS), w_in.dtype)
    wt = jnp.concatenate([av, bv, iw, pad], axis=1).T.astype(BF16)
    return w, wt


def kernel(x, rel_bias, norm_g, w_mlp_up, w_mlp_down, attn_w_in, attn_w_out, diff_lambda,
           diff_subln_g, conv_w_in, conv_w_out, conv_dw_w, conv_dw_b, conv_ln_g, conv_ln_b,
           sconv_w):
    bsz, s, d = x.shape
    depth = norm_g.shape[0]
    top_k = min(TOPK_MAX, s // 4)
    row = lambda v: v.reshape(1, -1)
    a_bias, b_bias = _bias_tables(rel_bias)
    w_up, w_down = w_mlp_up.astype(BF16), w_mlp_down.astype(BF16)
    for i in range(depth):
        j = i // 2
        if i % 2 == 0:
            lambda_init = 0.8 - 0.6 * math.exp(-0.3 * i)
            w, wt = _attn_weights(attn_w_in[j])
            proj, avt, bvt, iwt = _attn_in(x.reshape(bsz, s, d), row(norm_g[i, 0]), w, wt)
            ya, yb = _attention(proj, avt, bvt, iwt, diff_lambda[j], a_bias, b_bias,
                                diff_subln_g[j].reshape(-1, 1), lambda_init=lambda_init,
                                top_k=top_k)
            w_out = attn_w_out[j]
        else:
            ya, yb = _conv_mixer(x.reshape(bsz, s, d), row(norm_g[i, 0]),
                                 conv_w_in[j].astype(BF16), conv_dw_w[j], row(conv_dw_b[j]),
                                 row(conv_ln_g[j]), row(conv_ln_b[j]), sconv_w[j])
            w_out = conv_w_out[j]
        x = _layer_tail(ya.reshape(bsz * s, -1), yb.reshape(bsz * s, -1), w_out.astype(BF16),
                        x.reshape(bsz * s, d), row(norm_g[i, 1]), row(norm_g[i, 2]),
                        w_up, w_down, i, row(norm_g[i, 3]))
    return x.reshape(bsz, s, d)
```

```python
import functools
import math

import numpy as np
import jax
import jax.numpy as jnp
from jax import lax
from jax.experimental import pallas as pl
from jax.experimental.pallas import tpu as pltpu

F32 = jnp.float32
BF16 = jnp.bfloat16
I32 = jnp.int32

CHUNK = 64
NUM_BUCKETS = 32
MAX_DISTANCE = 128
EPS = 1e-6
NEG = -1e30
LOG2E = math.log2(math.e)
A_HEADS = 4
A_HEAD_DIM = 64
A_V_DIM = 2 * A_HEAD_DIM
B_HEADS = 8
B_HEAD_DIM = 64
IDX_HEADS = 8
IDX_DIM = 32
TOPK_MAX = 256
CONV_CH = 512
CONV_WIDTH = 31
SC_CH = 512
SC_WIDTH = 3

LANES = 128
SUBLANES = 8
VMEM_LIMIT = 56 * 1024 * 1024

TQ = 256
TK = 256
COL_AQ, COL_AK, COL_BQ, COL_IQ, COL_KB, COL_IK, PROJ_W = 0, 512, 1024, 1536, 1792, 1920, 2048
ROW_AV, ROW_BV, ROW_IW, VT_ROWS = 0, 512, 576, 640
ONES_ROWS = 16
INT_MIN = -2 ** 31
KEY_NEG_INF = INT_MIN + 0x7FFFFF


def _rms(x, g):
    return x * lax.rsqrt(jnp.mean(x * x, axis=-1, keepdims=True) + EPS) * g


def _dot_nt(a, b):
    return lax.dot_general(a, b, (((1,), (1,)), ((), ())), preferred_element_type=F32)


def _dot(a, b):
    return jnp.dot(a, b, preferred_element_type=F32)


def _lane_band(x, lo, width):
    lane = lax.broadcasted_iota(I32, x.shape, 1)
    return jnp.where((lane >= lo) & (lane < lo + width), x, jnp.zeros_like(x))


KEY_BITS = 32


def _bit_planes(keys):
    assert keys.shape[0] == KEY_BITS * SUBLANES
    u = keys ^ INT_MIN
    a = [u[SUBLANES * r:SUBLANES * (r + 1), :] for r in range(KEY_BITS)]
    j, m = KEY_BITS // 2, (1 << (KEY_BITS // 2)) - 1
    while j:
        mask = np.int32(np.uint32(m))
        k = 0
        while k < KEY_BITS:
            t = (a[k] ^ lax.shift_right_logical(a[k + j], jnp.int32(j))) & mask
            a[k] = a[k] ^ t
            a[k + j] = a[k + j] ^ lax.shift_left(t, jnp.int32(j))
            k = (k + j + 1) & ~j
        j >>= 1
        m = (m ^ (m << j)) & 0xFFFFFFFF
    return a


def _index_bits(seq):
    bits = seq.bit_length() - 1
    assert seq == 1 << bits and seq >= KEY_BITS * SUBLANES
    return bits


def _index_planes(nkb):
    shape = (nkb, SUBLANES, TQ)
    blk = lax.broadcasted_iota(I32, shape, 0)
    sub = lax.broadcasted_iota(I32, shape, 1)
    sub_bits = SUBLANES.bit_length() - 1
    word_bits = KEY_BITS.bit_length() - 1
    planes = []
    for b in range(_index_bits(nkb * TK) - 1, -1, -1):
        if b >= sub_bits + word_bits:
            on = ((nkb - 1 - blk) >> (b - sub_bits - word_bits)) & 1
            planes.append(jnp.where(on == 1, -1, 0))
        elif b >= sub_bits:
            t = b - sub_bits
            pattern = sum(1 << k for k in range(KEY_BITS) if (k >> t) & 1)
            planes.append(jnp.full(shape, np.int32(np.uint32(pattern)), I32))
        else:
            on = ((SUBLANES - 1 - sub) >> b) & 1
            planes.append(jnp.where(on == 1, -1, 0))
    return planes


def _bias_kind(j, i):
    return jnp.clip(j - i + 2, 0, 2)


def _key_block_stream(i, n_chains, s_ref, acc_ref, logits, v_aug):
    chains = range(n_chains)

    def stage(j, c, kind):
        s = logits(j, c, kind)
        s_ref[c] = s
        return jnp.max(s, axis=0, keepdims=True)

    def absorb(j, c, m_old, block_max):
        m_new = jnp.maximum(m_old, block_max)
        alpha = jnp.exp2(m_old - m_new)
        p = jnp.exp2(s_ref[c] - m_new).astype(BF16)
        acc_ref[c] = acc_ref[c] * alpha + _dot(v_aug(j, c), p)
        return m_new

    def start():
        acc_ref[...] = jnp.zeros_like(acc_ref)
        return (jnp.full((1, TQ), NEG, F32),) * n_chains, tuple(stage(0, c, 'any') for c in chains)

    def step(j, carry, kind):
        ms, bms = carry
        out = [(absorb(j, c, ms[c], bms[c]), stage(j + 1, c, kind)) for c in chains]
        return tuple(o[0] for o in out), tuple(o[1] for o in out)

    def finish(carry):
        ms, bms = carry
        for c in chains:
            absorb(i, c, ms[c], bms[c])

    return start, step, finish


def _run_key_block_streams(i, streams):
    def body(j, carries, kind):
        return tuple(step(j, c, kind) for (_, step, _), c in zip(streams, carries))

    carries = tuple(start() for start, _, _ in streams)
    n_far = jnp.maximum(i - 2, 0)
    carries = lax.fori_loop(0, n_far, functools.partial(body, kind='far'), carries)
    carries = lax.fori_loop(n_far, i, functools.partial(body, kind='near'), carries)
    for (_, _, finish), c in zip(streams, carries):
        finish(c)


def _attn_in_kernel(x_ref, g_ref, w_ref, wt_ref, proj_ref, avt_ref, bvt_ref, iwt_ref, *, tm):
    hb = _rms(x_ref[0], g_ref[...]).astype(BF16)
    proj_ref[0] = _dot(hb, w_ref[...]).astype(BF16)
    t = _dot_nt(wt_ref[...], hb)
    ones = jnp.ones((ONES_ROWS, TK), BF16)
    for c in range(tm // TK):
        cols = slice(c * TK, (c + 1) * TK)
        for h in range(A_HEADS):
            r = ROW_AV + h * A_V_DIM
            avt_ref[0, c, h, 0:A_V_DIM, :] = t[r:r + A_V_DIM, cols].astype(BF16)
            avt_ref[0, c, h, A_V_DIM:A_V_DIM + ONES_ROWS, :] = ones
        bvt_ref[0, c, 0:B_HEAD_DIM, :] = t[ROW_BV:ROW_IW, cols].astype(BF16)
        bvt_ref[0, c, B_HEAD_DIM:B_HEAD_DIM + ONES_ROWS, :] = ones
    iwt_ref[0] = t[ROW_IW:ROW_IW + IDX_HEADS, :] * (IDX_HEADS ** -0.5 * IDX_DIM ** -0.5)


def _attn_in(x, g, w, wt, *, tm=1024):
    bsz, s, d = x.shape
    av_blk = (A_HEADS, A_V_DIM + ONES_ROWS, TK)
    bv_blk = (B_HEAD_DIM + ONES_ROWS, TK)
    return pl.pallas_call(
        functools.partial(_attn_in_kernel, tm=tm),
        grid=(bsz, s // tm),
        in_specs=[
            pl.BlockSpec((1, tm, d), lambda b, i: (b, i, 0)),
            pl.BlockSpec((1, d), lambda b, i: (0, 0)),
            pl.BlockSpec((d, PROJ_W), lambda b, i: (0, 0)),
            pl.BlockSpec((VT_ROWS, d), lambda b, i: (0, 0)),
        ],
        out_specs=[
            pl.BlockSpec((1, tm, PROJ_W), lambda b, i: (b, i, 0)),
            pl.BlockSpec((1, tm // TK) + av_blk, lambda b, i: (b, i, 0, 0, 0)),
            pl.BlockSpec((1, tm // TK) + bv_blk, lambda b, i: (b, i, 0, 0)),
            pl.BlockSpec((1, IDX_HEADS, tm), lambda b, i: (b, 0, i)),
        ],
        out_shape=[
            jax.ShapeDtypeStruct((bsz, s, PROJ_W), BF16),
            jax.ShapeDtypeStruct((bsz, s // TK) + av_blk, BF16),
            jax.ShapeDtypeStruct((bsz, s // TK) + bv_blk, BF16),
            jax.ShapeDtypeStruct((bsz, IDX_HEADS, s), F32),
        ],
        compiler_params=pltpu.CompilerParams(
            dimension_semantics=("arbitrary", "arbitrary"), vmem_limit_bytes=VMEM_LIMIT),
        name="attn_in",
    )(x, g, w, wt)


def _key_rows(ref, j):
    return ref[0, pl.ds(pl.multiple_of(j * TK, TK), TK), :]


def _indexer(i, iq_ref, ik_ref, iwt_ref, score_ref, key_ref, plane_ref):
    iq = iq_ref[0]
    per_blk = LANES // IDX_DIM
    iqh = [_lane_band(iq[:, (h // per_blk) * LANES:(h // per_blk + 1) * LANES],
                      (h % per_blk) * IDX_DIM, IDX_DIM) for h in range(IDX_HEADS)]
    w = iwt_ref[0]

    def stage_scores(j):
        ikb = _key_rows(ik_ref, j)
        half = TK // 2
        for r in (0, half):
            acc = jnp.zeros((half, TQ), F32)
            for h in range(IDX_HEADS):
                acc = acc + jnp.maximum(_dot_nt(ikb[r:r + half], iqh[h]), 0.0) * w[h:h + 1, :]
            score_ref[r:r + half, :] = acc

    def staged_keys():
        acc = score_ref[...]
        acc = jnp.where(acc == 0.0, 0.0, acc)
        bits = lax.bitcast_convert_type(acc, I32)
        return bits ^ ((bits >> 31) & 0x7FFFFFFF)

    def store_keys(j, keys):
        key_ref[j] = keys
        for p, plane in enumerate(_bit_planes(keys)):
            plane_ref[p, j] = plane

    def per_block(j):
        keys = staged_keys()
        stage_scores(j + 1)
        store_keys(j, keys)

    def finish():
        krow = lax.broadcasted_iota(I32, (TK, TQ), 0)
        qcol = lax.broadcasted_iota(I32, (TK, TQ), 1)
        admissible = (krow // CHUNK) <= (qcol // CHUNK)
        store_keys(i, jnp.where(admissible, staged_keys(), KEY_NEG_INF))

    stage_scores(0)
    return per_block, finish


def _select_mask(i, key_ref, plane_ref, mask_ref, *, top_k, seq):
    nkb = seq // TK
    blk = lax.broadcasted_iota(I32, (nkb, SUBLANES, TQ), 0)
    alive0 = jnp.where(blk <= i, -1, 0)

    def bit_body(p, carry, n_bits):
        alive, n_gt, thr_u = carry
        hit = alive & plane_ref[p]
        pc = jnp.sum(lax.population_count(hit), axis=0)
        n_hit = jnp.sum(pc.astype(F32), axis=0, keepdims=True)
        take = (n_gt + n_hit) >= top_k
        thr_u = thr_u | jnp.where(take, lax.shift_left(jnp.int32(1), n_bits - 1 - p), 0)
        n_gt = n_gt + jnp.where(take, 0.0, n_hit)
        alive = jnp.where(take, hit, alive ^ hit)
        return alive, n_gt, thr_u

    zero = jnp.zeros((1, TQ), I32)
    alive, n_gt, thr_u = lax.fori_loop(
        0, KEY_BITS, lambda p, c: bit_body(p, c, KEY_BITS), (alive0, jnp.zeros((1, TQ), F32), zero))
    idx_bits = _index_bits(seq)
    _, _, rev_idx = lax.fori_loop(
        KEY_BITS, KEY_BITS + idx_bits,
        lambda p, c: bit_body(p, c, KEY_BITS + idx_bits), (alive, n_gt, zero))
    thr = thr_u ^ INT_MIN
    lim = seq - rev_idx
    lim = jnp.where(thr > KEY_NEG_INF, lim, seq)
    thr = jnp.maximum(thr, KEY_NEG_INF + 1)
    krow = lax.broadcasted_iota(I32, (TK, TQ), 0)

    def mask_body(j, c):
        late = jnp.where(krow >= lim - j * TK, 1, 0)
        mask_ref[j] = jnp.where(key_ref[j] - late >= thr, 0.0, NEG)
        return c

    lax.fori_loop(0, i + 1, mask_body, 0)


def _attn_kernel(lam_ref, aq_ref, ak_ref, avt_ref, abias_ref, g_ref,
                 bq_ref, iq_ref, kb_ref, ik_ref, bvt_ref, iwt_ref, bbias_ref,
                 ya_ref, yb_ref,
                 sa_ref, sb_ref, aacc_ref, bacc_ref, ot_ref,
                 score_ref, key_ref, plane_ref, mask_ref, *, lambda_init, top_k, seq):
    i = pl.program_id(1)
    nkb = seq // TK
    n_maps = 2 * A_HEADS

    @pl.when((i == 0) & (pl.program_id(0) == 0))
    def _():
        plane_ref[0:KEY_BITS] = jnp.zeros((KEY_BITS, nkb, SUBLANES, TQ), I32)
        for t, plane in enumerate(_index_planes(nkb)):
            plane_ref[KEY_BITS + t] = plane

    index_block, finish_index = _indexer(i, iq_ref, ik_ref, iwt_ref, score_ref, key_ref, plane_ref)

    def index_body(j, c):
        index_block(j)
        return c

    lax.fori_loop(0, i, index_body, 0)
    finish_index()
    _select_mask(i, key_ref, plane_ref, mask_ref, top_k=top_k, seq=seq)

    lv = lam_ref[...]
    lam = (jnp.exp(jnp.sum(lv[0:1] * lv[1:2], axis=-1, keepdims=True))
           - jnp.exp(jnp.sum(lv[2:3] * lv[3:4], axis=-1, keepdims=True)) + lambda_init)

    q = aq_ref[0]
    qm = []
    for h in range(A_HEADS):
        q12 = q[:, h * LANES:(h + 1) * LANES]
        qm += [_lane_band(q12, 0, A_HEAD_DIM), _lane_band(q12, A_HEAD_DIM, A_HEAD_DIM)]

    def a_logits(j, c, kind):
        h = c // 2
        kb = ak_ref[0, pl.ds(pl.multiple_of(j * TK, TK), TK), h * LANES:(h + 1) * LANES]
        s = _dot_nt(kb, qm[c])
        return s if kind == 'far' else s + abias_ref[_bias_kind(j, i), h]

    qb = bq_ref[0]
    qh = [_lane_band(qb[:, (h // 2) * LANES:(h // 2 + 1) * LANES],
                     (h % 2) * B_HEAD_DIM, B_HEAD_DIM) for h in range(B_HEADS)]

    def b_logits(j, h, kind):
        s = _dot_nt(_key_rows(kb_ref, j), qh[h]) + mask_ref[j]
        return s if kind == 'far' else s + bbias_ref[_bias_kind(j, i), h]

    _run_key_block_streams(i, [
        _key_block_stream(i, n_maps, sa_ref, aacc_ref, a_logits, lambda j, c: avt_ref[0, j, c // 2]),
        _key_block_stream(i, B_HEADS, sb_ref, bacc_ref, b_logits, lambda j, h: bvt_ref[0, j])])

    for h in range(A_HEADS):
        a1 = aacc_ref[2 * h]
        a2 = aacc_ref[2 * h + 1]
        a = (a1[0:A_V_DIM] / a1[A_V_DIM:A_V_DIM + 1]
             - lam * (a2[0:A_V_DIM] / a2[A_V_DIM:A_V_DIM + 1]))
        y = a * lax.rsqrt(jnp.mean(a * a, axis=0, keepdims=True) + EPS) * g_ref[...]
        ot_ref[h * A_V_DIM:(h + 1) * A_V_DIM, :] = y * (1.0 - lambda_init)
    ya_ref[0] = ot_ref[...].T.astype(BF16)

    for h in range(B_HEADS):
        a = bacc_ref[h]
        ot_ref[h * B_HEAD_DIM:(h + 1) * B_HEAD_DIM, :] = (
            a[0:B_HEAD_DIM] / a[B_HEAD_DIM:B_HEAD_DIM + 1])
    yb_ref[0] = ot_ref[...].T.astype(BF16)


def _attention(proj, avt, bvt, iwt, lam_vecs, a_bias, b_bias, subln_g_col, *, lambda_init, top_k):
    bsz, s, _ = proj.shape
    nq = s // TQ
    nkb = s // TK
    a_width = A_HEADS * A_V_DIM
    b_width = B_HEADS * B_HEAD_DIM
    assert a_width == b_width
    iq_w = IDX_HEADS * IDX_DIM
    once = pl.Buffered(1)
    return pl.pallas_call(
        functools.partial(_attn_kernel, lambda_init=lambda_init, top_k=top_k, seq=s),
        grid=(bsz, nq),
        in_specs=[
            pl.BlockSpec((4, A_HEAD_DIM), lambda b, i: (0, 0)),
            pl.BlockSpec((1, TQ, a_width), lambda b, i: (b, i, COL_AQ // a_width)),
            pl.BlockSpec((1, s, a_width), lambda b, i: (b, 0, COL_AK // a_width)),
            pl.BlockSpec((1, nkb, A_HEADS, A_V_DIM + ONES_ROWS, TK), lambda b, i: (b, 0, 0, 0, 0)),
            pl.BlockSpec((3, A_HEADS, TK, TQ), lambda b, i: (0, 0, 0, 0), once),
            pl.BlockSpec((A_V_DIM, 1), lambda b, i: (0, 0)),
            pl.BlockSpec((1, TQ, b_width), lambda b, i: (b, i, COL_BQ // b_width)),
            pl.BlockSpec((1, TQ, iq_w), lambda b, i: (b, i, COL_IQ // iq_w)),
            pl.BlockSpec((1, s, LANES), lambda b, i: (b, 0, COL_KB // LANES), once),
            pl.BlockSpec((1, s, LANES), lambda b, i: (b, 0, COL_IK // LANES), once),
            pl.BlockSpec((1, nkb, B_HEAD_DIM + ONES_ROWS, TK), lambda b, i: (b, 0, 0, 0), once),
            pl.BlockSpec((1, IDX_HEADS, TQ), lambda b, i: (b, 0, i)),
            pl.BlockSpec((3, B_HEADS, TK, TQ), lambda b, i: (0, 0, 0, 0), once),
        ],
        out_specs=[pl.BlockSpec((1, TQ, a_width), lambda b, i: (b, i, 0)),
                   pl.BlockSpec((1, TQ, b_width), lambda b, i: (b, i, 0))],
        out_shape=[jax.ShapeDtypeStruct((bsz, s, a_width), BF16),
                   jax.ShapeDtypeStruct((bsz, s, b_width), BF16)],
        scratch_shapes=[
            pltpu.VMEM((2 * A_HEADS, TK, TQ), F32),
            pltpu.VMEM((B_HEADS, TK, TQ), F32),
            pltpu.VMEM((2 * A_HEADS, A_V_DIM + ONES_ROWS, TQ), F32),
            pltpu.VMEM((B_HEADS, B_HEAD_DIM + ONES_ROWS, TQ), F32),
            pltpu.VMEM((a_width, TQ), F32),
            pltpu.VMEM((TK, TQ), F32),
            pltpu.VMEM((nkb, TK, TQ), I32),
            pltpu.VMEM((KEY_BITS + _index_bits(s), nkb, SUBLANES, TQ), I32),
            pltpu.VMEM((nkb, TK, TQ), F32),
        ],
        compiler_params=pltpu.CompilerParams(
            dimension_semantics=("arbitrary", "arbitrary"), vmem_limit_bytes=VMEM_LIMIT),
        name="attention",
    )(lam_vecs, proj, proj, avt, a_bias, subln_g_col, proj, proj, proj, proj, bvt, iwt, b_bias)


TAIL_ROWS = 256


def _tail_kernel(ya_ref, yb_ref, wo_ref, x_ref, g_mix_ref, g_in_ref, wu_ref, wd_ref, g_out_ref,
                 o_ref, x1_ref, h_ref, acc_ref):
    f = pl.program_id(1)
    last = pl.num_programs(1) - 1
    chunks = [slice(r, r + TAIL_ROWS) for r in range(0, x_ref.shape[0], TAIL_ROWS)]

    def mlp_part(rows):
        u = jnp.maximum(_dot(h_ref[rows, :], wu_ref[...]), 0.0)
        return _dot((u * u).astype(BF16), wd_ref[...])

    @pl.when(f == 0)
    def _():
        half = ya_ref.shape[-1]
        for rows in chunks:
            m = (_dot(ya_ref[rows, :], wo_ref[0:half, :])
                 + _dot(yb_ref[rows, :], wo_ref[half:2 * half, :]))
            x1 = x_ref[rows, :] + _rms(m, g_mix_ref[...])
            x1_ref[rows, :] = x1
            h_ref[rows, :] = _rms(x1, g_in_ref[...]).astype(BF16)
        acc_ref[...] = mlp_part(slice(None))

    @pl.when((f > 0) & (f < last))
    def _():
        acc_ref[...] += mlp_part(slice(None))

    @pl.when(f == last)
    def _():
        for rows in chunks:
            y = acc_ref[rows, :] + mlp_part(rows)
            o_ref[rows, :] = x1_ref[rows, :] + _rms(y, g_out_ref[...])


def _layer_tail(ya, yb, wo, x, g_mix, g_in, wu, wd, layer, g_out, *, tm=1024, tf=1024):
    n, d = x.shape
    half = ya.shape[-1]
    ff = wu.shape[-1]
    assert ff // tf >= 2 and tm % TAIL_ROWS == 0
    return pl.pallas_call(
        _tail_kernel,
        grid=(n // tm, ff // tf),
        in_specs=[
            pl.BlockSpec((tm, half), lambda i, f: (i, 0)),
            pl.BlockSpec((tm, half), lambda i, f: (i, 0)),
            pl.BlockSpec((2 * half, d), lambda i, f: (0, 0)),
            pl.BlockSpec((tm, d), lambda i, f: (i, 0)),
            pl.BlockSpec((1, d), lambda i, f: (0, 0)),
            pl.BlockSpec((1, d), lambda i, f: (0, 0)),
            pl.BlockSpec((None, d, tf), lambda i, f: (layer, 0, f)),
            pl.BlockSpec((None, tf, d), lambda i, f: (layer, f, 0)),
            pl.BlockSpec((1, d), lambda i, f: (0, 0)),
        ],
        out_specs=pl.BlockSpec((tm, d), lambda i, f: (i, 0)),
        out_shape=jax.ShapeDtypeStruct((n, d), F32),
        scratch_shapes=[pltpu.VMEM((tm, d), F32), pltpu.VMEM((tm, d), BF16),
                        pltpu.VMEM((tm, d), F32)],
        compiler_params=pltpu.CompilerParams(
            dimension_semantics=("arbitrary", "arbitrary"), vmem_limit_bytes=VMEM_LIMIT),
        name="layer_tail",
    )(ya, yb, wo, x, g_mix, g_in, wu, wd, g_out)


CONV_ROWS = 64
U_HIST = 32
Z_HIST = 8


def _conv_kernel(x_ref, g_in_ref, w_in_ref, dw_w_ref, dw_b_ref, ln_g_ref, ln_b_ref, sc_w_ref,
                 u_ref, z_ref, ubuf_ref, zbuf_ref, shift_ref, conv_ref, *, ts):
    @pl.when(pl.program_id(1) == 0)
    def _():
        ubuf_ref[0:U_HIST, :] = jnp.zeros((U_HIST, CONV_CH), F32)
        zbuf_ref[0:Z_HIST, :] = jnp.zeros((Z_HIST, SC_CH), F32)

    x = x_ref[0]
    hb = _rms(x, g_in_ref[...]).astype(BF16)
    c = CONV_CH
    ca = _dot(hb, w_in_ref[:, 0:c])
    cg = _dot(hb, w_in_ref[:, c:2 * c])
    ubuf_ref[U_HIST:U_HIST + ts, :] = ca * jax.nn.sigmoid(cg)
    dc = _dot(hb, w_in_ref[:, 3 * c:4 * c])
    dh = _dot(hb, w_in_ref[:, 4 * c:5 * c])
    zbuf_ref[Z_HIST:Z_HIST + ts, :] = dc * dh

    first_off = U_HIST - (CONV_WIDTH - 1)
    shift_rows = shift_ref.shape[1]
    for r in range(1, SUBLANES):
        shift_ref[r - 1] = ubuf_ref[r:r + shift_rows, :]

    for base in range(0, ts, CONV_ROWS):
        acc = jnp.zeros((CONV_ROWS, c), F32)
        for j in range(CONV_WIDTH):
            r, a = (first_off + j) % SUBLANES, (first_off + j) // SUBLANES
            rows = pl.ds(base + SUBLANES * a, CONV_ROWS)
            taps = ubuf_ref[rows, :] if r == 0 else shift_ref[r - 1, rows, :]
            acc = acc + dw_w_ref[j:j + 1, :] * taps
        conv_ref[base:base + CONV_ROWS, :] = acc
    u = conv_ref[...] + dw_b_ref[...]
    mu = jnp.mean(u, axis=-1, keepdims=True)
    ctr = u - mu
    var = jnp.mean(ctr * ctr, axis=-1, keepdims=True)
    u = ctr * lax.rsqrt(var + EPS) * ln_g_ref[...] + ln_b_ref[...]
    u = u * jax.nn.sigmoid(u)

    z = jnp.zeros((ts, c), F32)
    for j in range(SC_WIDTH):
        off = Z_HIST - (SC_WIDTH - 1) + j
        z = z + sc_w_ref[j:j + 1, :] * zbuf_ref[off:off + ts, :]
    z = _dot(hb, w_in_ref[:, 2 * c:3 * c]) * z

    ubuf_ref[0:U_HIST, :] = ubuf_ref[ts:ts + U_HIST, :]
    zbuf_ref[0:Z_HIST, :] = zbuf_ref[ts:ts + Z_HIST, :]

    u_ref[0] = u.astype(BF16)
    z_ref[0] = z.astype(BF16)


def _conv_mixer(x, g_in, w_in, dw_w, dw_b, ln_g, ln_b, sc_w, *, ts=512):
    bsz, s, d = x.shape
    full = lambda a: pl.BlockSpec(a.shape, lambda b, i: (0,) * a.ndim)
    return pl.pallas_call(
        functools.partial(_conv_kernel, ts=ts),
        grid=(bsz, s // ts),
        in_specs=[pl.BlockSpec((1, ts, d), lambda b, i: (b, i, 0)),
                  full(g_in), full(w_in), full(dw_w), full(dw_b), full(ln_g), full(ln_b),
                  full(sc_w)],
        out_specs=[pl.BlockSpec((1, ts, CONV_CH), lambda b, i: (b, i, 0)),
                   pl.BlockSpec((1, ts, SC_CH), lambda b, i: (b, i, 0))],
        out_shape=[jax.ShapeDtypeStruct((bsz, s, CONV_CH), BF16),
                   jax.ShapeDtypeStruct((bsz, s, SC_CH), BF16)],
        scratch_shapes=[pltpu.VMEM((U_HIST + ts, CONV_CH), F32),
                        pltpu.VMEM((Z_HIST + ts, SC_CH), F32),
                        pltpu.VMEM((SUBLANES - 1, ts + U_HIST - SUBLANES, CONV_CH), F32),
                        pltpu.VMEM((ts, CONV_CH), F32)],
        compiler_params=pltpu.CompilerParams(
            dimension_semantics=("arbitrary", "arbitrary"), vmem_limit_bytes=VMEM_LIMIT),
        name="conv_mixer",
    )(x, g_in, w_in, dw_w, dw_b, ln_g, ln_b, sc_w)


def _t5_bucket(rel):
    nb = NUM_BUCKETS // 2
    ret = jnp.where(rel > 0, nb, 0)
    n = jnp.abs(rel)
    max_exact = nb // 2
    nf = jnp.maximum(n, 1).astype(jnp.float32)
    large = max_exact + (jnp.log(nf / max_exact) / math.log(MAX_DISTANCE / max_exact)
                         * (nb - max_exact)).astype(jnp.int32)
    large = jnp.minimum(large, nb - 1)
    return ret + jnp.where(n < max_exact, n, large)


REL_LO = -(TQ + TK - 1)
REL_LEN = TK - REL_LO
REL_LANES = -(-REL_LEN // LANES) * LANES


def _bias_kernel(fr_ref, a_ref, b_ref):
    krow = lax.broadcasted_iota(I32, (TK, TQ), 0)
    qcol = lax.broadcasted_iota(I32, (TK, TQ), 1)
    admissible = (krow // CHUNK) <= (qcol // CHUNK)
    zeros = jnp.zeros((TK, TQ), F32)
    for h in range(A_HEADS + B_HEADS):
        table = jnp.broadcast_to(fr_ref[h], (TK, REL_LANES))
        blocks = []
        for shift in (TK, 0):
            s0 = (-shift - REL_LO - (REL_LEN - 1)) % REL_LANES
            blocks.append(pltpu.roll(table, s0, 1, stride=1, stride_axis=0)[:, 0:TQ])
        prev, diag = blocks
        if h < A_HEADS:
            a_ref[0, h], a_ref[1, h], a_ref[2, h] = zeros, prev, jnp.where(admissible, diag, NEG)
        else:
            g = h - A_HEADS
            b_ref[0, g], b_ref[1, g], b_ref[2, g] = zeros, prev, diag


def _bias_tables(rel_bias):
    rel = jnp.arange(REL_LO, TK, dtype=jnp.int32)
    far_bucket = NUM_BUCKETS // 2 - 1
    f = (rel_bias[_t5_bucket(rel)].astype(F32) - rel_bias[far_bucket].astype(F32)[None, :]).T
    fr = jnp.pad(f[:, ::-1] * LOG2E, ((0, 0), (0, REL_LANES - REL_LEN)))[:, None, :]
    heads = A_HEADS + B_HEADS
    return pl.pallas_call(
        _bias_kernel,
        grid=(1,),
        in_specs=[pl.BlockSpec((heads, 1, REL_LANES), lambda i: (0, 0, 0))],
        out_specs=[pl.BlockSpec((3, A_HEADS, TK, TQ), lambda i: (0, 0, 0, 0)),
                   pl.BlockSpec((3, B_HEADS, TK, TQ), lambda i: (0, 0, 0, 0))],
        out_shape=[jax.ShapeDtypeStruct((3, A_HEADS, TK, TQ), F32),
                   jax.ShapeDtypeStruct((3, B_HEADS, TK, TQ), F32)],
        compiler_params=pltpu.CompilerParams(
            dimension_semantics=("arbitrary",), vmem_limit_bytes=VMEM_LIMIT),
        name="bias_tables",
    )(fr)


def _attn_weights(w_in):
    o = np.cumsum([0, 512, 512, 512, 512, 64, 64, 256, 32, 8])
    aq, ak, av, bq, bk, bv, iq, ik, iw = [w_in[:, o[n]:o[n + 1]] for n in range(9)]
    aq = aq * (A_HEAD_DIM ** -0.5 * LOG2E)
    bq = bq * (B_HEAD_DIM ** -0.5 * LOG2E)
    w = jnp.concatenate([aq, ak, bq, iq, bk, bk, ik, ik, ik, ik], axis=1).astype(BF16)
    pad = jnp.zeros((w_in.shape[0], VT_ROWS - ROW_IW - IDX_HEADS), w_in.dtype)
    wt = jnp.concatenate([av, bv, iw, pad], axis=1).T.astype(BF16)
    return w, wt


def kernel(x, rel_bias, norm_g, w_mlp_up, w_mlp_down, attn_w_in, attn_w_out, diff_lambda,
           diff_subln_g, conv_w_in, conv_w_out, conv_dw_w, conv_dw_b, conv_ln_g, conv_ln_b,
           sconv_w):
    bsz, s, d = x.shape
    depth = norm_g.shape[0]
    top_k = min(TOPK_MAX, s // 4)
    row = lambda v: v.reshape(1, -1)
    a_bias, b_bias = _bias_tables(rel_bias)
    w_up, w_down = w_mlp_up.astype(BF16), w_mlp_down.astype(BF16)
    for i in range(depth):
        j = i // 2
        if i % 2 == 0:
            lambda_init = 0.8 - 0.6 * math.exp(-0.3 * i)
            w, wt = _attn_weights(attn_w_in[j])
            proj, avt, bvt, iwt = _attn_in(x.reshape(bsz, s, d), row(norm_g[i, 0]), w, wt)
            ya, yb = _attention(proj, avt, bvt, iwt, diff_lambda[j], a_bias, b_bias,
                                diff_subln_g[j].reshape(-1, 1), lambda_init=lambda_init,
                                top_k=top_k)
            w_out = attn_w_out[j]
        else:
            ya, yb = _conv_mixer(x.reshape(bsz, s, d), row(norm_g[i, 0]),
                                 conv_w_in[j].astype(BF16), conv_dw_w[j], row(conv_dw_b[j]),
                                 row(conv_ln_g[j]), row(conv_ln_b[j]), sconv_w[j])
            w_out = conv_w_out[j]
        x = _layer_tail(ya.reshape(bsz * s, -1), yb.reshape(bsz * s, -1), w_out.astype(BF16),
                        x.reshape(bsz * s, d), row(norm_g[i, 1]), row(norm_g[i, 2]),
                        w_up, w_down, i, row(norm_g[i, 3]))
    return x.reshape(bsz, s, d)
```

```python
import functools
import math

import numpy as np
import jax
import jax.numpy as jnp
from jax import lax
from jax.experimental import pallas as pl
from jax.experimental.pallas import tpu as pltpu

F32 = jnp.float32
BF16 = jnp.bfloat16
I32 = jnp.int32

CHUNK = 64
NUM_BUCKETS = 32
MAX_DISTANCE = 128
EPS = 1e-6
NEG = -1e30
LOG2E = math.log2(math.e)
A_HEADS = 4
A_HEAD_DIM = 64
A_V_DIM = 2 * A_HEAD_DIM
B_HEADS = 8
B_HEAD_DIM = 64
IDX_HEADS = 8
IDX_DIM = 32
TOPK_MAX = 256
CONV_CH = 512
CONV_WIDTH = 31
SC_CH = 512
SC_WIDTH = 3

LANES = 128
SUBLANES = 8
VMEM_LIMIT = 56 * 1024 * 1024
TAIL_VMEM_LIMIT = 60 * 1024 * 1024

TQ = 256
TK = 256
COL_AQ, COL_AK, COL_BQ, COL_IQ, COL_KB, COL_IK, PROJ_W = 0, 512, 1024, 1536, 1792, 1920, 2048
ROW_AV, ROW_BV, ROW_IW, VT_ROWS = 0, 512, 576, 640
ONES_ROWS = 16
INT_MIN = -2 ** 31
KEY_NEG_INF = INT_MIN + 0x7FFFFF


def _rms(x, g):
    return x * lax.rsqrt(jnp.mean(x * x, axis=-1, keepdims=True) + EPS) * g


def _dot_nt(a, b):
    return lax.dot_general(a, b, (((1,), (1,)), ((), ())), preferred_element_type=F32)


def _dot(a, b):
    return jnp.dot(a, b, preferred_element_type=F32)


def _lane_band(x, lo, width):
    lane = lax.broadcasted_iota(I32, x.shape, 1)
    return jnp.where((lane >= lo) & (lane < lo + width), x, jnp.zeros_like(x))


KEY_BITS = 32


def _bit_planes(keys):
    assert keys.shape[0] == KEY_BITS * SUBLANES
    u = keys ^ INT_MIN
    a = [u[SUBLANES * r:SUBLANES * (r + 1), :] for r in range(KEY_BITS)]
    j, m = KEY_BITS // 2, (1 << (KEY_BITS // 2)) - 1
    while j:
        mask = np.int32(np.uint32(m))
        k = 0
        while k < KEY_BITS:
            t = (a[k] ^ lax.shift_right_logical(a[k + j], jnp.int32(j))) & mask
            a[k] = a[k] ^ t
            a[k + j] = a[k + j] ^ lax.shift_left(t, jnp.int32(j))
            k = (k + j + 1) & ~j
        j >>= 1
        m = (m ^ (m << j)) & 0xFFFFFFFF
    return a


def _index_bits(seq):
    bits = seq.bit_length() - 1
    assert seq == 1 << bits and seq >= KEY_BITS * SUBLANES
    return bits


def _index_planes(nkb):
    shape = (nkb, SUBLANES, TQ)
    blk = lax.broadcasted_iota(I32, shape, 0)
    sub = lax.broadcasted_iota(I32, shape, 1)
    sub_bits = SUBLANES.bit_length() - 1
    word_bits = KEY_BITS.bit_length() - 1
    planes = []
    for b in range(_index_bits(nkb * TK) - 1, -1, -1):
        if b >= sub_bits + word_bits:
            on = ((nkb - 1 - blk) >> (b - sub_bits - word_bits)) & 1
            planes.append(jnp.where(on == 1, -1, 0))
        elif b >= sub_bits:
            t = b - sub_bits
            pattern = sum(1 << k for k in range(KEY_BITS) if (k >> t) & 1)
            planes.append(jnp.full(shape, np.int32(np.uint32(pattern)), I32))
        else:
            on = ((SUBLANES - 1 - sub) >> b) & 1
            planes.append(jnp.where(on == 1, -1, 0))
    return planes


def _bias_kind(j, i):
    return jnp.clip(j - i + 2, 0, 2)


def _key_block_stream(i, n_chains, s_ref, acc_ref, logits, v_aug):
    chains = range(n_chains)

    def stage(j, c, kind):
        s = logits(j, c, kind)
        s_ref[c] = s
        return jnp.max(s, axis=0, keepdims=True)

    def absorb(j, c, m_old, block_max):
        m_new = jnp.maximum(m_old, block_max)
        alpha = jnp.exp2(m_old - m_new)
        p = jnp.exp2(s_ref[c] - m_new).astype(BF16)
        acc_ref[c] = acc_ref[c] * alpha + _dot(v_aug(j, c), p)
        return m_new

    def start():
        acc_ref[...] = jnp.zeros_like(acc_ref)
        return (jnp.full((1, TQ), NEG, F32),) * n_chains, tuple(stage(0, c, 'any') for c in chains)

    def step(j, carry, kind):
        ms, bms = carry
        out = [(absorb(j, c, ms[c], bms[c]), stage(j + 1, c, kind)) for c in chains]
        return tuple(o[0] for o in out), tuple(o[1] for o in out)

    def finish(carry):
        ms, bms = carry
        for c in chains:
            absorb(i, c, ms[c], bms[c])

    return start, step, finish


def _run_key_block_streams(i, streams):
    def body(j, carries, kind):
        return tuple(step(j, c, kind) for (_, step, _), c in zip(streams, carries))

    carries = tuple(start() for start, _, _ in streams)
    n_far = jnp.maximum(i - 2, 0)
    carries = lax.fori_loop(0, n_far, functools.partial(body, kind='far'), carries)
    carries = lax.fori_loop(n_far, i, functools.partial(body, kind='near'), carries)
    for (_, _, finish), c in zip(streams, carries):
        finish(c)


def _attn_in_kernel(x_ref, g_ref, w_ref, wt_ref, proj_ref, avt_ref, bvt_ref, iwt_ref, *, tm):
    hb = _rms(x_ref[0], g_ref[...]).astype(BF16)
    proj_ref[0] = _dot(hb, w_ref[...]).astype(BF16)
    t = _dot_nt(wt_ref[...], hb)
    ones = jnp.ones((ONES_ROWS, TK), BF16)
    for c in range(tm // TK):
        cols = slice(c * TK, (c + 1) * TK)
        for h in range(A_HEADS):
            r = ROW_AV + h * A_V_DIM
            avt_ref[0, c, h, 0:A_V_DIM, :] = t[r:r + A_V_DIM, cols].astype(BF16)
            avt_ref[0, c, h, A_V_DIM:A_V_DIM + ONES_ROWS, :] = ones
        bvt_ref[0, c, 0:B_HEAD_DIM, :] = t[ROW_BV:ROW_IW, cols].astype(BF16)
        bvt_ref[0, c, B_HEAD_DIM:B_HEAD_DIM + ONES_ROWS, :] = ones
    iwt_ref[0] = t[ROW_IW:ROW_IW + IDX_HEADS, :] * (IDX_HEADS ** -0.5 * IDX_DIM ** -0.5)


def _attn_in(x, g, w, wt, *, tm=512):
    bsz, s, d = x.shape
    av_blk = (A_HEADS, A_V_DIM + ONES_ROWS, TK)
    bv_blk = (B_HEAD_DIM + ONES_ROWS, TK)
    return pl.pallas_call(
        functools.partial(_attn_in_kernel, tm=tm),
        grid=(bsz, s // tm),
        in_specs=[
            pl.BlockSpec((1, tm, d), lambda b, i: (b, i, 0)),
            pl.BlockSpec((1, d), lambda b, i: (0, 0)),
            pl.BlockSpec((d, PROJ_W), lambda b, i: (0, 0)),
            pl.BlockSpec((VT_ROWS, d), lambda b, i: (0, 0)),
        ],
        out_specs=[
            pl.BlockSpec((1, tm, PROJ_W), lambda b, i: (b, i, 0)),
            pl.BlockSpec((1, tm // TK) + av_blk, lambda b, i: (b, i, 0, 0, 0)),
            pl.BlockSpec((1, tm // TK) + bv_blk, lambda b, i: (b, i, 0, 0)),
            pl.BlockSpec((1, IDX_HEADS, tm), lambda b, i: (b, 0, i)),
        ],
        out_shape=[
            jax.ShapeDtypeStruct((bsz, s, PROJ_W), BF16),
            jax.ShapeDtypeStruct((bsz, s // TK) + av_blk, BF16),
            jax.ShapeDtypeStruct((bsz, s // TK) + bv_blk, BF16),
            jax.ShapeDtypeStruct((bsz, IDX_HEADS, s), F32),
        ],
        compiler_params=pltpu.CompilerParams(
            dimension_semantics=("arbitrary", "arbitrary"), vmem_limit_bytes=VMEM_LIMIT),
        name="attn_in",
    )(x, g, w, wt)


def _key_rows(ref, j):
    return ref[0, pl.ds(pl.multiple_of(j * TK, TK), TK), :]


def _indexer(i, iq_ref, ik_ref, iwt_ref, score_ref, key_ref, plane_ref):
    iq = iq_ref[0]
    per_blk = LANES // IDX_DIM
    iqh = [_lane_band(iq[:, (h // per_blk) * LANES:(h // per_blk + 1) * LANES],
                      (h % per_blk) * IDX_DIM, IDX_DIM) for h in range(IDX_HEADS)]
    w = iwt_ref[0]

    def stage_scores(j):
        ikb = _key_rows(ik_ref, j)
        half = TK // 2
        for r in (0, half):
            acc = jnp.zeros((half, TQ), F32)
            for h in range(IDX_HEADS):
                acc = acc + jnp.maximum(_dot_nt(ikb[r:r + half], iqh[h]), 0.0) * w[h:h + 1, :]
            score_ref[r:r + half, :] = acc

    def staged_keys():
        acc = score_ref[...]
        acc = jnp.where(acc == 0.0, 0.0, acc)
        bits = lax.bitcast_convert_type(acc, I32)
        return bits ^ ((bits >> 31) & 0x7FFFFFFF)

    def store_keys(j, keys):
        key_ref[j] = keys
        for p, plane in enumerate(_bit_planes(keys)):
            plane_ref[p, j] = plane

    def per_block(j):
        keys = staged_keys()
        stage_scores(j + 1)
        store_keys(j, keys)

    def finish():
        krow = lax.broadcasted_iota(I32, (TK, TQ), 0)
        qcol = lax.broadcasted_iota(I32, (TK, TQ), 1)
        admissible = (krow // CHUNK) <= (qcol // CHUNK)
        store_keys(i, jnp.where(admissible, staged_keys(), KEY_NEG_INF))

    stage_scores(0)
    return per_block, finish


def _select_mask(i, key_ref, plane_ref, mask_ref, *, top_k, seq):
    nkb = seq // TK
    blk = lax.broadcasted_iota(I32, (nkb, SUBLANES, TQ), 0)
    alive0 = jnp.where(blk <= i, -1, 0)

    def bit_body(p, carry, n_bits):
        alive, n_gt, thr_u = carry
        hit = alive & plane_ref[p]
        pc = jnp.sum(lax.population_count(hit), axis=0)
        n_hit = jnp.sum(pc.astype(F32), axis=0, keepdims=True)
        take = (n_gt + n_hit) >= top_k
        thr_u = thr_u | jnp.where(take, lax.shift_left(jnp.int32(1), n_bits - 1 - p), 0)
        n_gt = n_gt + jnp.where(take, 0.0, n_hit)
        alive = jnp.where(take, hit, alive ^ hit)
        return alive, n_gt, thr_u

    zero = jnp.zeros((1, TQ), I32)
    alive, n_gt, thr_u = lax.fori_loop(
        0, KEY_BITS, lambda p, c: bit_body(p, c, KEY_BITS), (alive0, jnp.zeros((1, TQ), F32), zero))
    idx_bits = _index_bits(seq)
    _, _, rev_idx = lax.fori_loop(
        KEY_BITS, KEY_BITS + idx_bits,
        lambda p, c: bit_body(p, c, KEY_BITS + idx_bits), (alive, n_gt, zero))
    thr = thr_u ^ INT_MIN
    lim = seq - rev_idx
    lim = jnp.where(thr > KEY_NEG_INF, lim, seq)
    thr = jnp.maximum(thr, KEY_NEG_INF + 1)
    krow = lax.broadcasted_iota(I32, (TK, TQ), 0)

    def mask_body(j, c):
        late = jnp.where(krow >= lim - j * TK, 1, 0)
        mask_ref[j] = jnp.where(key_ref[j] - late >= thr, 0.0, NEG)
        return c

    lax.fori_loop(0, i + 1, mask_body, 0)


def _attn_kernel(lam_ref, aq_ref, ak_ref, avt_ref, abias_ref, g_ref,
                 bq_ref, iq_ref, kb_ref, ik_ref, bvt_ref, iwt_ref, bbias_ref,
                 ya_ref, yb_ref,
                 sa_ref, sb_ref, aacc_ref, bacc_ref, ot_ref,
                 score_ref, key_ref, plane_ref, mask_ref, *, lambda_init, top_k, seq):
    i = pl.program_id(1)
    nkb = seq // TK
    n_maps = 2 * A_HEADS

    @pl.when((i == 0) & (pl.program_id(0) == 0))
    def _():
        plane_ref[0:KEY_BITS] = jnp.zeros((KEY_BITS, nkb, SUBLANES, TQ), I32)
        for t, plane in enumerate(_index_planes(nkb)):
            plane_ref[KEY_BITS + t] = plane

    index_block, finish_index = _indexer(i, iq_ref, ik_ref, iwt_ref, score_ref, key_ref, plane_ref)

    def index_body(j, c):
        index_block(j)
        return c

    lax.fori_loop(0, i, index_body, 0)
    finish_index()
    _select_mask(i, key_ref, plane_ref, mask_ref, top_k=top_k, seq=seq)

    lv = lam_ref[...]
    lam = (jnp.exp(jnp.sum(lv[0:1] * lv[1:2], axis=-1, keepdims=True))
           - jnp.exp(jnp.sum(lv[2:3] * lv[3:4], axis=-1, keepdims=True)) + lambda_init)

    q = aq_ref[0]
    qm = []
    for h in range(A_HEADS):
        q12 = q[:, h * LANES:(h + 1) * LANES]
        qm += [_lane_band(q12, 0, A_HEAD_DIM), _lane_band(q12, A_HEAD_DIM, A_HEAD_DIM)]

    def a_logits(j, c, kind):
        h = c // 2
        kb = ak_ref[0, pl.ds(pl.multiple_of(j * TK, TK), TK), h * LANES:(h + 1) * LANES]
        s = _dot_nt(kb, qm[c])
        return s if kind == 'far' else s + abias_ref[_bias_kind(j, i), h]

    qb = bq_ref[0]
    qh = [_lane_band(qb[:, (h // 2) * LANES:(h // 2 + 1) * LANES],
                     (h % 2) * B_HEAD_DIM, B_HEAD_DIM) for h in range(B_HEADS)]

    def b_logits(j, h, kind):
        s = _dot_nt(_key_rows(kb_ref, j), qh[h]) + mask_ref[j]
        return s if kind == 'far' else s + bbias_ref[_bias_kind(j, i), h]

    _run_key_block_streams(i, [
        _key_block_stream(i, n_maps, sa_ref, aacc_ref, a_logits, lambda j, c: avt_ref[0, j, c // 2]),
        _key_block_stream(i, B_HEADS, sb_ref, bacc_ref, b_logits, lambda j, h: bvt_ref[0, j])])

    for h in range(A_HEADS):
        a1 = aacc_ref[2 * h]
        a2 = aacc_ref[2 * h + 1]
        a = (a1[0:A_V_DIM] / a1[A_V_DIM:A_V_DIM + 1]
             - lam * (a2[0:A_V_DIM] / a2[A_V_DIM:A_V_DIM + 1]))
        y = a * lax.rsqrt(jnp.mean(a * a, axis=0, keepdims=True) + EPS) * g_ref[...]
        ot_ref[h * A_V_DIM:(h + 1) * A_V_DIM, :] = y * (1.0 - lambda_init)
    ya_ref[0] = ot_ref[...].T.astype(BF16)

    for h in range(B_HEADS):
        a = bacc_ref[h]
        ot_ref[h * B_HEAD_DIM:(h + 1) * B_HEAD_DIM, :] = (
            a[0:B_HEAD_DIM] / a[B_HEAD_DIM:B_HEAD_DIM + 1])
    yb_ref[0] = ot_ref[...].T.astype(BF16)


def _attention(proj, avt, bvt, iwt, lam_vecs, a_bias, b_bias, subln_g_col, *, lambda_init, top_k):
    bsz, s, _ = proj.shape
    nq = s // TQ
    nkb = s // TK
    a_width = A_HEADS * A_V_DIM
    b_width = B_HEADS * B_HEAD_DIM
    assert a_width == b_width
    iq_w = IDX_HEADS * IDX_DIM
    once = pl.Buffered(1)
    return pl.pallas_call(
        functools.partial(_attn_kernel, lambda_init=lambda_init, top_k=top_k, seq=s),
        grid=(bsz, nq),
        in_specs=[
            pl.BlockSpec((4, A_HEAD_DIM), lambda b, i: (0, 0)),
            pl.BlockSpec((1, TQ, a_width), lambda b, i: (b, i, COL_AQ // a_width)),
            pl.BlockSpec((1, s, a_width), lambda b, i: (b, 0, COL_AK // a_width)),
            pl.BlockSpec((1, nkb, A_HEADS, A_V_DIM + ONES_ROWS, TK), lambda b, i: (b, 0, 0, 0, 0)),
            pl.BlockSpec((3, A_HEADS, TK, TQ), lambda b, i: (0, 0, 0, 0), once),
            pl.BlockSpec((A_V_DIM, 1), lambda b, i: (0, 0)),
            pl.BlockSpec((1, TQ, b_width), lambda b, i: (b, i, COL_BQ // b_width)),
            pl.BlockSpec((1, TQ, iq_w), lambda b, i: (b, i, COL_IQ // iq_w)),
            pl.BlockSpec((1, s, LANES), lambda b, i: (b, 0, COL_KB // LANES), once),
            pl.BlockSpec((1, s, LANES), lambda b, i: (b, 0, COL_IK // LANES), once),
            pl.BlockSpec((1, nkb, B_HEAD_DIM + ONES_ROWS, TK), lambda b, i: (b, 0, 0, 0), once),
            pl.BlockSpec((1, IDX_HEADS, TQ), lambda b, i: (b, 0, i)),
            pl.BlockSpec((3, B_HEADS, TK, TQ), lambda b, i: (0, 0, 0, 0), once),
        ],
        out_specs=[pl.BlockSpec((1, TQ, a_width), lambda b, i: (b, i, 0)),
                   pl.BlockSpec((1, TQ, b_width), lambda b, i: (b, i, 0))],
        out_shape=[jax.ShapeDtypeStruct((bsz, s, a_width), BF16),
                   jax.ShapeDtypeStruct((bsz, s, b_width), BF16)],
        scratch_shapes=[
            pltpu.VMEM((2 * A_HEADS, TK, TQ), F32),
            pltpu.VMEM((B_HEADS, TK, TQ), F32),
            pltpu.VMEM((2 * A_HEADS, A_V_DIM + ONES_ROWS, TQ), F32),
            pltpu.VMEM((B_HEADS, B_HEAD_DIM + ONES_ROWS, TQ), F32),
            pltpu.VMEM((a_width, TQ), F32),
            pltpu.VMEM((TK, TQ), F32),
            pltpu.VMEM((nkb, TK, TQ), I32),
            pltpu.VMEM((KEY_BITS + _index_bits(s), nkb, SUBLANES, TQ), I32),
            pltpu.VMEM((nkb, TK, TQ), F32),
        ],
        compiler_params=pltpu.CompilerParams(
            dimension_semantics=("arbitrary", "arbitrary"), vmem_limit_bytes=VMEM_LIMIT),
        name="attention",
    )(lam_vecs, proj, proj, avt, a_bias, subln_g_col, proj, proj, proj, proj, bvt, iwt, b_bias)


TAIL_ROWS = 256
TAIL_FF = 1024


def _tail_kernel(ya_ref, yb_ref, wo_ref, x_ref, g_mix_ref, g_in_ref, wu_ref, wd_ref, g_out_ref,
                 o_ref, x1_ref, h_ref, acc_ref):
    f = pl.program_id(1)
    last = pl.num_programs(1) - 1
    chunks = [slice(r, r + TAIL_ROWS) for r in range(0, x_ref.shape[0], TAIL_ROWS)]

    def mlp_part(rows):
        h = h_ref[rows, :]
        out = None
        for k in range(0, wu_ref.shape[1], TAIL_FF):
            u = jnp.maximum(_dot(h, wu_ref[:, k:k + TAIL_FF]), 0.0)
            t = _dot((u * u).astype(BF16), wd_ref[k:k + TAIL_FF, :])
            out = t if out is None else out + t
        return out

    @pl.when(f == 0)
    def _():
        half = ya_ref.shape[-1]
        for rows in chunks:
            m = (_dot(ya_ref[rows, :], wo_ref[0:half, :])
                 + _dot(yb_ref[rows, :], wo_ref[half:2 * half, :]))
            x1 = x_ref[rows, :] + _rms(m, g_mix_ref[...])
            x1_ref[rows, :] = x1
            h_ref[rows, :] = _rms(x1, g_in_ref[...]).astype(BF16)
        acc_ref[...] = mlp_part(slice(None))

    @pl.when((f > 0) & (f < last))
    def _():
        acc_ref[...] += mlp_part(slice(None))

    @pl.when(f == last)
    def _():
        for rows in chunks:
            y = acc_ref[rows, :] + mlp_part(rows)
            o_ref[rows, :] = x1_ref[rows, :] + _rms(y, g_out_ref[...])


def _layer_tail(ya, yb, wo, x, g_mix, g_in, wu, wd, layer, g_out, *, tm=1024, tf=2048):
    n, d = x.shape
    half = ya.shape[-1]
    ff = wu.shape[-1]
    assert ff // tf >= 2 and tm % TAIL_ROWS == 0
    return pl.pallas_call(
        _tail_kernel,
        grid=(n // tm, ff // tf),
        in_specs=[
            pl.BlockSpec((tm, half), lambda i, f: (i, 0)),
            pl.BlockSpec((tm, half), lambda i, f: (i, 0)),
            pl.BlockSpec((2 * half, d), lambda i, f: (0, 0)),
            pl.BlockSpec((tm, d), lambda i, f: (i, 0)),
            pl.BlockSpec((1, d), lambda i, f: (0, 0)),
            pl.BlockSpec((1, d), lambda i, f: (0, 0)),
            pl.BlockSpec((None, d, tf), lambda i, f: (layer, 0, f)),
            pl.BlockSpec((None, tf, d), lambda i, f: (layer, f, 0)),
            pl.BlockSpec((1, d), lambda i, f: (0, 0)),
        ],
        out_specs=pl.BlockSpec((tm, d), lambda i, f: (i, 0)),
        out_shape=jax.ShapeDtypeStruct((n, d), F32),
        scratch_shapes=[pltpu.VMEM((tm, d), F32), pltpu.VMEM((tm, d), BF16),
                        pltpu.VMEM((tm, d), F32)],
        compiler_params=pltpu.CompilerParams(
            dimension_semantics=("arbitrary", "arbitrary"), vmem_limit_bytes=TAIL_VMEM_LIMIT),
        name="layer_tail",
    )(ya, yb, wo, x, g_mix, g_in, wu, wd, g_out)


CONV_ROWS = 64
U_HIST = 32
Z_HIST = 8


def _conv_kernel(x_ref, g_in_ref, w_in_ref, dw_w_ref, dw_b_ref, ln_g_ref, ln_b_ref, sc_w_ref,
                 u_ref, z_ref, ubuf_ref, zbuf_ref, shift_ref, conv_ref, *, ts):
    @pl.when(pl.program_id(1) == 0)
    def _():
        ubuf_ref[0:U_HIST, :] = jnp.zeros((U_HIST, CONV_CH), F32)
        zbuf_ref[0:Z_HIST, :] = jnp.zeros((Z_HIST, SC_CH), F32)

    x = x_ref[0]
    hb = _rms(x, g_in_ref[...]).astype(BF16)
    c = CONV_CH
    ca = _dot(hb, w_in_ref[:, 0:c])
    cg = _dot(hb, w_in_ref[:, c:2 * c])
    ubuf_ref[U_HIST:U_HIST + ts, :] = ca * jax.nn.sigmoid(cg)
    dc = _dot(hb, w_in_ref[:, 3 * c:4 * c])
    dh = _dot(hb, w_in_ref[:, 4 * c:5 * c])
    zbuf_ref[Z_HIST:Z_HIST + ts, :] = dc * dh

    first_off = U_HIST - (CONV_WIDTH - 1)
    shift_rows = shift_ref.shape[1]
    for r in range(1, SUBLANES):
        shift_ref[r - 1] = ubuf_ref[r:r + shift_rows, :]

    for base in range(0, ts, CONV_ROWS):
        acc = jnp.zeros((CONV_ROWS, c), F32)
        for j in range(CONV_WIDTH):
            r, a = (first_off + j) % SUBLANES, (first_off + j) // SUBLANES
            rows = pl.ds(base + SUBLANES * a, CONV_ROWS)
            taps = ubuf_ref[rows, :] if r == 0 else shift_ref[r - 1, rows, :]
            acc = acc + dw_w_ref[j:j + 1, :] * taps
        conv_ref[base:base + CONV_ROWS, :] = acc
    u = conv_ref[...] + dw_b_ref[...]
    mu = jnp.mean(u, axis=-1, keepdims=True)
    ctr = u - mu
    var = jnp.mean(ctr * ctr, axis=-1, keepdims=True)
    u = ctr * lax.rsqrt(var + EPS) * ln_g_ref[...] + ln_b_ref[...]
    u = u * jax.nn.sigmoid(u)

    z = jnp.zeros((ts, c), F32)
    for j in range(SC_WIDTH):
        off = Z_HIST - (SC_WIDTH - 1) + j
        z = z + sc_w_ref[j:j + 1, :] * zbuf_ref[off:off + ts, :]
    z = _dot(hb, w_in_ref[:, 2 * c:3 * c]) * z

    ubuf_ref[0:U_HIST, :] = ubuf_ref[ts:ts + U_HIST, :]
    zbuf_ref[0:Z_HIST, :] = zbuf_ref[ts:ts + Z_HIST, :]

    u_ref[0] = u.astype(BF16)
    z_ref[0] = z.astype(BF16)


def _conv_mixer(x, g_in, w_in, dw_w, dw_b, ln_g, ln_b, sc_w, *, ts=512):
    bsz, s, d = x.shape
    full = lambda a: pl.BlockSpec(a.shape, lambda b, i: (0,) * a.ndim)
    return pl.pallas_call(
        functools.partial(_conv_kernel, ts=ts),
        grid=(bsz, s // ts),
        in_specs=[pl.BlockSpec((1, ts, d), lambda b, i: (b, i, 0)),
                  full(g_in), full(w_in), full(dw_w), full(dw_b), full(ln_g), full(ln_b),
                  full(sc_w)],
        out_specs=[pl.BlockSpec((1, ts, CONV_CH), lambda b, i: (b, i, 0)),
                   pl.BlockSpec((1, ts, SC_CH), lambda b, i: (b, i, 0))],
        out_shape=[jax.ShapeDtypeStruct((bsz, s, CONV_CH), BF16),
                   jax.ShapeDtypeStruct((bsz, s, SC_CH), BF16)],
        scratch_shapes=[pltpu.VMEM((U_HIST + ts, CONV_CH), F32),
                        pltpu.VMEM((Z_HIST + ts, SC_CH), F32),
                        pltpu.VMEM((SUBLANES - 1, ts + U_HIST - SUBLANES, CONV_CH), F32),
                        pltpu.VMEM((ts, CONV_CH), F32)],
        compiler_params=pltpu.CompilerParams(
            dimension_semantics=("arbitrary", "arbitrary"), vmem_limit_bytes=VMEM_LIMIT),
        name="conv_mixer",
    )(x, g_in, w_in, dw_w, dw_b, ln_g, ln_b, sc_w)


def _t5_bucket(rel):
    nb = NUM_BUCKETS // 2
    ret = jnp.where(rel > 0, nb, 0)
    n = jnp.abs(rel)
    max_exact = nb // 2
    nf = jnp.maximum(n, 1).astype(jnp.float32)
    large = max_exact + (jnp.log(nf / max_exact) / math.log(MAX_DISTANCE / max_exact)
                         * (nb - max_exact)).astype(jnp.int32)
    large = jnp.minimum(large, nb - 1)
    return ret + jnp.where(n < max_exact, n, large)


REL_LO = -(TQ + TK - 1)
REL_LEN = TK - REL_LO
REL_LANES = -(-REL_LEN // LANES) * LANES


def _bias_kernel(fr_ref, a_ref, b_ref):
    krow = lax.broadcasted_iota(I32, (TK, TQ), 0)
    qcol = lax.broadcasted_iota(I32, (TK, TQ), 1)
    admissible = (krow // CHUNK) <= (qcol // CHUNK)
    zeros = jnp.zeros((TK, TQ), F32)
    for h in range(A_HEADS + B_HEADS):
        table = jnp.broadcast_to(fr_ref[h], (TK, REL_LANES))
        blocks = []
        for shift in (TK, 0):
            s0 = (-shift - REL_LO - (REL_LEN - 1)) % REL_LANES
            blocks.append(pltpu.roll(table, s0, 1, stride=1, stride_axis=0)[:, 0:TQ])
        prev, diag = blocks
        if h < A_HEADS:
            a_ref[0, h], a_ref[1, h], a_ref[2, h] = zeros, prev, jnp.where(admissible, diag, NEG)
        else:
            g = h - A_HEADS
            b_ref[0, g], b_ref[1, g], b_ref[2, g] = zeros, prev, diag


def _bias_tables(rel_bias):
    rel = jnp.arange(REL_LO, TK, dtype=jnp.int32)
    far_bucket = NUM_BUCKETS // 2 - 1
    f = (rel_bias[_t5_bucket(rel)].astype(F32) - rel_bias[far_bucket].astype(F32)[None, :]).T
    fr = jnp.pad(f[:, ::-1] * LOG2E, ((0, 0), (0, REL_LANES - REL_LEN)))[:, None, :]
    heads = A_HEADS + B_HEADS
    return pl.pallas_call(
        _bias_kernel,
        grid=(1,),
        in_specs=[pl.BlockSpec((heads, 1, REL_LANES), lambda i: (0, 0, 0))],
        out_specs=[pl.BlockSpec((3, A_HEADS, TK, TQ), lambda i: (0, 0, 0, 0)),
                   pl.BlockSpec((3, B_HEADS, TK, TQ), lambda i: (0, 0, 0, 0))],
        out_shape=[jax.ShapeDtypeStruct((3, A_HEADS, TK, TQ), F32),
                   jax.ShapeDtypeStruct((3, B_HEADS, TK, TQ), F32)],
        compiler_params=pltpu.CompilerParams(
            dimension_semantics=("arbitrary",), vmem_limit_bytes=VMEM_LIMIT),
        name="bias_tables",
    )(fr)


def _attn_weights(w_in):
    o = np.cumsum([0, 512, 512, 512, 512, 64, 64, 256, 32, 8])
    aq, ak, av, bq, bk, bv, iq, ik, iw = [w_in[:, o[n]:o[n + 1]] for n in range(9)]
    aq = aq * (A_HEAD_DIM ** -0.5 * LOG2E)
    bq = bq * (B_HEAD_DIM ** -0.5 * LOG2E)
    w = jnp.concatenate([aq, ak, bq, iq, bk, bk, ik, ik, ik, ik], axis=1).astype(BF16)
    pad = jnp.zeros((w_in.shape[0], VT_ROWS - ROW_IW - IDX_HEADS), w_in.dtype)
    wt = jnp.concatenate([av, bv, iw, pad], axis=1).T.astype(BF16)
    return w, wt


def kernel(x, rel_bias, norm_g, w_mlp_up, w_mlp_down, attn_w_in, attn_w_out, diff_lambda,
           diff_subln_g, conv_w_in, conv_w_out, conv_dw_w, conv_dw_b, conv_ln_g, conv_ln_b,
           sconv_w):
    bsz, s, d = x.shape
    depth = norm_g.shape[0]
    top_k = min(TOPK_MAX, s // 4)
    row = lambda v: v.reshape(1, -1)
    a_bias, b_bias = _bias_tables(rel_bias)
    w_up, w_down = w_mlp_up.astype(BF16), w_mlp_down.astype(BF16)
    for i in range(depth):
        j = i // 2
        if i % 2 == 0:
            lambda_init = 0.8 - 0.6 * math.exp(-0.3 * i)
            w, wt = _attn_weights(attn_w_in[j])
            proj, avt, bvt, iwt = _attn_in(x.reshape(bsz, s, d), row(norm_g[i, 0]), w, wt)
            ya, yb = _attention(proj, avt, bvt, iwt, diff_lambda[j], a_bias, b_bias,
                                diff_subln_g[j].reshape(-1, 1), lambda_init=lambda_init,
                                top_k=top_k)
            w_out = attn_w_out[j]
        else:
            ya, yb = _conv_mixer(x.reshape(bsz, s, d), row(norm_g[i, 0]),
                                 conv_w_in[j].astype(BF16), conv_dw_w[j], row(conv_dw_b[j]),
                                 row(conv_ln_g[j]), row(conv_ln_b[j]), sconv_w[j])
            w_out = conv_w_out[j]
        x = _layer_tail(ya.reshape(bsz * s, -1), yb.reshape(bsz * s, -1), w_out.astype(BF16),
                        x.reshape(bsz * s, d), row(norm_g[i, 1]), row(norm_g[i, 2]),
                        w_up, w_down, i, row(norm_g[i, 3]))
    return x.reshape(bsz, s, d)
```

```python
import functools
import math

import numpy as np
import jax
import jax.numpy as jnp
from jax import lax
from jax.experimental import pallas as pl
from jax.experimental.pallas import tpu as pltpu

F32 = jnp.float32
BF16 = jnp.bfloat16
I32 = jnp.int32

CHUNK = 64
NUM_BUCKETS = 32
MAX_DISTANCE = 128
EPS = 1e-6
NEG = -1e30
LOG2E = math.log2(math.e)
A_HEADS = 4
A_HEAD_DIM = 64
A_V_DIM = 2 * A_HEAD_DIM
B_HEADS = 8
B_HEAD_DIM = 64
IDX_HEADS = 8
IDX_DIM = 32
TOPK_MAX = 256
CONV_CH = 512
CONV_WIDTH = 31
SC_CH = 512
SC_WIDTH = 3

LANES = 128
SUBLANES = 8
VMEM_LIMIT = 56 * 1024 * 1024
TAIL_VMEM_LIMIT = 60 * 1024 * 1024

TQ = 256
TK = 256
COL_AQ, COL_AK, COL_BQ, COL_IQ, COL_KB, COL_IK, PROJ_W = 0, 512, 1024, 1536, 1792, 1920, 2048
ROW_AV, ROW_BV, ROW_IW, VT_ROWS = 0, 512, 576, 640
ONES_ROWS = 16
INT_MIN = -2 ** 31
KEY_NEG_INF = INT_MIN + 0x7FFFFF


def _rms(x, g):
    return x * lax.rsqrt(jnp.mean(x * x, axis=-1, keepdims=True) + EPS) * g


def _dot_nt(a, b):
    return lax.dot_general(a, b, (((1,), (1,)), ((), ())), preferred_element_type=F32)


def _dot(a, b):
    return jnp.dot(a, b, preferred_element_type=F32)


def _lane_band(x, lo, width):
    lane = lax.broadcasted_iota(I32, x.shape, 1)
    return jnp.where((lane >= lo) & (lane < lo + width), x, jnp.zeros_like(x))


KEY_BITS = 32


def _bit_planes(keys):
    assert keys.shape[0] == KEY_BITS * SUBLANES
    u = keys ^ INT_MIN
    a = [u[SUBLANES * r:SUBLANES * (r + 1), :] for r in range(KEY_BITS)]
    j, m = KEY_BITS // 2, (1 << (KEY_BITS // 2)) - 1
    while j:
        mask = np.int32(np.uint32(m))
        k = 0
        while k < KEY_BITS:
            t = (a[k] ^ lax.shift_right_logical(a[k + j], jnp.int32(j))) & mask
            a[k] = a[k] ^ t
            a[k + j] = a[k + j] ^ lax.shift_left(t, jnp.int32(j))
            k = (k + j + 1) & ~j
        j >>= 1
        m = (m ^ (m << j)) & 0xFFFFFFFF
    return a


def _index_bits(seq):
    bits = seq.bit_length() - 1
    assert seq == 1 << bits and seq >= KEY_BITS * SUBLANES
    return bits


def _index_planes(nkb):
    shape = (nkb, SUBLANES, TQ)
    blk = lax.broadcasted_iota(I32, shape, 0)
    sub = lax.broadcasted_iota(I32, shape, 1)
    sub_bits = SUBLANES.bit_length() - 1
    word_bits = KEY_BITS.bit_length() - 1
    planes = []
    for b in range(_index_bits(nkb * TK) - 1, -1, -1):
        if b >= sub_bits + word_bits:
            on = ((nkb - 1 - blk) >> (b - sub_bits - word_bits)) & 1
            planes.append(jnp.where(on == 1, -1, 0))
        elif b >= sub_bits:
            t = b - sub_bits
            pattern = sum(1 << k for k in range(KEY_BITS) if (k >> t) & 1)
            planes.append(jnp.full(shape, np.int32(np.uint32(pattern)), I32))
        else:
            on = ((SUBLANES - 1 - sub) >> b) & 1
            planes.append(jnp.where(on == 1, -1, 0))
    return planes


def _bias_kind(j, i):
    return jnp.clip(j - i + 2, 0, 2)


def _key_block_stream(i, n_chains, s_ref, acc_ref, logits, v_aug):
    chains = range(n_chains)

    def stage(j, c, kind):
        s = logits(j, c, kind)
        s_ref[c] = s
        return jnp.max(s, axis=0, keepdims=True)

    def absorb(j, c, m_old, block_max):
        m_new = jnp.maximum(m_old, block_max)
        alpha = jnp.exp2(m_old - m_new)
        p = jnp.exp2(s_ref[c] - m_new).astype(BF16)
        acc_ref[c] = acc_ref[c] * alpha + _dot(v_aug(j, c), p)
        return m_new

    def start():
        acc_ref[...] = jnp.zeros_like(acc_ref)
        return (jnp.full((1, TQ), NEG, F32),) * n_chains, tuple(stage(0, c, 'any') for c in chains)

    def step(j, carry, kind):
        ms, bms = carry
        out = [(absorb(j, c, ms[c], bms[c]), stage(j + 1, c, kind)) for c in chains]
        return tuple(o[0] for o in out), tuple(o[1] for o in out)

    def finish(carry):
        ms, bms = carry
        for c in chains:
            absorb(i, c, ms[c], bms[c])

    return start, step, finish


def _run_key_block_streams(i, streams):
    def body(j, carries, kind):
        return tuple(step(j, c, kind) for (_, step, _), c in zip(streams, carries))

    carries = tuple(start() for start, _, _ in streams)
    n_far = jnp.maximum(i - 2, 0)
    carries = lax.fori_loop(0, n_far, functools.partial(body, kind='far'), carries)
    carries = lax.fori_loop(n_far, i, functools.partial(body, kind='near'), carries)
    for (_, _, finish), c in zip(streams, carries):
        finish(c)


def _attn_in_kernel(x_ref, g_ref, w_ref, wt_ref, proj_ref, avt_ref, bvt_ref, iwt_ref, *, tm):
    hb = _rms(x_ref[0], g_ref[...]).astype(BF16)
    proj_ref[0] = _dot(hb, w_ref[...]).astype(BF16)
    t = _dot_nt(wt_ref[...], hb)
    ones = jnp.ones((ONES_ROWS, TK), BF16)
    for c in range(tm // TK):
        cols = slice(c * TK, (c + 1) * TK)
        for h in range(A_HEADS):
            r = ROW_AV + h * A_V_DIM
            avt_ref[0, c, h, 0:A_V_DIM, :] = t[r:r + A_V_DIM, cols].astype(BF16)
            avt_ref[0, c, h, A_V_DIM:A_V_DIM + ONES_ROWS, :] = ones
        bvt_ref[0, c, 0:B_HEAD_DIM, :] = t[ROW_BV:ROW_IW, cols].astype(BF16)
        bvt_ref[0, c, B_HEAD_DIM:B_HEAD_DIM + ONES_ROWS, :] = ones
    iwt_ref[0] = t[ROW_IW:ROW_IW + IDX_HEADS, :] * (IDX_HEADS ** -0.5 * IDX_DIM ** -0.5)


def _attn_in(x, g, w, wt, *, tm=1024):
    bsz, s, d = x.shape
    av_blk = (A_HEADS, A_V_DIM + ONES_ROWS, TK)
    bv_blk = (B_HEAD_DIM + ONES_ROWS, TK)
    return pl.pallas_call(
        functools.partial(_attn_in_kernel, tm=tm),
        grid=(bsz, s // tm),
        in_specs=[
            pl.BlockSpec((1, tm, d), lambda b, i: (b, i, 0)),
            pl.BlockSpec((1, d), lambda b, i: (0, 0)),
            pl.BlockSpec((d, PROJ_W), lambda b, i: (0, 0)),
            pl.BlockSpec((VT_ROWS, d), lambda b, i: (0, 0)),
        ],
        out_specs=[
            pl.BlockSpec((1, tm, PROJ_W), lambda b, i: (b, i, 0)),
            pl.BlockSpec((1, tm // TK) + av_blk, lambda b, i: (b, i, 0, 0, 0)),
            pl.BlockSpec((1, tm // TK) + bv_blk, lambda b, i: (b, i, 0, 0)),
            pl.BlockSpec((1, IDX_HEADS, tm), lambda b, i: (b, 0, i)),
        ],
        out_shape=[
            jax.ShapeDtypeStruct((bsz, s, PROJ_W), BF16),
            jax.ShapeDtypeStruct((bsz, s // TK) + av_blk, BF16),
            jax.ShapeDtypeStruct((bsz, s // TK) + bv_blk, BF16),
            jax.ShapeDtypeStruct((bsz, IDX_HEADS, s), F32),
        ],
        compiler_params=pltpu.CompilerParams(
            dimension_semantics=("arbitrary", "arbitrary"), vmem_limit_bytes=VMEM_LIMIT),
        name="attn_in",
    )(x, g, w, wt)


def _key_rows(ref, j):
    return ref[0, pl.ds(pl.multiple_of(j * TK, TK), TK), :]


def _indexer(i, iq_ref, ik_ref, iwt_ref, score_ref, key_ref, plane_ref):
    iq = iq_ref[0]
    per_blk = LANES // IDX_DIM
    iqh = [_lane_band(iq[:, (h // per_blk) * LANES:(h // per_blk + 1) * LANES],
                      (h % per_blk) * IDX_DIM, IDX_DIM) for h in range(IDX_HEADS)]
    w = iwt_ref[0]

    def stage_scores(j):
        ikb = _key_rows(ik_ref, j)
        half = TK // 2
        for r in (0, half):
            acc = jnp.zeros((half, TQ), F32)
            for h in range(IDX_HEADS):
                acc = acc + jnp.maximum(_dot_nt(ikb[r:r + half], iqh[h]), 0.0) * w[h:h + 1, :]
            score_ref[r:r + half, :] = acc

    def staged_keys():
        acc = score_ref[...]
        acc = jnp.where(acc == 0.0, 0.0, acc)
        bits = lax.bitcast_convert_type(acc, I32)
        return bits ^ ((bits >> 31) & 0x7FFFFFFF)

    def store_keys(j, keys):
        key_ref[j] = keys
        for p, plane in enumerate(_bit_planes(keys)):
            plane_ref[p, j] = plane

    def per_block(j):
        keys = staged_keys()
        stage_scores(j + 1)
        store_keys(j, keys)

    def finish():
        krow = lax.broadcasted_iota(I32, (TK, TQ), 0)
        qcol = lax.broadcasted_iota(I32, (TK, TQ), 1)
        admissible = (krow // CHUNK) <= (qcol // CHUNK)
        store_keys(i, jnp.where(admissible, staged_keys(), KEY_NEG_INF))

    stage_scores(0)
    return per_block, finish


def _select_mask(i, key_ref, plane_ref, mask_ref, *, top_k, seq):
    nkb = seq // TK
    blk = lax.broadcasted_iota(I32, (nkb, SUBLANES, TQ), 0)
    alive0 = jnp.where(blk <= i, -1, 0)

    def bit_body(p, carry, n_bits):
        alive, n_gt, thr_u = carry
        hit = alive & plane_ref[p]
        pc = jnp.sum(lax.population_count(hit), axis=0)
        n_hit = jnp.sum(pc.astype(F32), axis=0, keepdims=True)
        take = (n_gt + n_hit) >= top_k
        thr_u = thr_u | jnp.where(take, lax.shift_left(jnp.int32(1), n_bits - 1 - p), 0)
        n_gt = n_gt + jnp.where(take, 0.0, n_hit)
        alive = jnp.where(take, hit, alive ^ hit)
        return alive, n_gt, thr_u

    zero = jnp.zeros((1, TQ), I32)
    alive, n_gt, thr_u = lax.fori_loop(
        0, KEY_BITS, lambda p, c: bit_body(p, c, KEY_BITS), (alive0, jnp.zeros((1, TQ), F32), zero))
    idx_bits = _index_bits(seq)
    _, _, rev_idx = lax.fori_loop(
        KEY_BITS, KEY_BITS + idx_bits,
        lambda p, c: bit_body(p, c, KEY_BITS + idx_bits), (alive, n_gt, zero))
    thr = thr_u ^ INT_MIN
    lim = seq - rev_idx
    lim = jnp.where(thr > KEY_NEG_INF, lim, seq)
    thr = jnp.maximum(thr, KEY_NEG_INF + 1)
    krow = lax.broadcasted_iota(I32, (TK, TQ), 0)

    def mask_body(j, c):
        late = jnp.where(krow >= lim - j * TK, 1, 0)
        mask_ref[j] = jnp.where(key_ref[j] - late >= thr, 0.0, NEG)
        return c

    lax.fori_loop(0, i + 1, mask_body, 0)


def _attn_kernel(lam_ref, aq_ref, ak_ref, avt_ref, abias_ref, g_ref,
                 bq_ref, iq_ref, kb_ref, ik_ref, bvt_ref, iwt_ref, bbias_ref,
                 ya_ref, yb_ref,
                 sa_ref, sb_ref, aacc_ref, bacc_ref, ot_ref,
                 score_ref, key_ref, plane_ref, mask_ref, *, lambda_init, top_k, seq):
    i = pl.program_id(1)
    nkb = seq // TK
    n_maps = 2 * A_HEADS

    @pl.when((i == 0) & (pl.program_id(0) == 0))
    def _():
        plane_ref[0:KEY_BITS] = jnp.zeros((KEY_BITS, nkb, SUBLANES, TQ), I32)
        for t, plane in enumerate(_index_planes(nkb)):
            plane_ref[KEY_BITS + t] = plane

    index_block, finish_index = _indexer(i, iq_ref, ik_ref, iwt_ref, score_ref, key_ref, plane_ref)

    def index_body(j, c):
        index_block(j)
        return c

    lax.fori_loop(0, i, index_body, 0)
    finish_index()
    _select_mask(i, key_ref, plane_ref, mask_ref, top_k=top_k, seq=seq)

    lv = lam_ref[...]
    lam = (jnp.exp(jnp.sum(lv[0:1] * lv[1:2], axis=-1, keepdims=True))
           - jnp.exp(jnp.sum(lv[2:3] * lv[3:4], axis=-1, keepdims=True)) + lambda_init)

    q = aq_ref[0]
    qm = []
    for h in range(A_HEADS):
        q12 = q[:, h * LANES:(h + 1) * LANES]
        qm += [_lane_band(q12, 0, A_HEAD_DIM), _lane_band(q12, A_HEAD_DIM, A_HEAD_DIM)]

    def a_logits(j, c, kind):
        h = c // 2
        kb = ak_ref[0, pl.ds(pl.multiple_of(j * TK, TK), TK), h * LANES:(h + 1) * LANES]
        s = _dot_nt(kb, qm[c])
        return s if kind == 'far' else s + abias_ref[_bias_kind(j, i), h]

    qb = bq_ref[0]
    qh = [_lane_band(qb[:, (h // 2) * LANES:(h // 2 + 1) * LANES],
                     (h % 2) * B_HEAD_DIM, B_HEAD_DIM) for h in range(B_HEADS)]

    def b_logits(j, h, kind):
        s = _dot_nt(_key_rows(kb_ref, j), qh[h]) + mask_ref[j]
        return s if kind == 'far' else s + bbias_ref[_bias_kind(j, i), h]

    _run_key_block_streams(i, [
        _key_block_stream(i, n_maps, sa_ref, aacc_ref, a_logits, lambda j, c: avt_ref[0, j, c // 2]),
        _key_block_stream(i, B_HEADS, sb_ref, bacc_ref, b_logits, lambda j, h: bvt_ref[0, j])])

    for h in range(A_HEADS):
        a1 = aacc_ref[2 * h]
        a2 = aacc_ref[2 * h + 1]
        a = (a1[0:A_V_DIM] / a1[A_V_DIM:A_V_DIM + 1]
             - lam * (a2[0:A_V_DIM] / a2[A_V_DIM:A_V_DIM + 1]))
        y = a * lax.rsqrt(jnp.mean(a * a, axis=0, keepdims=True) + EPS) * g_ref[...]
        ot_ref[h * A_V_DIM:(h + 1) * A_V_DIM, :] = y * (1.0 - lambda_init)
    ya_ref[0] = ot_ref[...].T.astype(BF16)

    for h in range(B_HEADS):
        a = bacc_ref[h]
        ot_ref[h * B_HEAD_DIM:(h + 1) * B_HEAD_DIM, :] = (
            a[0:B_HEAD_DIM] / a[B_HEAD_DIM:B_HEAD_DIM + 1])
    yb_ref[0] = ot_ref[...].T.astype(BF16)


def _attention(proj, avt, bvt, iwt, lam_vecs, a_bias, b_bias, subln_g_col, *, lambda_init, top_k):
    bsz, s, _ = proj.shape
    nq = s // TQ
    nkb = s // TK
    a_width = A_HEADS * A_V_DIM
    b_width = B_HEADS * B_HEAD_DIM
    assert a_width == b_width
    iq_w = IDX_HEADS * IDX_DIM
    once = pl.Buffered(1)
    return pl.pallas_call(
        functools.partial(_attn_kernel, lambda_init=lambda_init, top_k=top_k, seq=s),
        grid=(bsz, nq),
        in_specs=[
            pl.BlockSpec((4, A_HEAD_DIM), lambda b, i: (0, 0)),
            pl.BlockSpec((1, TQ, a_width), lambda b, i: (b, i, COL_AQ // a_width)),
            pl.BlockSpec((1, s, a_width), lambda b, i: (b, 0, COL_AK // a_width)),
            pl.BlockSpec((1, nkb, A_HEADS, A_V_DIM + ONES_ROWS, TK), lambda b, i: (b, 0, 0, 0, 0)),
            pl.BlockSpec((3, A_HEADS, TK, TQ), lambda b, i: (0, 0, 0, 0), once),
            pl.BlockSpec((A_V_DIM, 1), lambda b, i: (0, 0)),
            pl.BlockSpec((1, TQ, b_width), lambda b, i: (b, i, COL_BQ // b_width)),
            pl.BlockSpec((1, TQ, iq_w), lambda b, i: (b, i, COL_IQ // iq_w)),
            pl.BlockSpec((1, s, LANES), lambda b, i: (b, 0, COL_KB // LANES), once),
            pl.BlockSpec((1, s, LANES), lambda b, i: (b, 0, COL_IK // LANES), once),
            pl.BlockSpec((1, nkb, B_HEAD_DIM + ONES_ROWS, TK), lambda b, i: (b, 0, 0, 0), once),
            pl.BlockSpec((1, IDX_HEADS, TQ), lambda b, i: (b, 0, i)),
            pl.BlockSpec((3, B_HEADS, TK, TQ), lambda b, i: (0, 0, 0, 0), once),
        ],
        out_specs=[pl.BlockSpec((1, TQ, a_width), lambda b, i: (b, i, 0)),
                   pl.BlockSpec((1, TQ, b_width), lambda b, i: (b, i, 0))],
        out_shape=[jax.ShapeDtypeStruct((bsz, s, a_width), BF16),
                   jax.ShapeDtypeStruct((bsz, s, b_width), BF16)],
        scratch_shapes=[
            pltpu.VMEM((2 * A_HEADS, TK, TQ), F32),
            pltpu.VMEM((B_HEADS, TK, TQ), F32),
            pltpu.VMEM((2 * A_HEADS, A_V_DIM + ONES_ROWS, TQ), F32),
            pltpu.VMEM((B_HEADS, B_HEAD_DIM + ONES_ROWS, TQ), F32),
            pltpu.VMEM((a_width, TQ), F32),
            pltpu.VMEM((TK, TQ), F32),
            pltpu.VMEM((nkb, TK, TQ), I32),
            pltpu.VMEM((KEY_BITS + _index_bits(s), nkb, SUBLANES, TQ), I32),
            pltpu.VMEM((nkb, TK, TQ), F32),
        ],
        compiler_params=pltpu.CompilerParams(
            dimension_semantics=("arbitrary", "arbitrary"), vmem_limit_bytes=VMEM_LIMIT),
        name="attention",
    )(lam_vecs, proj, proj, avt, a_bias, subln_g_col, proj, proj, proj, proj, bvt, iwt, b_bias)


TAIL_ROWS = 256
TAIL_FF = 1024


def _tail_kernel(ya_ref, yb_ref, wo_ref, x_ref, g_mix_ref, g_in_ref, wu_ref, wd_ref, g_out_ref,
                 o_ref, x1_ref, h_ref, acc_ref):
    f = pl.program_id(1)
    last = pl.num_programs(1) - 1
    chunks = [slice(r, r + TAIL_ROWS) for r in range(0, x_ref.shape[0], TAIL_ROWS)]

    def mlp_part(rows):
        h = h_ref[rows, :]
        out = None
        for k in range(0, wu_ref.shape[1], TAIL_FF):
            u = jnp.maximum(_dot(h, wu_ref[:, k:k + TAIL_FF]), 0.0)
            t = _dot((u * u).astype(BF16), wd_ref[k:k + TAIL_FF, :])
            out = t if out is None else out + t
        return out

    @pl.when(f == 0)
    def _():
        half = ya_ref.shape[-1]
        for rows in chunks:
            m = (_dot(ya_ref[rows, :], wo_ref[0:half, :])
                 + _dot(yb_ref[rows, :], wo_ref[half:2 * half, :]))
            x1 = x_ref[rows, :] + _rms(m, g_mix_ref[...])
            x1_ref[rows, :] = x1
            h_ref[rows, :] = _rms(x1, g_in_ref[...]).astype(BF16)
        acc_ref[...] = mlp_part(slice(None))

    @pl.when((f > 0) & (f < last))
    def _():
        acc_ref[...] += mlp_part(slice(None))

    @pl.when(f == last)
    def _():
        for rows in chunks:
            y = acc_ref[rows, :] + mlp_part(rows)
            o_ref[rows, :] = x1_ref[rows, :] + _rms(y, g_out_ref[...])


def _layer_tail(ya, yb, wo, x, g_mix, g_in, wu, wd, layer, g_out, *, tm=1024, tf=2048):
    n, d = x.shape
    half = ya.shape[-1]
    ff = wu.shape[-1]
    assert ff // tf >= 2 and tm % TAIL_ROWS == 0
    return pl.pallas_call(
        _tail_kernel,
        grid=(n // tm, ff // tf),
        in_specs=[
            pl.BlockSpec((tm, half), lambda i, f: (i, 0)),
            pl.BlockSpec((tm, half), lambda i, f: (i, 0)),
            pl.BlockSpec((2 * half, d), lambda i, f: (0, 0)),
            pl.BlockSpec((tm, d), lambda i, f: (i, 0)),
            pl.BlockSpec((1, d), lambda i, f: (0, 0)),
            pl.BlockSpec((1, d), lambda i, f: (0, 0)),
            pl.BlockSpec((None, d, tf), lambda i, f: (layer, 0, f)),
            pl.BlockSpec((None, tf, d), lambda i, f: (layer, f, 0)),
            pl.BlockSpec((1, d), lambda i, f: (0, 0)),
        ],
        out_specs=pl.BlockSpec((tm, d), lambda i, f: (i, 0)),
        out_shape=jax.ShapeDtypeStruct((n, d), F32),
        scratch_shapes=[pltpu.VMEM((tm, d), F32), pltpu.VMEM((tm, d), BF16),
                        pltpu.VMEM((tm, d), F32)],
        compiler_params=pltpu.CompilerParams(
            dimension_semantics=("arbitrary", "arbitrary"), vmem_limit_bytes=TAIL_VMEM_LIMIT),
        name="layer_tail",
    )(ya, yb, wo, x, g_mix, g_in, wu, wd, g_out)


CONV_ROWS = 64
U_HIST = 32
Z_HIST = 8


def _conv_kernel(x_ref, g_in_ref, w_in_ref, dw_w_ref, dw_b_ref, ln_g_ref, ln_b_ref, sc_w_ref,
                 u_ref, z_ref, ubuf_ref, zbuf_ref, shift_ref, conv_ref, *, ts):
    @pl.when(pl.program_id(1) == 0)
    def _():
        ubuf_ref[0:U_HIST, :] = jnp.zeros((U_HIST, CONV_CH), F32)
        zbuf_ref[0:Z_HIST, :] = jnp.zeros((Z_HIST, SC_CH), F32)

    x = x_ref[0]
    hb = _rms(x, g_in_ref[...]).astype(BF16)
    c = CONV_CH
    ca = _dot(hb, w_in_ref[:, 0:c])
    cg = _dot(hb, w_in_ref[:, c:2 * c])
    ubuf_ref[U_HIST:U_HIST + ts, :] = ca * jax.nn.sigmoid(cg)
    dc = _dot(hb, w_in_ref[:, 3 * c:4 * c])
    dh = _dot(hb, w_in_ref[:, 4 * c:5 * c])
    zbuf_ref[Z_HIST:Z_HIST + ts, :] = dc * dh

    first_off = U_HIST - (CONV_WIDTH - 1)
    shift_rows = shift_ref.shape[1]
    for r in range(1, SUBLANES):
        shift_ref[r - 1] = ubuf_ref[r:r + shift_rows, :]

    for base in range(0, ts, CONV_ROWS):
        acc = jnp.zeros((CONV_ROWS, c), F32)
        for j in range(CONV_WIDTH):
            r, a = (first_off + j) % SUBLANES, (first_off + j) // SUBLANES
            rows = pl.ds(base + SUBLANES * a, CONV_ROWS)
            taps = ubuf_ref[rows, :] if r == 0 else shift_ref[r - 1, rows, :]
            acc = acc + dw_w_ref[j:j + 1, :] * taps
        conv_ref[base:base + CONV_ROWS, :] = acc
    u = conv_ref[...] + dw_b_ref[...]
    mu = jnp.mean(u, axis=-1, keepdims=True)
    ctr = u - mu
    var = jnp.mean(ctr * ctr, axis=-1, keepdims=True)
    u = ctr * lax.rsqrt(var + EPS) * ln_g_ref[...] + ln_b_ref[...]
    u = u * jax.nn.sigmoid(u)

    z = jnp.zeros((ts, c), F32)
    for j in range(SC_WIDTH):
        off = Z_HIST - (SC_WIDTH - 1) + j
        z = z + sc_w_ref[j:j + 1, :] * zbuf_ref[off:off + ts, :]
    z = _dot(hb, w_in_ref[:, 2 * c:3 * c]) * z

    ubuf_ref[0:U_HIST, :] = ubuf_ref[ts:ts + U_HIST, :]
    zbuf_ref[0:Z_HIST, :] = zbuf_ref[ts:ts + Z_HIST, :]

    u_ref[0] = u.astype(BF16)
    z_ref[0] = z.astype(BF16)


def _conv_mixer(x, g_in, w_in, dw_w, dw_b, ln_g, ln_b, sc_w, *, ts=512):
    bsz, s, d = x.shape
    full = lambda a: pl.BlockSpec(a.shape, lambda b, i: (0,) * a.ndim)
    return pl.pallas_call(
        functools.partial(_conv_kernel, ts=ts),
        grid=(bsz, s // ts),
        in_specs=[pl.BlockSpec((1, ts, d), lambda b, i: (b, i, 0)),
                  full(g_in), full(w_in), full(dw_w), full(dw_b), full(ln_g), full(ln_b),
                  full(sc_w)],
        out_specs=[pl.BlockSpec((1, ts, CONV_CH), lambda b, i: (b, i, 0)),
                   pl.BlockSpec((1, ts, SC_CH), lambda b, i: (b, i, 0))],
        out_shape=[jax.ShapeDtypeStruct((bsz, s, CONV_CH), BF16),
                   jax.ShapeDtypeStruct((bsz, s, SC_CH), BF16)],
        scratch_shapes=[pltpu.VMEM((U_HIST + ts, CONV_CH), F32),
                        pltpu.VMEM((Z_HIST + ts, SC_CH), F32),
                        pltpu.VMEM((SUBLANES - 1, ts + U_HIST - SUBLANES, CONV_CH), F32),
                        pltpu.VMEM((ts, CONV_CH), F32)],
        compiler_params=pltpu.CompilerParams(
            dimension_semantics=("arbitrary", "arbitrary"), vmem_limit_bytes=VMEM_LIMIT),
        name="conv_mixer",
    )(x, g_in, w_in, dw_w, dw_b, ln_g, ln_b, sc_w)


def _t5_bucket(rel):
    nb = NUM_BUCKETS // 2
    ret = jnp.where(rel > 0, nb, 0)
    n = jnp.abs(rel)
    max_exact = nb // 2
    nf = jnp.maximum(n, 1).astype(jnp.float32)
    large = max_exact + (jnp.log(nf / max_exact) / math.log(MAX_DISTANCE / max_exact)
                         * (nb - max_exact)).astype(jnp.int32)
    large = jnp.minimum(large, nb - 1)
    return ret + jnp.where(n < max_exact, n, large)


REL_LO = -(TQ + TK - 1)
REL_LEN = TK - REL_LO
REL_LANES = -(-REL_LEN // LANES) * LANES


def _bias_kernel(fr_ref, a_ref, b_ref):
    krow = lax.broadcasted_iota(I32, (TK, TQ), 0)
    qcol = lax.broadcasted_iota(I32, (TK, TQ), 1)
    admissible = (krow // CHUNK) <= (qcol // CHUNK)
    zeros = jnp.zeros((TK, TQ), F32)
    for h in range(A_HEADS + B_HEADS):
        table = jnp.broadcast_to(fr_ref[h], (TK, REL_LANES))
        blocks = []
        for shift in (TK, 0):
            s0 = (-shift - REL_LO - (REL_LEN - 1)) % REL_LANES
            blocks.append(pltpu.roll(table, s0, 1, stride=1, stride_axis=0)[:, 0:TQ])
        prev, diag = blocks
        if h < A_HEADS:
            a_ref[0, h], a_ref[1, h], a_ref[2, h] = zeros, prev, jnp.where(admissible, diag, NEG)
        else:
            g = h - A_HEADS
            b_ref[0, g], b_ref[1, g], b_ref[2, g] = zeros, prev, diag


def _bias_tables(rel_bias):
    rel = jnp.arange(REL_LO, TK, dtype=jnp.int32)
    far_bucket = NUM_BUCKETS // 2 - 1
    f = (rel_bias[_t5_bucket(rel)].astype(F32) - rel_bias[far_bucket].astype(F32)[None, :]).T
    fr = jnp.pad(f[:, ::-1] * LOG2E, ((0, 0), (0, REL_LANES - REL_LEN)))[:, None, :]
    heads = A_HEADS + B_HEADS
    return pl.pallas_call(
        _bias_kernel,
        grid=(1,),
        in_specs=[pl.BlockSpec((heads, 1, REL_LANES), lambda i: (0, 0, 0))],
        out_specs=[pl.BlockSpec((3, A_HEADS, TK, TQ), lambda i: (0, 0, 0, 0)),
                   pl.BlockSpec((3, B_HEADS, TK, TQ), lambda i: (0, 0, 0, 0))],
        out_shape=[jax.ShapeDtypeStruct((3, A_HEADS, TK, TQ), F32),
                   jax.ShapeDtypeStruct((3, B_HEADS, TK, TQ), F32)],
        compiler_params=pltpu.CompilerParams(
            dimension_semantics=("arbitrary",), vmem_limit_bytes=VMEM_LIMIT),
        name="bias_tables",
    )(fr)


def _attn_weights(w_in):
    o = np.cumsum([0, 512, 512, 512, 512, 64, 64, 256, 32, 8])
    aq, ak, av, bq, bk, bv, iq, ik, iw = [w_in[:, o[n]:o[n + 1]] for n in range(9)]
    aq = aq * (A_HEAD_DIM ** -0.5 * LOG2E)
    bq = bq * (B_HEAD_DIM ** -0.5 * LOG2E)
    w = jnp.concatenate([aq, ak, bq, iq, bk, bk, ik, ik, ik, ik], axis=1).astype(BF16)
    pad = jnp.zeros((w_in.shape[0], VT_ROWS - ROW_IW - IDX_HEADS), w_in.dtype)
    wt = jnp.concatenate([av, bv, iw, pad], axis=1).T.astype(BF16)
    return w, wt


def kernel(x, rel_bias, norm_g, w_mlp_up, w_mlp_down, attn_w_in, attn_w_out, diff_lambda,
           diff_subln_g, conv_w_in, conv_w_out, conv_dw_w, conv_dw_b, conv_ln_g, conv_ln_b,
           sconv_w):
    bsz, s, d = x.shape
    depth = norm_g.shape[0]
    top_k = min(TOPK_MAX, s // 4)
    row = lambda v: v.reshape(1, -1)
    a_bias, b_bias = _bias_tables(rel_bias)
    w_up, w_down = w_mlp_up.astype(BF16), w_mlp_down.astype(BF16)
    for i in range(depth):
        j = i // 2
        if i % 2 == 0:
            lambda_init = 0.8 - 0.6 * math.exp(-0.3 * i)
            w, wt = _attn_weights(attn_w_in[j])
            proj, avt, bvt, iwt = _attn_in(x.reshape(bsz, s, d), row(norm_g[i, 0]), w, wt)
            ya, yb = _attention(proj, avt, bvt, iwt, diff_lambda[j], a_bias, b_bias,
                                diff_subln_g[j].reshape(-1, 1), lambda_init=lambda_init,
                                top_k=top_k)
            w_out = attn_w_out[j]
        else:
            ya, yb = _conv_mixer(x.reshape(bsz, s, d), row(norm_g[i, 0]),
                                 conv_w_in[j].astype(BF16), conv_dw_w[j], row(conv_dw_b[j]),
                                 row(conv_ln_g[j]), row(conv_ln_b[j]), sconv_w[j])
            w_out = conv_w_out[j]
        x = _layer_tail(ya.reshape(bsz * s, -1), yb.reshape(bsz * s, -1), w_out.astype(BF16),
                        x.reshape(bsz * s, d), row(norm_g[i, 1]), row(norm_g[i, 2]),
                        w_up, w_down, i, row(norm_g[i, 3]))
    return x.reshape(bsz, s, d)
```

```python
import functools
import math

import numpy as np
import jax
import jax.numpy as jnp
from jax import lax
from jax.experimental import pallas as pl
from jax.experimental.pallas import tpu as pltpu

F32 = jnp.float32
BF16 = jnp.bfloat16
I32 = jnp.int32

CHUNK = 64
NUM_BUCKETS = 32
MAX_DISTANCE = 128
EPS = 1e-6
NEG = -1e30
LOG2E = math.log2(math.e)
A_HEADS = 4
A_HEAD_DIM = 64
A_V_DIM = 2 * A_HEAD_DIM
B_HEADS = 8
B_HEAD_DIM = 64
IDX_HEADS = 8
IDX_DIM = 32
TOPK_MAX = 256
CONV_CH = 512
CONV_WIDTH = 31
SC_CH = 512
SC_WIDTH = 3

LANES = 128
SUBLANES = 8
VMEM_LIMIT = 56 * 1024 * 1024
TAIL_VMEM_LIMIT = 60 * 1024 * 1024

TQ = 256
TK = 256
COL_AQ, COL_AK, COL_BQ, COL_IQ, COL_KB, COL_IK, PROJ_W = 0, 512, 1024, 1536, 1792, 1920, 2048
ROW_AV, ROW_BV, ROW_IW, VT_ROWS = 0, 512, 576, 640
ONES_ROWS = 16
INT_MIN = -2 ** 31
KEY_NEG_INF = INT_MIN + 0x7FFFFF


def _rms(x, g):
    return x * lax.rsqrt(jnp.mean(x * x, axis=-1, keepdims=True) + EPS) * g


def _dot_nt(a, b):
    return lax.dot_general(a, b, (((1,), (1,)), ((), ())), preferred_element_type=F32)


def _dot(a, b):
    return jnp.dot(a, b, preferred_element_type=F32)


def _lane_band(x, lo, width):
    lane = lax.broadcasted_iota(I32, x.shape, 1)
    return jnp.where((lane >= lo) & (lane < lo + width), x, jnp.zeros_like(x))


KEY_BITS = 32


def _bit_planes(keys):
    assert keys.shape[0] == KEY_BITS * SUBLANES
    u = keys ^ INT_MIN
    a = [u[SUBLANES * r:SUBLANES * (r + 1), :] for r in range(KEY_BITS)]
    j, m = KEY_BITS // 2, (1 << (KEY_BITS // 2)) - 1
    while j:
        mask = np.int32(np.uint32(m))
        k = 0
        while k < KEY_BITS:
            t = (a[k] ^ lax.shift_right_logical(a[k + j], jnp.int32(j))) & mask
            a[k] = a[k] ^ t
            a[k + j] = a[k + j] ^ lax.shift_left(t, jnp.int32(j))
            k = (k + j + 1) & ~j
        j >>= 1
        m = (m ^ (m << j)) & 0xFFFFFFFF
    return a


def _index_bits(seq):
    bits = seq.bit_length() - 1
    assert seq == 1 << bits and seq >= KEY_BITS * SUBLANES
    return bits


def _index_planes(nkb):
    shape = (nkb, SUBLANES, TQ)
    blk = lax.broadcasted_iota(I32, shape, 0)
    sub = lax.broadcasted_iota(I32, shape, 1)
    sub_bits = SUBLANES.bit_length() - 1
    word_bits = KEY_BITS.bit_length() - 1
    planes = []
    for b in range(_index_bits(nkb * TK) - 1, -1, -1):
        if b >= sub_bits + word_bits:
            on = ((nkb - 1 - blk) >> (b - sub_bits - word_bits)) & 1
            planes.append(jnp.where(on == 1, -1, 0))
        elif b >= sub_bits:
            t = b - sub_bits
            pattern = sum(1 << k for k in range(KEY_BITS) if (k >> t) & 1)
            planes.append(jnp.full(shape, np.int32(np.uint32(pattern)), I32))
        else:
            on = ((SUBLANES - 1 - sub) >> b) & 1
            planes.append(jnp.where(on == 1, -1, 0))
    return planes


def _bias_kind(j, i):
    return jnp.clip(j - i + 2, 0, 2)


def _key_block_stream(i, n_chains, s_ref, acc_ref, logits, v_aug):
    chains = range(n_chains)

    def stage(j, c, kind):
        s = logits(j, c, kind)
        s_ref[c] = s
        return jnp.max(s, axis=0, keepdims=True)

    def absorb(j, c, m_old, block_max):
        m_new = jnp.maximum(m_old, block_max)
        alpha = jnp.exp2(m_old - m_new)
        p = jnp.exp2(s_ref[c] - m_new).astype(BF16)
        acc_ref[c] = acc_ref[c] * alpha + _dot(v_aug(j, c), p)
        return m_new

    def start():
        acc_ref[...] = jnp.zeros_like(acc_ref)
        return (jnp.full((1, TQ), NEG, F32),) * n_chains, tuple(stage(0, c, 'any') for c in chains)

    def step(j, carry, kind):
        ms, bms = carry
        out = [(absorb(j, c, ms[c], bms[c]), stage(j + 1, c, kind)) for c in chains]
        return tuple(o[0] for o in out), tuple(o[1] for o in out)

    def finish(carry):
        ms, bms = carry
        for c in chains:
            absorb(i, c, ms[c], bms[c])

    return start, step, finish


def _run_key_block_streams(i, streams):
    def body(j, carries, kind):
        return tuple(step(j, c, kind) for (_, step, _), c in zip(streams, carries))

    carries = tuple(start() for start, _, _ in streams)
    n_far = jnp.maximum(i - 2, 0)
    carries = lax.fori_loop(0, n_far, functools.partial(body, kind='far'), carries)
    carries = lax.fori_loop(n_far, i, functools.partial(body, kind='near'), carries)
    for (_, _, finish), c in zip(streams, carries):
        finish(c)


def _attn_in_kernel(x_ref, g_ref, w_ref, wt_ref, proj_ref, avt_ref, bvt_ref, iwt_ref, *, tm):
    hb = _rms(x_ref[0], g_ref[...]).astype(BF16)
    proj_ref[0] = _dot(hb, w_ref[...]).astype(BF16)
    t = _dot_nt(wt_ref[...], hb)
    ones = jnp.ones((ONES_ROWS, TK), BF16)
    for c in range(tm // TK):
        cols = slice(c * TK, (c + 1) * TK)
        for h in range(A_HEADS):
            r = ROW_AV + h * A_V_DIM
            avt_ref[0, c, h, 0:A_V_DIM, :] = t[r:r + A_V_DIM, cols].astype(BF16)
            avt_ref[0, c, h, A_V_DIM:A_V_DIM + ONES_ROWS, :] = ones
        bvt_ref[0, c, 0:B_HEAD_DIM, :] = t[ROW_BV:ROW_IW, cols].astype(BF16)
        bvt_ref[0, c, B_HEAD_DIM:B_HEAD_DIM + ONES_ROWS, :] = ones
    iwt_ref[0] = t[ROW_IW:ROW_IW + IDX_HEADS, :] * (IDX_HEADS ** -0.5 * IDX_DIM ** -0.5)


def _attn_in(x, g, w, wt, *, tm=1024):
    bsz, s, d = x.shape
    av_blk = (A_HEADS, A_V_DIM + ONES_ROWS, TK)
    bv_blk = (B_HEAD_DIM + ONES_ROWS, TK)
    return pl.pallas_call(
        functools.partial(_attn_in_kernel, tm=tm),
        grid=(bsz, s // tm),
        in_specs=[
            pl.BlockSpec((1, tm, d), lambda b, i: (b, i, 0)),
            pl.BlockSpec((1, d), lambda b, i: (0, 0)),
            pl.BlockSpec((d, PROJ_W), lambda b, i: (0, 0)),
            pl.BlockSpec((VT_ROWS, d), lambda b, i: (0, 0)),
        ],
        out_specs=[
            pl.BlockSpec((1, tm, PROJ_W), lambda b, i: (b, i, 0)),
            pl.BlockSpec((1, tm // TK) + av_blk, lambda b, i: (b, i, 0, 0, 0)),
            pl.BlockSpec((1, tm // TK) + bv_blk, lambda b, i: (b, i, 0, 0)),
            pl.BlockSpec((1, IDX_HEADS, tm), lambda b, i: (b, 0, i)),
        ],
        out_shape=[
            jax.ShapeDtypeStruct((bsz, s, PROJ_W), BF16),
            jax.ShapeDtypeStruct((bsz, s // TK) + av_blk, BF16),
            jax.ShapeDtypeStruct((bsz, s // TK) + bv_blk, BF16),
            jax.ShapeDtypeStruct((bsz, IDX_HEADS, s), F32),
        ],
        compiler_params=pltpu.CompilerParams(
            dimension_semantics=("arbitrary", "arbitrary"), vmem_limit_bytes=VMEM_LIMIT),
        name="attn_in",
    )(x, g, w, wt)


def _key_rows(ref, j):
    return ref[0, pl.ds(pl.multiple_of(j * TK, TK), TK), :]


def _indexer(i, iq_ref, ik_ref, iwt_ref, score_ref, key_ref, plane_ref):
    iq = iq_ref[0]
    per_blk = LANES // IDX_DIM
    iqh = [_lane_band(iq[:, (h // per_blk) * LANES:(h // per_blk + 1) * LANES],
                      (h % per_blk) * IDX_DIM, IDX_DIM) for h in range(IDX_HEADS)]
    w = iwt_ref[0]

    def stage_scores(j):
        ikb = _key_rows(ik_ref, j)
        half = TK // 2
        for r in (0, half):
            acc = jnp.zeros((half, TQ), F32)
            for h in range(IDX_HEADS):
                acc = acc + jnp.maximum(_dot_nt(ikb[r:r + half], iqh[h]), 0.0) * w[h:h + 1, :]
            score_ref[r:r + half, :] = acc

    def staged_keys():
        acc = score_ref[...]
        acc = jnp.where(acc == 0.0, 0.0, acc)
        bits = lax.bitcast_convert_type(acc, I32)
        return bits ^ ((bits >> 31) & 0x7FFFFFFF)

    def store_keys(j, keys):
        key_ref[j] = keys
        for p, plane in enumerate(_bit_planes(keys)):
            plane_ref[p, j] = plane

    def per_block(j):
        keys = staged_keys()
        stage_scores(j + 1)
        store_keys(j, keys)

    def finish():
        krow = lax.broadcasted_iota(I32, (TK, TQ), 0)
        qcol = lax.broadcasted_iota(I32, (TK, TQ), 1)
        admissible = (krow // CHUNK) <= (qcol // CHUNK)
        store_keys(i, jnp.where(admissible, staged_keys(), KEY_NEG_INF))

    stage_scores(0)
    return per_block, finish


def _select_mask(i, key_ref, plane_ref, mask_ref, *, top_k, seq):
    nkb = seq // TK
    blk = lax.broadcasted_iota(I32, (nkb, SUBLANES, TQ), 0)
    alive0 = jnp.where(blk <= i, -1, 0)

    def bit_body(p, carry, n_bits):
        alive, n_gt, thr_u = carry
        hit = alive & plane_ref[p]
        pc = jnp.sum(lax.population_count(hit), axis=0)
        n_hit = jnp.sum(pc.astype(F32), axis=0, keepdims=True)
        take = (n_gt + n_hit) >= top_k
        thr_u = thr_u | jnp.where(take, lax.shift_left(jnp.int32(1), n_bits - 1 - p), 0)
        n_gt = n_gt + jnp.where(take, 0.0, n_hit)
        alive = jnp.where(take, hit, alive ^ hit)
        return alive, n_gt, thr_u

    zero = jnp.zeros((1, TQ), I32)
    alive, n_gt, thr_u = lax.fori_loop(
        0, KEY_BITS, lambda p, c: bit_body(p, c, KEY_BITS), (alive0, jnp.zeros((1, TQ), F32), zero))
    idx_bits = _index_bits(seq)
    _, _, rev_idx = lax.fori_loop(
        KEY_BITS, KEY_BITS + idx_bits,
        lambda p, c: bit_body(p, c, KEY_BITS + idx_bits), (alive, n_gt, zero))
    thr = thr_u ^ INT_MIN
    lim = seq - rev_idx
    lim = jnp.where(thr > KEY_NEG_INF, lim, seq)
    thr = jnp.maximum(thr, KEY_NEG_INF + 1)
    krow = lax.broadcasted_iota(I32, (TK, TQ), 0)

    def mask_body(j, c):
        late = jnp.where(krow >= lim - j * TK, 1, 0)
        mask_ref[j] = jnp.where(key_ref[j] - late >= thr, 0.0, NEG)
        return c

    lax.fori_loop(0, i + 1, mask_body, 0)


def _attn_kernel(lam_ref, aq_ref, ak_ref, avt_ref, abias_ref, g_ref,
                 bq_ref, iq_ref, kb_ref, ik_ref, bvt_ref, iwt_ref, bbias_ref,
                 ya_ref, yb_ref,
                 sa_ref, sb_ref, aacc_ref, bacc_ref, ot_ref,
                 score_ref, key_ref, plane_ref, mask_ref, *, lambda_init, top_k, seq):
    i = pl.program_id(1)
    nkb = seq // TK
    n_maps = 2 * A_HEADS

    @pl.when((i == 0) & (pl.program_id(0) == 0))
    def _():
        plane_ref[0:KEY_BITS] = jnp.zeros((KEY_BITS, nkb, SUBLANES, TQ), I32)
        for t, plane in enumerate(_index_planes(nkb)):
            plane_ref[KEY_BITS + t] = plane

    index_block, finish_index = _indexer(i, iq_ref, ik_ref, iwt_ref, score_ref, key_ref, plane_ref)

    def index_body(j, c):
        index_block(j)
        return c

    lax.fori_loop(0, i, index_body, 0)
    finish_index()
    _select_mask(i, key_ref, plane_ref, mask_ref, top_k=top_k, seq=seq)

    lv = lam_ref[...]
    lam = (jnp.exp(jnp.sum(lv[0:1] * lv[1:2], axis=-1, keepdims=True))
           - jnp.exp(jnp.sum(lv[2:3] * lv[3:4], axis=-1, keepdims=True)) + lambda_init)

    q = aq_ref[0]
    qm = []
    for h in range(A_HEADS):
        q12 = q[:, h * LANES:(h + 1) * LANES]
        qm += [_lane_band(q12, 0, A_HEAD_DIM), _lane_band(q12, A_HEAD_DIM, A_HEAD_DIM)]

    def a_logits(j, c, kind):
        h = c // 2
        kb = ak_ref[0, pl.ds(pl.multiple_of(j * TK, TK), TK), h * LANES:(h + 1) * LANES]
        s = _dot_nt(kb, qm[c])
        return s if kind == 'far' else s + abias_ref[_bias_kind(j, i), h]

    qb = bq_ref[0]
    qh = [_lane_band(qb[:, (h // 2) * LANES:(h // 2 + 1) * LANES],
                     (h % 2) * B_HEAD_DIM, B_HEAD_DIM) for h in range(B_HEADS)]

    def b_logits(j, h, kind):
        s = _dot_nt(_key_rows(kb_ref, j), qh[h]) + mask_ref[j]
        return s if kind == 'far' else s + bbias_ref[_bias_kind(j, i), h]

    _run_key_block_streams(i, [
        _key_block_stream(i, n_maps, sa_ref, aacc_ref, a_logits, lambda j, c: avt_ref[0, j, c // 2]),
        _key_block_stream(i, B_HEADS, sb_ref, bacc_ref, b_logits, lambda j, h: bvt_ref[0, j])])

    for h in range(A_HEADS):
        a1 = aacc_ref[2 * h]
        a2 = aacc_ref[2 * h + 1]
        a = (a1[0:A_V_DIM] / a1[A_V_DIM:A_V_DIM + 1]
             - lam * (a2[0:A_V_DIM] / a2[A_V_DIM:A_V_DIM + 1]))
        y = a * lax.rsqrt(jnp.mean(a * a, axis=0, keepdims=True) + EPS) * g_ref[...]
        ot_ref[h * A_V_DIM:(h + 1) * A_V_DIM, :] = y * (1.0 - lambda_init)
    ya_ref[0] = ot_ref[...].T.astype(BF16)

    for h in range(B_HEADS):
        a = bacc_ref[h]
        ot_ref[h * B_HEAD_DIM:(h + 1) * B_HEAD_DIM, :] = (
            a[0:B_HEAD_DIM] / a[B_HEAD_DIM:B_HEAD_DIM + 1])
    yb_ref[0] = ot_ref[...].T.astype(BF16)


def _attention(proj, avt, bvt, iwt, lam_vecs, a_bias, b_bias, subln_g_col, *, lambda_init, top_k):
    bsz, s, _ = proj.shape
    nq = s // TQ
    nkb = s // TK
    a_width = A_HEADS * A_V_DIM
    b_width = B_HEADS * B_HEAD_DIM
    assert a_width == b_width
    iq_w = IDX_HEADS * IDX_DIM
    once = pl.Buffered(1)
    return pl.pallas_call(
        functools.partial(_attn_kernel, lambda_init=lambda_init, top_k=top_k, seq=s),
        grid=(bsz, nq),
        in_specs=[
            pl.BlockSpec((4, A_HEAD_DIM), lambda b, i: (0, 0)),
            pl.BlockSpec((1, TQ, a_width), lambda b, i: (b, i, COL_AQ // a_width)),
            pl.BlockSpec((1, s, a_width), lambda b, i: (b, 0, COL_AK // a_width)),
            pl.BlockSpec((1, nkb, A_HEADS, A_V_DIM + ONES_ROWS, TK), lambda b, i: (b, 0, 0, 0, 0)),
            pl.BlockSpec((3, A_HEADS, TK, TQ), lambda b, i: (0, 0, 0, 0), once),
            pl.BlockSpec((A_V_DIM, 1), lambda b, i: (0, 0)),
            pl.BlockSpec((1, TQ, b_width), lambda b, i: (b, i, COL_BQ // b_width)),
            pl.BlockSpec((1, TQ, iq_w), lambda b, i: (b, i, COL_IQ // iq_w)),
            pl.BlockSpec((1, s, LANES), lambda b, i: (b, 0, COL_KB // LANES), once),
            pl.BlockSpec((1, s, LANES), lambda b, i: (b, 0, COL_IK // LANES), once),
            pl.BlockSpec((1, nkb, B_HEAD_DIM + ONES_ROWS, TK), lambda b, i: (b, 0, 0, 0), once),
            pl.BlockSpec((1, IDX_HEADS, TQ), lambda b, i: (b, 0, i)),
            pl.BlockSpec((3, B_HEADS, TK, TQ), lambda b, i: (0, 0, 0, 0), once),
        ],
        out_specs=[pl.BlockSpec((1, TQ, a_width), lambda b, i: (b, i, 0)),
                   pl.BlockSpec((1, TQ, b_width), lambda b, i: (b, i, 0))],
        out_shape=[jax.ShapeDtypeStruct((bsz, s, a_width), BF16),
                   jax.ShapeDtypeStruct((bsz, s, b_width), BF16)],
        scratch_shapes=[
            pltpu.VMEM((2 * A_HEADS, TK, TQ), F32),
            pltpu.VMEM((B_HEADS, TK, TQ), F32),
            pltpu.VMEM((2 * A_HEADS, A_V_DIM + ONES_ROWS, TQ), F32),
            pltpu.VMEM((B_HEADS, B_HEAD_DIM + ONES_ROWS, TQ), F32),
            pltpu.VMEM((a_width, TQ), F32),
            pltpu.VMEM((TK, TQ), F32),
            pltpu.VMEM((nkb, TK, TQ), I32),
            pltpu.VMEM((KEY_BITS + _index_bits(s), nkb, SUBLANES, TQ), I32),
            pltpu.VMEM((nkb, TK, TQ), F32),
        ],
        compiler_params=pltpu.CompilerParams(
            dimension_semantics=("arbitrary", "arbitrary"), vmem_limit_bytes=VMEM_LIMIT),
        name="attention",
    )(lam_vecs, proj, proj, avt, a_bias, subln_g_col, proj, proj, proj, proj, bvt, iwt, b_bias)


TAIL_ROWS = 256
TAIL_FF = 1024


def _tail_kernel(ya_ref, yb_ref, wo_ref, x_ref, g_mix_ref, g_in_ref, wu_ref, wd_ref, g_out_ref,
                 o_ref, x1_ref, h_ref, acc_ref):
    chunks = [slice(r, r + TAIL_ROWS) for r in range(0, x_ref.shape[0], TAIL_ROWS)]
    ff_steps = list(range(0, wu_ref.shape[1], TAIL_FF))

    def mlp_part(rows, k):
        u = jnp.maximum(_dot(h_ref[rows, :], wu_ref[:, k:k + TAIL_FF]), 0.0)
        return _dot((u * u).astype(BF16), wd_ref[k:k + TAIL_FF, :])

    half = ya_ref.shape[-1]
    for rows in chunks:
        m = (_dot(ya_ref[rows, :], wo_ref[0:half, :])
             + _dot(yb_ref[rows, :], wo_ref[half:2 * half, :]))
        x1 = x_ref[rows, :] + _rms(m, g_mix_ref[...])
        x1_ref[rows, :] = x1
        h_ref[rows, :] = _rms(x1, g_in_ref[...]).astype(BF16)
    acc_ref[...] = mlp_part(slice(None), ff_steps[0])
    for k in ff_steps[1:-1]:
        acc_ref[...] += mlp_part(slice(None), k)
    for rows in chunks:
        y = acc_ref[rows, :] + mlp_part(rows, ff_steps[-1])
        o_ref[rows, :] = x1_ref[rows, :] + _rms(y, g_out_ref[...])


def _layer_tail(ya, yb, wo, x, g_mix, g_in, wu, wd, layer, g_out, *, tm=1024):
    n, d = x.shape
    half = ya.shape[-1]
    ff = wu.shape[-1]
    assert ff // TAIL_FF >= 2 and tm % TAIL_ROWS == 0
    once = pl.Buffered(1)
    return pl.pallas_call(
        _tail_kernel,
        grid=(n // tm,),
        in_specs=[
            pl.BlockSpec((tm, half), lambda i: (i, 0)),
            pl.BlockSpec((tm, half), lambda i: (i, 0)),
            pl.BlockSpec((2 * half, d), lambda i: (0, 0), once),
            pl.BlockSpec((tm, d), lambda i: (i, 0)),
            pl.BlockSpec((1, d), lambda i: (0, 0)),
            pl.BlockSpec((1, d), lambda i: (0, 0)),
            pl.BlockSpec((None, d, ff), lambda i: (layer, 0, 0), once),
            pl.BlockSpec((None, ff, d), lambda i: (layer, 0, 0), once),
            pl.BlockSpec((1, d), lambda i: (0, 0)),
        ],
        out_specs=pl.BlockSpec((tm, d), lambda i: (i, 0)),
        out_shape=jax.ShapeDtypeStruct((n, d), F32),
        scratch_shapes=[pltpu.VMEM((tm, d), F32), pltpu.VMEM((tm, d), BF16),
                        pltpu.VMEM((tm, d), F32)],
        compiler_params=pltpu.CompilerParams(
            dimension_semantics=("arbitrary",), vmem_limit_bytes=TAIL_VMEM_LIMIT),
        name="layer_tail",
    )(ya, yb, wo, x, g_mix, g_in, wu, wd, g_out)


CONV_ROWS = 64
U_HIST = 32
Z_HIST = 8


def _conv_kernel(x_ref, g_in_ref, w_in_ref, dw_w_ref, dw_b_ref, ln_g_ref, ln_b_ref, sc_w_ref,
                 u_ref, z_ref, ubuf_ref, zbuf_ref, shift_ref, conv_ref, *, ts):
    @pl.when(pl.program_id(1) == 0)
    def _():
        ubuf_ref[0:U_HIST, :] = jnp.zeros((U_HIST, CONV_CH), F32)
        zbuf_ref[0:Z_HIST, :] = jnp.zeros((Z_HIST, SC_CH), F32)

    x = x_ref[0]
    hb = _rms(x, g_in_ref[...]).astype(BF16)
    c = CONV_CH
    ca = _dot(hb, w_in_ref[:, 0:c])
    cg = _dot(hb, w_in_ref[:, c:2 * c])
    ubuf_ref[U_HIST:U_HIST + ts, :] = ca * jax.nn.sigmoid(cg)
    dc = _dot(hb, w_in_ref[:, 3 * c:4 * c])
    dh = _dot(hb, w_in_ref[:, 4 * c:5 * c])
    zbuf_ref[Z_HIST:Z_HIST + ts, :] = dc * dh

    first_off = U_HIST - (CONV_WIDTH - 1)
    shift_rows = shift_ref.shape[1]
    for r in range(1, SUBLANES):
        shift_ref[r - 1] = ubuf_ref[r:r + shift_rows, :]

    for base in range(0, ts, CONV_ROWS):
        acc = jnp.zeros((CONV_ROWS, c), F32)
        for j in range(CONV_WIDTH):
            r, a = (first_off + j) % SUBLANES, (first_off + j) // SUBLANES
            rows = pl.ds(base + SUBLANES * a, CONV_ROWS)
            taps = ubuf_ref[rows, :] if r == 0 else shift_ref[r - 1, rows, :]
            acc = acc + dw_w_ref[j:j + 1, :] * taps
        conv_ref[base:base + CONV_ROWS, :] = acc
    u = conv_ref[...] + dw_b_ref[...]
    mu = jnp.mean(u, axis=-1, keepdims=True)
    ctr = u - mu
    var = jnp.mean(ctr * ctr, axis=-1, keepdims=True)
    u = ctr * lax.rsqrt(var + EPS) * ln_g_ref[...] + ln_b_ref[...]
    u = u * jax.nn.sigmoid(u)

    z = jnp.zeros((ts, c), F32)
    for j in range(SC_WIDTH):
        off = Z_HIST - (SC_WIDTH - 1) + j
        z = z + sc_w_ref[j:j + 1, :] * zbuf_ref[off:off + ts, :]
    z = _dot(hb, w_in_ref[:, 2 * c:3 * c]) * z

    ubuf_ref[0:U_HIST, :] = ubuf_ref[ts:ts + U_HIST, :]
    zbuf_ref[0:Z_HIST, :] = zbuf_ref[ts:ts + Z_HIST, :]

    u_ref[0] = u.astype(BF16)
    z_ref[0] = z.astype(BF16)


def _conv_mixer(x, g_in, w_in, dw_w, dw_b, ln_g, ln_b, sc_w, *, ts=512):
    bsz, s, d = x.shape
    full = lambda a: pl.BlockSpec(a.shape, lambda b, i: (0,) * a.ndim)
    return pl.pallas_call(
        functools.partial(_conv_kernel, ts=ts),
        grid=(bsz, s // ts),
        in_specs=[pl.BlockSpec((1, ts, d), lambda b, i: (b, i, 0)),
                  full(g_in), full(w_in), full(dw_w), full(dw_b), full(ln_g), full(ln_b),
                  full(sc_w)],
        out_specs=[pl.BlockSpec((1, ts, CONV_CH), lambda b, i: (b, i, 0)),
                   pl.BlockSpec((1, ts, SC_CH), lambda b, i: (b, i, 0))],
        out_shape=[jax.ShapeDtypeStruct((bsz, s, CONV_CH), BF16),
                   jax.ShapeDtypeStruct((bsz, s, SC_CH), BF16)],
        scratch_shapes=[pltpu.VMEM((U_HIST + ts, CONV_CH), F32),
                        pltpu.VMEM((Z_HIST + ts, SC_CH), F32),
                        pltpu.VMEM((SUBLANES - 1, ts + U_HIST - SUBLANES, CONV_CH), F32),
                        pltpu.VMEM((ts, CONV_CH), F32)],
        compiler_params=pltpu.CompilerParams(
            dimension_semantics=("arbitrary", "arbitrary"), vmem_limit_bytes=VMEM_LIMIT),
        name="conv_mixer",
    )(x, g_in, w_in, dw_w, dw_b, ln_g, ln_b, sc_w)


def _t5_bucket(rel):
    nb = NUM_BUCKETS // 2
    ret = jnp.where(rel > 0, nb, 0)
    n = jnp.abs(rel)
    max_exact = nb // 2
    nf = jnp.maximum(n, 1).astype(jnp.float32)
    large = max_exact + (jnp.log(nf / max_exact) / math.log(MAX_DISTANCE / max_exact)
                         * (nb - max_exact)).astype(jnp.int32)
    large = jnp.minimum(large, nb - 1)
    return ret + jnp.where(n < max_exact, n, large)


REL_LO = -(TQ + TK - 1)
REL_LEN = TK - REL_LO
REL_LANES = -(-REL_LEN // LANES) * LANES


def _bias_kernel(fr_ref, a_ref, b_ref):
    krow = lax.broadcasted_iota(I32, (TK, TQ), 0)
    qcol = lax.broadcasted_iota(I32, (TK, TQ), 1)
    admissible = (krow // CHUNK) <= (qcol // CHUNK)
    zeros = jnp.zeros((TK, TQ), F32)
    for h in range(A_HEADS + B_HEADS):
        table = jnp.broadcast_to(fr_ref[h], (TK, REL_LANES))
        blocks = []
        for shift in (TK, 0):
            s0 = (-shift - REL_LO - (REL_LEN - 1)) % REL_LANES
            blocks.append(pltpu.roll(table, s0, 1, stride=1, stride_axis=0)[:, 0:TQ])
        prev, diag = blocks
        if h < A_HEADS:
            a_ref[0, h], a_ref[1, h], a_ref[2, h] = zeros, prev, jnp.where(admissible, diag, NEG)
        else:
            g = h - A_HEADS
            b_ref[0, g], b_ref[1, g], b_ref[2, g] = zeros, prev, diag


def _bias_tables(rel_bias):
    rel = jnp.arange(REL_LO, TK, dtype=jnp.int32)
    far_bucket = NUM_BUCKETS // 2 - 1
    f = (rel_bias[_t5_bucket(rel)].astype(F32) - rel_bias[far_bucket].astype(F32)[None, :]).T
    fr = jnp.pad(f[:, ::-1] * LOG2E, ((0, 0), (0, REL_LANES - REL_LEN)))[:, None, :]
    heads = A_HEADS + B_HEADS
    return pl.pallas_call(
        _bias_kernel,
        grid=(1,),
        in_specs=[pl.BlockSpec((heads, 1, REL_LANES), lambda i: (0, 0, 0))],
        out_specs=[pl.BlockSpec((3, A_HEADS, TK, TQ), lambda i: (0, 0, 0, 0)),
                   pl.BlockSpec((3, B_HEADS, TK, TQ), lambda i: (0, 0, 0, 0))],
        out_shape=[jax.ShapeDtypeStruct((3, A_HEADS, TK, TQ), F32),
                   jax.ShapeDtypeStruct((3, B_HEADS, TK, TQ), F32)],
        compiler_params=pltpu.CompilerParams(
            dimension_semantics=("arbitrary",), vmem_limit_bytes=VMEM_LIMIT),
        name="bias_tables",
    )(fr)


def _attn_weights(w_in):
    o = np.cumsum([0, 512, 512, 512, 512, 64, 64, 256, 32, 8])
    aq, ak, av, bq, bk, bv, iq, ik, iw = [w_in[:, o[n]:o[n + 1]] for n in range(9)]
    aq = aq * (A_HEAD_DIM ** -0.5 * LOG2E)
    bq = bq * (B_HEAD_DIM ** -0.5 * LOG2E)
    w = jnp.concatenate([aq, ak, bq, iq, bk, bk, ik, ik, ik, ik], axis=1).astype(BF16)
    pad = jnp.zeros((w_in.shape[0], VT_ROWS - ROW_IW - IDX_HEADS), w_in.dtype)
    wt = jnp.concatenate([av, bv, iw, pad], axis=1).T.astype(BF16)
    return w, wt


def kernel(x, rel_bias, norm_g, w_mlp_up, w_mlp_down, attn_w_in, attn_w_out, diff_lambda,
           diff_subln_g, conv_w_in, conv_w_out, conv_dw_w, conv_dw_b, conv_ln_g, conv_ln_b,
           sconv_w):
    bsz, s, d = x.shape
    depth = norm_g.shape[0]
    top_k = min(TOPK_MAX, s // 4)
    row = lambda v: v.reshape(1, -1)
    a_bias, b_bias = _bias_tables(rel_bias)
    w_up, w_down = w_mlp_up.astype(BF16), w_mlp_down.astype(BF16)
    for i in range(depth):
        j = i // 2
        if i % 2 == 0:
            lambda_init = 0.8 - 0.6 * math.exp(-0.3 * i)
            w, wt = _attn_weights(attn_w_in[j])
            proj, avt, bvt, iwt = _attn_in(x.reshape(bsz, s, d), row(norm_g[i, 0]), w, wt)
            ya, yb = _attention(proj, avt, bvt, iwt, diff_lambda[j], a_bias, b_bias,
                                diff_subln_g[j].reshape(-1, 1), lambda_init=lambda_init,
                                top_k=top_k)
            w_out = attn_w_out[j]
        else:
            ya, yb = _conv_mixer(x.reshape(bsz, s, d), row(norm_g[i, 0]),
                                 conv_w_in[j].astype(BF16), conv_dw_w[j], row(conv_dw_b[j]),
                                 row(conv_ln_g[j]), row(conv_ln_b[j]), sconv_w[j])
            w_out = conv_w_out[j]
        x = _layer_tail(ya.reshape(bsz * s, -1), yb.reshape(bsz * s, -1), w_out.astype(BF16),
                        x.reshape(bsz * s, d), row(norm_g[i, 1]), row(norm_g[i, 2]),
                        w_up, w_down, i, row(norm_g[i, 3]))
    return x.reshape(bsz, s, d)
```

```python
import functools
import math

import numpy as np
import jax
import jax.numpy as jnp
from jax import lax
from jax.experimental import pallas as pl
from jax.experimental.pallas import tpu as pltpu

F32 = jnp.float32
BF16 = jnp.bfloat16
I32 = jnp.int32

CHUNK = 64
NUM_BUCKETS = 32
MAX_DISTANCE = 128
EPS = 1e-6
NEG = -1e30
LOG2E = math.log2(math.e)
A_HEADS = 4
A_HEAD_DIM = 64
A_V_DIM = 2 * A_HEAD_DIM
B_HEADS = 8
B_HEAD_DIM = 64
IDX_HEADS = 8
IDX_DIM = 32
TOPK_MAX = 256
CONV_CH = 512
CONV_WIDTH = 31
SC_CH = 512
SC_WIDTH = 3

LANES = 128
SUBLANES = 8
VMEM_LIMIT = 56 * 1024 * 1024
TAIL_VMEM_LIMIT = 60 * 1024 * 1024

TQ = 256
TK = 256
COL_AQ, COL_AK, COL_BQ, COL_IQ, COL_KB, COL_IK, PROJ_W = 0, 512, 1024, 1536, 1792, 1920, 2048
ROW_AV, ROW_BV, ROW_IW, VT_ROWS = 0, 512, 576, 640
ONES_ROWS = 16
INT_MIN = -2 ** 31
KEY_NEG_INF = INT_MIN + 0x7FFFFF


def _rms(x, g):
    return x * lax.rsqrt(jnp.mean(x * x, axis=-1, keepdims=True) + EPS) * g


def _dot_nt(a, b):
    return lax.dot_general(a, b, (((1,), (1,)), ((), ())), preferred_element_type=F32)


def _dot(a, b):
    return jnp.dot(a, b, preferred_element_type=F32)


def _lane_band(x, lo, width):
    lane = lax.broadcasted_iota(I32, x.shape, 1)
    return jnp.where((lane >= lo) & (lane < lo + width), x, jnp.zeros_like(x))


KEY_BITS = 32


def _bit_planes(keys):
    assert keys.shape[0] == KEY_BITS * SUBLANES
    u = keys ^ INT_MIN
    a = [u[SUBLANES * r:SUBLANES * (r + 1), :] for r in range(KEY_BITS)]
    j, m = KEY_BITS // 2, (1 << (KEY_BITS // 2)) - 1
    while j:
        mask = np.int32(np.uint32(m))
        k = 0
        while k < KEY_BITS:
            t = (a[k] ^ lax.shift_right_logical(a[k + j], jnp.int32(j))) & mask
            a[k] = a[k] ^ t
            a[k + j] = a[k + j] ^ lax.shift_left(t, jnp.int32(j))
            k = (k + j + 1) & ~j
        j >>= 1
        m = (m ^ (m << j)) & 0xFFFFFFFF
    return a


def _index_bits(seq):
    bits = seq.bit_length() - 1
    assert seq == 1 << bits and seq >= KEY_BITS * SUBLANES
    return bits


def _index_planes(nkb):
    shape = (nkb, SUBLANES, TQ)
    blk = lax.broadcasted_iota(I32, shape, 0)
    sub = lax.broadcasted_iota(I32, shape, 1)
    sub_bits = SUBLANES.bit_length() - 1
    word_bits = KEY_BITS.bit_length() - 1
    planes = []
    for b in range(_index_bits(nkb * TK) - 1, -1, -1):
        if b >= sub_bits + word_bits:
            on = ((nkb - 1 - blk) >> (b - sub_bits - word_bits)) & 1
            planes.append(jnp.where(on == 1, -1, 0))
        elif b >= sub_bits:
            t = b - sub_bits
            pattern = sum(1 << k for k in range(KEY_BITS) if (k >> t) & 1)
            planes.append(jnp.full(shape, np.int32(np.uint32(pattern)), I32))
        else:
            on = ((SUBLANES - 1 - sub) >> b) & 1
            planes.append(jnp.where(on == 1, -1, 0))
    return planes


def _bias_kind(j, i):
    return jnp.clip(j - i + 2, 0, 2)


def _key_block_stream(i, n_chains, s_ref, acc_ref, logits, v_aug):
    chains = range(n_chains)

    def stage(j, c, kind):
        s = logits(j, c, kind)
        s_ref[c] = s
        return jnp.max(s, axis=0, keepdims=True)

    def absorb(j, c, m_old, block_max):
        m_new = jnp.maximum(m_old, block_max)
        alpha = jnp.exp2(m_old - m_new)
        p = jnp.exp2(s_ref[c] - m_new).astype(BF16)
        acc_ref[c] = acc_ref[c] * alpha + _dot(v_aug(j, c), p)
        return m_new

    def start():
        acc_ref[...] = jnp.zeros_like(acc_ref)
        return (jnp.full((1, TQ), NEG, F32),) * n_chains, tuple(stage(0, c, 'any') for c in chains)

    def step(j, carry, kind):
        ms, bms = carry
        out = [(absorb(j, c, ms[c], bms[c]), stage(j + 1, c, kind)) for c in chains]
        return tuple(o[0] for o in out), tuple(o[1] for o in out)

    def finish(carry):
        ms, bms = carry
        for c in chains:
            absorb(i, c, ms[c], bms[c])

    return start, step, finish


def _run_key_block_streams(i, streams):
    def body(j, carries, kind):
        return tuple(step(j, c, kind) for (_, step, _), c in zip(streams, carries))

    carries = tuple(start() for start, _, _ in streams)
    n_far = jnp.maximum(i - 2, 0)
    carries = lax.fori_loop(0, n_far, functools.partial(body, kind='far'), carries)
    carries = lax.fori_loop(n_far, i, functools.partial(body, kind='near'), carries)
    for (_, _, finish), c in zip(streams, carries):
        finish(c)


def _attn_in_kernel(x_ref, g_ref, w_ref, wt_ref, proj_ref, avt_ref, bvt_ref, iwt_ref, *, tm):
    hb = _rms(x_ref[0], g_ref[...]).astype(BF16)
    proj_ref[0] = _dot(hb, w_ref[...]).astype(BF16)
    t = _dot_nt(wt_ref[...], hb)
    ones = jnp.ones((ONES_ROWS, TK), BF16)
    for c in range(tm // TK):
        cols = slice(c * TK, (c + 1) * TK)
        for h in range(A_HEADS):
            r = ROW_AV + h * A_V_DIM
            avt_ref[0, c, h, 0:A_V_DIM, :] = t[r:r + A_V_DIM, cols].astype(BF16)
            avt_ref[0, c, h, A_V_DIM:A_V_DIM + ONES_ROWS, :] = ones
        bvt_ref[0, c, 0:B_HEAD_DIM, :] = t[ROW_BV:ROW_IW, cols].astype(BF16)
        bvt_ref[0, c, B_HEAD_DIM:B_HEAD_DIM + ONES_ROWS, :] = ones
    iwt_ref[0] = t[ROW_IW:ROW_IW + IDX_HEADS, :] * (IDX_HEADS ** -0.5 * IDX_DIM ** -0.5)


def _attn_in(x, g, w, wt, *, tm=1024):
    bsz, s, d = x.shape
    av_blk = (A_HEADS, A_V_DIM + ONES_ROWS, TK)
    bv_blk = (B_HEAD_DIM + ONES_ROWS, TK)
    return pl.pallas_call(
        functools.partial(_attn_in_kernel, tm=tm),
        grid=(bsz, s // tm),
        in_specs=[
            pl.BlockSpec((1, tm, d), lambda b, i: (b, i, 0)),
            pl.BlockSpec((1, d), lambda b, i: (0, 0)),
            pl.BlockSpec((d, PROJ_W), lambda b, i: (0, 0)),
            pl.BlockSpec((VT_ROWS, d), lambda b, i: (0, 0)),
        ],
        out_specs=[
            pl.BlockSpec((1, tm, PROJ_W), lambda b, i: (b, i, 0)),
            pl.BlockSpec((1, tm // TK) + av_blk, lambda b, i: (b, i, 0, 0, 0)),
            pl.BlockSpec((1, tm // TK) + bv_blk, lambda b, i: (b, i, 0, 0)),
            pl.BlockSpec((1, IDX_HEADS, tm), lambda b, i: (b, 0, i)),
        ],
        out_shape=[
            jax.ShapeDtypeStruct((bsz, s, PROJ_W), BF16),
            jax.ShapeDtypeStruct((bsz, s // TK) + av_blk, BF16),
            jax.ShapeDtypeStruct((bsz, s // TK) + bv_blk, BF16),
            jax.ShapeDtypeStruct((bsz, IDX_HEADS, s), F32),
        ],
        compiler_params=pltpu.CompilerParams(
            dimension_semantics=("arbitrary", "arbitrary"), vmem_limit_bytes=VMEM_LIMIT),
        name="attn_in",
    )(x, g, w, wt)


def _key_rows(ref, j):
    return ref[0, pl.ds(pl.multiple_of(j * TK, TK), TK), :]


def _indexer(i, iq_ref, ik_ref, iwt_ref, score_ref, key_ref, plane_ref):
    iq = iq_ref[0]
    per_blk = LANES // IDX_DIM
    iqh = [_lane_band(iq[:, (h // per_blk) * LANES:(h // per_blk + 1) * LANES],
                      (h % per_blk) * IDX_DIM, IDX_DIM) for h in range(IDX_HEADS)]
    w = iwt_ref[0]

    def stage_scores(j):
        ikb = _key_rows(ik_ref, j)
        half = TK // 2
        for r in (0, half):
            acc = jnp.zeros((half, TQ), F32)
            for h in range(IDX_HEADS):
                acc = acc + jnp.maximum(_dot_nt(ikb[r:r + half], iqh[h]), 0.0) * w[h:h + 1, :]
            score_ref[r:r + half, :] = acc

    def staged_keys():
        acc = score_ref[...]
        acc = jnp.where(acc == 0.0, 0.0, acc)
        bits = lax.bitcast_convert_type(acc, I32)
        return bits ^ ((bits >> 31) & 0x7FFFFFFF)

    def store_keys(j, keys):
        key_ref[j] = keys
        for p, plane in enumerate(_bit_planes(keys)):
            plane_ref[p, j] = plane

    def per_block(j):
        keys = staged_keys()
        stage_scores(j + 1)
        store_keys(j, keys)

    def finish():
        krow = lax.broadcasted_iota(I32, (TK, TQ), 0)
        qcol = lax.broadcasted_iota(I32, (TK, TQ), 1)
        admissible = (krow // CHUNK) <= (qcol // CHUNK)
        store_keys(i, jnp.where(admissible, staged_keys(), KEY_NEG_INF))

    stage_scores(0)
    return per_block, finish


def _select_mask(i, key_ref, plane_ref, mask_ref, *, top_k, seq):
    nkb = seq // TK
    blk = lax.broadcasted_iota(I32, (nkb, SUBLANES, TQ), 0)
    alive0 = jnp.where(blk <= i, -1, 0)

    def bit_body(p, carry, n_bits):
        alive, n_gt, thr_u = carry
        hit = alive & plane_ref[p]
        pc = jnp.sum(lax.population_count(hit), axis=0)
        n_hit = jnp.sum(pc.astype(F32), axis=0, keepdims=True)
        take = (n_gt + n_hit) >= top_k
        thr_u = thr_u | jnp.where(take, lax.shift_left(jnp.int32(1), n_bits - 1 - p), 0)
        n_gt = n_gt + jnp.where(take, 0.0, n_hit)
        alive = jnp.where(take, hit, alive ^ hit)
        return alive, n_gt, thr_u

    zero = jnp.zeros((1, TQ), I32)
    alive, n_gt, thr_u = lax.fori_loop(
        0, KEY_BITS, lambda p, c: bit_body(p, c, KEY_BITS), (alive0, jnp.zeros((1, TQ), F32), zero))
    idx_bits = _index_bits(seq)
    _, _, rev_idx = lax.fori_loop(
        KEY_BITS, KEY_BITS + idx_bits,
        lambda p, c: bit_body(p, c, KEY_BITS + idx_bits), (alive, n_gt, zero))
    thr = thr_u ^ INT_MIN
    lim = seq - rev_idx
    lim = jnp.where(thr > KEY_NEG_INF, lim, seq)
    thr = jnp.maximum(thr, KEY_NEG_INF + 1)
    krow = lax.broadcasted_iota(I32, (TK, TQ), 0)

    def mask_body(j, c):
        late = jnp.where(krow >= lim - j * TK, 1, 0)
        mask_ref[j] = jnp.where(key_ref[j] - late >= thr, 0.0, NEG)
        return c

    lax.fori_loop(0, i + 1, mask_body, 0)


def _attn_kernel(lam_ref, aq_ref, ak_ref, avt_ref, abias_ref, g_ref,
                 bq_ref, iq_ref, kb_ref, ik_ref, bvt_ref, iwt_ref, bbias_ref,
                 ya_ref, yb_ref,
                 sa_ref, sb_ref, aacc_ref, bacc_ref, ot_ref,
                 score_ref, key_ref, plane_ref, mask_ref, *, lambda_init, top_k, seq):
    i = pl.program_id(1)
    nkb = seq // TK
    n_maps = 2 * A_HEADS

    @pl.when((i == 0) & (pl.program_id(0) == 0))
    def _():
        plane_ref[0:KEY_BITS] = jnp.zeros((KEY_BITS, nkb, SUBLANES, TQ), I32)
        for t, plane in enumerate(_index_planes(nkb)):
            plane_ref[KEY_BITS + t] = plane

    index_block, finish_index = _indexer(i, iq_ref, ik_ref, iwt_ref, score_ref, key_ref, plane_ref)

    def index_body(j, c):
        index_block(j)
        return c

    lax.fori_loop(0, i, index_body, 0)
    finish_index()
    _select_mask(i, key_ref, plane_ref, mask_ref, top_k=top_k, seq=seq)

    lv = lam_ref[...]
    lam = (jnp.exp(jnp.sum(lv[0:1] * lv[1:2], axis=-1, keepdims=True))
           - jnp.exp(jnp.sum(lv[2:3] * lv[3:4], axis=-1, keepdims=True)) + lambda_init)

    q = aq_ref[0]
    qm = []
    for h in range(A_HEADS):
        q12 = q[:, h * LANES:(h + 1) * LANES]
        qm += [_lane_band(q12, 0, A_HEAD_DIM), _lane_band(q12, A_HEAD_DIM, A_HEAD_DIM)]

    def a_logits(j, c, kind):
        h = c // 2
        kb = ak_ref[0, pl.ds(pl.multiple_of(j * TK, TK), TK), h * LANES:(h + 1) * LANES]
        s = _dot_nt(kb, qm[c])
        return s if kind == 'far' else s + abias_ref[_bias_kind(j, i), h]

    qb = bq_ref[0]
    qh = [_lane_band(qb[:, (h // 2) * LANES:(h // 2 + 1) * LANES],
                     (h % 2) * B_HEAD_DIM, B_HEAD_DIM) for h in range(B_HEADS)]

    def b_logits(j, h, kind):
        s = _dot_nt(_key_rows(kb_ref, j), qh[h]) + mask_ref[j]
        return s if kind == 'far' else s + bbias_ref[_bias_kind(j, i), h]

    _run_key_block_streams(i, [
        _key_block_stream(i, n_maps, sa_ref, aacc_ref, a_logits, lambda j, c: avt_ref[0, j, c // 2]),
        _key_block_stream(i, B_HEADS, sb_ref, bacc_ref, b_logits, lambda j, h: bvt_ref[0, j])])

    for h in range(A_HEADS):
        a1 = aacc_ref[2 * h]
        a2 = aacc_ref[2 * h + 1]
        a = (a1[0:A_V_DIM] / a1[A_V_DIM:A_V_DIM + 1]
             - lam * (a2[0:A_V_DIM] / a2[A_V_DIM:A_V_DIM + 1]))
        y = a * lax.rsqrt(jnp.mean(a * a, axis=0, keepdims=True) + EPS) * g_ref[...]
        ot_ref[h * A_V_DIM:(h + 1) * A_V_DIM, :] = y * (1.0 - lambda_init)
    ya_ref[0] = ot_ref[...].T.astype(BF16)

    for h in range(B_HEADS):
        a = bacc_ref[h]
        ot_ref[h * B_HEAD_DIM:(h + 1) * B_HEAD_DIM, :] = (
            a[0:B_HEAD_DIM] / a[B_HEAD_DIM:B_HEAD_DIM + 1])
    yb_ref[0] = ot_ref[...].T.astype(BF16)


def _attention(proj, avt, bvt, iwt, lam_vecs, a_bias, b_bias, subln_g_col, *, lambda_init, top_k):
    bsz, s, _ = proj.shape
    nq = s // TQ
    nkb = s // TK
    a_width = A_HEADS * A_V_DIM
    b_width = B_HEADS * B_HEAD_DIM
    assert a_width == b_width
    iq_w = IDX_HEADS * IDX_DIM
    once = pl.Buffered(1)
    return pl.pallas_call(
        functools.partial(_attn_kernel, lambda_init=lambda_init, top_k=top_k, seq=s),
        grid=(bsz, nq),
        in_specs=[
            pl.BlockSpec((4, A_HEAD_DIM), lambda b, i: (0, 0)),
            pl.BlockSpec((1, TQ, a_width), lambda b, i: (b, i, COL_AQ // a_width)),
            pl.BlockSpec((1, s, a_width), lambda b, i: (b, 0, COL_AK // a_width)),
            pl.BlockSpec((1, nkb, A_HEADS, A_V_DIM + ONES_ROWS, TK), lambda b, i: (b, 0, 0, 0, 0)),
            pl.BlockSpec((3, A_HEADS, TK, TQ), lambda b, i: (0, 0, 0, 0), once),
            pl.BlockSpec((A_V_DIM, 1), lambda b, i: (0, 0)),
            pl.BlockSpec((1, TQ, b_width), lambda b, i: (b, i, COL_BQ // b_width)),
            pl.BlockSpec((1, TQ, iq_w), lambda b, i: (b, i, COL_IQ // iq_w)),
            pl.BlockSpec((1, s, LANES), lambda b, i: (b, 0, COL_KB // LANES), once),
            pl.BlockSpec((1, s, LANES), lambda b, i: (b, 0, COL_IK // LANES), once),
            pl.BlockSpec((1, nkb, B_HEAD_DIM + ONES_ROWS, TK), lambda b, i: (b, 0, 0, 0), once),
            pl.BlockSpec((1, IDX_HEADS, TQ), lambda b, i: (b, 0, i)),
            pl.BlockSpec((3, B_HEADS, TK, TQ), lambda b, i: (0, 0, 0, 0), once),
        ],
        out_specs=[pl.BlockSpec((1, TQ, a_width), lambda b, i: (b, i, 0)),
                   pl.BlockSpec((1, TQ, b_width), lambda b, i: (b, i, 0))],
        out_shape=[jax.ShapeDtypeStruct((bsz, s, a_width), BF16),
                   jax.ShapeDtypeStruct((bsz, s, b_width), BF16)],
        scratch_shapes=[
            pltpu.VMEM((2 * A_HEADS, TK, TQ), F32),
            pltpu.VMEM((B_HEADS, TK, TQ), F32),
            pltpu.VMEM((2 * A_HEADS, A_V_DIM + ONES_ROWS, TQ), F32),
            pltpu.VMEM((B_HEADS, B_HEAD_DIM + ONES_ROWS, TQ), F32),
            pltpu.VMEM((a_width, TQ), F32),
            pltpu.VMEM((TK, TQ), F32),
            pltpu.VMEM((nkb, TK, TQ), I32),
            pltpu.VMEM((KEY_BITS + _index_bits(s), nkb, SUBLANES, TQ), I32),
            pltpu.VMEM((nkb, TK, TQ), F32),
        ],
        compiler_params=pltpu.CompilerParams(
            dimension_semantics=("arbitrary", "arbitrary"), vmem_limit_bytes=VMEM_LIMIT),
        name="attention",
    )(lam_vecs, proj, proj, avt, a_bias, subln_g_col, proj, proj, proj, proj, bvt, iwt, b_bias)


TAIL_ROWS = 256
TAIL_FF = 1024


def _tail_kernel(ya_ref, yb_ref, wo_ref, x_ref, g_mix_ref, g_in_ref, wu_ref, wd_ref, g_out_ref,
                 o_ref, x1_ref, h_ref, acc_ref):
    chunks = [slice(r, r + TAIL_ROWS) for r in range(0, x_ref.shape[0], TAIL_ROWS)]
    ff_steps = list(range(0, wu_ref.shape[1], TAIL_FF))

    def mlp_part(rows, k):
        u = jnp.maximum(_dot(h_ref[rows, :], wu_ref[:, k:k + TAIL_FF]), 0.0)
        return _dot((u * u).astype(BF16), wd_ref[k:k + TAIL_FF, :])

    half = ya_ref.shape[-1]
    for rows in chunks:
        m = (_dot(ya_ref[rows, :], wo_ref[0:half, :])
             + _dot(yb_ref[rows, :], wo_ref[half:2 * half, :]))
        x1 = x_ref[rows, :] + _rms(m, g_mix_ref[...])
        x1_ref[rows, :] = x1
        h_ref[rows, :] = _rms(x1, g_in_ref[...]).astype(BF16)
    acc_ref[...] = mlp_part(slice(None), ff_steps[0])
    for k in ff_steps[1:-1]:
        acc_ref[...] += mlp_part(slice(None), k)
    for rows in chunks:
        y = acc_ref[rows, :] + mlp_part(rows, ff_steps[-1])
        o_ref[rows, :] = x1_ref[rows, :] + _rms(y, g_out_ref[...])


def _layer_tail(ya, yb, wo, x, g_mix, g_in, wu, wd, layer, g_out, *, tm=1024):
    n, d = x.shape
    half = ya.shape[-1]
    ff = wu.shape[-1]
    assert ff // TAIL_FF >= 2 and tm % TAIL_ROWS == 0
    once = pl.Buffered(1)
    return pl.pallas_call(
        _tail_kernel,
        grid=(n // tm,),
        in_specs=[
            pl.BlockSpec((tm, half), lambda i: (i, 0)),
            pl.BlockSpec((tm, half), lambda i: (i, 0)),
            pl.BlockSpec((2 * half, d), lambda i: (0, 0), once),
            pl.BlockSpec((tm, d), lambda i: (i, 0)),
            pl.BlockSpec((1, d), lambda i: (0, 0)),
            pl.BlockSpec((1, d), lambda i: (0, 0)),
            pl.BlockSpec((None, d, ff), lambda i: (layer, 0, 0), once),
            pl.BlockSpec((None, ff, d), lambda i: (layer, 0, 0), once),
            pl.BlockSpec((1, d), lambda i: (0, 0)),
        ],
        out_specs=pl.BlockSpec((tm, d), lambda i: (i, 0)),
        out_shape=jax.ShapeDtypeStruct((n, d), F32),
        scratch_shapes=[pltpu.VMEM((tm, d), F32), pltpu.VMEM((tm, d), BF16),
                        pltpu.VMEM((tm, d), F32)],
        compiler_params=pltpu.CompilerParams(
            dimension_semantics=("arbitrary",), vmem_limit_bytes=TAIL_VMEM_LIMIT),
        name="layer_tail",
    )(ya, yb, wo, x, g_mix, g_in, wu, wd, g_out)


CONV_ROWS = 64
U_HIST = 32
Z_HIST = 8


def _conv_kernel(x_ref, g_in_ref, w_in_ref, dw_w_ref, dw_b_ref, ln_g_ref, ln_b_ref, sc_w_ref,
                 u_ref, z_ref, ubuf_ref, zbuf_ref, shift_ref, conv_ref, *, ts):
    @pl.when(pl.program_id(1) == 0)
    def _():
        ubuf_ref[0:U_HIST, :] = jnp.zeros((U_HIST, CONV_CH), F32)
        zbuf_ref[0:Z_HIST, :] = jnp.zeros((Z_HIST, SC_CH), F32)

    x = x_ref[0]
    hb = _rms(x, g_in_ref[...]).astype(BF16)
    c = CONV_CH
    ca = _dot(hb, w_in_ref[:, 0:c])
    cg = _dot(hb, w_in_ref[:, c:2 * c])
    ubuf_ref[U_HIST:U_HIST + ts, :] = ca * jax.nn.sigmoid(cg)
    dc = _dot(hb, w_in_ref[:, 3 * c:4 * c])
    dh = _dot(hb, w_in_ref[:, 4 * c:5 * c])
    zbuf_ref[Z_HIST:Z_HIST + ts, :] = dc * dh

    first_off = U_HIST - (CONV_WIDTH - 1)
    shift_rows = shift_ref.shape[1]
    for r in range(1, SUBLANES):
        shift_ref[r - 1] = ubuf_ref[r:r + shift_rows, :]

    for base in range(0, ts, CONV_ROWS):
        acc = jnp.zeros((CONV_ROWS, c), F32)
        for j in range(CONV_WIDTH):
            r, a = (first_off + j) % SUBLANES, (first_off + j) // SUBLANES
            rows = pl.ds(base + SUBLANES * a, CONV_ROWS)
            taps = ubuf_ref[rows, :] if r == 0 else shift_ref[r - 1, rows, :]
            acc = acc + dw_w_ref[j:j + 1, :] * taps
        conv_ref[base:base + CONV_ROWS, :] = acc
    u = conv_ref[...] + dw_b_ref[...]
    mu = jnp.mean(u, axis=-1, keepdims=True)
    ctr = u - mu
    var = jnp.mean(ctr * ctr, axis=-1, keepdims=True)
    u = ctr * lax.rsqrt(var + EPS) * ln_g_ref[...] + ln_b_ref[...]
    u = u * jax.nn.sigmoid(u)

    z = jnp.zeros((ts, c), F32)
    for j in range(SC_WIDTH):
        off = Z_HIST - (SC_WIDTH - 1) + j
        z = z + sc_w_ref[j:j + 1, :] * zbuf_ref[off:off + ts, :]
    z = _dot(hb, w_in_ref[:, 2 * c:3 * c]) * z

    ubuf_ref[0:U_HIST, :] = ubuf_ref[ts:ts + U_HIST, :]
    zbuf_ref[0:Z_HIST, :] = zbuf_ref[ts:ts + Z_HIST, :]

    u_ref[0] = u.astype(BF16)
    z_ref[0] = z.astype(BF16)


def _conv_mixer(x, g_in, w_in, dw_w, dw_b, ln_g, ln_b, sc_w, *, ts=1024):
    bsz, s, d = x.shape
    full = lambda a: pl.BlockSpec(a.shape, lambda b, i: (0,) * a.ndim, pl.Buffered(1))
    return pl.pallas_call(
        functools.partial(_conv_kernel, ts=ts),
        grid=(bsz, s // ts),
        in_specs=[pl.BlockSpec((1, ts, d), lambda b, i: (b, i, 0)),
                  full(g_in), full(w_in), full(dw_w), full(dw_b), full(ln_g), full(ln_b),
                  full(sc_w)],
        out_specs=[pl.BlockSpec((1, ts, CONV_CH), lambda b, i: (b, i, 0)),
                   pl.BlockSpec((1, ts, SC_CH), lambda b, i: (b, i, 0))],
        out_shape=[jax.ShapeDtypeStruct((bsz, s, CONV_CH), BF16),
                   jax.ShapeDtypeStruct((bsz, s, SC_CH), BF16)],
        scratch_shapes=[pltpu.VMEM((U_HIST + ts, CONV_CH), F32),
                        pltpu.VMEM((Z_HIST + ts, SC_CH), F32),
                        pltpu.VMEM((SUBLANES - 1, ts + U_HIST - SUBLANES, CONV_CH), F32),
                        pltpu.VMEM((ts, CONV_CH), F32)],
        compiler_params=pltpu.CompilerParams(
            dimension_semantics=("arbitrary", "arbitrary"), vmem_limit_bytes=VMEM_LIMIT),
        name="conv_mixer",
    )(x, g_in, w_in, dw_w, dw_b, ln_g, ln_b, sc_w)


def _t5_bucket(rel):
    nb = NUM_BUCKETS // 2
    ret = jnp.where(rel > 0, nb, 0)
    n = jnp.abs(rel)
    max_exact = nb // 2
    nf = jnp.maximum(n, 1).astype(jnp.float32)
    large = max_exact + (jnp.log(nf / max_exact) / math.log(MAX_DISTANCE / max_exact)
                         * (nb - max_exact)).astype(jnp.int32)
    large = jnp.minimum(large, nb - 1)
    return ret + jnp.where(n < max_exact, n, large)


REL_LO = -(TQ + TK - 1)
REL_LEN = TK - REL_LO
REL_LANES = -(-REL_LEN // LANES) * LANES


def _bias_kernel(fr_ref, a_ref, b_ref):
    krow = lax.broadcasted_iota(I32, (TK, TQ), 0)
    qcol = lax.broadcasted_iota(I32, (TK, TQ), 1)
    admissible = (krow // CHUNK) <= (qcol // CHUNK)
    zeros = jnp.zeros((TK, TQ), F32)
    for h in range(A_HEADS + B_HEADS):
        table = jnp.broadcast_to(fr_ref[h], (TK, REL_LANES))
        blocks = []
        for shift in (TK, 0):
            s0 = (-shift - REL_LO - (REL_LEN - 1)) % REL_LANES
            blocks.append(pltpu.roll(table, s0, 1, stride=1, stride_axis=0)[:, 0:TQ])
        prev, diag = blocks
        if h < A_HEADS:
            a_ref[0, h], a_ref[1, h], a_ref[2, h] = zeros, prev, jnp.where(admissible, diag, NEG)
        else:
            g = h - A_HEADS
            b_ref[0, g], b_ref[1, g], b_ref[2, g] = zeros, prev, diag


def _bias_tables(rel_bias):
    rel = jnp.arange(REL_LO, TK, dtype=jnp.int32)
    far_bucket = NUM_BUCKETS // 2 - 1
    f = (rel_bias[_t5_bucket(rel)].astype(F32) - rel_bias[far_bucket].astype(F32)[None, :]).T
    fr = jnp.pad(f[:, ::-1] * LOG2E, ((0, 0), (0, REL_LANES - REL_LEN)))[:, None, :]
    heads = A_HEADS + B_HEADS
    return pl.pallas_call(
        _bias_kernel,
        grid=(1,),
        in_specs=[pl.BlockSpec((heads, 1, REL_LANES), lambda i: (0, 0, 0))],
        out_specs=[pl.BlockSpec((3, A_HEADS, TK, TQ), lambda i: (0, 0, 0, 0)),
                   pl.BlockSpec((3, B_HEADS, TK, TQ), lambda i: (0, 0, 0, 0))],
        out_shape=[jax.ShapeDtypeStruct((3, A_HEADS, TK, TQ), F32),
                   jax.ShapeDtypeStruct((3, B_HEADS, TK, TQ), F32)],
        compiler_params=pltpu.CompilerParams(
            dimension_semantics=("arbitrary",), vmem_limit_bytes=VMEM_LIMIT),
        name="bias_tables",
    )(fr)


def _attn_weights(w_in):
    o = np.cumsum([0, 512, 512, 512, 512, 64, 64, 256, 32, 8])
    aq, ak, av, bq, bk, bv, iq, ik, iw = [w_in[:, o[n]:o[n + 1]] for n in range(9)]
    aq = aq * (A_HEAD_DIM ** -0.5 * LOG2E)
    bq = bq * (B_HEAD_DIM ** -0.5 * LOG2E)
    w = jnp.concatenate([aq, ak, bq, iq, bk, bk, ik, ik, ik, ik], axis=1).astype(BF16)
    pad = jnp.zeros((w_in.shape[0], VT_ROWS - ROW_IW - IDX_HEADS), w_in.dtype)
    wt = jnp.concatenate([av, bv, iw, pad], axis=1).T.astype(BF16)
    return w, wt


def kernel(x, rel_bias, norm_g, w_mlp_up, w_mlp_down, attn_w_in, attn_w_out, diff_lambda,
           diff_subln_g, conv_w_in, conv_w_out, conv_dw_w, conv_dw_b, conv_ln_g, conv_ln_b,
           sconv_w):
    bsz, s, d = x.shape
    depth = norm_g.shape[0]
    top_k = min(TOPK_MAX, s // 4)
    row = lambda v: v.reshape(1, -1)
    a_bias, b_bias = _bias_tables(rel_bias)
    w_up, w_down = w_mlp_up.astype(BF16), w_mlp_down.astype(BF16)
    for i in range(depth):
        j = i // 2
        if i % 2 == 0:
            lambda_init = 0.8 - 0.6 * math.exp(-0.3 * i)
            w, wt = _attn_weights(attn_w_in[j])
            proj, avt, bvt, iwt = _attn_in(x.reshape(bsz, s, d), row(norm_g[i, 0]), w, wt)
            ya, yb = _attention(proj, avt, bvt, iwt, diff_lambda[j], a_bias, b_bias,
                                diff_subln_g[j].reshape(-1, 1), lambda_init=lambda_init,
                                top_k=top_k)
            w_out = attn_w_out[j]
        else:
            ya, yb = _conv_mixer(x.reshape(bsz, s, d), row(norm_g[i, 0]),
                                 conv_w_in[j].astype(BF16), conv_dw_w[j], row(conv_dw_b[j]),
                                 row(conv_ln_g[j]), row(conv_ln_b[j]), sconv_w[j])
            w_out = conv_w_out[j]
        x = _layer_tail(ya.reshape(bsz * s, -1), yb.reshape(bsz * s, -1), w_out.astype(BF16),
                        x.reshape(bsz * s, d), row(norm_g[i, 1]), row(norm_g[i, 2]),
                        w_up, w_down, i, row(norm_g[i, 3]))
    return x.reshape(bsz, s, d)
```

```python
import functools
import math

import numpy as np
import jax
import jax.numpy as jnp
from jax import lax
from jax.experimental import pallas as pl
from jax.experimental.pallas import tpu as pltpu

F32 = jnp.float32
BF16 = jnp.bfloat16
I32 = jnp.int32

CHUNK = 64
NUM_BUCKETS = 32
MAX_DISTANCE = 128
EPS = 1e-6
NEG = -1e30
LOG2E = math.log2(math.e)
A_HEADS = 4
A_HEAD_DIM = 64
A_V_DIM = 2 * A_HEAD_DIM
B_HEADS = 8
B_HEAD_DIM = 64
IDX_HEADS = 8
IDX_DIM = 32
TOPK_MAX = 256
CONV_CH = 512
CONV_WIDTH = 31
SC_CH = 512
SC_WIDTH = 3

LANES = 128
SUBLANES = 8
VMEM_LIMIT = 56 * 1024 * 1024
TAIL_VMEM_LIMIT = 60 * 1024 * 1024

TQ = 256
TK = 256
COL_AQ, COL_AK, COL_BQ, COL_IQ, COL_KB, COL_IK, PROJ_W = 0, 512, 1024, 1536, 1792, 1920, 2048
ROW_AV, ROW_BV, ROW_IW, VT_ROWS = 0, 512, 576, 640
ONES_ROWS = 16
INT_MIN = -2 ** 31
KEY_NEG_INF = INT_MIN + 0x7FFFFF


def _rms(x, g):
    return x * lax.rsqrt(jnp.mean(x * x, axis=-1, keepdims=True) + EPS) * g


def _dot_nt(a, b):
    return lax.dot_general(a, b, (((1,), (1,)), ((), ())), preferred_element_type=F32)


def _dot(a, b):
    return jnp.dot(a, b, preferred_element_type=F32)


def _lane_band(x, lo, width):
    lane = lax.broadcasted_iota(I32, x.shape, 1)
    return jnp.where((lane >= lo) & (lane < lo + width), x, jnp.zeros_like(x))


KEY_BITS = 32


def _bit_planes(keys):
    assert keys.shape[0] == KEY_BITS * SUBLANES
    u = keys ^ INT_MIN
    a = [u[SUBLANES * r:SUBLANES * (r + 1), :] for r in range(KEY_BITS)]
    j, m = KEY_BITS // 2, (1 << (KEY_BITS // 2)) - 1
    while j:
        mask = np.int32(np.uint32(m))
        k = 0
        while k < KEY_BITS:
            t = (a[k] ^ lax.shift_right_logical(a[k + j], jnp.int32(j))) & mask
            a[k] = a[k] ^ t
            a[k + j] = a[k + j] ^ lax.shift_left(t, jnp.int32(j))
            k = (k + j + 1) & ~j
        j >>= 1
        m = (m ^ (m << j)) & 0xFFFFFFFF
    return a


def _index_bits(seq):
    bits = seq.bit_length() - 1
    assert seq == 1 << bits and seq >= KEY_BITS * SUBLANES
    return bits


def _index_planes(nkb):
    shape = (nkb, SUBLANES, TQ)
    blk = lax.broadcasted_iota(I32, shape, 0)
    sub = lax.broadcasted_iota(I32, shape, 1)
    sub_bits = SUBLANES.bit_length() - 1
    word_bits = KEY_BITS.bit_length() - 1
    planes = []
    for b in range(_index_bits(nkb * TK) - 1, -1, -1):
        if b >= sub_bits + word_bits:
            on = ((nkb - 1 - blk) >> (b - sub_bits - word_bits)) & 1
            planes.append(jnp.where(on == 1, -1, 0))
        elif b >= sub_bits:
            t = b - sub_bits
            pattern = sum(1 << k for k in range(KEY_BITS) if (k >> t) & 1)
            planes.append(jnp.full(shape, np.int32(np.uint32(pattern)), I32))
        else:
            on = ((SUBLANES - 1 - sub) >> b) & 1
            planes.append(jnp.where(on == 1, -1, 0))
    return planes


def _bias_kind(j, i):
    return jnp.clip(j - i + 2, 0, 2)


def _key_block_stream(i, n_chains, s_ref, acc_ref, logits, v_aug):
    chains = range(n_chains)

    def stage(j, c, kind):
        s = logits(j, c, kind)
        s_ref[c] = s
        return jnp.max(s, axis=0, keepdims=True)

    def absorb(j, c, m_old, block_max):
        m_new = jnp.maximum(m_old, block_max)
        alpha = jnp.exp2(m_old - m_new)
        p = jnp.exp2(s_ref[c] - m_new).astype(BF16)
        acc_ref[c] = acc_ref[c] * alpha + _dot(v_aug(j, c), p)
        return m_new

    def start():
        acc_ref[...] = jnp.zeros_like(acc_ref)
        return (jnp.full((1, TQ), NEG, F32),) * n_chains, tuple(stage(0, c, 'any') for c in chains)

    def step(j, carry, kind):
        ms, bms = carry
        out = [(absorb(j, c, ms[c], bms[c]), stage(j + 1, c, kind)) for c in chains]
        return tuple(o[0] for o in out), tuple(o[1] for o in out)

    def finish(carry):
        ms, bms = carry
        for c in chains:
            absorb(i, c, ms[c], bms[c])

    return start, step, finish


def _run_key_block_streams(i, streams):
    def body(j, carries, kind):
        return tuple(step(j, c, kind) for (_, step, _), c in zip(streams, carries))

    carries = tuple(start() for start, _, _ in streams)
    n_far = jnp.maximum(i - 2, 0)
    carries = lax.fori_loop(0, n_far, functools.partial(body, kind='far'), carries)
    carries = lax.fori_loop(n_far, i, functools.partial(body, kind='near'), carries)
    for (_, _, finish), c in zip(streams, carries):
        finish(c)


def _attn_in_kernel(x_ref, g_ref, w_ref, wt_ref, proj_ref, avt_ref, bvt_ref, iwt_ref, *, tm):
    hb = _rms(x_ref[0], g_ref[...]).astype(BF16)
    proj_ref[0] = _dot(hb, w_ref[...]).astype(BF16)
    t = _dot_nt(wt_ref[...], hb)
    ones = jnp.ones((ONES_ROWS, TK), BF16)
    for c in range(tm // TK):
        cols = slice(c * TK, (c + 1) * TK)
        for h in range(A_HEADS):
            r = ROW_AV + h * A_V_DIM
            avt_ref[0, c, h, 0:A_V_DIM, :] = t[r:r + A_V_DIM, cols].astype(BF16)
            avt_ref[0, c, h, A_V_DIM:A_V_DIM + ONES_ROWS, :] = ones
        bvt_ref[0, c, 0:B_HEAD_DIM, :] = t[ROW_BV:ROW_IW, cols].astype(BF16)
        bvt_ref[0, c, B_HEAD_DIM:B_HEAD_DIM + ONES_ROWS, :] = ones
    iwt_ref[0] = t[ROW_IW:ROW_IW + IDX_HEADS, :] * (IDX_HEADS ** -0.5 * IDX_DIM ** -0.5)


def _attn_in(x, g, w, wt, *, tm=1024):
    bsz, s, d = x.shape
    av_blk = (A_HEADS, A_V_DIM + ONES_ROWS, TK)
    bv_blk = (B_HEAD_DIM + ONES_ROWS, TK)
    return pl.pallas_call(
        functools.partial(_attn_in_kernel, tm=tm),
        grid=(bsz, s // tm),
        in_specs=[
            pl.BlockSpec((1, tm, d), lambda b, i: (b, i, 0)),
            pl.BlockSpec((1, d), lambda b, i: (0, 0)),
            pl.BlockSpec((d, PROJ_W), lambda b, i: (0, 0)),
            pl.BlockSpec((VT_ROWS, d), lambda b, i: (0, 0)),
        ],
        out_specs=[
            pl.BlockSpec((1, tm, PROJ_W), lambda b, i: (b, i, 0)),
            pl.BlockSpec((1, tm // TK) + av_blk, lambda b, i: (b, i, 0, 0, 0)),
            pl.BlockSpec((1, tm // TK) + bv_blk, lambda b, i: (b, i, 0, 0)),
            pl.BlockSpec((1, IDX_HEADS, tm), lambda b, i: (b, 0, i)),
        ],
        out_shape=[
            jax.ShapeDtypeStruct((bsz, s, PROJ_W), BF16),
            jax.ShapeDtypeStruct((bsz, s // TK) + av_blk, BF16),
            jax.ShapeDtypeStruct((bsz, s // TK) + bv_blk, BF16),
            jax.ShapeDtypeStruct((bsz, IDX_HEADS, s), F32),
        ],
        compiler_params=pltpu.CompilerParams(
            dimension_semantics=("arbitrary", "arbitrary"), vmem_limit_bytes=VMEM_LIMIT),
        name="attn_in",
    )(x, g, w, wt)


def _key_rows(ref, j):
    return ref[0, pl.ds(pl.multiple_of(j * TK, TK), TK), :]


def _indexer(i, iq_ref, ik_ref, iwt_ref, score_ref, key_ref, plane_ref):
    iq = iq_ref[0]
    per_blk = LANES // IDX_DIM
    iqh = [_lane_band(iq[:, (h // per_blk) * LANES:(h // per_blk + 1) * LANES],
                      (h % per_blk) * IDX_DIM, IDX_DIM) for h in range(IDX_HEADS)]
    w = iwt_ref[0]

    def stage_scores(j):
        ikb = _key_rows(ik_ref, j)
        half = TK // 2
        for r in (0, half):
            acc = jnp.zeros((half, TQ), F32)
            for h in range(IDX_HEADS):
                acc = acc + jnp.maximum(_dot_nt(ikb[r:r + half], iqh[h]), 0.0) * w[h:h + 1, :]
            score_ref[r:r + half, :] = acc

    def staged_keys():
        acc = score_ref[...]
        acc = jnp.where(acc == 0.0, 0.0, acc)
        bits = lax.bitcast_convert_type(acc, I32)
        return bits ^ ((bits >> 31) & 0x7FFFFFFF)

    def store_keys(j, keys):
        key_ref[j] = keys
        for p, plane in enumerate(_bit_planes(keys)):
            plane_ref[p, j] = plane

    def per_block(j):
        keys = staged_keys()
        stage_scores(j + 1)
        store_keys(j, keys)

    def finish():
        krow = lax.broadcasted_iota(I32, (TK, TQ), 0)
        qcol = lax.broadcasted_iota(I32, (TK, TQ), 1)
        admissible = (krow // CHUNK) <= (qcol // CHUNK)
        store_keys(i, jnp.where(admissible, staged_keys(), KEY_NEG_INF))

    stage_scores(0)
    return per_block, finish


def _select_mask(i, key_ref, plane_ref, mask_ref, *, top_k, seq):
    nkb = seq // TK
    blk = lax.broadcasted_iota(I32, (nkb, SUBLANES, TQ), 0)
    alive0 = jnp.where(blk <= i, -1, 0)

    def bit_body(p, carry, n_bits):
        alive, n_gt, thr_u = carry
        hit = alive & plane_ref[p]
        pc = jnp.sum(lax.population_count(hit), axis=0)
        n_hit = jnp.sum(pc.astype(F32), axis=0, keepdims=True)
        take = (n_gt + n_hit) >= top_k
        thr_u = thr_u | jnp.where(take, lax.shift_left(jnp.int32(1), n_bits - 1 - p), 0)
        n_gt = n_gt + jnp.where(take, 0.0, n_hit)
        alive = jnp.where(take, hit, alive ^ hit)
        return alive, n_gt, thr_u

    zero = jnp.zeros((1, TQ), I32)
    alive, n_gt, thr_u = lax.fori_loop(
        0, KEY_BITS, lambda p, c: bit_body(p, c, KEY_BITS), (alive0, jnp.zeros((1, TQ), F32), zero))
    idx_bits = _index_bits(seq)
    _, _, rev_idx = lax.fori_loop(
        KEY_BITS, KEY_BITS + idx_bits,
        lambda p, c: bit_body(p, c, KEY_BITS + idx_bits), (alive, n_gt, zero))
    thr = thr_u ^ INT_MIN
    lim = seq - rev_idx
    lim = jnp.where(thr > KEY_NEG_INF, lim, seq)
    thr = jnp.maximum(thr, KEY_NEG_INF + 1)
    krow = lax.broadcasted_iota(I32, (TK, TQ), 0)

    def mask_body(j, c):
        late = jnp.where(krow >= lim - j * TK, 1, 0)
        mask_ref[j] = jnp.where(key_ref[j] - late >= thr, 0.0, NEG)
        return c

    lax.fori_loop(0, i + 1, mask_body, 0)


def _attn_kernel(lam_ref, aq_ref, ak_ref, avt_ref, abias_ref, g_ref,
                 bq_ref, iq_ref, kb_ref, ik_ref, bvt_ref, iwt_ref, bbias_ref,
                 ya_ref, yb_ref,
                 sa_ref, sb_ref, aacc_ref, bacc_ref, ot_ref,
                 score_ref, key_ref, plane_ref, mask_ref, *, lambda_init, top_k, seq):
    i = pl.program_id(1)
    nkb = seq // TK
    n_maps = 2 * A_HEADS

    @pl.when((i == 0) & (pl.program_id(0) == 0))
    def _():
        plane_ref[0:KEY_BITS] = jnp.zeros((KEY_BITS, nkb, SUBLANES, TQ), I32)
        for t, plane in enumerate(_index_planes(nkb)):
            plane_ref[KEY_BITS + t] = plane

    index_block, finish_index = _indexer(i, iq_ref, ik_ref, iwt_ref, score_ref, key_ref, plane_ref)

    def index_body(j, c):
        index_block(j)
        return c

    lax.fori_loop(0, i, index_body, 0)
    finish_index()
    _select_mask(i, key_ref, plane_ref, mask_ref, top_k=top_k, seq=seq)

    lv = lam_ref[...]
    lam = (jnp.exp(jnp.sum(lv[0:1] * lv[1:2], axis=-1, keepdims=True))
           - jnp.exp(jnp.sum(lv[2:3] * lv[3:4], axis=-1, keepdims=True)) + lambda_init)

    q = aq_ref[0]
    qm = []
    for h in range(A_HEADS):
        q12 = q[:, h * LANES:(h + 1) * LANES]
        qm += [_lane_band(q12, 0, A_HEAD_DIM), _lane_band(q12, A_HEAD_DIM, A_HEAD_DIM)]

    def a_logits(j, c, kind):
        h = c // 2
        kb = ak_ref[0, pl.ds(pl.multiple_of(j * TK, TK), TK), h * LANES:(h + 1) * LANES]
        s = _dot_nt(kb, qm[c])
        return s if kind == 'far' else s + abias_ref[_bias_kind(j, i), h]

    qb = bq_ref[0]
    qh = [_lane_band(qb[:, (h // 2) * LANES:(h // 2 + 1) * LANES],
                     (h % 2) * B_HEAD_DIM, B_HEAD_DIM) for h in range(B_HEADS)]

    def b_logits(j, h, kind):
        s = _dot_nt(_key_rows(kb_ref, j), qh[h]) + mask_ref[j]
        return s if kind == 'far' else s + bbias_ref[_bias_kind(j, i), h]

    _run_key_block_streams(i, [
        _key_block_stream(i, n_maps, sa_ref, aacc_ref, a_logits, lambda j, c: avt_ref[0, j, c // 2]),
        _key_block_stream(i, B_HEADS, sb_ref, bacc_ref, b_logits, lambda j, h: bvt_ref[0, j])])

    for h in range(A_HEADS):
        a1 = aacc_ref[2 * h]
        a2 = aacc_ref[2 * h + 1]
        a = (a1[0:A_V_DIM] / a1[A_V_DIM:A_V_DIM + 1]
             - lam * (a2[0:A_V_DIM] / a2[A_V_DIM:A_V_DIM + 1]))
        y = a * lax.rsqrt(jnp.mean(a * a, axis=0, keepdims=True) + EPS) * g_ref[...]
        ot_ref[h * A_V_DIM:(h + 1) * A_V_DIM, :] = y * (1.0 - lambda_init)
    ya_ref[0] = ot_ref[...].astype(BF16)

    for h in range(B_HEADS):
        a = bacc_ref[h]
        ot_ref[h * B_HEAD_DIM:(h + 1) * B_HEAD_DIM, :] = (
            a[0:B_HEAD_DIM] / a[B_HEAD_DIM:B_HEAD_DIM + 1])
    yb_ref[0] = ot_ref[...].astype(BF16)


def _attention(proj, avt, bvt, iwt, lam_vecs, a_bias, b_bias, subln_g_col, *, lambda_init, top_k):
    bsz, s, _ = proj.shape
    nq = s // TQ
    nkb = s // TK
    a_width = A_HEADS * A_V_DIM
    b_width = B_HEADS * B_HEAD_DIM
    assert a_width == b_width
    iq_w = IDX_HEADS * IDX_DIM
    once = pl.Buffered(1)
    return pl.pallas_call(
        functools.partial(_attn_kernel, lambda_init=lambda_init, top_k=top_k, seq=s),
        grid=(bsz, nq),
        in_specs=[
            pl.BlockSpec((4, A_HEAD_DIM), lambda b, i: (0, 0)),
            pl.BlockSpec((1, TQ, a_width), lambda b, i: (b, i, COL_AQ // a_width)),
            pl.BlockSpec((1, s, a_width), lambda b, i: (b, 0, COL_AK // a_width)),
            pl.BlockSpec((1, nkb, A_HEADS, A_V_DIM + ONES_ROWS, TK), lambda b, i: (b, 0, 0, 0, 0)),
            pl.BlockSpec((3, A_HEADS, TK, TQ), lambda b, i: (0, 0, 0, 0), once),
            pl.BlockSpec((A_V_DIM, 1), lambda b, i: (0, 0)),
            pl.BlockSpec((1, TQ, b_width), lambda b, i: (b, i, COL_BQ // b_width)),
            pl.BlockSpec((1, TQ, iq_w), lambda b, i: (b, i, COL_IQ // iq_w)),
            pl.BlockSpec((1, s, LANES), lambda b, i: (b, 0, COL_KB // LANES), once),
            pl.BlockSpec((1, s, LANES), lambda b, i: (b, 0, COL_IK // LANES), once),
            pl.BlockSpec((1, nkb, B_HEAD_DIM + ONES_ROWS, TK), lambda b, i: (b, 0, 0, 0), once),
            pl.BlockSpec((1, IDX_HEADS, TQ), lambda b, i: (b, 0, i)),
            pl.BlockSpec((3, B_HEADS, TK, TQ), lambda b, i: (0, 0, 0, 0), once),
        ],
        out_specs=[pl.BlockSpec((1, a_width, TQ), lambda b, i: (b, 0, i)),
                   pl.BlockSpec((1, b_width, TQ), lambda b, i: (b, 0, i))],
        out_shape=[jax.ShapeDtypeStruct((bsz, a_width, s), BF16),
                   jax.ShapeDtypeStruct((bsz, b_width, s), BF16)],
        scratch_shapes=[
            pltpu.VMEM((2 * A_HEADS, TK, TQ), F32),
            pltpu.VMEM((B_HEADS, TK, TQ), F32),
            pltpu.VMEM((2 * A_HEADS, A_V_DIM + ONES_ROWS, TQ), F32),
            pltpu.VMEM((B_HEADS, B_HEAD_DIM + ONES_ROWS, TQ), F32),
            pltpu.VMEM((a_width, TQ), F32),
            pltpu.VMEM((TK, TQ), F32),
            pltpu.VMEM((nkb, TK, TQ), I32),
            pltpu.VMEM((KEY_BITS + _index_bits(s), nkb, SUBLANES, TQ), I32),
            pltpu.VMEM((nkb, TK, TQ), F32),
        ],
        compiler_params=pltpu.CompilerParams(
            dimension_semantics=("arbitrary", "arbitrary"), vmem_limit_bytes=VMEM_LIMIT),
        name="attention",
    )(lam_vecs, proj, proj, avt, a_bias, subln_g_col, proj, proj, proj, proj, bvt, iwt, b_bias)


TAIL_ROWS = 256
TAIL_FF = 1024


def _tail_kernel(ya_ref, yb_ref, wo_ref, x_ref, g_mix_ref, g_in_ref, wu_ref, wd_ref, g_out_ref,
                 o_ref, x1_ref, h_ref, acc_ref, *, mix_transposed):
    chunks = [slice(r, r + TAIL_ROWS) for r in range(0, x_ref.shape[0], TAIL_ROWS)]
    ff_steps = list(range(0, wu_ref.shape[1], TAIL_FF))

    def mlp_part(rows, k):
        u = jnp.maximum(_dot(h_ref[rows, :], wu_ref[:, k:k + TAIL_FF]), 0.0)
        return _dot((u * u).astype(BF16), wd_ref[k:k + TAIL_FF, :])

    half = wo_ref.shape[0] // 2

    def mix_dot(y_ref, rows, w):
        if mix_transposed:
            return lax.dot_general(y_ref[:, rows], w, (((0,), (0,)), ((), ())),
                                   preferred_element_type=F32)
        return _dot(y_ref[rows, :], w)

    for rows in chunks:
        m = (mix_dot(ya_ref, rows, wo_ref[0:half, :])
             + mix_dot(yb_ref, rows, wo_ref[half:2 * half, :]))
        x1 = x_ref[rows, :] + _rms(m, g_mix_ref[...])
        x1_ref[rows, :] = x1
        h_ref[rows, :] = _rms(x1, g_in_ref[...]).astype(BF16)
    acc_ref[...] = mlp_part(slice(None), ff_steps[0])
    for k in ff_steps[1:-1]:
        acc_ref[...] += mlp_part(slice(None), k)
    for rows in chunks:
        y = acc_ref[rows, :] + mlp_part(rows, ff_steps[-1])
        o_ref[rows, :] = x1_ref[rows, :] + _rms(y, g_out_ref[...])


def _layer_tail(ya, yb, wo, x, g_mix, g_in, wu, wd, layer, g_out, *, mix_transposed, tm=1024):
    n, d = x.shape
    half = wo.shape[0] // 2
    ff = wu.shape[-1]
    assert ff // TAIL_FF >= 2 and tm % TAIL_ROWS == 0
    once = pl.Buffered(1)
    if mix_transposed:
        per_batch = ya.shape[-1] // tm
        mix_spec = pl.BlockSpec((None, half, tm), lambda i: (i // per_batch, 0, i % per_batch))
    else:
        mix_spec = pl.BlockSpec((tm, half), lambda i: (i, 0))
    return pl.pallas_call(
        functools.partial(_tail_kernel, mix_transposed=mix_transposed),
        grid=(n // tm,),
        in_specs=[
            mix_spec,
            mix_spec,
            pl.BlockSpec((2 * half, d), lambda i: (0, 0), once),
            pl.BlockSpec((tm, d), lambda i: (i, 0)),
            pl.BlockSpec((1, d), lambda i: (0, 0)),
            pl.BlockSpec((1, d), lambda i: (0, 0)),
            pl.BlockSpec((None, d, ff), lambda i: (layer, 0, 0), once),
            pl.BlockSpec((None, ff, d), lambda i: (layer, 0, 0), once),
            pl.BlockSpec((1, d), lambda i: (0, 0)),
        ],
        out_specs=pl.BlockSpec((tm, d), lambda i: (i, 0)),
        out_shape=jax.ShapeDtypeStruct((n, d), F32),
        scratch_shapes=[pltpu.VMEM((tm, d), F32), pltpu.VMEM((tm, d), BF16),
                        pltpu.VMEM((tm, d), F32)],
        compiler_params=pltpu.CompilerParams(
            dimension_semantics=("arbitrary",), vmem_limit_bytes=TAIL_VMEM_LIMIT),
        name="layer_tail",
    )(ya, yb, wo, x, g_mix, g_in, wu, wd, g_out)


CONV_ROWS = 64
U_HIST = 32
Z_HIST = 8


def _conv_kernel(x_ref, g_in_ref, w_in_ref, dw_w_ref, dw_b_ref, ln_g_ref, ln_b_ref, sc_w_ref,
                 u_ref, z_ref, ubuf_ref, zbuf_ref, shift_ref, conv_ref, *, ts):
    @pl.when(pl.program_id(1) == 0)
    def _():
        ubuf_ref[0:U_HIST, :] = jnp.zeros((U_HIST, CONV_CH), F32)
        zbuf_ref[0:Z_HIST, :] = jnp.zeros((Z_HIST, SC_CH), F32)

    x = x_ref[0]
    hb = _rms(x, g_in_ref[...]).astype(BF16)
    c = CONV_CH
    ca = _dot(hb, w_in_ref[:, 0:c])
    cg = _dot(hb, w_in_ref[:, c:2 * c])
    ubuf_ref[U_HIST:U_HIST + ts, :] = ca * jax.nn.sigmoid(cg)
    dc = _dot(hb, w_in_ref[:, 3 * c:4 * c])
    dh = _dot(hb, w_in_ref[:, 4 * c:5 * c])
    zbuf_ref[Z_HIST:Z_HIST + ts, :] = dc * dh

    first_off = U_HIST - (CONV_WIDTH - 1)
    shift_rows = shift_ref.shape[1]
    for r in range(1, SUBLANES):
        shift_ref[r - 1] = ubuf_ref[r:r + shift_rows, :]

    for base in range(0, ts, CONV_ROWS):
        acc = jnp.zeros((CONV_ROWS, c), F32)
        for j in range(CONV_WIDTH):
            r, a = (first_off + j) % SUBLANES, (first_off + j) // SUBLANES
            rows = pl.ds(base + SUBLANES * a, CONV_ROWS)
            taps = ubuf_ref[rows, :] if r == 0 else shift_ref[r - 1, rows, :]
            acc = acc + dw_w_ref[j:j + 1, :] * taps
        conv_ref[base:base + CONV_ROWS, :] = acc
    u = conv_ref[...] + dw_b_ref[...]
    mu = jnp.mean(u, axis=-1, keepdims=True)
    ctr = u - mu
    var = jnp.mean(ctr * ctr, axis=-1, keepdims=True)
    u = ctr * lax.rsqrt(var + EPS) * ln_g_ref[...] + ln_b_ref[...]
    u = u * jax.nn.sigmoid(u)

    z = jnp.zeros((ts, c), F32)
    for j in range(SC_WIDTH):
        off = Z_HIST - (SC_WIDTH - 1) + j
        z = z + sc_w_ref[j:j + 1, :] * zbuf_ref[off:off + ts, :]
    z = _dot(hb, w_in_ref[:, 2 * c:3 * c]) * z

    ubuf_ref[0:U_HIST, :] = ubuf_ref[ts:ts + U_HIST, :]
    zbuf_ref[0:Z_HIST, :] = zbuf_ref[ts:ts + Z_HIST, :]

    u_ref[0] = u.astype(BF16)
    z_ref[0] = z.astype(BF16)


def _conv_mixer(x, g_in, w_in, dw_w, dw_b, ln_g, ln_b, sc_w, *, ts=1024):
    bsz, s, d = x.shape
    full = lambda a: pl.BlockSpec(a.shape, lambda b, i: (0,) * a.ndim, pl.Buffered(1))
    return pl.pallas_call(
        functools.partial(_conv_kernel, ts=ts),
        grid=(bsz, s // ts),
        in_specs=[pl.BlockSpec((1, ts, d), lambda b, i: (b, i, 0)),
                  full(g_in), full(w_in), full(dw_w), full(dw_b), full(ln_g), full(ln_b),
                  full(sc_w)],
        out_specs=[pl.BlockSpec((1, ts, CONV_CH), lambda b, i: (b, i, 0)),
                   pl.BlockSpec((1, ts, SC_CH), lambda b, i: (b, i, 0))],
        out_shape=[jax.ShapeDtypeStruct((bsz, s, CONV_CH), BF16),
                   jax.ShapeDtypeStruct((bsz, s, SC_CH), BF16)],
        scratch_shapes=[pltpu.VMEM((U_HIST + ts, CONV_CH), F32),
                        pltpu.VMEM((Z_HIST + ts, SC_CH), F32),
                        pltpu.VMEM((SUBLANES - 1, ts + U_HIST - SUBLANES, CONV_CH), F32),
                        pltpu.VMEM((ts, CONV_CH), F32)],
        compiler_params=pltpu.CompilerParams(
            dimension_semantics=("arbitrary", "arbitrary"), vmem_limit_bytes=VMEM_LIMIT),
        name="conv_mixer",
    )(x, g_in, w_in, dw_w, dw_b, ln_g, ln_b, sc_w)


def _t5_bucket(rel):
    nb = NUM_BUCKETS // 2
    ret = jnp.where(rel > 0, nb, 0)
    n = jnp.abs(rel)
    max_exact = nb // 2
    nf = jnp.maximum(n, 1).astype(jnp.float32)
    large = max_exact + (jnp.log(nf / max_exact) / math.log(MAX_DISTANCE / max_exact)
                         * (nb - max_exact)).astype(jnp.int32)
    large = jnp.minimum(large, nb - 1)
    return ret + jnp.where(n < max_exact, n, large)


REL_LO = -(TQ + TK - 1)
REL_LEN = TK - REL_LO
REL_LANES = -(-REL_LEN // LANES) * LANES


def _bias_kernel(fr_ref, a_ref, b_ref):
    krow = lax.broadcasted_iota(I32, (TK, TQ), 0)
    qcol = lax.broadcasted_iota(I32, (TK, TQ), 1)
    admissible = (krow // CHUNK) <= (qcol // CHUNK)
    zeros = jnp.zeros((TK, TQ), F32)
    for h in range(A_HEADS + B_HEADS):
        table = jnp.broadcast_to(fr_ref[h], (TK, REL_LANES))
        blocks = []
        for shift in (TK, 0):
            s0 = (-shift - REL_LO - (REL_LEN - 1)) % REL_LANES
            blocks.append(pltpu.roll(table, s0, 1, stride=1, stride_axis=0)[:, 0:TQ])
        prev, diag = blocks
        if h < A_HEADS:
            a_ref[0, h], a_ref[1, h], a_ref[2, h] = zeros, prev, jnp.where(admissible, diag, NEG)
        else:
            g = h - A_HEADS
            b_ref[0, g], b_ref[1, g], b_ref[2, g] = zeros, prev, diag


def _bias_tables(rel_bias):
    rel = jnp.arange(REL_LO, TK, dtype=jnp.int32)
    far_bucket = NUM_BUCKETS // 2 - 1
    f = (rel_bias[_t5_bucket(rel)].astype(F32) - rel_bias[far_bucket].astype(F32)[None, :]).T
    fr = jnp.pad(f[:, ::-1] * LOG2E, ((0, 0), (0, REL_LANES - REL_LEN)))[:, None, :]
    heads = A_HEADS + B_HEADS
    return pl.pallas_call(
        _bias_kernel,
        grid=(1,),
        in_specs=[pl.BlockSpec((heads, 1, REL_LANES), lambda i: (0, 0, 0))],
        out_specs=[pl.BlockSpec((3, A_HEADS, TK, TQ), lambda i: (0, 0, 0, 0)),
                   pl.BlockSpec((3, B_HEADS, TK, TQ), lambda i: (0, 0, 0, 0))],
        out_shape=[jax.ShapeDtypeStruct((3, A_HEADS, TK, TQ), F32),
                   jax.ShapeDtypeStruct((3, B_HEADS, TK, TQ), F32)],
        compiler_params=pltpu.CompilerParams(
            dimension_semantics=("arbitrary",), vmem_limit_bytes=VMEM_LIMIT),
        name="bias_tables",
    )(fr)


def _attn_weights(w_in):
    o = np.cumsum([0, 512, 512, 512, 512, 64, 64, 256, 32, 8])
    aq, ak, av, bq, bk, bv, iq, ik, iw = [w_in[:, o[n]:o[n + 1]] for n in range(9)]
    aq = aq * (A_HEAD_DIM ** -0.5 * LOG2E)
    bq = bq * (B_HEAD_DIM ** -0.5 * LOG2E)
    w = jnp.concatenate([aq, ak, bq, iq, bk, bk, ik, ik, ik, ik], axis=1).astype(BF16)
    pad = jnp.zeros((w_in.shape[0], VT_ROWS - ROW_IW - IDX_HEADS), w_in.dtype)
    wt = jnp.concatenate([av, bv, iw, pad], axis=1).T.astype(BF16)
    return w, wt


def kernel(x, rel_bias, norm_g, w_mlp_up, w_mlp_down, attn_w_in, attn_w_out, diff_lambda,
           diff_subln_g, conv_w_in, conv_w_out, conv_dw_w, conv_dw_b, conv_ln_g, conv_ln_b,
           sconv_w):
    bsz, s, d = x.shape
    depth = norm_g.shape[0]
    top_k = min(TOPK_MAX, s // 4)
    row = lambda v: v.reshape(1, -1)
    a_bias, b_bias = _bias_tables(rel_bias)
    w_up, w_down = w_mlp_up.astype(BF16), w_mlp_down.astype(BF16)
    for i in range(depth):
        j = i // 2
        if i % 2 == 0:
            lambda_init = 0.8 - 0.6 * math.exp(-0.3 * i)
            w, wt = _attn_weights(attn_w_in[j])
            proj, avt, bvt, iwt = _attn_in(x.reshape(bsz, s, d), row(norm_g[i, 0]), w, wt)
            ya, yb = _attention(proj, avt, bvt, iwt, diff_lambda[j], a_bias, b_bias,
                                diff_subln_g[j].reshape(-1, 1), lambda_init=lambda_init,
                                top_k=top_k)
            w_out = attn_w_out[j]
        else:
            ya, yb = _conv_mixer(x.reshape(bsz, s, d), row(norm_g[i, 0]),
                                 conv_w_in[j].astype(BF16), conv_dw_w[j], row(conv_dw_b[j]),
                                 row(conv_ln_g[j]), row(conv_ln_b[j]), sconv_w[j])
            w_out = conv_w_out[j]
        if i % 2:
            ya, yb = ya.reshape(bsz * s, -1), yb.reshape(bsz * s, -1)
        x = _layer_tail(ya, yb, w_out.astype(BF16), x.reshape(bsz * s, d), row(norm_g[i, 1]),
                        row(norm_g[i, 2]), w_up, w_down, i, row(norm_g[i, 3]),
                        mix_transposed=(i % 2 == 0))
    return x.reshape(bsz, s, d)
```

```python
import functools
import math

import numpy as np
import jax
import jax.numpy as jnp
from jax import lax
from jax.experimental import pallas as pl
from jax.experimental.pallas import tpu as pltpu

F32 = jnp.float32
BF16 = jnp.bfloat16
I32 = jnp.int32

CHUNK = 64
NUM_BUCKETS = 32
MAX_DISTANCE = 128
EPS = 1e-6
NEG = -1e30
LOG2E = math.log2(math.e)
A_HEADS = 4
A_HEAD_DIM = 64
A_V_DIM = 2 * A_HEAD_DIM
B_HEADS = 8
B_HEAD_DIM = 64
IDX_HEADS = 8
IDX_DIM = 32
TOPK_MAX = 256
CONV_CH = 512
CONV_WIDTH = 31
SC_CH = 512
SC_WIDTH = 3

LANES = 128
SUBLANES = 8
VMEM_LIMIT = 56 * 1024 * 1024
TAIL_VMEM_LIMIT = 60 * 1024 * 1024

TQ = 256
TK = 256
COL_AQ, COL_AK, COL_BQ, COL_IQ, COL_KB, COL_IK, PROJ_W = 0, 512, 1024, 1536, 1792, 1920, 2048
ROW_AV, ROW_BV, ROW_IW, VT_ROWS = 0, 512, 576, 640
ONES_ROWS = 16
INT_MIN = -2 ** 31
KEY_NEG_INF = INT_MIN + 0x7FFFFF


def _rms(x, g):
    return x * lax.rsqrt(jnp.mean(x * x, axis=-1, keepdims=True) + EPS) * g


def _dot_nt(a, b):
    return lax.dot_general(a, b, (((1,), (1,)), ((), ())), preferred_element_type=F32)


def _dot(a, b):
    return jnp.dot(a, b, preferred_element_type=F32)


def _lane_band(x, lo, width):
    lane = lax.broadcasted_iota(I32, x.shape, 1)
    return jnp.where((lane >= lo) & (lane < lo + width), x, jnp.zeros_like(x))


KEY_BITS = 32


def _bit_planes(keys):
    assert keys.shape[0] == KEY_BITS * SUBLANES
    u = keys ^ INT_MIN
    a = [u[SUBLANES * r:SUBLANES * (r + 1), :] for r in range(KEY_BITS)]
    j, m = KEY_BITS // 2, (1 << (KEY_BITS // 2)) - 1
    while j:
        mask = np.int32(np.uint32(m))
        k = 0
        while k < KEY_BITS:
            t = (a[k] ^ lax.shift_right_logical(a[k + j], jnp.int32(j))) & mask
            a[k] = a[k] ^ t
            a[k + j] = a[k + j] ^ lax.shift_left(t, jnp.int32(j))
            k = (k + j + 1) & ~j
        j >>= 1
        m = (m ^ (m << j)) & 0xFFFFFFFF
    return a


def _index_bits(seq):
    bits = seq.bit_length() - 1
    assert seq == 1 << bits and seq >= KEY_BITS * SUBLANES
    return bits


def _index_planes(nkb):
    shape = (nkb, SUBLANES, TQ)
    blk = lax.broadcasted_iota(I32, shape, 0)
    sub = lax.broadcasted_iota(I32, shape, 1)
    sub_bits = SUBLANES.bit_length() - 1
    word_bits = KEY_BITS.bit_length() - 1
    planes = []
    for b in range(_index_bits(nkb * TK) - 1, -1, -1):
        if b >= sub_bits + word_bits:
            on = ((nkb - 1 - blk) >> (b - sub_bits - word_bits)) & 1
            planes.append(jnp.where(on == 1, -1, 0))
        elif b >= sub_bits:
            t = b - sub_bits
            pattern = sum(1 << k for k in range(KEY_BITS) if (k >> t) & 1)
            planes.append(jnp.full(shape, np.int32(np.uint32(pattern)), I32))
        else:
            on = ((SUBLANES - 1 - sub) >> b) & 1
            planes.append(jnp.where(on == 1, -1, 0))
    return planes


def _bias_kind(j, i):
    return jnp.clip(j - i + 2, 0, 2)


def _key_block_stream(i, n_chains, s_ref, acc_ref, logits, v_aug):
    chains = range(n_chains)

    def stage(j, c, kind):
        s = logits(j, c, kind)
        s_ref[c] = s
        return jnp.max(s, axis=0, keepdims=True)

    def absorb(j, c, m_old, block_max):
        m_new = jnp.maximum(m_old, block_max)
        alpha = jnp.exp2(m_old - m_new)
        p = jnp.exp2(s_ref[c] - m_new).astype(BF16)
        acc_ref[c] = acc_ref[c] * alpha + _dot(v_aug(j, c), p)
        return m_new

    def start():
        acc_ref[...] = jnp.zeros_like(acc_ref)
        return (jnp.full((1, TQ), NEG, F32),) * n_chains, tuple(stage(0, c, 'any') for c in chains)

    def step(j, carry, kind):
        ms, bms = carry
        out = [(absorb(j, c, ms[c], bms[c]), stage(j + 1, c, kind)) for c in chains]
        return tuple(o[0] for o in out), tuple(o[1] for o in out)

    def finish(carry):
        ms, bms = carry
        for c in chains:
            absorb(i, c, ms[c], bms[c])

    return start, step, finish


def _run_key_block_streams(i, streams):
    def body(j, carries, kind):
        return tuple(step(j, c, kind) for (_, step, _), c in zip(streams, carries))

    carries = tuple(start() for start, _, _ in streams)
    n_far = jnp.maximum(i - 2, 0)
    carries = lax.fori_loop(0, n_far, functools.partial(body, kind='far'), carries)
    carries = lax.fori_loop(n_far, i, functools.partial(body, kind='near'), carries)
    for (_, _, finish), c in zip(streams, carries):
        finish(c)


def _attn_in_kernel(x_ref, g_ref, w_ref, wt_ref, proj_ref, avt_ref, bvt_ref, iwt_ref, *, tm):
    hb = _rms(x_ref[0], g_ref[...]).astype(BF16)
    proj_ref[0] = _dot(hb, w_ref[...]).astype(BF16)
    t = _dot_nt(wt_ref[...], hb)
    ones = jnp.ones((ONES_ROWS, TK), BF16)
    for c in range(tm // TK):
        cols = slice(c * TK, (c + 1) * TK)
        for h in range(A_HEADS):
            r = ROW_AV + h * A_V_DIM
            avt_ref[0, c, h, 0:A_V_DIM, :] = t[r:r + A_V_DIM, cols].astype(BF16)
            avt_ref[0, c, h, A_V_DIM:A_V_DIM + ONES_ROWS, :] = ones
        bvt_ref[0, c, 0:B_HEAD_DIM, :] = t[ROW_BV:ROW_IW, cols].astype(BF16)
        bvt_ref[0, c, B_HEAD_DIM:B_HEAD_DIM + ONES_ROWS, :] = ones
    iwt_ref[0] = t[ROW_IW:ROW_IW + IDX_HEADS, :] * (IDX_HEADS ** -0.5 * IDX_DIM ** -0.5)


def _attn_in(x, g, w, wt, *, tm=1024):
    bsz, s, d = x.shape
    av_blk = (A_HEADS, A_V_DIM + ONES_ROWS, TK)
    bv_blk = (B_HEAD_DIM + ONES_ROWS, TK)
    return pl.pallas_call(
        functools.partial(_attn_in_kernel, tm=tm),
        grid=(bsz, s // tm),
        in_specs=[
            pl.BlockSpec((1, tm, d), lambda b, i: (b, i, 0)),
            pl.BlockSpec((1, d), lambda b, i: (0, 0)),
            pl.BlockSpec((d, PROJ_W), lambda b, i: (0, 0)),
            pl.BlockSpec((VT_ROWS, d), lambda b, i: (0, 0)),
        ],
        out_specs=[
            pl.BlockSpec((1, tm, PROJ_W), lambda b, i: (b, i, 0)),
            pl.BlockSpec((1, tm // TK) + av_blk, lambda b, i: (b, i, 0, 0, 0)),
            pl.BlockSpec((1, tm // TK) + bv_blk, lambda b, i: (b, i, 0, 0)),
            pl.BlockSpec((1, IDX_HEADS, tm), lambda b, i: (b, 0, i)),
        ],
        out_shape=[
            jax.ShapeDtypeStruct((bsz, s, PROJ_W), BF16),
            jax.ShapeDtypeStruct((bsz, s // TK) + av_blk, BF16),
            jax.ShapeDtypeStruct((bsz, s // TK) + bv_blk, BF16),
            jax.ShapeDtypeStruct((bsz, IDX_HEADS, s), F32),
        ],
        compiler_params=pltpu.CompilerParams(
            dimension_semantics=("arbitrary", "arbitrary"), vmem_limit_bytes=VMEM_LIMIT),
        name="attn_in",
    )(x, g, w, wt)


def _key_rows(ref, j):
    return ref[0, pl.ds(pl.multiple_of(j * TK, TK), TK), :]


def _indexer(i, iq_ref, ik_ref, iwt_ref, score_ref, key_ref, plane_ref):
    iq = iq_ref[0]
    per_blk = LANES // IDX_DIM
    iqh = [_lane_band(iq[:, (h // per_blk) * LANES:(h // per_blk + 1) * LANES],
                      (h % per_blk) * IDX_DIM, IDX_DIM) for h in range(IDX_HEADS)]
    w = iwt_ref[0]

    def stage_scores(j):
        ikb = _key_rows(ik_ref, j)
        half = TK // 2
        for r in (0, half):
            acc = jnp.zeros((half, TQ), F32)
            for h in range(IDX_HEADS):
                acc = acc + jnp.maximum(_dot_nt(ikb[r:r + half], iqh[h]), 0.0) * w[h:h + 1, :]
            score_ref[r:r + half, :] = acc

    def staged_keys():
        acc = score_ref[...]
        acc = jnp.where(acc == 0.0, 0.0, acc)
        bits = lax.bitcast_convert_type(acc, I32)
        return bits ^ ((bits >> 31) & 0x7FFFFFFF)

    def store_keys(j, keys):
        key_ref[j] = keys
        for p, plane in enumerate(_bit_planes(keys)):
            plane_ref[p, j] = plane

    def per_block(j):
        keys = staged_keys()
        stage_scores(j + 1)
        store_keys(j, keys)

    def finish():
        krow = lax.broadcasted_iota(I32, (TK, TQ), 0)
        qcol = lax.broadcasted_iota(I32, (TK, TQ), 1)
        admissible = (krow // CHUNK) <= (qcol // CHUNK)
        store_keys(i, jnp.where(admissible, staged_keys(), KEY_NEG_INF))

    stage_scores(0)
    return per_block, finish


def _select_mask(i, key_ref, plane_ref, mask_ref, *, top_k, seq):
    nkb = seq // TK
    blk = lax.broadcasted_iota(I32, (nkb, SUBLANES, TQ), 0)
    alive0 = jnp.where(blk <= i, -1, 0)

    def bit_body(p, carry, n_bits):
        alive, n_gt, thr_u = carry
        hit = alive & plane_ref[p]
        pc = jnp.sum(lax.population_count(hit), axis=0)
        n_hit = jnp.sum(pc.astype(F32), axis=0, keepdims=True)
        take = (n_gt + n_hit) >= top_k
        thr_u = thr_u | jnp.where(take, lax.shift_left(jnp.int32(1), n_bits - 1 - p), 0)
        n_gt = n_gt + jnp.where(take, 0.0, n_hit)
        alive = jnp.where(take, hit, alive ^ hit)
        return alive, n_gt, thr_u

    zero = jnp.zeros((1, TQ), I32)
    alive, n_gt, thr_u = lax.fori_loop(
        0, KEY_BITS, lambda p, c: bit_body(p, c, KEY_BITS), (alive0, jnp.zeros((1, TQ), F32), zero))
    idx_bits = _index_bits(seq)
    _, _, rev_idx = lax.fori_loop(
        KEY_BITS, KEY_BITS + idx_bits,
        lambda p, c: bit_body(p, c, KEY_BITS + idx_bits), (alive, n_gt, zero))
    thr = thr_u ^ INT_MIN
    lim = seq - rev_idx
    lim = jnp.where(thr > KEY_NEG_INF, lim, seq)
    thr = jnp.maximum(thr, KEY_NEG_INF + 1)
    krow = lax.broadcasted_iota(I32, (TK, TQ), 0)

    def mask_body(j, c):
        late = jnp.where(krow >= lim - j * TK, 1, 0)
        mask_ref[j] = jnp.where(key_ref[j] - late >= thr, 0.0, NEG)
        return c

    lax.fori_loop(0, i + 1, mask_body, 0)


def _attn_kernel(lam_ref, aq_ref, ak_ref, avt_ref, abias_ref, g_ref,
                 bq_ref, iq_ref, kb_ref, ik_ref, bvt_ref, iwt_ref, bbias_ref,
                 ya_ref, yb_ref,
                 sa_ref, sb_ref, aacc_ref, bacc_ref, ot_ref,
                 score_ref, key_ref, plane_ref, mask_ref, *, lambda_init, top_k, seq):
    i = pl.program_id(1)
    nkb = seq // TK
    n_maps = 2 * A_HEADS

    @pl.when((i == 0) & (pl.program_id(0) == 0))
    def _():
        plane_ref[0:KEY_BITS] = jnp.zeros((KEY_BITS, nkb, SUBLANES, TQ), I32)
        for t, plane in enumerate(_index_planes(nkb)):
            plane_ref[KEY_BITS + t] = plane

    index_block, finish_index = _indexer(i, iq_ref, ik_ref, iwt_ref, score_ref, key_ref, plane_ref)

    def index_body(j, c):
        index_block(j)
        return c

    lax.fori_loop(0, i, index_body, 0)
    finish_index()
    _select_mask(i, key_ref, plane_ref, mask_ref, top_k=top_k, seq=seq)

    lv = lam_ref[...]
    lam = (jnp.exp(jnp.sum(lv[0:1] * lv[1:2], axis=-1, keepdims=True))
           - jnp.exp(jnp.sum(lv[2:3] * lv[3:4], axis=-1, keepdims=True)) + lambda_init)

    q = aq_ref[0]
    qm = []
    for h in range(A_HEADS):
        q12 = q[:, h * LANES:(h + 1) * LANES]
        qm += [_lane_band(q12, 0, A_HEAD_DIM), _lane_band(q12, A_HEAD_DIM, A_HEAD_DIM)]

    def a_logits(j, c, kind):
        h = c // 2
        kb = ak_ref[0, pl.ds(pl.multiple_of(j * TK, TK), TK), h * LANES:(h + 1) * LANES]
        s = _dot_nt(kb, qm[c])
        return s if kind == 'far' else s + abias_ref[_bias_kind(j, i), h]

    qb = bq_ref[0]
    qh = [_lane_band(qb[:, (h // 2) * LANES:(h // 2 + 1) * LANES],
                     (h % 2) * B_HEAD_DIM, B_HEAD_DIM) for h in range(B_HEADS)]

    def b_logits(j, h, kind):
        s = _dot_nt(_key_rows(kb_ref, j), qh[h]) + mask_ref[j]
        return s if kind == 'far' else s + bbias_ref[_bias_kind(j, i), h]

    _run_key_block_streams(i, [
        _key_block_stream(i, n_maps, sa_ref, aacc_ref, a_logits, lambda j, c: avt_ref[0, j, c // 2]),
        _key_block_stream(i, B_HEADS, sb_ref, bacc_ref, b_logits, lambda j, h: bvt_ref[0, j])])

    for h in range(A_HEADS):
        a1 = aacc_ref[2 * h]
        a2 = aacc_ref[2 * h + 1]
        a = (a1[0:A_V_DIM] / a1[A_V_DIM:A_V_DIM + 1]
             - lam * (a2[0:A_V_DIM] / a2[A_V_DIM:A_V_DIM + 1]))
        y = a * lax.rsqrt(jnp.mean(a * a, axis=0, keepdims=True) + EPS) * g_ref[...]
        ot_ref[h * A_V_DIM:(h + 1) * A_V_DIM, :] = y * (1.0 - lambda_init)
    ya_ref[0] = ot_ref[...].astype(BF16)

    for h in range(B_HEADS):
        a = bacc_ref[h]
        ot_ref[h * B_HEAD_DIM:(h + 1) * B_HEAD_DIM, :] = (
            a[0:B_HEAD_DIM] / a[B_HEAD_DIM:B_HEAD_DIM + 1])
    yb_ref[0] = ot_ref[...].astype(BF16)


def _attention(proj, avt, bvt, iwt, lam_vecs, a_bias, b_bias, subln_g_col, *, lambda_init, top_k):
    bsz, s, _ = proj.shape
    nq = s // TQ
    nkb = s // TK
    a_width = A_HEADS * A_V_DIM
    b_width = B_HEADS * B_HEAD_DIM
    assert a_width == b_width
    iq_w = IDX_HEADS * IDX_DIM
    once = pl.Buffered(1)
    return pl.pallas_call(
        functools.partial(_attn_kernel, lambda_init=lambda_init, top_k=top_k, seq=s),
        grid=(bsz, nq),
        in_specs=[
            pl.BlockSpec((4, A_HEAD_DIM), lambda b, i: (0, 0)),
            pl.BlockSpec((1, TQ, a_width), lambda b, i: (b, i, COL_AQ // a_width)),
            pl.BlockSpec((1, s, a_width), lambda b, i: (b, 0, COL_AK // a_width)),
            pl.BlockSpec((1, nkb, A_HEADS, A_V_DIM + ONES_ROWS, TK), lambda b, i: (b, 0, 0, 0, 0)),
            pl.BlockSpec((3, A_HEADS, TK, TQ), lambda b, i: (0, 0, 0, 0), once),
            pl.BlockSpec((A_V_DIM, 1), lambda b, i: (0, 0)),
            pl.BlockSpec((1, TQ, b_width), lambda b, i: (b, i, COL_BQ // b_width)),
            pl.BlockSpec((1, TQ, iq_w), lambda b, i: (b, i, COL_IQ // iq_w)),
            pl.BlockSpec((1, s, LANES), lambda b, i: (b, 0, COL_KB // LANES), once),
            pl.BlockSpec((1, s, LANES), lambda b, i: (b, 0, COL_IK // LANES), once),
            pl.BlockSpec((1, nkb, B_HEAD_DIM + ONES_ROWS, TK), lambda b, i: (b, 0, 0, 0), once),
            pl.BlockSpec((1, IDX_HEADS, TQ), lambda b, i: (b, 0, i)),
            pl.BlockSpec((3, B_HEADS, TK, TQ), lambda b, i: (0, 0, 0, 0), once),
        ],
        out_specs=[pl.BlockSpec((1, a_width, TQ), lambda b, i: (b, 0, i)),
                   pl.BlockSpec((1, b_width, TQ), lambda b, i: (b, 0, i))],
        out_shape=[jax.ShapeDtypeStruct((bsz, a_width, s), BF16),
                   jax.ShapeDtypeStruct((bsz, b_width, s), BF16)],
        scratch_shapes=[
            pltpu.VMEM((2 * A_HEADS, TK, TQ), F32),
            pltpu.VMEM((B_HEADS, TK, TQ), F32),
            pltpu.VMEM((2 * A_HEADS, A_V_DIM + ONES_ROWS, TQ), F32),
            pltpu.VMEM((B_HEADS, B_HEAD_DIM + ONES_ROWS, TQ), F32),
            pltpu.VMEM((a_width, TQ), F32),
            pltpu.VMEM((TK, TQ), F32),
            pltpu.VMEM((nkb, TK, TQ), I32),
            pltpu.VMEM((KEY_BITS + _index_bits(s), nkb, SUBLANES, TQ), I32),
            pltpu.VMEM((nkb, TK, TQ), F32),
        ],
        compiler_params=pltpu.CompilerParams(
            dimension_semantics=("arbitrary", "arbitrary"), vmem_limit_bytes=VMEM_LIMIT),
        name="attention",
    )(lam_vecs, proj, proj, avt, a_bias, subln_g_col, proj, proj, proj, proj, bvt, iwt, b_bias)


TAIL_ROWS = 256
TAIL_FF = 1024


def _tail_kernel(ya_ref, yb_ref, wo_ref, x_ref, g_mix_ref, g_in_ref, wu_ref, wd_ref, g_out_ref,
                 o_ref, x1_ref, h_ref, acc_ref, *, mix_transposed):
    chunks = [slice(r, r + TAIL_ROWS) for r in range(0, x_ref.shape[0], TAIL_ROWS)]
    ff_steps = list(range(0, wu_ref.shape[1], TAIL_FF))

    def mlp_part(rows, k):
        u = jnp.maximum(_dot(h_ref[rows, :], wu_ref[:, k:k + TAIL_FF]), 0.0)
        return _dot((u * u).astype(BF16), wd_ref[k:k + TAIL_FF, :])

    half = wo_ref.shape[0] // 2

    def mix_dot(y_ref, rows, w):
        if mix_transposed:
            return lax.dot_general(y_ref[:, rows], w, (((0,), (0,)), ((), ())),
                                   preferred_element_type=F32)
        return _dot(y_ref[rows, :], w)

    for rows in chunks:
        m = (mix_dot(ya_ref, rows, wo_ref[0:half, :])
             + mix_dot(yb_ref, rows, wo_ref[half:2 * half, :]))
        x1 = x_ref[rows, :] + _rms(m, g_mix_ref[...])
        x1_ref[rows, :] = x1
        h_ref[rows, :] = _rms(x1, g_in_ref[...]).astype(BF16)
    acc_ref[...] = mlp_part(slice(None), ff_steps[0])
    for k in ff_steps[1:-1]:
        acc_ref[...] += mlp_part(slice(None), k)
    for rows in chunks:
        y = acc_ref[rows, :] + mlp_part(rows, ff_steps[-1])
        o_ref[rows, :] = x1_ref[rows, :] + _rms(y, g_out_ref[...])


def _layer_tail(ya, yb, wo, x, g_mix, g_in, wu, wd, layer, g_out, *, mix_transposed, tm=1024):
    n, d = x.shape
    half = wo.shape[0] // 2
    ff = wu.shape[-1]
    assert ff // TAIL_FF >= 2 and tm % TAIL_ROWS == 0
    once = pl.Buffered(1)
    if mix_transposed:
        per_batch = ya.shape[-1] // tm
        mix_spec = pl.BlockSpec((None, half, tm), lambda i: (i // per_batch, 0, i % per_batch))
    else:
        mix_spec = pl.BlockSpec((tm, half), lambda i: (i, 0))
    return pl.pallas_call(
        functools.partial(_tail_kernel, mix_transposed=mix_transposed),
        grid=(n // tm,),
        in_specs=[
            mix_spec,
            mix_spec,
            pl.BlockSpec((2 * half, d), lambda i: (0, 0), once),
            pl.BlockSpec((tm, d), lambda i: (i, 0)),
            pl.BlockSpec((1, d), lambda i: (0, 0)),
            pl.BlockSpec((1, d), lambda i: (0, 0)),
            pl.BlockSpec((None, d, ff), lambda i: (layer, 0, 0), once),
            pl.BlockSpec((None, ff, d), lambda i: (layer, 0, 0), once),
            pl.BlockSpec((1, d), lambda i: (0, 0)),
        ],
        out_specs=pl.BlockSpec((tm, d), lambda i: (i, 0)),
        out_shape=jax.ShapeDtypeStruct((n, d), F32),
        scratch_shapes=[pltpu.VMEM((tm, d), F32), pltpu.VMEM((tm, d), BF16),
                        pltpu.VMEM((tm, d), F32)],
        compiler_params=pltpu.CompilerParams(
            dimension_semantics=("arbitrary",), vmem_limit_bytes=TAIL_VMEM_LIMIT),
        name="layer_tail",
    )(ya, yb, wo, x, g_mix, g_in, wu, wd, g_out)


CONV_ROWS = 64
GLU_ROWS = 256
U_HIST = 32
Z_HIST = 8


def _conv_kernel(x_ref, g_in_ref, w_in_ref, dw_w_ref, dw_b_ref, ln_g_ref, ln_b_ref, sc_w_ref,
                 u_ref, z_ref, ubuf_ref, zbuf_ref, shift_ref, conv_ref, *, ts):
    @pl.when(pl.program_id(1) == 0)
    def _():
        ubuf_ref[0:U_HIST, :] = jnp.zeros((U_HIST, CONV_CH), F32)
        zbuf_ref[0:Z_HIST, :] = jnp.zeros((Z_HIST, SC_CH), F32)

    x = x_ref[0]
    hb = _rms(x, g_in_ref[...]).astype(BF16)
    c = CONV_CH
    for r0 in range(0, ts, GLU_ROWS):
        hr = hb[r0:r0 + GLU_ROWS]
        ca = _dot(hr, w_in_ref[:, 0:c])
        cg = _dot(hr, w_in_ref[:, c:2 * c])
        ubuf_ref[U_HIST + r0:U_HIST + r0 + GLU_ROWS, :] = ca * jax.nn.sigmoid(cg)
        dc = _dot(hr, w_in_ref[:, 3 * c:4 * c])
        dh = _dot(hr, w_in_ref[:, 4 * c:5 * c])
        zbuf_ref[Z_HIST + r0:Z_HIST + r0 + GLU_ROWS, :] = dc * dh

    first_off = U_HIST - (CONV_WIDTH - 1)
    shift_rows = shift_ref.shape[1]
    for r in range(1, SUBLANES):
        shift_ref[r - 1] = ubuf_ref[r:r + shift_rows, :]

    for base in range(0, ts, CONV_ROWS):
        acc = jnp.zeros((CONV_ROWS, c), F32)
        for j in range(CONV_WIDTH):
            r, a = (first_off + j) % SUBLANES, (first_off + j) // SUBLANES
            rows = pl.ds(base + SUBLANES * a, CONV_ROWS)
            taps = ubuf_ref[rows, :] if r == 0 else shift_ref[r - 1, rows, :]
            acc = acc + dw_w_ref[j:j + 1, :] * taps
        conv_ref[base:base + CONV_ROWS, :] = acc
    u = conv_ref[...] + dw_b_ref[...]
    mu = jnp.mean(u, axis=-1, keepdims=True)
    ctr = u - mu
    var = jnp.mean(ctr * ctr, axis=-1, keepdims=True)
    u = ctr * lax.rsqrt(var + EPS) * ln_g_ref[...] + ln_b_ref[...]
    u = u * jax.nn.sigmoid(u)

    z = jnp.zeros((ts, c), F32)
    for j in range(SC_WIDTH):
        off = Z_HIST - (SC_WIDTH - 1) + j
        z = z + sc_w_ref[j:j + 1, :] * zbuf_ref[off:off + ts, :]
    z = _dot(hb, w_in_ref[:, 2 * c:3 * c]) * z

    ubuf_ref[0:U_HIST, :] = ubuf_ref[ts:ts + U_HIST, :]
    zbuf_ref[0:Z_HIST, :] = zbuf_ref[ts:ts + Z_HIST, :]

    u_ref[0] = u.astype(BF16)
    z_ref[0] = z.astype(BF16)


def _conv_mixer(x, g_in, w_in, dw_w, dw_b, ln_g, ln_b, sc_w, *, ts=1024):
    bsz, s, d = x.shape
    full = lambda a: pl.BlockSpec(a.shape, lambda b, i: (0,) * a.ndim, pl.Buffered(1))
    return pl.pallas_call(
        functools.partial(_conv_kernel, ts=ts),
        grid=(bsz, s // ts),
        in_specs=[pl.BlockSpec((1, ts, d), lambda b, i: (b, i, 0)),
                  full(g_in), full(w_in), full(dw_w), full(dw_b), full(ln_g), full(ln_b),
                  full(sc_w)],
        out_specs=[pl.BlockSpec((1, ts, CONV_CH), lambda b, i: (b, i, 0)),
                   pl.BlockSpec((1, ts, SC_CH), lambda b, i: (b, i, 0))],
        out_shape=[jax.ShapeDtypeStruct((bsz, s, CONV_CH), BF16),
                   jax.ShapeDtypeStruct((bsz, s, SC_CH), BF16)],
        scratch_shapes=[pltpu.VMEM((U_HIST + ts, CONV_CH), F32),
                        pltpu.VMEM((Z_HIST + ts, SC_CH), F32),
                        pltpu.VMEM((SUBLANES - 1, ts + U_HIST - SUBLANES, CONV_CH), F32),
                        pltpu.VMEM((ts, CONV_CH), F32)],
        compiler_params=pltpu.CompilerParams(
            dimension_semantics=("arbitrary", "arbitrary"), vmem_limit_bytes=VMEM_LIMIT),
        name="conv_mixer",
    )(x, g_in, w_in, dw_w, dw_b, ln_g, ln_b, sc_w)


def _t5_bucket(rel):
    nb = NUM_BUCKETS // 2
    ret = jnp.where(rel > 0, nb, 0)
    n = jnp.abs(rel)
    max_exact = nb // 2
    nf = jnp.maximum(n, 1).astype(jnp.float32)
    large = max_exact + (jnp.log(nf / max_exact) / math.log(MAX_DISTANCE / max_exact)
                         * (nb - max_exact)).astype(jnp.int32)
    large = jnp.minimum(large, nb - 1)
    return ret + jnp.where(n < max_exact, n, large)


REL_LO = -(TQ + TK - 1)
REL_LEN = TK - REL_LO
REL_LANES = -(-REL_LEN // LANES) * LANES


def _bias_kernel(fr_ref, a_ref, b_ref):
    krow = lax.broadcasted_iota(I32, (TK, TQ), 0)
    qcol = lax.broadcasted_iota(I32, (TK, TQ), 1)
    admissible = (krow // CHUNK) <= (qcol // CHUNK)
    zeros = jnp.zeros((TK, TQ), F32)
    for h in range(A_HEADS + B_HEADS):
        table = jnp.broadcast_to(fr_ref[h], (TK, REL_LANES))
        blocks = []
        for shift in (TK, 0):
            s0 = (-shift - REL_LO - (REL_LEN - 1)) % REL_LANES
            blocks.append(pltpu.roll(table, s0, 1, stride=1, stride_axis=0)[:, 0:TQ])
        prev, diag = blocks
        if h < A_HEADS:
            a_ref[0, h], a_ref[1, h], a_ref[2, h] = zeros, prev, jnp.where(admissible, diag, NEG)
        else:
            g = h - A_HEADS
            b_ref[0, g], b_ref[1, g], b_ref[2, g] = zeros, prev, diag


def _bias_tables(rel_bias):
    rel = jnp.arange(REL_LO, TK, dtype=jnp.int32)
    far_bucket = NUM_BUCKETS // 2 - 1
    f = (rel_bias[_t5_bucket(rel)].astype(F32) - rel_bias[far_bucket].astype(F32)[None, :]).T
    fr = jnp.pad(f[:, ::-1] * LOG2E, ((0, 0), (0, REL_LANES - REL_LEN)))[:, None, :]
    heads = A_HEADS + B_HEADS
    return pl.pallas_call(
        _bias_kernel,
        grid=(1,),
        in_specs=[pl.BlockSpec((heads, 1, REL_LANES), lambda i: (0, 0, 0))],
        out_specs=[pl.BlockSpec((3, A_HEADS, TK, TQ), lambda i: (0, 0, 0, 0)),
                   pl.BlockSpec((3, B_HEADS, TK, TQ), lambda i: (0, 0, 0, 0))],
        out_shape=[jax.ShapeDtypeStruct((3, A_HEADS, TK, TQ), F32),
                   jax.ShapeDtypeStruct((3, B_HEADS, TK, TQ), F32)],
        compiler_params=pltpu.CompilerParams(
            dimension_semantics=("arbitrary",), vmem_limit_bytes=VMEM_LIMIT),
        name="bias_tables",
    )(fr)


def _attn_weights(w_in):
    o = np.cumsum([0, 512, 512, 512, 512, 64, 64, 256, 32, 8])
    aq, ak, av, bq, bk, bv, iq, ik, iw = [w_in[:, o[n]:o[n + 1]] for n in range(9)]
    aq = aq * (A_HEAD_DIM ** -0.5 * LOG2E)
    bq = bq * (B_HEAD_DIM ** -0.5 * LOG2E)
    w = jnp.concatenate([aq, ak, bq, iq, bk, bk, ik, ik, ik, ik], axis=1).astype(BF16)
    pad = jnp.zeros((w_in.shape[0], VT_ROWS - ROW_IW - IDX_HEADS), w_in.dtype)
    wt = jnp.concatenate([av, bv, iw, pad], axis=1).T.astype(BF16)
    return w, wt


def kernel(x, rel_bias, norm_g, w_mlp_up, w_mlp_down, attn_w_in, attn_w_out, diff_lambda,
           diff_subln_g, conv_w_in, conv_w_out, conv_dw_w, conv_dw_b, conv_ln_g, conv_ln_b,
           sconv_w):
    bsz, s, d = x.shape
    depth = norm_g.shape[0]
    top_k = min(TOPK_MAX, s // 4)
    row = lambda v: v.reshape(1, -1)
    a_bias, b_bias = _bias_tables(rel_bias)
    w_up, w_down = w_mlp_up.astype(BF16), w_mlp_down.astype(BF16)
    for i in range(depth):
        j = i // 2
        if i % 2 == 0:
            lambda_init = 0.8 - 0.6 * math.exp(-0.3 * i)
            w, wt = _attn_weights(attn_w_in[j])
            proj, avt, bvt, iwt = _attn_in(x.reshape(bsz, s, d), row(norm_g[i, 0]), w, wt)
            ya, yb = _attention(proj, avt, bvt, iwt, diff_lambda[j], a_bias, b_bias,
                                diff_subln_g[j].reshape(-1, 1), lambda_init=lambda_init,
                                top_k=top_k)
            w_out = attn_w_out[j]
        else:
            ya, yb = _conv_mixer(x.reshape(bsz, s, d), row(norm_g[i, 0]),
                                 conv_w_in[j].astype(BF16), conv_dw_w[j], row(conv_dw_b[j]),
                                 row(conv_ln_g[j]), row(conv_ln_b[j]), sconv_w[j])
            w_out = conv_w_out[j]
        if i % 2:
            ya, yb = ya.reshape(bsz * s, -1), yb.reshape(bsz * s, -1)
        x = _layer_tail(ya, yb, w_out.astype(BF16), x.reshape(bsz * s, d), row(norm_g[i, 1]),
                        row(norm_g[i, 2]), w_up, w_down, i, row(norm_g[i, 3]),
                        mix_transposed=(i % 2 == 0))
    return x.reshape(bsz, s, d)
```
